```python
import math
import jax, jax.numpy as jnp
from jax import lax
import numpy as np

D_MODEL = 1024
BATCH = 16
SEQ = 2048
DEPTH = 4

N_MIXERS = 3
N_A = (DEPTH + 2) // 3
N_B = (DEPTH + 1) // 3
N_C = DEPTH // 3
EPS = 1e-6
CONV_W = 4
N_MEM = 256

M_D_INNER = 2 * D_MODEL
M_HEAD_DIM = 64
M_HEADS = M_D_INNER // M_HEAD_DIM
M_GROUPS = 8
M_STATE = 128
M_CONV_DIM = M_D_INNER + 2 * M_GROUPS * M_STATE
M_IN = M_D_INNER + M_CONV_DIM + M_HEADS
SSD_CHUNK = 64
DT_MIN = 1e-3
DT_MAX = 1e-1

H_EXPAND = 128
H_HEADS = D_MODEL // H_EXPAND
H_DV = D_MODEL // H_HEADS
HGRN_CHUNK = 32

G_HEAD_DIM = 128
G_QK_HEADS = D_MODEL // G_HEAD_DIM
G_V_HEADS = 2 * G_QK_HEADS
G_KEY_DIM = G_QK_HEADS * G_HEAD_DIM
G_VAL_DIM = G_V_HEADS * G_HEAD_DIM
G_CONV_DIM = 2 * G_KEY_DIM + G_VAL_DIM
G_IN = G_CONV_DIM + G_VAL_DIM + 2 * G_V_HEADS
GDN_CHUNK = 64

X_HEADS = 4
X_HEAD_DIM = D_MODEL // X_HEADS

D_FF = 2816
FFN_CONV_W = 3

kernel_name = 'hybrid_ssd_hgrn2_gdn_memxattn_block'


def rmsnorm(x, w):
    x32 = x.astype(jnp.float32)
    y = x32 * lax.rsqrt(jnp.mean(x32 * x32, axis=-1, keepdims=True) + EPS)
    return (y * w.astype(jnp.float32)).astype(x.dtype)


def l2norm(x):
    x32 = x.astype(jnp.float32)
    return x32 * lax.rsqrt(jnp.sum(x32 * x32, axis=-1, keepdims=True) + EPS)


def causal_dwconv(x, w):
    width, ch = w.shape
    return lax.conv_general_dilated(x, w[:, None, :].astype(x.dtype), window_strides=(1,),
                                    padding=[(width - 1, 0)],
                                    dimension_numbers=('NWC', 'WIO', 'NWC'),
                                    feature_group_count=ch)


def ssd_chunk(x, dt, a, bm, cm, chunk):
    bsz, seq, nh, p = x.shape
    ng, ns = bm.shape[2], bm.shape[3]
    r = nh // ng
    nc = seq // chunk
    f32 = jnp.float32
    xc = (x * dt[..., None]).astype(f32).reshape(bsz, nc, chunk, ng, r, p)
    acum = jnp.cumsum((dt * a).astype(f32).reshape(bsz, nc, chunk, ng, r), axis=2)
    bc = bm.astype(f32).reshape(bsz, nc, chunk, ng, ns)
    cc = cm.astype(f32).reshape(bsz, nc, chunk, ng, ns)
    causal = jnp.tril(jnp.ones((chunk, chunk), dtype=bool))[:, :, None, None]
    decay = jnp.exp(jnp.where(causal, acum[:, :, :, None] - acum[:, :, None, :], -jnp.inf))
    cb = jnp.einsum('bnlgk,bnsgk->bnlsg', cc, bc)
    y_diag = jnp.einsum('bnlsgr,bnsgrp->bnlgrp', cb[..., None] * decay, xc)

    def step(state, inp):
        c_, b_, x_, ac_ = inp
        y = jnp.einsum('blgk,bgrpk->blgrp', c_, state) * jnp.exp(ac_)[..., None]
        last = ac_[:, -1]
        ds = jnp.einsum('bsgk,bsgrp->bgrpk', b_, x_ * jnp.exp(last[:, None] - ac_)[..., None])
        state = state * jnp.exp(last)[..., None, None] + ds
        return state, y

    s0 = jnp.zeros((bsz, ng, r, p, ns), f32)
    xs = tuple(jnp.moveaxis(t, 1, 0) for t in (cc, bc, xc, acum))
    _, y_off = lax.scan(step, s0, xs)
    return (y_diag + jnp.moveaxis(y_off, 0, 1)).reshape(bsz, seq, nh, p)


def gla_chunk(q, k, v, log_f, chunk):
    bsz, seq, nh, dk = q.shape
    nc = seq // chunk

    def blocks(t):
        return t.astype(jnp.float32).reshape(bsz, nc, chunk, nh, t.shape[-1]).transpose(0, 1, 3, 2, 4)

    q, k, v = blocks(q), blocks(k), blocks(v)
    gc = jnp.cumsum(blocks(log_f), axis=3)
    g_last = gc[:, :, :, -1]
    q_dec = q * jnp.exp(gc)
    k_inv = k * jnp.exp(-gc)
    k_end = k * jnp.exp(g_last[:, :, :, None] - gc)
    causal = jnp.tril(jnp.ones((chunk, chunk), dtype=bool))
    att = jnp.where(causal, jnp.einsum('bnhlk,bnhsk->bnhls', q_dec, k_inv), 0.0)
    o_intra = jnp.einsum('bnhls,bnhsv->bnhlv', att, v)

    def step(state, inp):
        qd, ke, vv, gl = inp
        o = jnp.einsum('bhlk,bhkv->bhlv', qd, state)
        state = state * jnp.exp(gl)[..., None] + jnp.einsum('bhsk,bhsv->bhkv', ke, vv)
        return state, o

    s0 = jnp.zeros((bsz, nh, dk, v.shape[-1]), jnp.float32)
    xs = tuple(jnp.moveaxis(t, 1, 0) for t in (q_dec, k_end, v, g_last))
    _, o_inter = lax.scan(step, s0, xs)
    o = o_intra + jnp.moveaxis(o_inter, 0, 1)
    return o.transpose(0, 1, 3, 2, 4).reshape(bsz, seq, nh, -1)


def gated_delta_chunk(q, k, v, g, beta, chunk):
    bsz, seq, nh, dk = q.shape
    dv = v.shape[-1]
    nc = seq // chunk

    def blocks(t):
        return t.astype(jnp.float32).reshape(bsz, nc, chunk, nh, t.shape[-1]).transpose(0, 1, 3, 2, 4)

    def blocks_s(t):
        return t.astype(jnp.float32).reshape(bsz, nc, chunk, nh).transpose(0, 1, 3, 2)

    q, k, v = blocks(q), blocks(k), blocks(v)
    beta = blocks_s(beta)
    gc = jnp.cumsum(blocks_s(g), axis=-1)
    incl = jnp.tril(jnp.ones((chunk, chunk), dtype=bool))
    strict = jnp.tril(jnp.ones((chunk, chunk), dtype=bool), k=-1)
    decay = jnp.exp(jnp.where(incl, gc[..., :, None] - gc[..., None, :], -jnp.inf))
    kb = k * beta[..., None]
    m = jnp.where(strict, jnp.einsum('bnhlk,bnhsk->bnhls', kb, k) * decay, 0.0)
    a_mat = m + jnp.eye(chunk, dtype=jnp.float32)
    rhs = jnp.concatenate([v * beta[..., None], kb * jnp.exp(gc)[..., None]], axis=-1)
    sol = lax.linalg.triangular_solve(a_mat, rhs, left_side=True, lower=True, unit_diagonal=True)
    u, w = sol[..., :dv], sol[..., dv:]
    att = jnp.einsum('bnhlk,bnhsk->bnhls', q, k) * decay
    q_dec = q * jnp.exp(gc)[..., None]
    g_last = gc[..., -1]
    k_end = k * jnp.exp(g_last[..., None] - gc)[..., None]

    def step(state, inp):
        qd, aa, uu, ww, ke, gl = inp
        v_new = uu - jnp.einsum('bhlk,bhkv->bhlv', ww, state)
        o = jnp.einsum('bhlk,bhkv->bhlv', qd, state) + jnp.einsum('bhls,bhsv->bhlv', aa, v_new)
        state = state * jnp.exp(gl)[..., None, None] + jnp.einsum('bhsk,bhsv->bhkv', ke, v_new)
        return state, o

    s0 = jnp.zeros((bsz, nh, dk, dv), jnp.float32)
    xs = tuple(jnp.moveaxis(t, 1, 0) for t in (q_dec, att, u, w, k_end, g_last))
    _, o = lax.scan(step, s0, xs)
    o = jnp.moveaxis(o, 0, 1)
    return o.transpose(0, 1, 3, 2, 4).reshape(bsz, seq, nh, dv)


def mamba2_mixer(h, in_w, conv_w, conv_b, dt_bias, a_log, d_skip, norm_w, out_w):
    bsz, seq, _ = h.shape
    f32 = jnp.float32
    proj = h @ in_w
    z = proj[..., :M_D_INNER]
    xbc = jax.nn.silu(causal_dwconv(proj[..., M_D_INNER:M_D_INNER + M_CONV_DIM], conv_w) + conv_b)
    dt = jax.nn.softplus(proj[..., M_D_INNER + M_CONV_DIM:].astype(f32) + dt_bias.astype(f32))
    xs = xbc[..., :M_D_INNER].reshape(bsz, seq, M_HEADS, M_HEAD_DIM)
    bm = xbc[..., M_D_INNER:M_D_INNER + M_GROUPS * M_STATE].reshape(bsz, seq, M_GROUPS, M_STATE)
    cm = xbc[..., M_D_INNER + M_GROUPS * M_STATE:].reshape(bsz, seq, M_GROUPS, M_STATE)
    a = -jnp.exp(a_log.astype(f32))
    y = ssd_chunk(xs, dt, a, bm, cm, SSD_CHUNK) + d_skip.astype(f32)[:, None] * xs.astype(f32)
    y = y.reshape(bsz, seq, M_D_INNER).astype(h.dtype) * jax.nn.silu(z)
    gs = M_D_INNER // M_GROUPS
    y = rmsnorm(y.reshape(bsz, seq, M_GROUPS, gs), norm_w.reshape(M_GROUPS, gs))
    return y.reshape(bsz, seq, M_D_INNER) @ out_w


def hgrn2_mixer(h, in_w, lower_bound, norm_w, out_w):
    bsz, seq, _ = h.shape
    q, f, i, g = jnp.split(h @ in_w, 4, axis=-1)

    def heads(t):
        return t.reshape(bsz, seq, H_HEADS, -1)

    lb = lower_bound.astype(jnp.float32)
    forget = lb + (1.0 - lb) * jax.nn.sigmoid(f.astype(jnp.float32))
    o = gla_chunk(heads(jax.nn.silu(q)) * H_EXPAND ** -0.5, heads(1.0 - forget), heads(i),
                  heads(jnp.log(forget)), HGRN_CHUNK)
    o = rmsnorm(o.astype(h.dtype), norm_w) * jax.nn.silu(heads(g))
    return o.reshape(bsz, seq, D_MODEL) @ out_w


def gated_deltanet_mixer(h, in_w, conv_w, a_log, dt_bias, norm_w, out_w):
    bsz, seq, _ = h.shape
    f32 = jnp.float32
    proj = h @ in_w
    qkv = jax.nn.silu(causal_dwconv(proj[..., :G_CONV_DIM], conv_w))
    z = proj[..., G_CONV_DIM:G_CONV_DIM + G_VAL_DIM]
    b = proj[..., G_CONV_DIM + G_VAL_DIM:G_CONV_DIM + G_VAL_DIM + G_V_HEADS]
    a = proj[..., G_CONV_DIM + G_VAL_DIM + G_V_HEADS:]
    q = l2norm(qkv[..., :G_KEY_DIM].reshape(bsz, seq, G_QK_HEADS, G_HEAD_DIM))
    k = l2norm(qkv[..., G_KEY_DIM:2 * G_KEY_DIM].reshape(bsz, seq, G_QK_HEADS, G_HEAD_DIM))
    v = qkv[..., 2 * G_KEY_DIM:].reshape(bsz, seq, G_V_HEADS, G_HEAD_DIM)
    rep = G_V_HEADS // G_QK_HEADS
    q = jnp.repeat(q, rep, axis=2) * G_HEAD_DIM ** -0.5
    k = jnp.repeat(k, rep, axis=2)
    beta = jax.nn.sigmoid(b.astype(f32))
    g = -jnp.exp(a_log.astype(f32)) * jax.nn.softplus(a.astype(f32) + dt_bias.astype(f32))
    o = gated_delta_chunk(q, k, v, g, beta, GDN_CHUNK)
    o = rmsnorm(o.astype(h.dtype), norm_w) * jax.nn.silu(z.reshape(bsz, seq, G_V_HEADS, G_HEAD_DIM))
    return o.reshape(bsz, seq, G_VAL_DIM) @ out_w


def memory_cross_attention(h, mem_n, wq, wkv, wo):
    bsz, seq, _ = h.shape
    q = (h @ wq).reshape(bsz, seq, X_HEADS, X_HEAD_DIM)
    k, v = jnp.split(mem_n @ wkv, 2, axis=-1)
    k = k.reshape(bsz, -1, X_HEADS, X_HEAD_DIM)
    v = v.reshape(bsz, -1, X_HEADS, X_HEAD_DIM)
    s = jnp.einsum('blhd,bmhd->bhlm', q, k).astype(jnp.float32) * X_HEAD_DIM ** -0.5
    p = jax.nn.softmax(s, axis=-1).astype(h.dtype)
    o = jnp.einsum('bhlm,bmhd->blhd', p, v).reshape(bsz, seq, D_MODEL)
    return o @ wo


def conv_glu_ffn(h, up_w, conv_w, conv_b, down_w):
    gate, up = jnp.split(h @ up_w, 2, axis=-1)
    gate = causal_dwconv(gate, conv_w) + conv_b
    return (jax.nn.silu(gate) * up) @ down_w


def _fwd_setup_inputs(seed: int = 0) -> dict:
    key = jax.random.key(seed)
    keys = iter(jax.random.split(key, 48))
    d = D_MODEL
    out_scale = (2 * DEPTH) ** -0.5

    def normal(shape, scale):
        return scale * jax.random.normal(next(keys), shape, jnp.float32)

    def gain(shape):
        return 1.0 + 0.05 * jax.random.normal(next(keys), shape, jnp.float32)

    def dt_bias(shape):
        u = jax.random.uniform(next(keys), shape, jnp.float32)
        dt = jnp.exp(u * (math.log(DT_MAX) - math.log(DT_MIN)) + math.log(DT_MIN))
        return dt + jnp.log(-jnp.expm1(-dt))

    def a_log(shape):
        return jnp.log(jax.random.uniform(next(keys), shape, jnp.float32, 1.0, 16.0))

    return {
        'x': normal((BATCH, SEQ, d), 1.0),
        'mem': normal((BATCH, N_MEM, d), 1.0),
        'ln_mix': gain((DEPTH, d)),
        'ln_xattn': gain((DEPTH, d)),
        'ln_mem': gain((DEPTH, d)),
        'ln_ffn': gain((DEPTH, d)),
        'final_norm': gain((d,)),
        'm_in_w': normal((N_A, d, M_IN), d ** -0.5),
        'm_conv_w': normal((N_A, CONV_W, M_CONV_DIM), CONV_W ** -0.5),
        'm_conv_b': normal((N_A, M_CONV_DIM), 0.02),
        'm_dt_bias': dt_bias((N_A, M_HEADS)),
        'm_a_log': a_log((N_A, M_HEADS)),
        'm_d': gain((N_A, M_HEADS)),
        'm_norm_w': gain((N_A, M_D_INNER)),
        'm_out_w': normal((N_A, M_D_INNER, d), M_D_INNER ** -0.5 * out_scale),
        'h_in_w': normal((N_B, d, 4 * d), d ** -0.5),
        'h_lower_bounds': normal((DEPTH, d), 0.1),
        'h_norm_w': gain((N_B, H_DV)),
        'h_out_w': normal((N_B, d, d), d ** -0.5 * out_scale),
        'g_in_w': normal((N_C, d, G_IN), d ** -0.5),
        'g_conv_w': normal((N_C, CONV_W, G_CONV_DIM), CONV_W ** -0.5),
        'g_a_log': a_log((N_C, G_V_HEADS)),
        'g_dt_bias': dt_bias((N_C, G_V_HEADS)),
        'g_norm_w': gain((N_C, G_HEAD_DIM)),
        'g_out_w': normal((N_C, G_VAL_DIM, d), G_VAL_DIM ** -0.5 * out_scale),
        'xa_q': normal((DEPTH, d, d), d ** -0.5),
        'xa_kv': normal((DEPTH, d, 2 * d), d ** -0.5),
        'xa_o': normal((DEPTH, d, d), d ** -0.5 * out_scale),
        'f_up': normal((DEPTH, d, 2 * D_FF), d ** -0.5),
        'f_conv_w': normal((DEPTH, FFN_CONV_W, D_FF), FFN_CONV_W ** -0.5),
        'f_conv_b': normal((DEPTH, D_FF), 0.02),
        'f_down': normal((DEPTH, D_FF, d), D_FF ** -0.5 * out_scale),
    }


def _fwd_reference(x, mem, ln_mix, ln_xattn, ln_mem, ln_ffn, final_norm,
              m_in_w, m_conv_w, m_conv_b, m_dt_bias, m_a_log, m_d, m_norm_w, m_out_w,
              h_in_w, h_lower_bounds, h_norm_w, h_out_w,
              g_in_w, g_conv_w, g_a_log, g_dt_bias, g_norm_w, g_out_w,
              xa_q, xa_kv, xa_o, f_up, f_conv_w, f_conv_b, f_down):
    lb = jnp.cumsum(jax.nn.softmax(h_lower_bounds.astype(jnp.float32), axis=0), axis=0)
    lb = lb - lb[:1]
    ia = 0
    ib = 0
    ic = 0
    for i in range(DEPTH):
        hn = rmsnorm(x, ln_mix[i])
        if i % N_MIXERS == 0:
            mix = mamba2_mixer(hn, m_in_w[ia], m_conv_w[ia], m_conv_b[ia], m_dt_bias[ia],
                               m_a_log[ia], m_d[ia], m_norm_w[ia], m_out_w[ia])
            ia += 1
        elif i % N_MIXERS == 1:
            mix = hgrn2_mixer(hn, h_in_w[ib], lb[i], h_norm_w[ib], h_out_w[ib])
            ib += 1
        else:
            mix = gated_deltanet_mixer(hn, g_in_w[ic], g_conv_w[ic], g_a_log[ic], g_dt_bias[ic],
                                       g_norm_w[ic], g_out_w[ic])
            ic += 1
        x = x + mix.astype(x.dtype)
        x = x + memory_cross_attention(rmsnorm(x, ln_xattn[i]), rmsnorm(mem, ln_mem[i]),
                                       xa_q[i], xa_kv[i], xa_o[i]).astype(x.dtype)
        x = x + conv_glu_ffn(rmsnorm(x, ln_ffn[i]), f_up[i], f_conv_w[i], f_conv_b[i],
                             f_down[i]).astype(x.dtype)
    return rmsnorm(x, final_norm)


import jax as _jax
import jax.numpy as _jnp

TWIN_FORMAT = 'train_step'
FWD_PARAMS = ['x', 'mem', 'ln_mix', 'ln_xattn', 'ln_mem', 'ln_ffn', 'final_norm', 'm_in_w', 'm_conv_w', 'm_conv_b', 'm_dt_bias', 'm_a_log', 'm_d', 'm_norm_w', 'm_out_w', 'h_in_w', 'h_lower_bounds', 'h_norm_w', 'h_out_w', 'g_in_w', 'g_conv_w', 'g_a_log', 'g_dt_bias', 'g_norm_w', 'g_out_w', 'xa_q', 'xa_kv', 'xa_o', 'f_up', 'f_conv_w', 'f_conv_b', 'f_down']
TWIN_WEIGHTS = ['ln_mix', 'ln_xattn', 'ln_mem', 'ln_ffn', 'final_norm', 'm_in_w', 'm_conv_w', 'm_conv_b', 'm_dt_bias', 'm_a_log', 'm_d', 'm_norm_w', 'm_out_w', 'h_in_w', 'h_lower_bounds', 'h_norm_w', 'h_out_w', 'g_in_w', 'g_conv_w', 'g_a_log', 'g_dt_bias', 'g_norm_w', 'g_out_w', 'xa_q', 'xa_kv', 'xa_o', 'f_up', 'f_conv_w', 'f_conv_b', 'f_down']
TWIN_DIFF_INPUT = 'x'
TWIN_INPUTS = ['x', 'mem', 'ln_mix', 'ln_xattn', 'ln_mem', 'ln_ffn', 'final_norm', 'm_in_w', 'm_conv_w', 'm_conv_b', 'm_dt_bias', 'm_a_log', 'm_d', 'm_norm_w', 'm_out_w', 'h_in_w', 'h_lower_bounds', 'h_norm_w', 'h_out_w', 'g_in_w', 'g_conv_w', 'g_a_log', 'g_dt_bias', 'g_norm_w', 'g_out_w', 'xa_q', 'xa_kv', 'xa_o', 'f_up', 'f_conv_w', 'f_conv_b', 'f_down', 'loss_target', 'm_ln_mix', 'm_ln_xattn', 'm_ln_mem', 'm_ln_ffn', 'm_final_norm', 'm_m_in_w', 'm_m_conv_w', 'm_m_conv_b', 'm_m_dt_bias', 'm_m_a_log', 'm_m_d', 'm_m_norm_w', 'm_m_out_w', 'm_h_in_w', 'm_h_lower_bounds', 'm_h_norm_w', 'm_h_out_w', 'm_g_in_w', 'm_g_conv_w', 'm_g_a_log', 'm_g_dt_bias', 'm_g_norm_w', 'm_g_out_w', 'm_xa_q', 'm_xa_kv', 'm_xa_o', 'm_f_up', 'm_f_conv_w', 'm_f_conv_b', 'm_f_down', 'v_ln_mix', 'v_ln_xattn', 'v_ln_mem', 'v_ln_ffn', 'v_final_norm', 'v_m_in_w', 'v_m_conv_w', 'v_m_conv_b', 'v_m_dt_bias', 'v_m_a_log', 'v_m_d', 'v_m_norm_w', 'v_m_out_w', 'v_h_in_w', 'v_h_lower_bounds', 'v_h_norm_w', 'v_h_out_w', 'v_g_in_w', 'v_g_conv_w', 'v_g_a_log', 'v_g_dt_bias', 'v_g_norm_w', 'v_g_out_w', 'v_xa_q', 'v_xa_kv', 'v_xa_o', 'v_f_up', 'v_f_conv_w', 'v_f_conv_b', 'v_f_down']
TWIN_OUTPUTS = ['loss', 'grad_x', 'grad_ln_mix', 'grad_ln_xattn', 'grad_ln_mem', 'grad_ln_ffn', 'grad_final_norm', 'grad_m_in_w', 'grad_m_conv_w', 'grad_m_conv_b', 'grad_m_dt_bias', 'grad_m_a_log', 'grad_m_d', 'grad_m_norm_w', 'grad_m_out_w', 'grad_h_in_w', 'grad_h_lower_bounds', 'grad_h_norm_w', 'grad_h_out_w', 'grad_g_in_w', 'grad_g_conv_w', 'grad_g_a_log', 'grad_g_dt_bias', 'grad_g_norm_w', 'grad_g_out_w', 'grad_xa_q', 'grad_xa_kv', 'grad_xa_o', 'grad_f_up', 'grad_f_conv_w', 'grad_f_conv_b', 'grad_f_down', 'delta_ln_mix', 'delta_ln_xattn', 'delta_ln_mem', 'delta_ln_ffn', 'delta_final_norm', 'delta_m_in_w', 'delta_m_conv_w', 'delta_m_conv_b', 'delta_m_dt_bias', 'delta_m_a_log', 'delta_m_d', 'delta_m_norm_w', 'delta_m_out_w', 'delta_h_in_w', 'delta_h_lower_bounds', 'delta_h_norm_w', 'delta_h_out_w', 'delta_g_in_w', 'delta_g_conv_w', 'delta_g_a_log', 'delta_g_dt_bias', 'delta_g_norm_w', 'delta_g_out_w', 'delta_xa_q', 'delta_xa_kv', 'delta_xa_o', 'delta_f_up', 'delta_f_conv_w', 'delta_f_conv_b', 'delta_f_down', 'new_m_ln_mix', 'new_m_ln_xattn', 'new_m_ln_mem', 'new_m_ln_ffn', 'new_m_final_norm', 'new_m_m_in_w', 'new_m_m_conv_w', 'new_m_m_conv_b', 'new_m_m_dt_bias', 'new_m_m_a_log', 'new_m_m_d', 'new_m_m_norm_w', 'new_m_m_out_w', 'new_m_h_in_w', 'new_m_h_lower_bounds', 'new_m_h_norm_w', 'new_m_h_out_w', 'new_m_g_in_w', 'new_m_g_conv_w', 'new_m_g_a_log', 'new_m_g_dt_bias', 'new_m_g_norm_w', 'new_m_g_out_w', 'new_m_xa_q', 'new_m_xa_kv', 'new_m_xa_o', 'new_m_f_up', 'new_m_f_conv_w', 'new_m_f_conv_b', 'new_m_f_down', 'new_v_ln_mix', 'new_v_ln_xattn', 'new_v_ln_mem', 'new_v_ln_ffn', 'new_v_final_norm', 'new_v_m_in_w', 'new_v_m_conv_w', 'new_v_m_conv_b', 'new_v_m_dt_bias', 'new_v_m_a_log', 'new_v_m_d', 'new_v_m_norm_w', 'new_v_m_out_w', 'new_v_h_in_w', 'new_v_h_lower_bounds', 'new_v_h_norm_w', 'new_v_h_out_w', 'new_v_g_in_w', 'new_v_g_conv_w', 'new_v_g_a_log', 'new_v_g_dt_bias', 'new_v_g_norm_w', 'new_v_g_out_w', 'new_v_xa_q', 'new_v_xa_kv', 'new_v_xa_o', 'new_v_f_up', 'new_v_f_conv_w', 'new_v_f_conv_b', 'new_v_f_down']
TWIN_LEAF_KINDS = {'loss': 'loss', 'grad_x': 'grad_x', 'grad_ln_mix': 'grad_w', 'grad_ln_xattn': 'grad_w', 'grad_ln_mem': 'grad_w', 'grad_ln_ffn': 'grad_w', 'grad_final_norm': 'grad_w', 'grad_m_in_w': 'grad_w', 'grad_m_conv_w': 'grad_w', 'grad_m_conv_b': 'grad_w', 'grad_m_dt_bias': 'grad_w', 'grad_m_a_log': 'grad_w', 'grad_m_d': 'grad_w', 'grad_m_norm_w': 'grad_w', 'grad_m_out_w': 'grad_w', 'grad_h_in_w': 'grad_w', 'grad_h_lower_bounds': 'grad_w', 'grad_h_norm_w': 'grad_w', 'grad_h_out_w': 'grad_w', 'grad_g_in_w': 'grad_w', 'grad_g_conv_w': 'grad_w', 'grad_g_a_log': 'grad_w', 'grad_g_dt_bias': 'grad_w', 'grad_g_norm_w': 'grad_w', 'grad_g_out_w': 'grad_w', 'grad_xa_q': 'grad_w', 'grad_xa_kv': 'grad_w', 'grad_xa_o': 'grad_w', 'grad_f_up': 'grad_w', 'grad_f_conv_w': 'grad_w', 'grad_f_conv_b': 'grad_w', 'grad_f_down': 'grad_w', 'delta_ln_mix': 'delta_w', 'delta_ln_xattn': 'delta_w', 'delta_ln_mem': 'delta_w', 'delta_ln_ffn': 'delta_w', 'delta_final_norm': 'delta_w', 'delta_m_in_w': 'delta_w', 'delta_m_conv_w': 'delta_w', 'delta_m_conv_b': 'delta_w', 'delta_m_dt_bias': 'delta_w', 'delta_m_a_log': 'delta_w', 'delta_m_d': 'delta_w', 'delta_m_norm_w': 'delta_w', 'delta_m_out_w': 'delta_w', 'delta_h_in_w': 'delta_w', 'delta_h_lower_bounds': 'delta_w', 'delta_h_norm_w': 'delta_w', 'delta_h_out_w': 'delta_w', 'delta_g_in_w': 'delta_w', 'delta_g_conv_w': 'delta_w', 'delta_g_a_log': 'delta_w', 'delta_g_dt_bias': 'delta_w', 'delta_g_norm_w': 'delta_w', 'delta_g_out_w': 'delta_w', 'delta_xa_q': 'delta_w', 'delta_xa_kv': 'delta_w', 'delta_xa_o': 'delta_w', 'delta_f_up': 'delta_w', 'delta_f_conv_w': 'delta_w', 'delta_f_conv_b': 'delta_w', 'delta_f_down': 'delta_w', 'new_m_ln_mix': 'new_m', 'new_m_ln_xattn': 'new_m', 'new_m_ln_mem': 'new_m', 'new_m_ln_ffn': 'new_m', 'new_m_final_norm': 'new_m', 'new_m_m_in_w': 'new_m', 'new_m_m_conv_w': 'new_m', 'new_m_m_conv_b': 'new_m', 'new_m_m_dt_bias': 'new_m', 'new_m_m_a_log': 'new_m', 'new_m_m_d': 'new_m', 'new_m_m_norm_w': 'new_m', 'new_m_m_out_w': 'new_m', 'new_m_h_in_w': 'new_m', 'new_m_h_lower_bounds': 'new_m', 'new_m_h_norm_w': 'new_m', 'new_m_h_out_w': 'new_m', 'new_m_g_in_w': 'new_m', 'new_m_g_conv_w': 'new_m', 'new_m_g_a_log': 'new_m', 'new_m_g_dt_bias': 'new_m', 'new_m_g_norm_w': 'new_m', 'new_m_g_out_w': 'new_m', 'new_m_xa_q': 'new_m', 'new_m_xa_kv': 'new_m', 'new_m_xa_o': 'new_m', 'new_m_f_up': 'new_m', 'new_m_f_conv_w': 'new_m', 'new_m_f_conv_b': 'new_m', 'new_m_f_down': 'new_m', 'new_v_ln_mix': 'new_v', 'new_v_ln_xattn': 'new_v', 'new_v_ln_mem': 'new_v', 'new_v_ln_ffn': 'new_v', 'new_v_final_norm': 'new_v', 'new_v_m_in_w': 'new_v', 'new_v_m_conv_w': 'new_v', 'new_v_m_conv_b': 'new_v', 'new_v_m_dt_bias': 'new_v', 'new_v_m_a_log': 'new_v', 'new_v_m_d': 'new_v', 'new_v_m_norm_w': 'new_v', 'new_v_m_out_w': 'new_v', 'new_v_h_in_w': 'new_v', 'new_v_h_lower_bounds': 'new_v', 'new_v_h_norm_w': 'new_v', 'new_v_h_out_w': 'new_v', 'new_v_g_in_w': 'new_v', 'new_v_g_conv_w': 'new_v', 'new_v_g_a_log': 'new_v', 'new_v_g_dt_bias': 'new_v', 'new_v_g_norm_w': 'new_v', 'new_v_g_out_w': 'new_v', 'new_v_xa_q': 'new_v', 'new_v_xa_kv': 'new_v', 'new_v_xa_o': 'new_v', 'new_v_f_up': 'new_v', 'new_v_f_conv_w': 'new_v', 'new_v_f_conv_b': 'new_v', 'new_v_f_down': 'new_v'}


def _forward(args):
    return _fwd_reference(*[args[k] for k in FWD_PARAMS])


def _output_shape():
    out = _jax.eval_shape(lambda: _forward(_fwd_setup_inputs(0)))
    return out.shape, out.dtype

N_MICROBATCH = 1
ADAM_LR = 0.001
ADAM_B1 = 0.9
ADAM_B2 = 0.999
ADAM_EPS = 1e-08
ADAM_WD = 0.01
ADAM_STEP = 10
PER_EXAMPLE_BATCH_AXIS = {'x': 0, 'mem': 0, 'loss_target': 0}
SHARED_INPUTS = []
_WEIGHT_DTYPES = {'ln_mix': _jnp.float32, 'ln_xattn': _jnp.float32, 'ln_mem': _jnp.float32, 'ln_ffn': _jnp.float32, 'final_norm': _jnp.float32, 'm_in_w': _jnp.float32, 'm_conv_w': _jnp.float32, 'm_conv_b': _jnp.float32, 'm_dt_bias': _jnp.float32, 'm_a_log': _jnp.float32, 'm_d': _jnp.float32, 'm_norm_w': _jnp.float32, 'm_out_w': _jnp.float32, 'h_in_w': _jnp.float32, 'h_lower_bounds': _jnp.float32, 'h_norm_w': _jnp.float32, 'h_out_w': _jnp.float32, 'g_in_w': _jnp.float32, 'g_conv_w': _jnp.float32, 'g_a_log': _jnp.float32, 'g_dt_bias': _jnp.float32, 'g_norm_w': _jnp.float32, 'g_out_w': _jnp.float32, 'xa_q': _jnp.float32, 'xa_kv': _jnp.float32, 'xa_o': _jnp.float32, 'f_up': _jnp.float32, 'f_conv_w': _jnp.float32, 'f_conv_b': _jnp.float32, 'f_down': _jnp.float32}
MOMENT_SCALE = {'ln_mix': 7.837701e-02, 'ln_xattn': 6.519764e-03, 'ln_mem': 9.681923e-03, 'ln_ffn': 5.479901e-02, 'final_norm': 3.197094e+01, 'm_in_w': 3.836329e-02, 'm_conv_w': 3.401142e-02, 'm_conv_b': 4.927725e-02, 'm_dt_bias': 9.715213e-02, 'm_a_log': 1.560667e-01, 'm_d': 1.963190e-01, 'm_norm_w': 4.435203e-02, 'm_out_w': 1.792715e-01, 'h_in_w': 2.840309e-02, 'h_lower_bounds': 1.199896e-03, 'h_norm_w': 1.430363e-01, 'h_out_w': 1.108612e-01, 'g_in_w': 2.474509e-02, 'g_conv_w': 2.453450e-02, 'g_a_log': 9.529286e-02, 'g_dt_bias': 9.243833e-02, 'g_norm_w': 9.684918e-02, 'g_out_w': 1.010727e-01, 'xa_q': 6.543385e-03, 'xa_kv': 6.686109e-03, 'xa_o': 1.929350e-02, 'f_up': 2.266669e-02, 'f_conv_w': 2.309544e-02, 'f_conv_b': 2.202032e-02, 'f_down': 1.048745e-01}


def _to_microbatches(a, axis):
    t = _jnp.moveaxis(a, axis, 0)
    t = t.reshape((N_MICROBATCH, t.shape[0] // N_MICROBATCH) + t.shape[1:])
    return _jnp.moveaxis(t, 1, axis + 1)


def setup_inputs(seed: int = 0) -> dict:
    inp = _fwd_setup_inputs(seed)
    key = _jax.random.fold_in(_jax.random.key(seed), 7919)
    shape, _ = _output_shape()
    out = dict(inp)
    out["loss_target"] = _jax.random.normal(_jax.random.fold_in(key, 0), shape, _jnp.float32)
    for i, name in enumerate(TWIN_WEIGHTS):
        w = inp[name].astype(_jnp.float32)
        if MOMENT_SCALE is None:
            s = _jnp.sqrt(_jnp.mean(_jnp.square(w)) + 1e-30)
        else:
            s = MOMENT_SCALE[name]
        km, kv = _jax.random.split(_jax.random.fold_in(key, i + 1))
        out[name] = w
        out["m_" + name] = s * _jax.random.normal(km, w.shape, _jnp.float32)
        out["v_" + name] = (s * s) * _jax.random.uniform(kv, w.shape, _jnp.float32, 0.5, 1.5)
    if N_MICROBATCH > 1:
        for name, axis in PER_EXAMPLE_BATCH_AXIS.items():
            out[name] = _to_microbatches(out[name], axis)
    return {'x': out['x'], 'mem': out['mem'], 'ln_mix': out['ln_mix'], 'ln_xattn': out['ln_xattn'], 'ln_mem': out['ln_mem'], 'ln_ffn': out['ln_ffn'], 'final_norm': out['final_norm'], 'm_in_w': out['m_in_w'], 'm_conv_w': out['m_conv_w'], 'm_conv_b': out['m_conv_b'], 'm_dt_bias': out['m_dt_bias'], 'm_a_log': out['m_a_log'], 'm_d': out['m_d'], 'm_norm_w': out['m_norm_w'], 'm_out_w': out['m_out_w'], 'h_in_w': out['h_in_w'], 'h_lower_bounds': out['h_lower_bounds'], 'h_norm_w': out['h_norm_w'], 'h_out_w': out['h_out_w'], 'g_in_w': out['g_in_w'], 'g_conv_w': out['g_conv_w'], 'g_a_log': out['g_a_log'], 'g_dt_bias': out['g_dt_bias'], 'g_norm_w': out['g_norm_w'], 'g_out_w': out['g_out_w'], 'xa_q': out['xa_q'], 'xa_kv': out['xa_kv'], 'xa_o': out['xa_o'], 'f_up': out['f_up'], 'f_conv_w': out['f_conv_w'], 'f_conv_b': out['f_conv_b'], 'f_down': out['f_down'], 'loss_target': out['loss_target'], 'm_ln_mix': out['m_ln_mix'], 'm_ln_xattn': out['m_ln_xattn'], 'm_ln_mem': out['m_ln_mem'], 'm_ln_ffn': out['m_ln_ffn'], 'm_final_norm': out['m_final_norm'], 'm_m_in_w': out['m_m_in_w'], 'm_m_conv_w': out['m_m_conv_w'], 'm_m_conv_b': out['m_m_conv_b'], 'm_m_dt_bias': out['m_m_dt_bias'], 'm_m_a_log': out['m_m_a_log'], 'm_m_d': out['m_m_d'], 'm_m_norm_w': out['m_m_norm_w'], 'm_m_out_w': out['m_m_out_w'], 'm_h_in_w': out['m_h_in_w'], 'm_h_lower_bounds': out['m_h_lower_bounds'], 'm_h_norm_w': out['m_h_norm_w'], 'm_h_out_w': out['m_h_out_w'], 'm_g_in_w': out['m_g_in_w'], 'm_g_conv_w': out['m_g_conv_w'], 'm_g_a_log': out['m_g_a_log'], 'm_g_dt_bias': out['m_g_dt_bias'], 'm_g_norm_w': out['m_g_norm_w'], 'm_g_out_w': out['m_g_out_w'], 'm_xa_q': out['m_xa_q'], 'm_xa_kv': out['m_xa_kv'], 'm_xa_o': out['m_xa_o'], 'm_f_up': out['m_f_up'], 'm_f_conv_w': out['m_f_conv_w'], 'm_f_conv_b': out['m_f_conv_b'], 'm_f_down': out['m_f_down'], 'v_ln_mix': out['v_ln_mix'], 'v_ln_xattn': out['v_ln_xattn'], 'v_ln_mem': out['v_ln_mem'], 'v_ln_ffn': out['v_ln_ffn'], 'v_final_norm': out['v_final_norm'], 'v_m_in_w': out['v_m_in_w'], 'v_m_conv_w': out['v_m_conv_w'], 'v_m_conv_b': out['v_m_conv_b'], 'v_m_dt_bias': out['v_m_dt_bias'], 'v_m_a_log': out['v_m_a_log'], 'v_m_d': out['v_m_d'], 'v_m_norm_w': out['v_m_norm_w'], 'v_m_out_w': out['v_m_out_w'], 'v_h_in_w': out['v_h_in_w'], 'v_h_lower_bounds': out['v_h_lower_bounds'], 'v_h_norm_w': out['v_h_norm_w'], 'v_h_out_w': out['v_h_out_w'], 'v_g_in_w': out['v_g_in_w'], 'v_g_conv_w': out['v_g_conv_w'], 'v_g_a_log': out['v_g_a_log'], 'v_g_dt_bias': out['v_g_dt_bias'], 'v_g_norm_w': out['v_g_norm_w'], 'v_g_out_w': out['v_g_out_w'], 'v_xa_q': out['v_xa_q'], 'v_xa_kv': out['v_xa_kv'], 'v_xa_o': out['v_xa_o'], 'v_f_up': out['v_f_up'], 'v_f_conv_w': out['v_f_conv_w'], 'v_f_conv_b': out['v_f_conv_b'], 'v_f_down': out['v_f_down']}


def _loss(weights, diff, rest, loss_target):
    with _jax.named_scope("forward"):
        args = {**rest, TWIN_DIFF_INPUT: diff, **{k: w.astype(_WEIGHT_DTYPES[k]) for k, w in weights.items()}}
        y = _forward(args)
    with _jax.named_scope("loss_head"):
        err = _jnp.square(y.astype(_jnp.float32) - loss_target)
        return 0.5 * _jnp.sum(_jnp.mean(err, axis=-1)) if err.ndim else 0.5 * err


def _adamw(w, g, m, v):
    m = ADAM_B1 * m + (1.0 - ADAM_B1) * g
    v = ADAM_B2 * v + (1.0 - ADAM_B2) * _jnp.square(g)
    m_hat = m / (1.0 - ADAM_B1 ** ADAM_STEP)
    v_hat = v / (1.0 - ADAM_B2 ** ADAM_STEP)
    delta = -ADAM_LR * (m_hat / (_jnp.sqrt(v_hat) + ADAM_EPS) + ADAM_WD * w)
    return delta, m, v


def reference(x, mem, ln_mix, ln_xattn, ln_mem, ln_ffn, final_norm, m_in_w, m_conv_w, m_conv_b, m_dt_bias, m_a_log, m_d, m_norm_w, m_out_w, h_in_w, h_lower_bounds, h_norm_w, h_out_w, g_in_w, g_conv_w, g_a_log, g_dt_bias, g_norm_w, g_out_w, xa_q, xa_kv, xa_o, f_up, f_conv_w, f_conv_b, f_down, loss_target, m_ln_mix, m_ln_xattn, m_ln_mem, m_ln_ffn, m_final_norm, m_m_in_w, m_m_conv_w, m_m_conv_b, m_m_dt_bias, m_m_a_log, m_m_d, m_m_norm_w, m_m_out_w, m_h_in_w, m_h_lower_bounds, m_h_norm_w, m_h_out_w, m_g_in_w, m_g_conv_w, m_g_a_log, m_g_dt_bias, m_g_norm_w, m_g_out_w, m_xa_q, m_xa_kv, m_xa_o, m_f_up, m_f_conv_w, m_f_conv_b, m_f_down, v_ln_mix, v_ln_xattn, v_ln_mem, v_ln_ffn, v_final_norm, v_m_in_w, v_m_conv_w, v_m_conv_b, v_m_dt_bias, v_m_a_log, v_m_d, v_m_norm_w, v_m_out_w, v_h_in_w, v_h_lower_bounds, v_h_norm_w, v_h_out_w, v_g_in_w, v_g_conv_w, v_g_a_log, v_g_dt_bias, v_g_norm_w, v_g_out_w, v_xa_q, v_xa_kv, v_xa_o, v_f_up, v_f_conv_w, v_f_conv_b, v_f_down):
    given = dict(x=x, mem=mem, ln_mix=ln_mix, ln_xattn=ln_xattn, ln_mem=ln_mem, ln_ffn=ln_ffn, final_norm=final_norm, m_in_w=m_in_w, m_conv_w=m_conv_w, m_conv_b=m_conv_b, m_dt_bias=m_dt_bias, m_a_log=m_a_log, m_d=m_d, m_norm_w=m_norm_w, m_out_w=m_out_w, h_in_w=h_in_w, h_lower_bounds=h_lower_bounds, h_norm_w=h_norm_w, h_out_w=h_out_w, g_in_w=g_in_w, g_conv_w=g_conv_w, g_a_log=g_a_log, g_dt_bias=g_dt_bias, g_norm_w=g_norm_w, g_out_w=g_out_w, xa_q=xa_q, xa_kv=xa_kv, xa_o=xa_o, f_up=f_up, f_conv_w=f_conv_w, f_conv_b=f_conv_b, f_down=f_down, loss_target=loss_target, m_ln_mix=m_ln_mix, m_ln_xattn=m_ln_xattn, m_ln_mem=m_ln_mem, m_ln_ffn=m_ln_ffn, m_final_norm=m_final_norm, m_m_in_w=m_m_in_w, m_m_conv_w=m_m_conv_w, m_m_conv_b=m_m_conv_b, m_m_dt_bias=m_m_dt_bias, m_m_a_log=m_m_a_log, m_m_d=m_m_d, m_m_norm_w=m_m_norm_w, m_m_out_w=m_m_out_w, m_h_in_w=m_h_in_w, m_h_lower_bounds=m_h_lower_bounds, m_h_norm_w=m_h_norm_w, m_h_out_w=m_h_out_w, m_g_in_w=m_g_in_w, m_g_conv_w=m_g_conv_w, m_g_a_log=m_g_a_log, m_g_dt_bias=m_g_dt_bias, m_g_norm_w=m_g_norm_w, m_g_out_w=m_g_out_w, m_xa_q=m_xa_q, m_xa_kv=m_xa_kv, m_xa_o=m_xa_o, m_f_up=m_f_up, m_f_conv_w=m_f_conv_w, m_f_conv_b=m_f_conv_b, m_f_down=m_f_down, v_ln_mix=v_ln_mix, v_ln_xattn=v_ln_xattn, v_ln_mem=v_ln_mem, v_ln_ffn=v_ln_ffn, v_final_norm=v_final_norm, v_m_in_w=v_m_in_w, v_m_conv_w=v_m_conv_w, v_m_conv_b=v_m_conv_b, v_m_dt_bias=v_m_dt_bias, v_m_a_log=v_m_a_log, v_m_d=v_m_d, v_m_norm_w=v_m_norm_w, v_m_out_w=v_m_out_w, v_h_in_w=v_h_in_w, v_h_lower_bounds=v_h_lower_bounds, v_h_norm_w=v_h_norm_w, v_h_out_w=v_h_out_w, v_g_in_w=v_g_in_w, v_g_conv_w=v_g_conv_w, v_g_a_log=v_g_a_log, v_g_dt_bias=v_g_dt_bias, v_g_norm_w=v_g_norm_w, v_g_out_w=v_g_out_w, v_xa_q=v_xa_q, v_xa_kv=v_xa_kv, v_xa_o=v_xa_o, v_f_up=v_f_up, v_f_conv_w=v_f_conv_w, v_f_conv_b=v_f_conv_b, v_f_down=v_f_down)
    weights = {n: given[n] for n in TWIN_WEIGHTS}
    shared = {n: given[n] for n in SHARED_INPUTS}
    per_example = {n: given[n] for n in ['x', 'mem']}
    grad_fn = _jax.value_and_grad(_loss, argnums=(0, 1))

    def one_microbatch(ex, loss_target):
        ex = dict(ex)
        diff = ex.pop(TWIN_DIFF_INPUT)
        return grad_fn(weights, diff, {**shared, **ex}, loss_target)

    if N_MICROBATCH == 1:
        loss, (grad_w, grad_x) = one_microbatch(per_example, given["loss_target"])
    else:
        def body(carry, xs):
            loss_sum, grad_sum = carry
            l_k, (gw_k, gx_k) = one_microbatch(xs[0], xs[1])
            with _jax.named_scope("update"):
                return (loss_sum + l_k, _jax.tree.map(_jnp.add, grad_sum, gw_k)), gx_k

        init = (_jnp.zeros((), _jnp.float32), _jax.tree.map(_jnp.zeros_like, weights))
        (loss, grad_w), grad_x = _jax.lax.scan(body, init, (per_example, given["loss_target"]))
    with _jax.named_scope("update"):
        delta_w, new_m, new_v = {}, {}, {}
        for n in TWIN_WEIGHTS:
            delta_w[n], new_m[n], new_v[n] = _adamw(weights[n], grad_w[n], given["m_" + n], given["v_" + n])
    return (loss, grad_x, *[grad_w[n] for n in TWIN_WEIGHTS], *[delta_w[n] for n in TWIN_WEIGHTS],
            *[new_m[n] for n in TWIN_WEIGHTS], *[new_v[n] for n in TWIN_WEIGHTS])
```

```python
import functools

import jax
import jax.numpy as jnp
from jax import lax
from jax.experimental import pallas as pl
from jax.experimental.pallas import tpu as pltpu

f32 = jnp.float32
bf16 = jnp.bfloat16
HIGHEST = lax.Precision.HIGHEST
MESH = pl.DeviceIdType.MESH

D = 1024
DEPTH = 4
EPS = 1e-6
N_MEM = 256
M_INNER, M_P, M_H, M_G, M_N, M_Q = 2048, 64, 32, 8, 128, 64
M_CONV = M_INNER + 2 * M_G * M_N
M_MAIN = M_INNER + M_CONV
M_IN = M_MAIN + M_H
H_H, H_K, H_Q = 8, 128, 32
G_HV, G_HK, G_K, G_Q = 16, 8, 128, 64
G_CONV, G_VAL = 4096, 2048
G_MAIN = G_CONV + G_VAL
G_IN = G_MAIN + 2 * G_HV
X_H, X_D = 4, 256
D_FF = 2816
ADAM_LR, ADAM_B1, ADAM_B2, ADAM_EPS, ADAM_WD, ADAM_STEP = 0.001, 0.9, 0.999, 1e-08, 0.01, 10
VMEM_LIMIT = 56 * 1024 * 1024
NCHIP = 4


def _cp(**kw):
    return pltpu.CompilerParams(vmem_limit_bytes=VMEM_LIMIT, **kw)


def _S(shape, dtype):
    return jax.ShapeDtypeStruct(tuple(shape), dtype)


def _dg(a, b, ca, cb, prec=None):
    return lax.dot_general(a, b, (((ca,), (cb,)), ((), ())), precision=prec, preferred_element_type=f32)


def _hdot(a, b, ca=1, cb=0):
    return _dg(a.astype(f32), b.astype(f32), ca, cb, HIGHEST)


def _bdot_raw(a, b, ca, cb):
    return _dg(a.astype(bf16), b.astype(bf16), ca, cb)


@functools.partial(jax.custom_vjp, nondiff_argnums=(2, 3))
def _bdot(a, b, ca, cb):
    return _bdot_raw(a, b, ca, cb)


def _bdot_fwd(a, b, ca, cb):
    return _bdot_raw(a, b, ca, cb), (a, b)


def _bdot_bwd(ca, cb, res, g):
    a, b = res
    if ca == 1:
        da = _bdot_raw(g, b, 1, 1 if cb == 0 else 0)
    else:
        da = _bdot_raw(b, g, 1 if cb == 0 else 0, 1)
    if cb == 0:
        db = _bdot_raw(a, g, 0 if ca == 1 else 1, 0)
    else:
        db = _bdot_raw(g, a, 0, 0 if ca == 1 else 1)
    return da.astype(a.dtype), db.astype(b.dtype)


_bdot.defvjp(_bdot_fwd, _bdot_bwd)


def _shift_down_raw(x, k):
    r = lax.broadcasted_iota(jnp.int32, x.shape, 0)
    return jnp.where(r >= k, pltpu.roll(x, k, 0), 0.0)


def _shift_up_raw(x, k):
    n = x.shape[0]
    r = lax.broadcasted_iota(jnp.int32, x.shape, 0)
    return jnp.where(r < n - k, pltpu.roll(x, n - k, 0), 0.0)


@functools.partial(jax.custom_vjp, nondiff_argnums=(1,))
def _shift_down(x, k):
    return _shift_down_raw(x, k)


_shift_down.defvjp(lambda x, k: (_shift_down_raw(x, k), None), lambda k, _, g: (_shift_up_raw(g, k),))


def _rms(x, w):
    return x * lax.rsqrt(jnp.mean(x * x, axis=-1, keepdims=True) + EPS) * w


def _silu(x):
    return x * jax.nn.sigmoid(x)


def _masks(q):
    r = lax.broadcasted_iota(jnp.int32, (q, q), 0)
    c = lax.broadcasted_iota(jnp.int32, (q, q), 1)
    return r >= c, r > c


def _colvec(row):
    return jnp.transpose(jnp.broadcast_to(row, (8, row.shape[1])))[:, 0:1]


def _tile(n, cands):
    for c in cands:
        if n % c == 0:
            return c
    return n


def mm(a, b, *, ta=False, tb=False, res=None, out_dtype=f32, name):
    m, k = (a.shape[1], a.shape[0]) if ta else a.shape
    n = b.shape[0] if tb else b.shape[1]
    tm = _tile(m, (512, 256, 128))
    tn = _tile(n, (512, 256, 128))
    tk = k if (k <= 4096 and not ta) else _tile(k, (1024, 512, 256, 128))
    nk = k // tk
    ca, cb = (0 if ta else 1), (1 if tb else 0)

    def body(*refs):
        if res is None:
            a_ref, b_ref, o_ref, acc = refs
            r_ref = None
        else:
            a_ref, b_ref, r_ref, o_ref, acc = refs
        kk = pl.program_id(2)

        @pl.when(kk == 0)
        def _():
            acc[...] = jnp.zeros_like(acc)

        acc[...] += _bdot_raw(a_ref[...], b_ref[...], ca, cb)

        @pl.when(kk == nk - 1)
        def _():
            v = acc[...]
            if r_ref is not None:
                v = v + r_ref[...]
            o_ref[...] = v.astype(o_ref.dtype)

    a_spec = pl.BlockSpec((tk, tm), lambda i, j, kk: (kk, i)) if ta else pl.BlockSpec((tm, tk), lambda i, j, kk: (i, kk))
    b_spec = pl.BlockSpec((tn, tk), lambda i, j, kk: (j, kk)) if tb else pl.BlockSpec((tk, tn), lambda i, j, kk: (kk, j))
    in_specs = [a_spec, b_spec]
    args = [a, b]
    if res is not None:
        in_specs.append(pl.BlockSpec((tm, tn), lambda i, j, kk: (i, j)))
        args.append(res)
    return pl.pallas_call(
        body, name=name, grid=(m // tm, n // tn, nk), in_specs=in_specs,
        out_specs=pl.BlockSpec((tm, tn), lambda i, j, kk: (i, j)), out_shape=_S((m, n), out_dtype),
        scratch_shapes=[pltpu.VMEM((tm, tn), f32)], compiler_params=_cp())(*args)


def rows_call(name, fn, rows, pars, row_out, acc_out=(), tm=256):
    t = rows[0].shape[0]
    tm = min(tm, t)
    assert t % tm == 0, (name, t, tm)
    nr, npar, nro = len(rows), len(pars), len(row_out)

    def body(*refs):
        rv = [r[...] for r in refs[:nr]]
        pv = [r[...] for r in refs[nr:nr + npar]]
        ro_refs = refs[nr + npar:nr + npar + nro]
        ao_refs = refs[nr + npar + nro:]
        ro, ao = fn(*rv, *pv)
        for r, v in zip(ro_refs, ro, strict=True):
            r[...] = v.astype(r.dtype)
        if ao_refs:
            @pl.when(pl.program_id(0) == 0)
            def _():
                for r in ao_refs:
                    r[...] = jnp.zeros_like(r)
            for r, v in zip(ao_refs, ao, strict=True):
                r[...] += v.astype(r.dtype)

    in_specs = [pl.BlockSpec((tm, r.shape[1]), lambda i: (i, 0)) for r in rows]
    in_specs += [pl.BlockSpec(p.shape, lambda i: (0, 0)) for p in pars]
    out_specs = [pl.BlockSpec((tm, c), lambda i: (i, 0)) for c, _ in row_out]
    out_specs += [pl.BlockSpec(s, lambda i: (0, 0)) for s, _ in acc_out]
    out_shape = [_S((t, c), dt) for c, dt in row_out] + [_S(s, dt) for s, dt in acc_out]
    return pl.pallas_call(body, name=name, grid=(t // tm,), in_specs=in_specs, out_specs=out_specs,
                          out_shape=out_shape, compiler_params=_cp())(*rows, *pars)


def rms_fwd(x, w, name):
    return rows_call(name, lambda xv, wv: ((_rms(xv, wv),), ()), [x], [w], [(x.shape[1], bf16)])[0]


def rms_bwd(x, w, dy, dres, name):
    def fn(*a):
        if dres is None:
            xv, dyv, wv = a
        else:
            xv, dyv, drv, wv = a
        _, vjp = jax.vjp(_rms, xv, wv)
        dx, dw = vjp(dyv.astype(f32))
        if dres is not None:
            dx = dx + drv
        return (dx,), (dw,)
    rows = [x, dy] + ([] if dres is None else [dres])
    return rows_call(name, fn, rows, [w], [(x.shape[1], f32)], [(w.shape, f32)])


def cols_call(name, fn, seqs, pars, outs, *, nb, ct, ncol, dseed=None):
    ns, npar = len(seqs), len(pars)
    seq_len = seqs[0].shape[0] // nb
    nd = 0 if dseed is None else len(dseed)

    def body(*refs):
        sv = [r[...] for r in refs[:ns]]
        pv = [r[...] for r in refs[ns:ns + npar]]
        if dseed is None:
            o_refs = refs[ns + npar:]
            for r, v in zip(o_refs, fn(*sv, *pv), strict=True):
                r[...] = v.astype(r.dtype)
            return
        dv = [r[...].astype(f32) for r in refs[ns + npar:ns + npar + nd]]
        ds_refs = refs[ns + npar + nd:ns + npar + nd + ns]
        dp_refs = refs[ns + npar + nd + ns:]
        _, vjp = jax.vjp(fn, *[v.astype(f32) for v in sv], *pv)
        g = vjp(tuple(dv))
        for r, v in zip(ds_refs, g[:ns], strict=True):
            r[...] = v.astype(r.dtype)

        @pl.when(pl.program_id(1) == 0)
        def _():
            for r in dp_refs:
                r[...] = jnp.zeros_like(r)
        for r, v in zip(dp_refs, g[ns:], strict=True):
            r[...] += v

    full = pl.BlockSpec((seq_len, ct), lambda j, b: (b, j))
    in_specs = [full for _ in seqs]
    in_specs += [pl.BlockSpec((p.shape[0], ct), lambda j, b: (0, j)) for p in pars]
    args = list(seqs) + list(pars)
    if dseed is None:
        out_specs = [full for _ in outs]
        out_shape = [_S((nb * seq_len, ncol * ct), dt) for dt in outs]
    else:
        in_specs += [full for _ in dseed]
        args += list(dseed)
        out_specs = [full for _ in seqs] + [pl.BlockSpec((p.shape[0], ct), lambda j, b: (0, j)) for p in pars]
        out_shape = [_S((nb * seq_len, ncol * ct), f32) for _ in seqs] + [_S(p.shape, f32) for p in pars]
    return pl.pallas_call(body, name=name, grid=(ncol, nb), in_specs=in_specs, out_specs=out_specs,
                          out_shape=out_shape, compiler_params=_cp())(*args)


def _conv4_silu(x, w, b):
    y = x * w[3:4] + _shift_down(x, 1) * w[2:3] + _shift_down(x, 2) * w[1:2] + _shift_down(x, 3) * w[0:1] + b
    return (_silu(y),)


def _conv4_silu_nobias(x, w):
    y = x * w[3:4] + _shift_down(x, 1) * w[2:3] + _shift_down(x, 2) * w[1:2] + _shift_down(x, 3) * w[0:1]
    return (_silu(y),)


def _ffn_act(gate, up, w, b):
    y = gate * w[2:3] + _shift_down(gate, 1) * w[1:2] + _shift_down(gate, 2) * w[0:1] + b
    return (_silu(y) * up,)


def scan_call(name, chunk_fn, seqs, pars, consts, outs, *, nb, nh, q, state_shape, states=None, dseed=None):
    t = seqs[0][0].shape[0]
    nc = t // (nb * q)
    ns, npar, ncon, no = len(seqs), len(pars), len(consts), len(outs)
    s0, s1 = state_shape
    bwd = dseed is not None

    def cidx(c):
        return (nc - 1 - c) if bwd else c

    def rowblk(b, c):
        return b * nc + cidx(c)

    def seq_spec(w, colfn):
        return pl.BlockSpec((q, w), lambda b, c, h: (rowblk(b, c), colfn(h)))

    def par_spec(shape, idxfn):
        return pl.BlockSpec(shape, lambda b, c, h: idxfn(h))

    st_spec = pl.BlockSpec((s0, s1), lambda b, c, h: ((rowblk(b, c)) * nh + h, 0))
    in_specs = [seq_spec(w, cf) for _, w, cf, _ in seqs]
    in_specs += [par_spec(s, f) for _, s, f in pars] + [par_spec(s, f) for _, s, f in consts]
    args = [a for a, _, _, _ in seqs] + [a for a, _, _ in pars] + [a for a, _, _ in consts]

    if not bwd:
        def body(*refs):
            sv = [r[...] for r in refs[:ns]]
            pv = [r[...] for r in refs[ns:ns + npar]]
            cv = [r[...] for r in refs[ns + npar:ns + npar + ncon]]
            o_refs = refs[ns + npar + ncon:ns + npar + ncon + no]
            save_ref = refs[ns + npar + ncon + no]
            st = refs[-1]
            c, h = pl.program_id(1), pl.program_id(2)

            @pl.when(c == 0)
            def _():
                st[h] = jnp.zeros((s0, s1), f32)
            s_in = st[h]
            save_ref[...] = s_in
            o, s_out = chunk_fn(*sv, *pv, s_in, *cv)
            st[h] = s_out
            for r, v in zip(o_refs, o, strict=True):
                r[...] = v.astype(r.dtype)

        out_specs = [seq_spec(w, cf) for _, w, cf, _ in outs] + [st_spec]
        out_shape = [_S((t, cc), dt) for cc, _, _, dt in outs] + [_S((nb * nc * nh * s0, s1), f32)]
        return pl.pallas_call(body, name=name, grid=(nb, nc, nh), in_specs=in_specs, out_specs=out_specs,
                              out_shape=out_shape, scratch_shapes=[pltpu.VMEM((nh, s0, s1), f32)],
                              compiler_params=_cp())(*args)

    def body(*refs):
        i = 0
        sv = [r[...] for r in refs[i:i + ns]]; i += ns
        pv = [r[...] for r in refs[i:i + npar]]; i += npar
        cv = [r[...] for r in refs[i:i + ncon]]; i += ncon
        dv = [r[...].astype(f32) for r in refs[i:i + no]]; i += no
        s_in = refs[i][...]; i += 1
        ds_refs = refs[i:i + ns]; i += ns
        dp_refs = refs[i:i + npar]; i += npar
        dst = refs[-1]
        b, c, h = pl.program_id(0), pl.program_id(1), pl.program_id(2)

        @pl.when(c == 0)
        def _():
            dst[h] = jnp.zeros((s0, s1), f32)

        @pl.when((b == 0) & (c == 0) & (h == 0))
        def _():
            for r in dp_refs:
                r[...] = jnp.zeros_like(r)

        fn = lambda *a: chunk_fn(*a, *cv)
        _, vjp = jax.vjp(fn, *[v.astype(f32) for v in sv], *pv, s_in)
        g = vjp((tuple(dv), dst[h]))
        dst[h] = g[ns + npar]
        for (_, _, _, rep), r, v in zip(seqs, ds_refs, g[:ns], strict=True):
            if rep == 1:
                r[...] = v.astype(r.dtype)
            else:
                @pl.when(h % rep == 0)
                def _(r=r, v=v):
                    r[...] = v.astype(r.dtype)

                @pl.when(h % rep != 0)
                def _(r=r, v=v):
                    r[...] += v.astype(r.dtype)
        for r, v in zip(dp_refs, g[ns:ns + npar], strict=True):
            r[h] += v

    in_specs += [seq_spec(w, cf) for _, w, cf, _ in outs] + [st_spec]
    args += list(dseed) + [states]
    out_specs = [seq_spec(w, cf) for _, w, cf, _ in seqs]
    out_specs += [pl.BlockSpec((nh,) + tuple(s), lambda b, c, h: (0, 0, 0)) for _, s, _ in pars]
    out_shape = [_S(a.shape, f32) for a, _, _, _ in seqs] + [_S((nh,) + tuple(s), f32) for _, s, _ in pars]
    return pl.pallas_call(body, name=name, grid=(nb, nc, nh), in_specs=in_specs, out_specs=out_specs,
                          out_shape=out_shape, scratch_shapes=[pltpu.VMEM((nh, s0, s1), f32)],
                          compiler_params=_cp())(*args)


def _ssd_chunk(xs, bm, cm, z, dtr, dtb, alog, dsk, nw, st, e):
    q = xs.shape[0]
    incl, _ = _masks(q)
    dt = jax.nn.softplus(dtr + dtb[0:1])
    dte = _hdot(dt, e)
    ae = _hdot(-jnp.exp(alog), e)[0:1]
    de = _hdot(dsk, e)[0:1]
    xc = xs * dte
    acum = _hdot(incl.astype(f32), dte * ae)
    last = acum[q - 1:q]
    cb = _bdot(cm, bm, 1, 1)
    eac, eend, elast = jnp.exp(acum), jnp.exp(last - acum), jnp.exp(last)
    ys, new = [], []
    for r in range(4):
        sl = slice(r * M_P, (r + 1) * M_P)
        a_r = acum[:, sl]
        diff = jnp.where(incl, a_r[:, 0:1] - jnp.transpose(a_r)[0:1, :], 0.0)
        dec = jnp.where(incl, jnp.exp(diff), 0.0)
        xcr, s_r = xc[:, sl], st[sl, :]
        yd = _bdot(cb * dec, xcr, 1, 0)
        yo = _bdot(cm, s_r, 1, 1) * eac[:, sl]
        ys.append(yd + yo)
        new.append(s_r * elast[:, r * M_P:r * M_P + 1] + _bdot(xcr * eend[:, sl], bm, 0, 0))
    y = jnp.concatenate(ys, axis=1) + de * xs
    y = y * _silu(z)
    return (_rms(y, nw),), jnp.concatenate(new, axis=0)


def _gla_chunk(qr, fr, ir, gr, lb, nw, st):
    q = qr.shape[0]
    incl, _ = _masks(q)
    fg = lb + (1.0 - lb) * jax.nn.sigmoid(fr)
    qq = _silu(qr) * (H_K ** -0.5)
    k = 1.0 - fg
    gc = _hdot(incl.astype(f32), jnp.log(fg))
    gl = gc[q - 1:q]
    qd, ki, ke = qq * jnp.exp(gc), k * jnp.exp(-gc), k * jnp.exp(gl - gc)
    att = jnp.where(incl, _bdot(qd, ki, 1, 1), 0.0)
    o = _bdot(att, ir, 1, 0) + _bdot(qd, st, 1, 0)
    new = st * _colvec(jnp.exp(gl)) + _bdot(ke, ir, 0, 0)
    return (_rms(o, nw) * _silu(gr),), new


def _tri_inv(m):
    n = m.shape[0]
    r = lax.broadcasted_iota(jnp.int32, (n, n), 0)
    c = lax.broadcasted_iota(jnp.int32, (n, n), 1)
    t = (r == c).astype(f32) - m
    p = m
    steps = max(1, (n - 1).bit_length() - 1)
    for _ in range(steps):
        p = _hdot(p, p)
        t = t + _hdot(t, p)
    return t


def _gdn_chunk(qr, kr, v, z, ba, alog, dtb, nw, st, eb, ea):
    q = qr.shape[0]
    incl, strict = _masks(q)
    qn = qr * lax.rsqrt(jnp.sum(qr * qr, axis=-1, keepdims=True) + EPS) * (G_K ** -0.5)
    kn = kr * lax.rsqrt(jnp.sum(kr * kr, axis=-1, keepdims=True) + EPS)
    beta = _hdot(jax.nn.sigmoid(ba), eb)
    g = _hdot(-jnp.exp(alog[0:1]) * jax.nn.softplus(ba + dtb[0:1]), ea)
    gc = _hdot(incl.astype(f32), g)
    diff = jnp.where(incl, gc[:, 0:1] - jnp.transpose(gc)[0:1, :], 0.0)
    dec = jnp.where(incl, jnp.exp(diff), 0.0)
    kb = kn * beta
    tinv = _tri_inv(jnp.where(strict, _bdot(kb, kn, 1, 1) * dec, 0.0))
    egc = jnp.exp(gc)
    u = _hdot(tinv, v * beta)
    w = _hdot(tinv, kb * egc)
    att = _bdot(qn, kn, 1, 1) * dec
    gl = gc[q - 1:q]
    ke = kn * jnp.exp(gl - gc)
    v_new = u - _bdot(w, st, 1, 0)
    o = _bdot(qn * egc, st, 1, 0) + _bdot(att, v_new, 1, 0)
    new = st * jnp.exp(gl)[:, 0:1] + _bdot(ke, v_new, 0, 0)
    return (_rms(o, nw) * _silu(z),), new


def _xattn_fn(q, k, v):
    s = _bdot(q, k, 1, 1) * (X_D ** -0.5)
    return _bdot(jax.nn.softmax(s, axis=-1), v, 1, 0)


def xattn_fwd(q, k, v, nb, name, tl=512):
    t = q.shape[0]
    tl = min(tl, t // nb)
    nl = t // nb // tl

    def body(q_ref, k_ref, v_ref, o_ref):
        o_ref[...] = _xattn_fn(q_ref[...], k_ref[...], v_ref[...]).astype(o_ref.dtype)

    qs = pl.BlockSpec((tl, X_D), lambda b, i, h: (b * nl + i, h))
    ks = pl.BlockSpec((N_MEM, X_D), lambda b, i, h: (b, h))
    return pl.pallas_call(body, name=name, grid=(nb, nl, X_H), in_specs=[qs, ks, ks], out_specs=qs,
                          out_shape=_S(q.shape, bf16), compiler_params=_cp())(q, k, v)


def xattn_bwd(q, k, v, do, nb, name, tl=512):
    t = q.shape[0]
    tl = min(tl, t // nb)
    nl = t // nb // tl

    def body(q_ref, k_ref, v_ref, do_ref, dq_ref, dk_ref, dv_ref):
        _, vjp = jax.vjp(_xattn_fn, q_ref[...], k_ref[...], v_ref[...])
        dq, dk, dv = vjp(do_ref[...].astype(f32))
        dq_ref[...] = dq

        @pl.when(pl.program_id(2) == 0)
        def _():
            dk_ref[...] = jnp.zeros_like(dk_ref)
            dv_ref[...] = jnp.zeros_like(dv_ref)
        dk_ref[...] += dk
        dv_ref[...] += dv

    qs = pl.BlockSpec((tl, X_D), lambda b, h, i: (b * nl + i, h))
    ks = pl.BlockSpec((N_MEM, X_D), lambda b, h, i: (b, h))
    return pl.pallas_call(body, name=name, grid=(nb, X_H, nl), in_specs=[qs, ks, ks, qs], out_specs=[qs, ks, ks],
                          out_shape=[_S(q.shape, f32), _S(k.shape, f32), _S(v.shape, f32)],
                          compiler_params=_cp())(q, k, v, do)


def _lower_bounds(hlb):
    sm = jax.nn.softmax(hlb, axis=0)
    rows, run = [], None
    for r in range(hlb.shape[0]):
        run = sm[r:r + 1] if run is None else run + sm[r:r + 1]
        rows.append(run - sm[0:1])
    return jnp.concatenate(rows, axis=0)


def lower_bounds_fwd(hlb):
    return rows_call("lb_fwd", lambda v: ((_lower_bounds(v),), ()), [hlb], [], [(hlb.shape[1], f32)], tm=hlb.shape[0])[0]


def lower_bounds_bwd(hlb, dlb):
    def fn(v, d):
        _, vjp = jax.vjp(_lower_bounds, v)
        return (vjp(d)[0],), ()
    return rows_call("lb_bwd", fn, [hlb, dlb], [], [(hlb.shape[1], f32)], tm=hlb.shape[0])[0]


def loss_head(x, target, w):
    def fn(xv, tv, wv):
        def loss(xx, ww):
            err = _rms(xx, ww) - tv
            return 0.5 * jnp.sum(jnp.mean(err * err, axis=-1))
        val, (dx, dw) = jax.value_and_grad(loss, argnums=(0, 1))(xv, wv)
        return (dx,), (jnp.broadcast_to(val, (1, 128)), dw)
    dx, loss, dw = rows_call("loss_head", fn, [x, target], [w], [(x.shape[1], f32)], [((1, 128), f32), (w.shape, f32)])
    return dx, loss, dw


def _adamw_fn(w, g, m, v):
    m2 = ADAM_B1 * m + (1.0 - ADAM_B1) * g
    v2 = ADAM_B2 * v + (1.0 - ADAM_B2) * (g * g)
    m_hat = m2 / (1.0 - ADAM_B1 ** ADAM_STEP)
    v_hat = v2 / (1.0 - ADAM_B2 ** ADAM_STEP)
    delta = -ADAM_LR * (m_hat / (jnp.sqrt(v_hat) + ADAM_EPS) + ADAM_WD * w)
    return delta, m2, v2


def adamw(w, g, m, v, name, g2=None):
    shape = w.shape
    c = shape[-1]
    r = w.size // c
    to2 = lambda a: a.reshape(r, c)
    tm = r if r * c * 4 <= (1 << 20) else _tile(r, (256, 128, 64, 32, 16, 8))

    def fn(*a):
        if g2 is None:
            wv, gv, mv, vv = a
        else:
            wv, gv, g2v, mv, vv = a
            gv = gv + g2v
        return (gv,) + _adamw_fn(wv, gv, mv, vv), ()
    rows = [to2(w), to2(g)] + ([] if g2 is None else [to2(g2)]) + [to2(m), to2(v)]
    outs = rows_call(name, fn, rows, [], [(c, f32)] * 4, tm=tm)
    return tuple(o.reshape(shape) for o in outs)


def _expand(first_row, nheads, width):
    r = jnp.arange(128)[:, None]
    c = jnp.arange(nheads * width)[None, :]
    return (r == first_row + c // width).astype(f32)


def _pad_row(v, lane0=0):
    return jnp.zeros((8, 128), f32).at[0, lane0:lane0 + v.shape[0]].set(v.astype(f32))


def _pad_cols(w, n=128):
    return jnp.pad(w, ((0, 0), (0, n - w.shape[1])))


_COL = lambda h: h
_C00 = lambda h: (0, 0)
_CONV_CT = 256


def _conv(name, x, w, b, nb, dseed=None):
    fn = _conv4_silu if b is not None else _conv4_silu_nobias
    pars = [w] + ([] if b is None else [b])
    return cols_call(name, fn, [x], pars, [f32], nb=nb, ct=_CONV_CT, ncol=x.shape[1] // _CONV_CT,
                     dseed=None if dseed is None else [dseed])


def _ssd_scan(name, xs, bm, cm, z, dtr, p, nb, states=None, dseed=None):
    seqs = [(xs, 256, _COL, 1), (bm, 128, _COL, 1), (cm, 128, _COL, 1), (z, 256, _COL, 1), (dtr, 128, lambda h: 0, M_G)]
    pars = [(p["dtb"], (8, 128), _C00), (p["alog"], (8, 128), _C00), (p["dsk"], (8, 128), _C00),
            (p["nw"], (1, 256), lambda h: (0, h))]
    consts = [(_expand(0, M_H, M_P), (128, 256), lambda h: (0, h))]
    outs = [(M_INNER, 256, _COL, bf16)]
    return scan_call(name, _ssd_chunk, seqs, pars, consts, outs, nb=nb, nh=M_G, q=M_Q, state_shape=(4 * M_P, M_N),
                     states=states, dseed=dseed)


def _gla_scan(name, qr, fr, ir, gr, p, nb, states=None, dseed=None):
    seqs = [(a, 128, _COL, 1) for a in (qr, fr, ir, gr)]
    pars = [(p["lb"], (1, 128), lambda h: (0, h)), (p["nw"], (1, 128), _C00)]
    outs = [(D, 128, _COL, bf16)]
    return scan_call(name, _gla_chunk, seqs, pars, [], outs, nb=nb, nh=H_H, q=H_Q, state_shape=(H_K, H_K),
                     states=states, dseed=dseed)


def _gdn_scan(name, qc, kc, vc, z, ba, p, nb, states=None, dseed=None):
    half = lambda h: h // 2
    seqs = [(qc, 128, half, 2), (kc, 128, half, 2), (vc, 128, _COL, 1), (z, 128, _COL, 1), (ba, 128, lambda h: 0, G_HV)]
    pars = [(p["alog"], (8, 128), _C00), (p["dtb"], (8, 128), _C00), (p["nw"], (1, 128), _C00)]
    consts = [(_expand(0, G_HV, G_K), (128, 128), lambda h: (0, h)), (_expand(G_HV, G_HV, G_K), (128, 128), lambda h: (0, h))]
    outs = [(G_VAL, 128, _COL, bf16)]
    return scan_call(name, _gdn_chunk, seqs, pars, consts, outs, nb=nb, nh=G_HV, q=G_Q, state_shape=(G_K, G_K),
                     states=states, dseed=dseed)


def _proj_bwd(tag, hn, pieces):
    dhn, dws = None, []
    for i, (d, w) in enumerate(pieces):
        dws.append(mm(hn, d, ta=True, out_dtype=bf16, name=f"{tag}_dw{i}"))
        dhn = mm(d, w, tb=True, res=dhn, name=f"{tag}_dh{i}")
    return dhn, dws


def ssd_mixer_fwd(tag, hn, w, nb):
    z, xr, br, cr, dtr = (mm(hn, w[k], name=f"{tag}_in_{k}") for k in ("wz", "wx", "wb", "wc", "wdt"))
    xs = _conv(f"{tag}_convx", xr, w["cwx"], w["cbx"], nb)[0]
    bm = _conv(f"{tag}_convb", br, w["cwb"], w["cbb"], nb)[0]
    cm = _conv(f"{tag}_convc", cr, w["cwc"], w["cbc"], nb)[0]
    yn, states = _ssd_scan(f"{tag}_scan", xs, bm, cm, z, dtr, w, nb)
    return yn, (hn, z, xr, br, cr, dtr, xs, bm, cm, yn, states)


def ssd_mixer_bwd(tag, saved, dout, w, nb):
    hn, z, xr, br, cr, dtr, xs, bm, cm, yn, states = saved
    g = {"wout": mm(yn, dout, ta=True, out_dtype=bf16, name=f"{tag}_dwout")}
    dyn = mm(dout, w["wout"], tb=True, name=f"{tag}_dyn")
    dxs, dbm, dcm, dz, ddtr, ddtb, dalog, ddsk, dnw = _ssd_scan(f"{tag}_scanb", xs, bm, cm, z, dtr, w, nb, states, [dyn])
    dxr, g["cwx"], g["cbx"] = _conv(f"{tag}_convxb", xr, w["cwx"], w["cbx"], nb, dxs)
    dbr, g["cwb"], g["cbb"] = _conv(f"{tag}_convbb", br, w["cwb"], w["cbb"], nb, dbm)
    dcr, g["cwc"], g["cbc"] = _conv(f"{tag}_convcb", cr, w["cwc"], w["cbc"], nb, dcm)
    dhn, (g["wz"], g["wx"], g["wb"], g["wc"], g["wdt"]) = _proj_bwd(
        tag, hn, [(dz, w["wz"]), (dxr, w["wx"]), (dbr, w["wb"]), (dcr, w["wc"]), (ddtr, w["wdt"])])
    g["dtb"], g["alog"], g["dsk"] = (jnp.sum(a, axis=0)[0, :M_H] for a in (ddtb, dalog, ddsk))
    g["nw"] = dnw.reshape(M_INNER)
    return dhn, g


def gla_mixer_fwd(tag, hn, w, nb):
    qr, fr, ir, gr = (mm(hn, w[k], name=f"{tag}_in_{k}") for k in ("wq", "wf", "wi", "wg"))
    on, states = _gla_scan(f"{tag}_scan", qr, fr, ir, gr, w, nb)
    return on, (hn, qr, fr, ir, gr, on, states)


def gla_mixer_bwd(tag, saved, dout, w, nb):
    hn, qr, fr, ir, gr, on, states = saved
    g = {"wout": mm(on, dout, ta=True, out_dtype=bf16, name=f"{tag}_dwout")}
    don = mm(dout, w["wout"], tb=True, name=f"{tag}_don")
    dq, df, di, dg, dlb, dnw = _gla_scan(f"{tag}_scanb", qr, fr, ir, gr, w, nb, states, [don])
    dhn, (g["wq"], g["wf"], g["wi"], g["wg"]) = _proj_bwd(tag, hn, [(dq, w["wq"]), (df, w["wf"]), (di, w["wi"]), (dg, w["wg"])])
    g["lb"] = dlb.reshape(1, D)
    g["nw"] = jnp.sum(dnw, axis=0).reshape(H_K)
    return dhn, g


def gdn_mixer_fwd(tag, hn, w, nb):
    qr, kr, vr, z, ba = (mm(hn, w[k], name=f"{tag}_in_{k}") for k in ("wq", "wk", "wv", "wz", "wba"))
    qc = _conv(f"{tag}_convq", qr, w["cwq"], None, nb)[0]
    kc = _conv(f"{tag}_convk", kr, w["cwk"], None, nb)[0]
    vc = _conv(f"{tag}_convv", vr, w["cwv"], None, nb)[0]
    on, states = _gdn_scan(f"{tag}_scan", qc, kc, vc, z, ba, w, nb)
    return on, (hn, qr, kr, vr, z, ba, qc, kc, vc, on, states)


def gdn_mixer_bwd(tag, saved, dout, w, nb):
    hn, qr, kr, vr, z, ba, qc, kc, vc, on, states = saved
    g = {"wout": mm(on, dout, ta=True, out_dtype=bf16, name=f"{tag}_dwout")}
    don = mm(dout, w["wout"], tb=True, name=f"{tag}_don")
    dqc, dkc, dvc, dz, dba, dalog, ddtb, dnw = _gdn_scan(f"{tag}_scanb", qc, kc, vc, z, ba, w, nb, states, [don])
    dqr, g["cwq"] = _conv(f"{tag}_convqb", qr, w["cwq"], None, nb, dqc)
    dkr, g["cwk"] = _conv(f"{tag}_convkb", kr, w["cwk"], None, nb, dkc)
    dvr, g["cwv"] = _conv(f"{tag}_convvb", vr, w["cwv"], None, nb, dvc)
    dhn, (g["wq"], g["wk"], g["wv"], g["wz"], g["wba"]) = _proj_bwd(
        tag, hn, [(dqr, w["wq"]), (dkr, w["wk"]), (dvr, w["wv"]), (dz, w["wz"]), (dba, w["wba"])])
    g["alog"], g["dtb"] = (jnp.sum(a, axis=0)[0, G_HV:2 * G_HV] for a in (dalog, ddtb))
    g["nw"] = jnp.sum(dnw, axis=0).reshape(G_K)
    return dhn, g


_MIXERS = {0: (ssd_mixer_fwd, ssd_mixer_bwd), 1: (gla_mixer_fwd, gla_mixer_bwd), 2: (gdn_mixer_fwd, gdn_mixer_bwd)}


def layer_fwd(i, x, mem, w, nb):
    t = f"l{i}"
    hn = rms_fwd(x, w["ln_mix"], f"{t}_ln_mix")
    mix, s_mix = _MIXERS[i % 3][0](f"{t}_mix", hn, w["mix"], nb)
    x1 = mm(mix, w["mix"]["wout"], res=x, name=f"{t}_mix_out")
    hx = rms_fwd(x1, w["ln_xattn"], f"{t}_ln_xattn")
    mn = rms_fwd(mem, w["ln_mem"], f"{t}_ln_mem")
    q = mm(hx, w["xq"], name=f"{t}_xa_q")
    k = mm(mn, w["xk"], name=f"{t}_xa_k")
    v = mm(mn, w["xv"], name=f"{t}_xa_v")
    o = xattn_fwd(q, k, v, nb, f"{t}_xattn")
    x2 = mm(o, w["xo"], res=x1, name=f"{t}_xa_o")
    hf = rms_fwd(x2, w["ln_ffn"], f"{t}_ln_ffn")
    gate = mm(hf, w["fg"], name=f"{t}_ffn_gate")
    up = mm(hf, w["fu"], name=f"{t}_ffn_up")
    act = cols_call(f"{t}_ffn_act", _ffn_act, [gate, up], [w["fcw"], w["fcb"]], [bf16], nb=nb, ct=_CONV_CT,
                    ncol=D_FF // _CONV_CT)[0]
    x3 = mm(act, w["fd"], res=x2, name=f"{t}_ffn_down")
    return x3, (x, s_mix, x1, hx, mn, q, k, v, o, x2, hf, gate, up, act)


def layer_bwd(i, saved, dx, mem, w, nb):
    t = f"l{i}b"
    x, s_mix, x1, hx, mn, q, k, v, o, x2, hf, gate, up, act = saved
    g = {}
    g["fd"] = mm(act, dx, ta=True, out_dtype=bf16, name=f"{t}_dwd")
    dact = mm(dx, w["fd"], tb=True, name=f"{t}_dact")
    dgate, dup, g["fcw"], g["fcb"] = cols_call(f"{t}_ffn_act", _ffn_act, [gate, up], [w["fcw"], w["fcb"]], [bf16], nb=nb,
                                               ct=_CONV_CT, ncol=D_FF // _CONV_CT, dseed=[dact])
    dhf, (g["fg"], g["fu"]) = _proj_bwd(f"{t}_ffn", hf, [(dgate, w["fg"]), (dup, w["fu"])])
    dx, g["ln_ffn"] = rms_bwd(x2, w["ln_ffn"], dhf, dx, f"{t}_ln_ffn")
    g["xo"] = mm(o, dx, ta=True, out_dtype=bf16, name=f"{t}_dwo")
    do = mm(dx, w["xo"], tb=True, name=f"{t}_do")
    dq, dk, dv = xattn_bwd(q, k, v, do, nb, f"{t}_xattn")
    dhx, (g["xq"],) = _proj_bwd(f"{t}_xq", hx, [(dq, w["xq"])])
    dmn, (g["xk"], g["xv"]) = _proj_bwd(f"{t}_xkv", mn, [(dk, w["xk"]), (dv, w["xv"])])
    _, g["ln_mem"] = rms_bwd(mem, w["ln_mem"], dmn, None, f"{t}_ln_mem")
    dx, g["ln_xattn"] = rms_bwd(x1, w["ln_xattn"], dhx, dx, f"{t}_ln_xattn")
    dhn, g["mix"] = _MIXERS[i % 3][1](f"{t}_mix", s_mix, dx, w["mix"], nb)
    dx, g["ln_mix"] = rms_bwd(x, w["ln_mix"], dhn, dx, f"{t}_ln_mix")
    return dx, g


def local_step(x, mem, target, layers, final_norm, nb):
    saved = []
    for i, w in enumerate(layers):
        x, s = layer_fwd(i, x, mem, w, nb)
        saved.append(s)
    dx, loss, dfinal = loss_head(x, target, final_norm)
    grads = [None] * len(layers)
    for i in reversed(range(len(layers))):
        dx, grads[i] = layer_bwd(i, saved[i], dx, mem, layers[i], nb)
    return loss, dx, grads, dfinal


def prep_layers(W, lb):
    layers, ia, ib, ic = [], 0, 0, 0
    row = lambda a: a.reshape(1, -1)
    for i in range(DEPTH):
        kv, fup = W["xa_kv"][i], W["f_up"][i]
        layer = dict(ln_mix=W["ln_mix"][i:i + 1], ln_xattn=W["ln_xattn"][i:i + 1], ln_mem=W["ln_mem"][i:i + 1],
                     ln_ffn=W["ln_ffn"][i:i + 1], xq=W["xa_q"][i], xk=kv[:, :D], xv=kv[:, D:], xo=W["xa_o"][i],
                     fg=fup[:, :D_FF], fu=fup[:, D_FF:], fcw=W["f_conv_w"][i], fcb=W["f_conv_b"][i:i + 1], fd=W["f_down"][i])
        if i % 3 == 0:
            inw, cw, cb = W["m_in_w"][ia], W["m_conv_w"][ia], row(W["m_conv_b"][ia])
            a, b, c = M_INNER, M_INNER + M_G * M_N, M_CONV
            layer["mix"] = dict(
                wz=inw[:, :M_INNER], wx=inw[:, M_INNER:M_INNER + a], wb=inw[:, M_INNER + a:M_INNER + b],
                wc=inw[:, M_INNER + b:M_MAIN], wdt=_pad_cols(inw[:, M_MAIN:]),
                cwx=cw[:, :a], cwb=cw[:, a:b], cwc=cw[:, b:c], cbx=cb[:, :a], cbb=cb[:, a:b], cbc=cb[:, b:c],
                dtb=_pad_row(W["m_dt_bias"][ia]), alog=_pad_row(W["m_a_log"][ia]), dsk=_pad_row(W["m_d"][ia]),
                nw=row(W["m_norm_w"][ia]), wout=W["m_out_w"][ia])
            ia += 1
        elif i % 3 == 1:
            inw = W["h_in_w"][ib]
            layer["mix"] = dict(wq=inw[:, :D], wf=inw[:, D:2 * D], wi=inw[:, 2 * D:3 * D], wg=inw[:, 3 * D:],
                                lb=lb[i:i + 1], nw=row(W["h_norm_w"][ib]), wout=W["h_out_w"][ib])
            ib += 1
        else:
            inw, cw = W["g_in_w"][ic], W["g_conv_w"][ic]
            layer["mix"] = dict(
                wq=inw[:, :D], wk=inw[:, D:2 * D], wv=inw[:, 2 * D:G_CONV], wz=inw[:, G_CONV:G_MAIN], wba=_pad_cols(inw[:, G_MAIN:]),
                cwq=cw[:, :D], cwk=cw[:, D:2 * D], cwv=cw[:, 2 * D:],
                alog=_pad_row(W["g_a_log"][ic], G_HV), dtb=_pad_row(W["g_dt_bias"][ic], G_HV),
                nw=row(W["g_norm_w"][ic]), wout=W["g_out_w"][ic])
            ic += 1
        layers.append(layer)
    return layers


def assemble_grads(grads, dfinal, hlb):
    cat = lambda xs: jnp.concatenate(xs, axis=1)
    out = {k: jnp.concatenate([g[k] for g in grads], axis=0) for k in ("ln_mix", "ln_xattn", "ln_mem", "ln_ffn")}
    out["final_norm"] = dfinal.reshape(D)
    out["xa_q"] = jnp.stack([g["xq"] for g in grads])
    out["xa_kv"] = jnp.stack([cat([g["xk"], g["xv"]]) for g in grads])
    out["xa_o"] = jnp.stack([g["xo"] for g in grads])
    out["f_up"] = jnp.stack([cat([g["fg"], g["fu"]]) for g in grads])
    out["f_conv_w"] = jnp.stack([g["fcw"] for g in grads])
    out["f_conv_b"] = jnp.concatenate([g["fcb"] for g in grads], axis=0)
    out["f_down"] = jnp.stack([g["fd"] for g in grads])
    ms = [g["mix"] for i, g in enumerate(grads) if i % 3 == 0]
    out["m_in_w"] = jnp.stack([cat([m["wz"], m["wx"], m["wb"], m["wc"], m["wdt"][:, :M_H]]) for m in ms])
    out["m_conv_w"] = jnp.stack([cat([m["cwx"], m["cwb"], m["cwc"]]) for m in ms])
    out["m_conv_b"] = jnp.concatenate([cat([m["cbx"], m["cbb"], m["cbc"]]) for m in ms], axis=0)
    out["m_dt_bias"] = jnp.stack([m["dtb"] for m in ms])
    out["m_a_log"] = jnp.stack([m["alog"] for m in ms])
    out["m_d"] = jnp.stack([m["dsk"] for m in ms])
    out["m_norm_w"] = jnp.stack([m["nw"] for m in ms])
    out["m_out_w"] = jnp.stack([m["wout"] for m in ms])
    hs = [(i, g["mix"]) for i, g in enumerate(grads) if i % 3 == 1]
    out["h_in_w"] = jnp.stack([cat([m["wq"], m["wf"], m["wi"], m["wg"]]) for _, m in hs])
    dlb = jnp.zeros((DEPTH, D), f32)
    for i, m in hs:
        dlb = dlb.at[i:i + 1].set(m["lb"])
    out["h_lower_bounds"] = lower_bounds_bwd(hlb, dlb)
    out["h_norm_w"] = jnp.stack([m["nw"] for _, m in hs])
    out["h_out_w"] = jnp.stack([m["wout"] for _, m in hs])
    gs = [g["mix"] for i, g in enumerate(grads) if i % 3 == 2]
    out["g_in_w"] = jnp.stack([cat([m["wq"], m["wk"], m["wv"], m["wz"], m["wba"][:, :2 * G_HV]]) for m in gs])
    out["g_conv_w"] = jnp.stack([cat([m["cwq"], m["cwk"], m["cwv"]]) for m in gs])
    out["g_a_log"] = jnp.stack([m["alog"] for m in gs])
    out["g_dt_bias"] = jnp.stack([m["dtb"] for m in gs])
    out["g_norm_w"] = jnp.stack([m["nw"] for m in gs])
    out["g_out_w"] = jnp.stack([m["wout"] for m in gs])
    return out


_HBM = pl.BlockSpec(memory_space=pltpu.HBM)


def _place():
    x, y, c = lax.axis_index("x"), lax.axis_index("y"), lax.axis_index("c")
    chips = [(1 - x, y), (x, 1 - y), (1 - x, 1 - y)]
    return x, y, c, chips


def gather_shards(tensors):
    n = len(tensors)

    def body(*refs):
        ins, outs = refs[:n], refs[n:2 * n]
        send_sems, recv_sems, loc_sems = refs[2 * n:]
        x, y, c, chips = _place()
        me = 2 * x + y
        local_copies, sends = [], []
        for t in range(n):
            loc = pltpu.make_async_copy(ins[t], outs[t].at[me], loc_sems.at[t])
            loc.start()
            local_copies.append(loc)
            for j, (px, py) in enumerate(chips):
                cp = pltpu.make_async_remote_copy(src_ref=ins[t], dst_ref=outs[t].at[me], send_sem=send_sems.at[3 * t + j],
                                                  recv_sem=recv_sems.at[3 * t + j], device_id=(px, py, c), device_id_type=MESH)
                cp.start()
                sends.append(cp)
        for t in range(n):
            for j, (px, py) in enumerate(chips):
                pltpu.make_async_remote_copy(src_ref=ins[t], dst_ref=outs[t].at[2 * px + py], send_sem=send_sems.at[3 * t + j],
                                             recv_sem=recv_sems.at[3 * t + j], device_id=(px, py, c),
                                             device_id_type=MESH).wait_recv()
        for cp in sends:
            cp.wait_send()
        for cp in local_copies:
            cp.wait()

    return pl.pallas_call(
        body, name="gather_shards", in_specs=[_HBM] * n, out_specs=[_HBM] * n,
        out_shape=[_S((NCHIP,) + a.shape, a.dtype) for a in tensors],
        scratch_shapes=[pltpu.SemaphoreType.DMA((3 * n,)), pltpu.SemaphoreType.DMA((3 * n,)), pltpu.SemaphoreType.DMA((n,))])(*tensors)


def scatter_parts(parts):
    n = len(parts)

    def body(*refs):
        ins, outs = refs[:n], refs[n:2 * n]
        send_sems, recv_sems = refs[2 * n:]
        x, y, c, chips = _place()
        started = []
        for t in range(n):
            for j, (px, py) in enumerate(chips):
                cp = pltpu.make_async_remote_copy(src_ref=ins[t].at[2 * px + py], dst_ref=outs[t].at[j],
                                                  send_sem=send_sems.at[3 * t + j], recv_sem=recv_sems.at[3 * t + j],
                                                  device_id=(px, py, c), device_id_type=MESH)
                cp.start()
                started.append(cp)
        for cp in started:
            cp.wait_recv()
        for cp in started:
            cp.wait_send()

    return pl.pallas_call(
        body, name="scatter_parts", in_specs=[_HBM] * n, out_specs=[_HBM] * n,
        out_shape=[_S((3,) + a.shape[1:], a.dtype) for a in parts],
        scratch_shapes=[pltpu.SemaphoreType.DMA((3 * n,)), pltpu.SemaphoreType.DMA((3 * n,))])(*parts)


def sibling_swap(tensors):
    n = len(tensors)

    def body(*refs):
        ins, outs = refs[:n], refs[n:2 * n]
        send_sems, recv_sems = refs[2 * n:]
        x, y, c, _ = _place()
        started = []
        for t in range(n):
            cp = pltpu.make_async_remote_copy(src_ref=ins[t], dst_ref=outs[t], send_sem=send_sems.at[t], recv_sem=recv_sems.at[t],
                                              device_id=(x, y, 1 - c), device_id_type=MESH)
            cp.start()
            started.append(cp)
        for cp in started:
            cp.wait_recv()
        for cp in started:
            cp.wait_send()

    return pl.pallas_call(
        body, name="sibling_swap", in_specs=[_HBM] * n, out_specs=[_HBM] * n,
        out_shape=[_S(a.shape, a.dtype) for a in tensors],
        scratch_shapes=[pltpu.SemaphoreType.DMA((n,)), pltpu.SemaphoreType.DMA((n,))])(*tensors)


def allreduce_small(v):
    r, n = v.shape

    def body(x_ref, out_ref, gat, send_sems, recv_sems, local_sem):
        x, y, c, chips = _place()
        me, sibling = (x, y, c), (x, y, 1 - c)

        def rows(px, py, pc):
            return gat.at[pl.ds((4 * px + 2 * py + pc) * r, r), :]

        def copy(k, block, to, src=None):
            return pltpu.make_async_remote_copy(src_ref=rows(*block) if src is None else src, dst_ref=rows(*block),
                                                send_sem=send_sems.at[k], recv_sem=recv_sems.at[k], device_id=to,
                                                device_id_type=MESH)

        mine = pltpu.make_async_copy(x_ref, rows(*me), local_sem)
        mine.start()
        first = [copy(0, me, sibling, src=x_ref)] + [copy(1 + j, me, (*chip, c), src=x_ref) for j, chip in enumerate(chips)]
        for cp in first:
            cp.start()
        passed = [copy(4 + j, (*chip, c), sibling) for j, chip in enumerate(chips)]
        for j, chip in enumerate(chips):
            copy(1 + j, (*chip, c), me).wait_recv()
            passed[j].start()
        copy(0, sibling, me).wait_recv()
        for j, chip in enumerate(chips):
            copy(4 + j, (*chip, 1 - c), me).wait_recv()
        for cp in first + passed:
            cp.wait_send()
        mine.wait()
        acc = gat[0:r, :]
        for d in range(1, 8):
            acc = acc + gat[d * r:(d + 1) * r, :]
        out_ref[...] = acc

    vm = pl.BlockSpec(memory_space=pltpu.VMEM)
    return pl.pallas_call(
        body, name="allreduce_small", in_specs=[vm], out_specs=vm, out_shape=_S((r, n), v.dtype),
        scratch_shapes=[pltpu.VMEM((8 * r, n), v.dtype), pltpu.SemaphoreType.DMA((7,)), pltpu.SemaphoreType.DMA((7,)),
                        pltpu.SemaphoreType.DMA],
        compiler_params=_cp())(v)


WEIGHTS = ["ln_mix", "ln_xattn", "ln_mem", "ln_ffn", "final_norm", "m_in_w", "m_conv_w", "m_conv_b", "m_dt_bias", "m_a_log",
           "m_d", "m_norm_w", "m_out_w", "h_in_w", "h_lower_bounds", "h_norm_w", "h_out_w", "g_in_w", "g_conv_w", "g_a_log",
           "g_dt_bias", "g_norm_w", "g_out_w", "xa_q", "xa_kv", "xa_o", "f_up", "f_conv_w", "f_conv_b", "f_down"]
SHARD_AXIS = {"m_in_w": 2, "m_conv_w": 2, "m_conv_b": 1, "m_norm_w": 1, "m_out_w": 1, "h_in_w": 2, "h_out_w": 1, "g_in_w": 2,
              "g_conv_w": 2, "g_out_w": 1, "xa_q": 1, "xa_kv": 2, "xa_o": 1, "f_up": 2, "f_conv_w": 2, "f_down": 1}
MATRICES = ["m_in_w", "m_out_w", "h_in_w", "h_out_w", "g_in_w", "g_out_w", "xa_q", "xa_kv", "xa_o", "f_up", "f_down"]
SMALL_SHARDED = [n for n in WEIGHTS if n in SHARD_AXIS and n not in MATRICES]
REPLICATED = [n for n in WEIGHTS if n not in SHARD_AXIS]
SMALL_ROW = 1024


def _unstack(g, axis):
    g = jnp.moveaxis(g, 0, axis)
    return g.reshape(g.shape[:axis] + (g.shape[axis] * g.shape[axis + 1],) + g.shape[axis + 2:])


def _stack(a, axis):
    a = a.reshape(a.shape[:axis] + (NCHIP, a.shape[axis] // NCHIP) + a.shape[axis + 1:])
    return jnp.moveaxis(a, axis, 0)


def kernel(x, mem, ln_mix, ln_xattn, ln_mem, ln_ffn, final_norm, m_in_w, m_conv_w, m_conv_b, m_dt_bias, m_a_log, m_d, m_norm_w, m_out_w, h_in_w, h_lower_bounds, h_norm_w, h_out_w, g_in_w, g_conv_w, g_a_log, g_dt_bias, g_norm_w, g_out_w, xa_q, xa_kv, xa_o, f_up, f_conv_w, f_conv_b, f_down, loss_target, m_ln_mix, m_ln_xattn, m_ln_mem, m_ln_ffn, m_final_norm, m_m_in_w, m_m_conv_w, m_m_conv_b, m_m_dt_bias, m_m_a_log, m_m_d, m_m_norm_w, m_m_out_w, m_h_in_w, m_h_lower_bounds, m_h_norm_w, m_h_out_w, m_g_in_w, m_g_conv_w, m_g_a_log, m_g_dt_bias, m_g_norm_w, m_g_out_w, m_xa_q, m_xa_kv, m_xa_o, m_f_up, m_f_conv_w, m_f_conv_b, m_f_down, v_ln_mix, v_ln_xattn, v_ln_mem, v_ln_ffn, v_final_norm, v_m_in_w, v_m_conv_w, v_m_conv_b, v_m_dt_bias, v_m_a_log, v_m_d, v_m_norm_w, v_m_out_w, v_h_in_w, v_h_lower_bounds, v_h_norm_w, v_h_out_w, v_g_in_w, v_g_conv_w, v_g_a_log, v_g_dt_bias, v_g_norm_w, v_g_out_w, v_xa_q, v_xa_kv, v_xa_o, v_f_up, v_f_conv_w, v_f_conv_b, v_f_down):
    local = dict(zip(WEIGHTS, (ln_mix, ln_xattn, ln_mem, ln_ffn, final_norm, m_in_w, m_conv_w, m_conv_b, m_dt_bias, m_a_log, m_d, m_norm_w, m_out_w, h_in_w, h_lower_bounds, h_norm_w, h_out_w, g_in_w, g_conv_w, g_a_log, g_dt_bias, g_norm_w, g_out_w, xa_q, xa_kv, xa_o, f_up, f_conv_w, f_conv_b, f_down), strict=True))
    mom_m = dict(zip(WEIGHTS, (m_ln_mix, m_ln_xattn, m_ln_mem, m_ln_ffn, m_final_norm, m_m_in_w, m_m_conv_w, m_m_conv_b, m_m_dt_bias, m_m_a_log, m_m_d, m_m_norm_w, m_m_out_w, m_h_in_w, m_h_lower_bounds, m_h_norm_w, m_h_out_w, m_g_in_w, m_g_conv_w, m_g_a_log, m_g_dt_bias, m_g_norm_w, m_g_out_w, m_xa_q, m_xa_kv, m_xa_o, m_f_up, m_f_conv_w, m_f_conv_b, m_f_down), strict=True))
    mom_v = dict(zip(WEIGHTS, (v_ln_mix, v_ln_xattn, v_ln_mem, v_ln_ffn, v_final_norm, v_m_in_w, v_m_conv_w, v_m_conv_b, v_m_dt_bias, v_m_a_log, v_m_d, v_m_norm_w, v_m_out_w, v_h_in_w, v_h_lower_bounds, v_h_norm_w, v_h_out_w, v_g_in_w, v_g_conv_w, v_g_a_log, v_g_dt_bias, v_g_norm_w, v_g_out_w, v_xa_q, v_xa_kv, v_xa_o, v_f_up, v_f_conv_w, v_f_conv_b, v_f_down), strict=True))
    nb, seq, _ = x.shape
    me = 2 * lax.axis_index("x") + lax.axis_index("y")

    sharded = MATRICES + SMALL_SHARDED
    gathered = gather_shards([local[n].astype(bf16) if n in MATRICES else local[n] for n in sharded])
    full = {n: _unstack(g, SHARD_AXIS[n]) for n, g in zip(sharded, gathered, strict=True)}
    full.update({n: local[n] for n in REPLICATED})

    lb = lower_bounds_fwd(full["h_lower_bounds"])
    layers = prep_layers(full, lb)
    loss, dx, lgrads, dfinal = local_step(x.reshape(nb * seq, D), mem.reshape(nb * N_MEM, D), loss_target.reshape(nb * seq, D),
                                          layers, full["final_norm"].reshape(1, D), nb)
    grads = assemble_grads(lgrads, dfinal, full["h_lower_bounds"])

    small_names = REPLICATED + SMALL_SHARDED
    flat = jnp.concatenate([grads[n].astype(f32).reshape(-1) for n in small_names] + [loss[0, 0:1]])
    rows = -(-flat.shape[0] // (8 * SMALL_ROW)) * 8
    flat = jnp.pad(flat, (0, rows * SMALL_ROW - flat.shape[0])).reshape(rows, SMALL_ROW)
    red = allreduce_small(flat).reshape(-1)
    gsum, off = {}, 0
    for n in small_names:
        size = grads[n].size
        g = red[off:off + size].reshape(grads[n].shape)
        off += size
        if n in SHARD_AXIS:
            ax = SHARD_AXIS[n]
            w = g.shape[ax] // NCHIP
            g = lax.dynamic_slice_in_dim(g, me * w, w, axis=ax)
        gsum[n] = g
    loss_out = red[off]

    parts = [_stack(grads[n], SHARD_AXIS[n]) for n in MATRICES]
    landed = scatter_parts(parts)
    sums = []
    for n, p, l in zip(MATRICES, parts, landed, strict=True):
        own = lax.dynamic_index_in_dim(p, me, axis=0, keepdims=False)
        c = own.shape[-1]
        to2 = lambda a: a.reshape(-1, c)
        r = own.size // c
        s = rows_call(f"sum_{n}", lambda a, b, cc, d: ((a.astype(f32) + b.astype(f32) + cc.astype(f32) + d.astype(f32),), ()),
                      [to2(own), to2(l[0]), to2(l[1]), to2(l[2])], [], [(c, f32)], tm=_tile(r, (256, 128, 64, 32, 16, 8)))[0]
        sums.append(s.reshape(own.shape))
    swapped = sibling_swap(sums)

    outs = {}
    for n, s, o in zip(MATRICES, sums, swapped, strict=True):
        outs[n] = adamw(local[n], s, mom_m[n], mom_v[n], f"adamw_{n}", g2=o)
    for n in small_names:
        outs[n] = adamw(local[n], gsum[n].reshape(local[n].shape), mom_m[n], mom_v[n], f"adamw_{n}")
    res = [loss_out, dx.reshape(nb, seq, D)]
    for k in range(4):
        res += [outs[n][k] for n in WEIGHTS]
    return tuple(res)
```

```python
import functools

import jax
import jax.numpy as jnp
from jax import lax
from jax.experimental import pallas as pl
from jax.experimental.pallas import tpu as pltpu

f32 = jnp.float32
bf16 = jnp.bfloat16
HIGHEST = lax.Precision.HIGHEST
MESH = pl.DeviceIdType.MESH

D = 1024
DEPTH = 4
EPS = 1e-6
N_MEM = 256
M_INNER, M_P, M_H, M_G, M_N, M_Q = 2048, 64, 32, 8, 128, 64
M_CONV = M_INNER + 2 * M_G * M_N
M_MAIN = M_INNER + M_CONV
M_IN = M_MAIN + M_H
H_H, H_K, H_Q = 8, 128, 32
G_HV, G_HK, G_K, G_Q = 16, 8, 128, 64
G_CONV, G_VAL = 4096, 2048
G_MAIN = G_CONV + G_VAL
G_IN = G_MAIN + 2 * G_HV
X_H, X_D = 4, 256
D_FF = 2816
ADAM_LR, ADAM_B1, ADAM_B2, ADAM_EPS, ADAM_WD, ADAM_STEP = 0.001, 0.9, 0.999, 1e-08, 0.01, 10
VMEM_LIMIT = 56 * 1024 * 1024
NCHIP = 4


def _cp(**kw):
    return pltpu.CompilerParams(vmem_limit_bytes=VMEM_LIMIT, **kw)


def _S(shape, dtype):
    return jax.ShapeDtypeStruct(tuple(shape), dtype)


def _dg(a, b, ca, cb, prec=None):
    return lax.dot_general(a, b, (((ca,), (cb,)), ((), ())), precision=prec, preferred_element_type=f32)


def _hdot(a, b, ca=1, cb=0):
    return _dg(a.astype(f32), b.astype(f32), ca, cb, HIGHEST)


def _bdot_raw(a, b, ca, cb):
    return _dg(a.astype(bf16), b.astype(bf16), ca, cb)


@functools.partial(jax.custom_vjp, nondiff_argnums=(2, 3))
def _bdot(a, b, ca, cb):
    return _bdot_raw(a, b, ca, cb)


def _bdot_fwd(a, b, ca, cb):
    return _bdot_raw(a, b, ca, cb), (a, b)


def _bdot_bwd(ca, cb, res, g):
    a, b = res
    if ca == 1:
        da = _bdot_raw(g, b, 1, 1 if cb == 0 else 0)
    else:
        da = _bdot_raw(b, g, 1 if cb == 0 else 0, 1)
    if cb == 0:
        db = _bdot_raw(a, g, 0 if ca == 1 else 1, 0)
    else:
        db = _bdot_raw(g, a, 0, 0 if ca == 1 else 1)
    return da.astype(a.dtype), db.astype(b.dtype)


_bdot.defvjp(_bdot_fwd, _bdot_bwd)


def _shift_down_raw(x, k):
    r = lax.broadcasted_iota(jnp.int32, x.shape, 0)
    return jnp.where(r >= k, pltpu.roll(x, k, 0), 0.0)


def _shift_up_raw(x, k):
    n = x.shape[0]
    r = lax.broadcasted_iota(jnp.int32, x.shape, 0)
    return jnp.where(r < n - k, pltpu.roll(x, n - k, 0), 0.0)


@functools.partial(jax.custom_vjp, nondiff_argnums=(1,))
def _shift_down(x, k):
    return _shift_down_raw(x, k)


_shift_down.defvjp(lambda x, k: (_shift_down_raw(x, k), None), lambda k, _, g: (_shift_up_raw(g, k),))


def _rms(x, w):
    return x * lax.rsqrt(jnp.mean(x * x, axis=-1, keepdims=True) + EPS) * w


def _silu(x):
    return x * jax.nn.sigmoid(x)


def _masks(q):
    r = lax.broadcasted_iota(jnp.int32, (q, q), 0)
    c = lax.broadcasted_iota(jnp.int32, (q, q), 1)
    return r >= c, r > c


def _colvec(row):
    return jnp.transpose(jnp.broadcast_to(row, (8, row.shape[1])))[:, 0:1]


def _tile(n, cands):
    for c in cands:
        if n % c == 0:
            return c
    return n


def mm(a, b, *, ta=False, tb=False, res=None, out_dtype=f32, name):
    m, k = (a.shape[1], a.shape[0]) if ta else a.shape
    n = b.shape[0] if tb else b.shape[1]
    tm = _tile(m, (512, 256, 128))
    tn = n if n <= 2816 else _tile(n, (2048, 1024, 512, 256, 128))
    tk = k if (k <= 4096 and not ta) else _tile(k, (1024, 512, 256, 128))
    nk = k // tk
    ca, cb = (0 if ta else 1), (1 if tb else 0)

    def body(*refs):
        if res is None:
            a_ref, b_ref, o_ref, acc = refs
            r_ref = None
        else:
            a_ref, b_ref, r_ref, o_ref, acc = refs
        kk = pl.program_id(2)

        @pl.when(kk == 0)
        def _():
            acc[...] = jnp.zeros_like(acc)

        acc[...] += _bdot_raw(a_ref[...], b_ref[...], ca, cb)

        @pl.when(kk == nk - 1)
        def _():
            v = acc[...]
            if r_ref is not None:
                v = v + r_ref[...]
            o_ref[...] = v.astype(o_ref.dtype)

    a_spec = pl.BlockSpec((tk, tm), lambda i, j, kk: (kk, i)) if ta else pl.BlockSpec((tm, tk), lambda i, j, kk: (i, kk))
    b_spec = pl.BlockSpec((tn, tk), lambda i, j, kk: (j, kk)) if tb else pl.BlockSpec((tk, tn), lambda i, j, kk: (kk, j))
    in_specs = [a_spec, b_spec]
    args = [a, b]
    if res is not None:
        in_specs.append(pl.BlockSpec((tm, tn), lambda i, j, kk: (i, j)))
        args.append(res)
    return pl.pallas_call(
        body, name=name, grid=(m // tm, n // tn, nk), in_specs=in_specs,
        out_specs=pl.BlockSpec((tm, tn), lambda i, j, kk: (i, j)), out_shape=_S((m, n), out_dtype),
        scratch_shapes=[pltpu.VMEM((tm, tn), f32)], compiler_params=_cp())(*args)


def rows_call(name, fn, rows, pars, row_out, acc_out=(), tm=256):
    t = rows[0].shape[0]
    tm = min(tm, t)
    assert t % tm == 0, (name, t, tm)
    nr, npar, nro = len(rows), len(pars), len(row_out)

    def body(*refs):
        rv = [r[...] for r in refs[:nr]]
        pv = [r[...] for r in refs[nr:nr + npar]]
        ro_refs = refs[nr + npar:nr + npar + nro]
        ao_refs = refs[nr + npar + nro:]
        ro, ao = fn(*rv, *pv)
        for r, v in zip(ro_refs, ro, strict=True):
            r[...] = v.astype(r.dtype)
        if ao_refs:
            @pl.when(pl.program_id(0) == 0)
            def _():
                for r in ao_refs:
                    r[...] = jnp.zeros_like(r)
            for r, v in zip(ao_refs, ao, strict=True):
                r[...] += v.astype(r.dtype)

    in_specs = [pl.BlockSpec((tm, r.shape[1]), lambda i: (i, 0)) for r in rows]
    in_specs += [pl.BlockSpec(p.shape, lambda i: (0, 0)) for p in pars]
    out_specs = [pl.BlockSpec((tm, c), lambda i: (i, 0)) for c, _ in row_out]
    out_specs += [pl.BlockSpec(s, lambda i: (0, 0)) for s, _ in acc_out]
    out_shape = [_S((t, c), dt) for c, dt in row_out] + [_S(s, dt) for s, dt in acc_out]
    return pl.pallas_call(body, name=name, grid=(t // tm,), in_specs=in_specs, out_specs=out_specs,
                          out_shape=out_shape, compiler_params=_cp())(*rows, *pars)


def rms_fwd(x, w, name):
    return rows_call(name, lambda xv, wv: ((_rms(xv, wv),), ()), [x], [w], [(x.shape[1], bf16)])[0]


def rms_bwd(x, w, dy, dres, name):
    def fn(*a):
        if dres is None:
            xv, dyv, wv = a
        else:
            xv, dyv, drv, wv = a
        _, vjp = jax.vjp(_rms, xv, wv)
        dx, dw = vjp(dyv.astype(f32))
        if dres is not None:
            dx = dx + drv
        return (dx,), (dw,)
    rows = [x, dy] + ([] if dres is None else [dres])
    return rows_call(name, fn, rows, [w], [(x.shape[1], f32)], [(w.shape, f32)])


def cols_call(name, fn, seqs, pars, outs, *, nb, ct, ncol, dseed=None):
    ns, npar = len(seqs), len(pars)
    seq_len = seqs[0].shape[0] // nb
    nd = 0 if dseed is None else len(dseed)

    def body(*refs):
        sv = [r[...] for r in refs[:ns]]
        pv = [r[...] for r in refs[ns:ns + npar]]
        if dseed is None:
            o_refs = refs[ns + npar:]
            for r, v in zip(o_refs, fn(*sv, *pv), strict=True):
                r[...] = v.astype(r.dtype)
            return
        dv = [r[...].astype(f32) for r in refs[ns + npar:ns + npar + nd]]
        ds_refs = refs[ns + npar + nd:ns + npar + nd + ns]
        dp_refs = refs[ns + npar + nd + ns:]
        _, vjp = jax.vjp(fn, *[v.astype(f32) for v in sv], *pv)
        g = vjp(tuple(dv))
        for r, v in zip(ds_refs, g[:ns], strict=True):
            r[...] = v.astype(r.dtype)

        @pl.when(pl.program_id(1) == 0)
        def _():
            for r in dp_refs:
                r[...] = jnp.zeros_like(r)
        for r, v in zip(dp_refs, g[ns:], strict=True):
            r[...] += v

    full = pl.BlockSpec((seq_len, ct), lambda j, b: (b, j))
    in_specs = [full for _ in seqs]
    in_specs += [pl.BlockSpec((p.shape[0], ct), lambda j, b: (0, j)) for p in pars]
    args = list(seqs) + list(pars)
    if dseed is None:
        out_specs = [full for _ in outs]
        out_shape = [_S((nb * seq_len, ncol * ct), dt) for dt in outs]
    else:
        in_specs += [full for _ in dseed]
        args += list(dseed)
        out_specs = [full for _ in seqs] + [pl.BlockSpec((p.shape[0], ct), lambda j, b: (0, j)) for p in pars]
        out_shape = [_S((nb * seq_len, ncol * ct), f32) for _ in seqs] + [_S(p.shape, f32) for p in pars]
    return pl.pallas_call(body, name=name, grid=(ncol, nb), in_specs=in_specs, out_specs=out_specs,
                          out_shape=out_shape, compiler_params=_cp())(*args)


def _conv4_silu(x, w, b):
    y = x * w[3:4] + _shift_down(x, 1) * w[2:3] + _shift_down(x, 2) * w[1:2] + _shift_down(x, 3) * w[0:1] + b
    return (_silu(y),)


def _conv4_silu_nobias(x, w):
    y = x * w[3:4] + _shift_down(x, 1) * w[2:3] + _shift_down(x, 2) * w[1:2] + _shift_down(x, 3) * w[0:1]
    return (_silu(y),)


def _ffn_act(gate, up, w, b):
    y = gate * w[2:3] + _shift_down(gate, 1) * w[1:2] + _shift_down(gate, 2) * w[0:1] + b
    return (_silu(y) * up,)


def scan_call(name, chunk_fn, seqs, pars, consts, outs, *, nb, nh, q, state_shape, states=None, dseed=None):
    t = seqs[0][0].shape[0]
    nc = t // (nb * q)
    ns, npar, ncon, no = len(seqs), len(pars), len(consts), len(outs)
    s0, s1 = state_shape
    bwd = dseed is not None

    def cidx(c):
        return (nc - 1 - c) if bwd else c

    def rowblk(b, c):
        return b * nc + cidx(c)

    def seq_spec(w, colfn):
        return pl.BlockSpec((q, w), lambda b, c, h: (rowblk(b, c), colfn(h)))

    def par_spec(shape, idxfn):
        return pl.BlockSpec(shape, lambda b, c, h: idxfn(h))

    st_spec = pl.BlockSpec((s0, s1), lambda b, c, h: ((rowblk(b, c)) * nh + h, 0))
    in_specs = [seq_spec(w, cf) for _, w, cf, _ in seqs]
    in_specs += [par_spec(s, f) for _, s, f in pars] + [par_spec(s, f) for _, s, f in consts]
    args = [a for a, _, _, _ in seqs] + [a for a, _, _ in pars] + [a for a, _, _ in consts]

    if not bwd:
        def body(*refs):
            sv = [r[...] for r in refs[:ns]]
            pv = [r[...] for r in refs[ns:ns + npar]]
            cv = [r[...] for r in refs[ns + npar:ns + npar + ncon]]
            o_refs = refs[ns + npar + ncon:ns + npar + ncon + no]
            save_ref = refs[ns + npar + ncon + no]
            st = refs[-1]
            c, h = pl.program_id(1), pl.program_id(2)

            @pl.when(c == 0)
            def _():
                st[h] = jnp.zeros((s0, s1), f32)
            s_in = st[h]
            save_ref[...] = s_in
            o, s_out = chunk_fn(*sv, *pv, s_in, *cv)
            st[h] = s_out
            for r, v in zip(o_refs, o, strict=True):
                r[...] = v.astype(r.dtype)

        out_specs = [seq_spec(w, cf) for _, w, cf, _ in outs] + [st_spec]
        out_shape = [_S((t, cc), dt) for cc, _, _, dt in outs] + [_S((nb * nc * nh * s0, s1), f32)]
        return pl.pallas_call(body, name=name, grid=(nb, nc, nh), in_specs=in_specs, out_specs=out_specs,
                              out_shape=out_shape, scratch_shapes=[pltpu.VMEM((nh, s0, s1), f32)],
                              compiler_params=_cp())(*args)

    def body(*refs):
        i = 0
        sv = [r[...] for r in refs[i:i + ns]]; i += ns
        pv = [r[...] for r in refs[i:i + npar]]; i += npar
        cv = [r[...] for r in refs[i:i + ncon]]; i += ncon
        dv = [r[...].astype(f32) for r in refs[i:i + no]]; i += no
        s_in = refs[i][...]; i += 1
        ds_refs = refs[i:i + ns]; i += ns
        dp_refs = refs[i:i + npar]; i += npar
        dst = refs[-1]
        b, c, h = pl.program_id(0), pl.program_id(1), pl.program_id(2)

        @pl.when(c == 0)
        def _():
            dst[h] = jnp.zeros((s0, s1), f32)

        @pl.when((b == 0) & (c == 0) & (h == 0))
        def _():
            for r in dp_refs:
                r[...] = jnp.zeros_like(r)

        fn = lambda *a: chunk_fn(*a, *cv)
        _, vjp = jax.vjp(fn, *[v.astype(f32) for v in sv], *pv, s_in)
        g = vjp((tuple(dv), dst[h]))
        dst[h] = g[ns + npar]
        for (_, _, _, rep), r, v in zip(seqs, ds_refs, g[:ns], strict=True):
            if rep == 1:
                r[...] = v.astype(r.dtype)
            else:
                @pl.when(h % rep == 0)
                def _(r=r, v=v):
                    r[...] = v.astype(r.dtype)

                @pl.when(h % rep != 0)
                def _(r=r, v=v):
                    r[...] += v.astype(r.dtype)
        for r, v in zip(dp_refs, g[ns:ns + npar], strict=True):
            r[h] += v

    in_specs += [seq_spec(w, cf) for _, w, cf, _ in outs] + [st_spec]
    args += list(dseed) + [states]
    out_specs = [seq_spec(w, cf) for _, w, cf, _ in seqs]
    out_specs += [pl.BlockSpec((nh,) + tuple(s), lambda b, c, h: (0, 0, 0)) for _, s, _ in pars]
    out_shape = [_S(a.shape, f32) for a, _, _, _ in seqs] + [_S((nh,) + tuple(s), f32) for _, s, _ in pars]
    return pl.pallas_call(body, name=name, grid=(nb, nc, nh), in_specs=in_specs, out_specs=out_specs,
                          out_shape=out_shape, scratch_shapes=[pltpu.VMEM((nh, s0, s1), f32)],
                          compiler_params=_cp())(*args)


def _ssd_group(xs, bm, cm, z, dtr, dtb, alog, dsk, nw, st, e):
    q, gp = xs.shape[0], SSD_GP
    heads = range(4 * gp)
    sl = [slice(i * M_P, (i + 1) * M_P) for i in heads]
    gsl = [slice(g * M_N, (g + 1) * M_N) for g in range(gp)]
    incl, _ = _masks(q)
    dt = jax.nn.softplus(dtr + dtb[0:1])
    dte = _hdot(dt, e)
    ae = _hdot(-jnp.exp(alog), e)[0:1]
    de = _hdot(dsk, e)[0:1]
    xc = xs * dte
    acum = _hdot(incl.astype(f32), dte * ae)
    last = acum[q - 1:q]
    eac, eend, elast = jnp.exp(acum), jnp.exp(last - acum), jnp.exp(last)
    xe = xc * eend
    bms, cms = [bm[:, s] for s in gsl], [cm[:, s] for s in gsl]
    cb = [_bdot(cms[g], bms[g], 1, 1) for g in range(gp)]
    decs = []
    for i in heads:
        a_i = acum[:, sl[i]]
        diff = jnp.where(incl, a_i[:, 0:1] - jnp.transpose(a_i)[0:1, :], 0.0)
        decs.append(jnp.where(incl, jnp.exp(diff), 0.0))
    sts = [st[sl[i], :] for i in heads]
    yd = [_bdot(cb[i // 4] * decs[i], xc[:, sl[i]], 1, 0) for i in heads]
    yo = [_bdot(cms[i // 4], sts[i], 1, 1) for i in heads]
    ds = [_bdot(xe[:, sl[i]], bms[i // 4], 0, 0) for i in heads]
    new = [sts[i] * elast[:, i * M_P:i * M_P + 1] + ds[i] for i in heads]
    y = jnp.concatenate(yd, axis=1) + jnp.concatenate(yo, axis=1) * eac + de * xs
    y = y * _silu(z)
    yn = [_rms(y[:, g * 256:(g + 1) * 256], nw[:, g * 256:(g + 1) * 256]) for g in range(gp)]
    return (jnp.concatenate(yn, axis=1),), jnp.concatenate(new, axis=0)


def _gla_group(qr, fr, ir, gr, lb, nw, st):
    q, hp = qr.shape[0], GLA_HP
    heads = range(hp)
    sl = [slice(i * H_K, (i + 1) * H_K) for i in heads]
    incl, _ = _masks(q)
    fg = lb + (1.0 - lb) * jax.nn.sigmoid(fr)
    qq = _silu(qr) * (H_K ** -0.5)
    k = 1.0 - fg
    gc = _hdot(incl.astype(f32), jnp.log(fg))
    gl = gc[q - 1:q]
    qd, ki, ke = qq * jnp.exp(gc), k * jnp.exp(-gc), k * jnp.exp(gl - gc)
    egl = jnp.exp(gl)
    sts = [st[sl[i], :] for i in heads]
    att = [jnp.where(incl, _bdot(qd[:, sl[i]], ki[:, sl[i]], 1, 1), 0.0) for i in heads]
    o1 = [_bdot(att[i], ir[:, sl[i]], 1, 0) for i in heads]
    o2 = [_bdot(qd[:, sl[i]], sts[i], 1, 0) for i in heads]
    kv = [_bdot(ke[:, sl[i]], ir[:, sl[i]], 0, 0) for i in heads]
    new = [sts[i] * _colvec(egl[:, sl[i]]) + kv[i] for i in heads]
    on = [_rms(o1[i] + o2[i], nw) * _silu(gr[:, sl[i]]) for i in heads]
    return (jnp.concatenate(on, axis=1),), jnp.concatenate(new, axis=0)


def _tri_inv_many(ms):
    n = ms[0].shape[0]
    r = lax.broadcasted_iota(jnp.int32, (n, n), 0)
    c = lax.broadcasted_iota(jnp.int32, (n, n), 1)
    eye = (r == c).astype(f32)
    ts = [eye - m for m in ms]
    ps = list(ms)
    for _ in range(max(1, (n - 1).bit_length() - 1)):
        ps = [_hdot(p, p) for p in ps]
        ts = [t + _hdot(t, p) for t, p in zip(ts, ps)]
    return ts


def _gdn_group(qr, kr, v, z, ba, alog, dtb, nw, st, eb, ea):
    q, hp = qr.shape[0], GDN_HP
    heads = range(hp)
    sl = [slice(i * G_K, (i + 1) * G_K) for i in heads]
    incl, strict = _masks(q)
    beta = _hdot(jax.nn.sigmoid(ba), eb)
    g = _hdot(-jnp.exp(alog[0:1]) * jax.nn.softplus(ba + dtb[0:1]), ea)
    gc = _hdot(incl.astype(f32), g)
    egc = jnp.exp(gc)
    gl = gc[q - 1:q]
    eend = jnp.exp(gl - gc)
    egl = jnp.exp(gl)
    vb = v * beta
    qn, kn = [], []
    for j in range(hp // 2):
        qj, kj = qr[:, sl[j]], kr[:, sl[j]]
        qn.append(qj * lax.rsqrt(jnp.sum(qj * qj, axis=-1, keepdims=True) + EPS) * (G_K ** -0.5))
        kn.append(kj * lax.rsqrt(jnp.sum(kj * kj, axis=-1, keepdims=True) + EPS))
    qk = [_bdot(qn[j], kn[j], 1, 1) for j in range(hp // 2)]
    gcs = [gc[:, sl[i]] for i in heads]
    decs = []
    for i in heads:
        diff = jnp.where(incl, gcs[i][:, 0:1] - jnp.transpose(gcs[i])[0:1, :], 0.0)
        decs.append(jnp.where(incl, jnp.exp(diff), 0.0))
    kbs = [kn[i // 2] * beta[:, sl[i]] for i in heads]
    kk = [_bdot(kbs[i], kn[i // 2], 1, 1) for i in heads]
    tinv = _tri_inv_many([jnp.where(strict, kk[i] * decs[i], 0.0) for i in heads])
    uw = [_hdot(tinv[i], jnp.concatenate([vb[:, sl[i]], kbs[i] * egc[:, sl[i]]], axis=1)) for i in heads]
    sts = [st[sl[i], :] for i in heads]
    ws = [_bdot(jnp.concatenate([uw[i][:, G_K:], qn[i // 2] * egc[:, sl[i]]], axis=0), sts[i], 1, 0) for i in heads]
    v_new = [uw[i][:, :G_K] - ws[i][:q] for i in heads]
    o = [ws[i][q:] + _bdot(qk[i // 2] * decs[i], v_new[i], 1, 0) for i in heads]
    new = [sts[i] * egl[:, i * G_K:i * G_K + 1] + _bdot(kn[i // 2] * eend[:, sl[i]], v_new[i], 0, 0) for i in heads]
    on = [_rms(o[i], nw) * _silu(z[:, sl[i]]) for i in heads]
    return (jnp.concatenate(on, axis=1),), jnp.concatenate(new, axis=0)


def _xattn_fn(q, k, v):
    s = _bdot(q, k, 1, 1) * (X_D ** -0.5)
    return _bdot(jax.nn.softmax(s, axis=-1), v, 1, 0)


def xattn_fwd(q, k, v, nb, name, tl=512):
    t = q.shape[0]
    tl = min(tl, t // nb)
    nl = t // nb // tl

    def body(q_ref, k_ref, v_ref, o_ref):
        o_ref[...] = _xattn_fn(q_ref[...], k_ref[...], v_ref[...]).astype(o_ref.dtype)

    qs = pl.BlockSpec((tl, X_D), lambda b, i, h: (b * nl + i, h))
    ks = pl.BlockSpec((N_MEM, X_D), lambda b, i, h: (b, h))
    return pl.pallas_call(body, name=name, grid=(nb, nl, X_H), in_specs=[qs, ks, ks], out_specs=qs,
                          out_shape=_S(q.shape, bf16), compiler_params=_cp())(q, k, v)


def xattn_bwd(q, k, v, do, nb, name, tl=512):
    t = q.shape[0]
    tl = min(tl, t // nb)
    nl = t // nb // tl

    def body(q_ref, k_ref, v_ref, do_ref, dq_ref, dk_ref, dv_ref):
        _, vjp = jax.vjp(_xattn_fn, q_ref[...], k_ref[...], v_ref[...])
        dq, dk, dv = vjp(do_ref[...].astype(f32))
        dq_ref[...] = dq

        @pl.when(pl.program_id(2) == 0)
        def _():
            dk_ref[...] = jnp.zeros_like(dk_ref)
            dv_ref[...] = jnp.zeros_like(dv_ref)
        dk_ref[...] += dk
        dv_ref[...] += dv

    qs = pl.BlockSpec((tl, X_D), lambda b, h, i: (b * nl + i, h))
    ks = pl.BlockSpec((N_MEM, X_D), lambda b, h, i: (b, h))
    return pl.pallas_call(body, name=name, grid=(nb, X_H, nl), in_specs=[qs, ks, ks, qs], out_specs=[qs, ks, ks],
                          out_shape=[_S(q.shape, f32), _S(k.shape, f32), _S(v.shape, f32)],
                          compiler_params=_cp())(q, k, v, do)


def _lower_bounds(hlb):
    sm = jax.nn.softmax(hlb, axis=0)
    rows, run = [], None
    for r in range(hlb.shape[0]):
        run = sm[r:r + 1] if run is None else run + sm[r:r + 1]
        rows.append(run - sm[0:1])
    return jnp.concatenate(rows, axis=0)


def lower_bounds_fwd(hlb):
    return rows_call("lb_fwd", lambda v: ((_lower_bounds(v),), ()), [hlb], [], [(hlb.shape[1], f32)], tm=hlb.shape[0])[0]


def lower_bounds_bwd(hlb, dlb):
    def fn(v, d):
        _, vjp = jax.vjp(_lower_bounds, v)
        return (vjp(d)[0],), ()
    return rows_call("lb_bwd", fn, [hlb, dlb], [], [(hlb.shape[1], f32)], tm=hlb.shape[0])[0]


def loss_head(x, target, w):
    def fn(xv, tv, wv):
        def loss(xx, ww):
            err = _rms(xx, ww) - tv
            return 0.5 * jnp.sum(jnp.mean(err * err, axis=-1))
        val, (dx, dw) = jax.value_and_grad(loss, argnums=(0, 1))(xv, wv)
        return (dx,), (jnp.broadcast_to(val, (1, 128)), dw)
    dx, loss, dw = rows_call("loss_head", fn, [x, target], [w], [(x.shape[1], f32)], [((1, 128), f32), (w.shape, f32)])
    return dx, loss, dw


def _adamw_fn(w, g, m, v):
    m2 = ADAM_B1 * m + (1.0 - ADAM_B1) * g
    v2 = ADAM_B2 * v + (1.0 - ADAM_B2) * (g * g)
    m_hat = m2 / (1.0 - ADAM_B1 ** ADAM_STEP)
    v_hat = v2 / (1.0 - ADAM_B2 ** ADAM_STEP)
    delta = -ADAM_LR * (m_hat / (jnp.sqrt(v_hat) + ADAM_EPS) + ADAM_WD * w)
    return delta, m2, v2


def adamw(w, g, m, v, name, g2=None):
    shape = w.shape
    c = shape[-1]
    r = w.size // c
    to2 = lambda a: a.reshape(r, c)
    tm = r if r * c * 4 <= (1 << 20) else _tile(r, (256, 128, 64, 32, 16, 8))

    def fn(*a):
        if g2 is None:
            wv, gv, mv, vv = a
        else:
            wv, gv, g2v, mv, vv = a
            gv = gv + g2v
        return (gv,) + _adamw_fn(wv, gv, mv, vv), ()
    rows = [to2(w), to2(g)] + ([] if g2 is None else [to2(g2)]) + [to2(m), to2(v)]
    outs = rows_call(name, fn, rows, [], [(c, f32)] * 4, tm=tm)
    return tuple(o.reshape(shape) for o in outs)


def _expand(first_row, nheads, width):
    r = jnp.arange(128)[:, None]
    c = jnp.arange(nheads * width)[None, :]
    return (r == first_row + c // width).astype(f32)


def _pad_row(v, lane0=0):
    return jnp.pad(v.astype(f32).reshape(1, -1), ((0, 7), (lane0, 128 - lane0 - v.shape[0])))


def _pad_cols(w, n=128):
    return jnp.pad(w, ((0, 0), (0, n - w.shape[1])))


_COL = lambda h: h
_C00 = lambda h: (0, 0)
_CONV_CT = 256


def _conv(name, x, w, b, nb, dseed=None):
    fn = _conv4_silu if b is not None else _conv4_silu_nobias
    pars = [w] + ([] if b is None else [b])
    return cols_call(name, fn, [x], pars, [f32], nb=nb, ct=_CONV_CT, ncol=x.shape[1] // _CONV_CT,
                     dseed=None if dseed is None else [dseed])


SSD_GP, GLA_HP, GDN_HP = 8, 8, 8


def _ssd_scan(name, xs, bm, cm, z, dtr, p, nb, states=None, dseed=None):
    gp, ng = SSD_GP, M_G // SSD_GP
    seqs = [(xs, 256 * gp, _COL, 1), (bm, 128 * gp, _COL, 1), (cm, 128 * gp, _COL, 1), (z, 256 * gp, _COL, 1),
            (dtr, 128, lambda h: 0, ng)]
    pars = [(p["dtb"], (8, 128), _C00), (p["alog"], (8, 128), _C00), (p["dsk"], (8, 128), _C00),
            (p["nw"], (1, 256 * gp), lambda h: (0, h))]
    consts = [(_expand(0, M_H, M_P), (128, 256 * gp), lambda h: (0, h))]
    outs = [(M_INNER, 256 * gp, _COL, bf16)]
    return scan_call(name, _ssd_group, seqs, pars, consts, outs, nb=nb, nh=ng, q=M_Q, state_shape=(gp * 4 * M_P, M_N),
                     states=states, dseed=dseed)


def _gla_scan(name, qr, fr, ir, gr, p, nb, states=None, dseed=None):
    hp, ng = GLA_HP, H_H // GLA_HP
    seqs = [(a, 128 * hp, _COL, 1) for a in (qr, fr, ir, gr)]
    pars = [(p["lb"], (1, 128 * hp), lambda h: (0, h)), (p["nw"], (1, 128), _C00)]
    outs = [(D, 128 * hp, _COL, bf16)]
    return scan_call(name, _gla_group, seqs, pars, [], outs, nb=nb, nh=ng, q=H_Q, state_shape=(hp * H_K, H_K),
                     states=states, dseed=dseed)


def _gdn_scan(name, qc, kc, vc, z, ba, p, nb, states=None, dseed=None):
    hp, ng = GDN_HP, G_HV // GDN_HP
    seqs = [(qc, 64 * hp, _COL, 1), (kc, 64 * hp, _COL, 1), (vc, 128 * hp, _COL, 1), (z, 128 * hp, _COL, 1),
            (ba, 128, lambda h: 0, ng)]
    pars = [(p["alog"], (8, 128), _C00), (p["dtb"], (8, 128), _C00), (p["nw"], (1, 128), _C00)]
    consts = [(_expand(0, G_HV, G_K), (128, 128 * hp), lambda h: (0, h)),
              (_expand(G_HV, G_HV, G_K), (128, 128 * hp), lambda h: (0, h))]
    outs = [(G_VAL, 128 * hp, _COL, bf16)]
    return scan_call(name, _gdn_group, seqs, pars, consts, outs, nb=nb, nh=ng, q=G_Q, state_shape=(hp * G_K, G_K),
                     states=states, dseed=dseed)


def _proj_bwd(tag, hn, pieces):
    dhn, dws = None, []
    for i, (d, w) in enumerate(pieces):
        dws.append(mm(hn, d, ta=True, out_dtype=bf16, name=f"{tag}_dw{i}"))
        dhn = mm(d, w, tb=True, res=dhn, name=f"{tag}_dh{i}")
    return dhn, dws


def ssd_mixer_fwd(tag, hn, w, nb):
    z, xr, br, cr, dtr = (mm(hn, w[k], name=f"{tag}_in_{k}") for k in ("wz", "wx", "wb", "wc", "wdt"))
    xs = _conv(f"{tag}_convx", xr, w["cwx"], w["cbx"], nb)[0]
    bm = _conv(f"{tag}_convb", br, w["cwb"], w["cbb"], nb)[0]
    cm = _conv(f"{tag}_convc", cr, w["cwc"], w["cbc"], nb)[0]
    yn, states = _ssd_scan(f"{tag}_scan", xs, bm, cm, z, dtr, w, nb)
    return yn, (hn, z, xr, br, cr, dtr, xs, bm, cm, yn, states)


def ssd_mixer_bwd(tag, saved, dout, w, nb):
    hn, z, xr, br, cr, dtr, xs, bm, cm, yn, states = saved
    g = {"wout": mm(yn, dout, ta=True, out_dtype=bf16, name=f"{tag}_dwout")}
    dyn = mm(dout, w["wout"], tb=True, name=f"{tag}_dyn")
    dxs, dbm, dcm, dz, ddtr, ddtb, dalog, ddsk, dnw = _ssd_scan(f"{tag}_scanb", xs, bm, cm, z, dtr, w, nb, states, [dyn])
    dxr, g["cwx"], g["cbx"] = _conv(f"{tag}_convxb", xr, w["cwx"], w["cbx"], nb, dxs)
    dbr, g["cwb"], g["cbb"] = _conv(f"{tag}_convbb", br, w["cwb"], w["cbb"], nb, dbm)
    dcr, g["cwc"], g["cbc"] = _conv(f"{tag}_convcb", cr, w["cwc"], w["cbc"], nb, dcm)
    dhn, (g["wz"], g["wx"], g["wb"], g["wc"], g["wdt"]) = _proj_bwd(
        tag, hn, [(dz, w["wz"]), (dxr, w["wx"]), (dbr, w["wb"]), (dcr, w["wc"]), (ddtr, w["wdt"])])
    g["dtb"], g["alog"], g["dsk"] = (jnp.sum(a, axis=0)[0, :M_H] for a in (ddtb, dalog, ddsk))
    g["nw"] = dnw.reshape(M_INNER)
    return dhn, g


def gla_mixer_fwd(tag, hn, w, nb):
    qr, fr, ir, gr = (mm(hn, w[k], name=f"{tag}_in_{k}") for k in ("wq", "wf", "wi", "wg"))
    on, states = _gla_scan(f"{tag}_scan", qr, fr, ir, gr, w, nb)
    return on, (hn, qr, fr, ir, gr, on, states)


def gla_mixer_bwd(tag, saved, dout, w, nb):
    hn, qr, fr, ir, gr, on, states = saved
    g = {"wout": mm(on, dout, ta=True, out_dtype=bf16, name=f"{tag}_dwout")}
    don = mm(dout, w["wout"], tb=True, name=f"{tag}_don")
    dq, df, di, dg, dlb, dnw = _gla_scan(f"{tag}_scanb", qr, fr, ir, gr, w, nb, states, [don])
    dhn, (g["wq"], g["wf"], g["wi"], g["wg"]) = _proj_bwd(tag, hn, [(dq, w["wq"]), (df, w["wf"]), (di, w["wi"]), (dg, w["wg"])])
    g["lb"] = dlb.reshape(1, D)
    g["nw"] = jnp.sum(dnw, axis=0).reshape(H_K)
    return dhn, g


def gdn_mixer_fwd(tag, hn, w, nb):
    qr, kr, vr, z, ba = (mm(hn, w[k], name=f"{tag}_in_{k}") for k in ("wq", "wk", "wv", "wz", "wba"))
    qc = _conv(f"{tag}_convq", qr, w["cwq"], None, nb)[0]
    kc = _conv(f"{tag}_convk", kr, w["cwk"], None, nb)[0]
    vc = _conv(f"{tag}_convv", vr, w["cwv"], None, nb)[0]
    on, states = _gdn_scan(f"{tag}_scan", qc, kc, vc, z, ba, w, nb)
    return on, (hn, qr, kr, vr, z, ba, qc, kc, vc, on, states)


def gdn_mixer_bwd(tag, saved, dout, w, nb):
    hn, qr, kr, vr, z, ba, qc, kc, vc, on, states = saved
    g = {"wout": mm(on, dout, ta=True, out_dtype=bf16, name=f"{tag}_dwout")}
    don = mm(dout, w["wout"], tb=True, name=f"{tag}_don")
    dqc, dkc, dvc, dz, dba, dalog, ddtb, dnw = _gdn_scan(f"{tag}_scanb", qc, kc, vc, z, ba, w, nb, states, [don])
    dqr, g["cwq"] = _conv(f"{tag}_convqb", qr, w["cwq"], None, nb, dqc)
    dkr, g["cwk"] = _conv(f"{tag}_convkb", kr, w["cwk"], None, nb, dkc)
    dvr, g["cwv"] = _conv(f"{tag}_convvb", vr, w["cwv"], None, nb, dvc)
    dhn, (g["wq"], g["wk"], g["wv"], g["wz"], g["wba"]) = _proj_bwd(
        tag, hn, [(dqr, w["wq"]), (dkr, w["wk"]), (dvr, w["wv"]), (dz, w["wz"]), (dba, w["wba"])])
    g["alog"], g["dtb"] = (jnp.sum(a, axis=0)[0, G_HV:2 * G_HV] for a in (dalog, ddtb))
    g["nw"] = jnp.sum(dnw, axis=0).reshape(G_K)
    return dhn, g


_MIXERS = {0: (ssd_mixer_fwd, ssd_mixer_bwd), 1: (gla_mixer_fwd, gla_mixer_bwd), 2: (gdn_mixer_fwd, gdn_mixer_bwd)}


def layer_fwd(i, x, mem, w, nb):
    t = f"l{i}"
    hn = rms_fwd(x, w["ln_mix"], f"{t}_ln_mix")
    mix, s_mix = _MIXERS[i % 3][0](f"{t}_mix", hn, w["mix"], nb)
    x1 = mm(mix, w["mix"]["wout"], res=x, name=f"{t}_mix_out")
    hx = rms_fwd(x1, w["ln_xattn"], f"{t}_ln_xattn")
    mn = rms_fwd(mem, w["ln_mem"], f"{t}_ln_mem")
    q = mm(hx, w["xq"], name=f"{t}_xa_q")
    k = mm(mn, w["xk"], name=f"{t}_xa_k")
    v = mm(mn, w["xv"], name=f"{t}_xa_v")
    o = xattn_fwd(q, k, v, nb, f"{t}_xattn")
    x2 = mm(o, w["xo"], res=x1, name=f"{t}_xa_o")
    hf = rms_fwd(x2, w["ln_ffn"], f"{t}_ln_ffn")
    gate = mm(hf, w["fg"], name=f"{t}_ffn_gate")
    up = mm(hf, w["fu"], name=f"{t}_ffn_up")
    act = cols_call(f"{t}_ffn_act", _ffn_act, [gate, up], [w["fcw"], w["fcb"]], [bf16], nb=nb, ct=_CONV_CT,
                    ncol=D_FF // _CONV_CT)[0]
    x3 = mm(act, w["fd"], res=x2, name=f"{t}_ffn_down")
    return x3, (x, s_mix, x1, hx, mn, q, k, v, o, x2, hf, gate, up, act)


def layer_bwd(i, saved, dx, mem, w, nb):
    t = f"l{i}b"
    x, s_mix, x1, hx, mn, q, k, v, o, x2, hf, gate, up, act = saved
    g = {}
    g["fd"] = mm(act, dx, ta=True, out_dtype=bf16, name=f"{t}_dwd")
    dact = mm(dx, w["fd"], tb=True, name=f"{t}_dact")
    dgate, dup, g["fcw"], g["fcb"] = cols_call(f"{t}_ffn_act", _ffn_act, [gate, up], [w["fcw"], w["fcb"]], [bf16], nb=nb,
                                               ct=_CONV_CT, ncol=D_FF // _CONV_CT, dseed=[dact])
    dhf, (g["fg"], g["fu"]) = _proj_bwd(f"{t}_ffn", hf, [(dgate, w["fg"]), (dup, w["fu"])])
    dx, g["ln_ffn"] = rms_bwd(x2, w["ln_ffn"], dhf, dx, f"{t}_ln_ffn")
    g["xo"] = mm(o, dx, ta=True, out_dtype=bf16, name=f"{t}_dwo")
    do = mm(dx, w["xo"], tb=True, name=f"{t}_do")
    dq, dk, dv = xattn_bwd(q, k, v, do, nb, f"{t}_xattn")
    dhx, (g["xq"],) = _proj_bwd(f"{t}_xq", hx, [(dq, w["xq"])])
    dmn, (g["xk"], g["xv"]) = _proj_bwd(f"{t}_xkv", mn, [(dk, w["xk"]), (dv, w["xv"])])
    _, g["ln_mem"] = rms_bwd(mem, w["ln_mem"], dmn, None, f"{t}_ln_mem")
    dx, g["ln_xattn"] = rms_bwd(x1, w["ln_xattn"], dhx, dx, f"{t}_ln_xattn")
    dhn, g["mix"] = _MIXERS[i % 3][1](f"{t}_mix", s_mix, dx, w["mix"], nb)
    dx, g["ln_mix"] = rms_bwd(x, w["ln_mix"], dhn, dx, f"{t}_ln_mix")
    return dx, g


def local_step(x, mem, target, layers, final_norm, nb):
    saved = []
    for i, w in enumerate(layers):
        x, s = layer_fwd(i, x, mem, w, nb)
        saved.append(s)
    dx, loss, dfinal = loss_head(x, target, final_norm)
    grads = [None] * len(layers)
    for i in reversed(range(len(layers))):
        dx, grads[i] = layer_bwd(i, saved[i], dx, mem, layers[i], nb)
    return loss, dx, grads, dfinal


def prep_layers(W, lb):
    layers, ia, ib, ic = [], 0, 0, 0
    row = lambda a: a.reshape(1, -1)
    for i in range(DEPTH):
        kv, fup = W["xa_kv"][i], W["f_up"][i]
        layer = dict(ln_mix=W["ln_mix"][i:i + 1], ln_xattn=W["ln_xattn"][i:i + 1], ln_mem=W["ln_mem"][i:i + 1],
                     ln_ffn=W["ln_ffn"][i:i + 1], xq=W["xa_q"][i], xk=kv[:, :D], xv=kv[:, D:], xo=W["xa_o"][i],
                     fg=fup[:, :D_FF], fu=fup[:, D_FF:], fcw=W["f_conv_w"][i], fcb=W["f_conv_b"][i:i + 1], fd=W["f_down"][i])
        if i % 3 == 0:
            inw, cw, cb = W["m_in_w"][ia], W["m_conv_w"][ia], row(W["m_conv_b"][ia])
            a, b, c = M_INNER, M_INNER + M_G * M_N, M_CONV
            layer["mix"] = dict(
                wz=inw[:, :M_INNER], wx=inw[:, M_INNER:M_INNER + a], wb=inw[:, M_INNER + a:M_INNER + b],
                wc=inw[:, M_INNER + b:M_MAIN], wdt=_pad_cols(inw[:, M_MAIN:]),
                cwx=cw[:, :a], cwb=cw[:, a:b], cwc=cw[:, b:c], cbx=cb[:, :a], cbb=cb[:, a:b], cbc=cb[:, b:c],
                dtb=_pad_row(W["m_dt_bias"][ia]), alog=_pad_row(W["m_a_log"][ia]), dsk=_pad_row(W["m_d"][ia]),
                nw=row(W["m_norm_w"][ia]), wout=W["m_out_w"][ia])
            ia += 1
        elif i % 3 == 1:
            inw = W["h_in_w"][ib]
            layer["mix"] = dict(wq=inw[:, :D], wf=inw[:, D:2 * D], wi=inw[:, 2 * D:3 * D], wg=inw[:, 3 * D:],
                                lb=lb[i:i + 1], nw=row(W["h_norm_w"][ib]), wout=W["h_out_w"][ib])
            ib += 1
        else:
            inw, cw = W["g_in_w"][ic], W["g_conv_w"][ic]
            layer["mix"] = dict(
                wq=inw[:, :D], wk=inw[:, D:2 * D], wv=inw[:, 2 * D:G_CONV], wz=inw[:, G_CONV:G_MAIN], wba=_pad_cols(inw[:, G_MAIN:]),
                cwq=cw[:, :D], cwk=cw[:, D:2 * D], cwv=cw[:, 2 * D:],
                alog=_pad_row(W["g_a_log"][ic], G_HV), dtb=_pad_row(W["g_dt_bias"][ic], G_HV),
                nw=row(W["g_norm_w"][ic]), wout=W["g_out_w"][ic])
            ic += 1
        layers.append(layer)
    return layers


def assemble_grads(grads, dfinal, hlb):
    cat = lambda xs: jnp.concatenate(xs, axis=1)
    out = {k: jnp.concatenate([g[k] for g in grads], axis=0) for k in ("ln_mix", "ln_xattn", "ln_mem", "ln_ffn")}
    out["final_norm"] = dfinal.reshape(D)
    out["xa_q"] = jnp.stack([g["xq"] for g in grads])
    out["xa_kv"] = jnp.stack([cat([g["xk"], g["xv"]]) for g in grads])
    out["xa_o"] = jnp.stack([g["xo"] for g in grads])
    out["f_up"] = jnp.stack([cat([g["fg"], g["fu"]]) for g in grads])
    out["f_conv_w"] = jnp.stack([g["fcw"] for g in grads])
    out["f_conv_b"] = jnp.concatenate([g["fcb"] for g in grads], axis=0)
    out["f_down"] = jnp.stack([g["fd"] for g in grads])
    ms = [g["mix"] for i, g in enumerate(grads) if i % 3 == 0]
    out["m_in_w"] = jnp.stack([cat([m["wz"], m["wx"], m["wb"], m["wc"], m["wdt"][:, :M_H]]) for m in ms])
    out["m_conv_w"] = jnp.stack([cat([m["cwx"], m["cwb"], m["cwc"]]) for m in ms])
    out["m_conv_b"] = jnp.concatenate([cat([m["cbx"], m["cbb"], m["cbc"]]) for m in ms], axis=0)
    out["m_dt_bias"] = jnp.stack([m["dtb"] for m in ms])
    out["m_a_log"] = jnp.stack([m["alog"] for m in ms])
    out["m_d"] = jnp.stack([m["dsk"] for m in ms])
    out["m_norm_w"] = jnp.stack([m["nw"] for m in ms])
    out["m_out_w"] = jnp.stack([m["wout"] for m in ms])
    hs = [(i, g["mix"]) for i, g in enumerate(grads) if i % 3 == 1]
    out["h_in_w"] = jnp.stack([cat([m["wq"], m["wf"], m["wi"], m["wg"]]) for _, m in hs])
    lb_rows = dict(hs)
    dlb = jnp.concatenate([lb_rows[i]["lb"] if i in lb_rows else jnp.zeros((1, D), f32) for i in range(DEPTH)], axis=0)
    out["h_lower_bounds"] = lower_bounds_bwd(hlb, dlb)
    out["h_norm_w"] = jnp.stack([m["nw"] for _, m in hs])
    out["h_out_w"] = jnp.stack([m["wout"] for _, m in hs])
    gs = [g["mix"] for i, g in enumerate(grads) if i % 3 == 2]
    out["g_in_w"] = jnp.stack([cat([m["wq"], m["wk"], m["wv"], m["wz"], m["wba"][:, :2 * G_HV]]) for m in gs])
    out["g_conv_w"] = jnp.stack([cat([m["cwq"], m["cwk"], m["cwv"]]) for m in gs])
    out["g_a_log"] = jnp.stack([m["alog"] for m in gs])
    out["g_dt_bias"] = jnp.stack([m["dtb"] for m in gs])
    out["g_norm_w"] = jnp.stack([m["nw"] for m in gs])
    out["g_out_w"] = jnp.stack([m["wout"] for m in gs])
    return out


_HBM = pl.BlockSpec(memory_space=pltpu.HBM)


def _place():
    x, y, c = lax.axis_index("x"), lax.axis_index("y"), lax.axis_index("c")
    chips = [(1 - x, y), (x, 1 - y), (1 - x, 1 - y)]
    return x, y, c, chips


def gather_shards(tensors):
    n = len(tensors)

    def body(*refs):
        ins, outs = refs[:n], refs[n:2 * n]
        send_sems, recv_sems, loc_sems = refs[2 * n:]
        x, y, c, chips = _place()
        me = 2 * x + y
        local_copies, sends = [], []
        for t in range(n):
            loc = pltpu.make_async_copy(ins[t], outs[t].at[me], loc_sems.at[t])
            loc.start()
            local_copies.append(loc)
            for j, (px, py) in enumerate(chips):
                cp = pltpu.make_async_remote_copy(src_ref=ins[t], dst_ref=outs[t].at[me], send_sem=send_sems.at[3 * t + j],
                                                  recv_sem=recv_sems.at[3 * t + j], device_id=(px, py, c), device_id_type=MESH)
                cp.start()
                sends.append(cp)
        for t in range(n):
            for j, (px, py) in enumerate(chips):
                pltpu.make_async_remote_copy(src_ref=ins[t], dst_ref=outs[t].at[2 * px + py], send_sem=send_sems.at[3 * t + j],
                                             recv_sem=recv_sems.at[3 * t + j], device_id=(px, py, c),
                                             device_id_type=MESH).wait_recv()
        for cp in sends:
            cp.wait_send()
        for cp in local_copies:
            cp.wait()

    return pl.pallas_call(
        body, name="gather_shards", in_specs=[_HBM] * n, out_specs=[_HBM] * n,
        out_shape=[_S((NCHIP,) + a.shape, a.dtype) for a in tensors],
        scratch_shapes=[pltpu.SemaphoreType.DMA((3 * n,)), pltpu.SemaphoreType.DMA((3 * n,)), pltpu.SemaphoreType.DMA((n,))])(*tensors)


def scatter_parts(parts):
    n = len(parts)

    def body(*refs):
        ins, outs = refs[:n], refs[n:2 * n]
        send_sems, recv_sems = refs[2 * n:]
        x, y, c, chips = _place()
        started = []
        for t in range(n):
            for j, (px, py) in enumerate(chips):
                cp = pltpu.make_async_remote_copy(src_ref=ins[t].at[2 * px + py], dst_ref=outs[t].at[j],
                                                  send_sem=send_sems.at[3 * t + j], recv_sem=recv_sems.at[3 * t + j],
                                                  device_id=(px, py, c), device_id_type=MESH)
                cp.start()
                started.append(cp)
        for cp in started:
            cp.wait_recv()
        for cp in started:
            cp.wait_send()

    return pl.pallas_call(
        body, name="scatter_parts", in_specs=[_HBM] * n, out_specs=[_HBM] * n,
        out_shape=[_S((3,) + a.shape[1:], a.dtype) for a in parts],
        scratch_shapes=[pltpu.SemaphoreType.DMA((3 * n,)), pltpu.SemaphoreType.DMA((3 * n,))])(*parts)


def sibling_swap(tensors):
    n = len(tensors)

    def body(*refs):
        ins, outs = refs[:n], refs[n:2 * n]
        send_sems, recv_sems = refs[2 * n:]
        x, y, c, _ = _place()
        started = []
        for t in range(n):
            cp = pltpu.make_async_remote_copy(src_ref=ins[t], dst_ref=outs[t], send_sem=send_sems.at[t], recv_sem=recv_sems.at[t],
                                              device_id=(x, y, 1 - c), device_id_type=MESH)
            cp.start()
            started.append(cp)
        for cp in started:
            cp.wait_recv()
        for cp in started:
            cp.wait_send()

    return pl.pallas_call(
        body, name="sibling_swap", in_specs=[_HBM] * n, out_specs=[_HBM] * n,
        out_shape=[_S(a.shape, a.dtype) for a in tensors],
        scratch_shapes=[pltpu.SemaphoreType.DMA((n,)), pltpu.SemaphoreType.DMA((n,))])(*tensors)


def allreduce_small(v):
    r, n = v.shape

    def body(x_ref, out_ref, gat, send_sems, recv_sems, local_sem):
        x, y, c, chips = _place()
        me, sibling = (x, y, c), (x, y, 1 - c)

        def rows(px, py, pc):
            return gat.at[pl.ds((4 * px + 2 * py + pc) * r, r), :]

        def copy(k, block, to, src=None):
            return pltpu.make_async_remote_copy(src_ref=rows(*block) if src is None else src, dst_ref=rows(*block),
                                                send_sem=send_sems.at[k], recv_sem=recv_sems.at[k], device_id=to,
                                                device_id_type=MESH)

        mine = pltpu.make_async_copy(x_ref, rows(*me), local_sem)
        mine.start()
        first = [copy(0, me, sibling, src=x_ref)] + [copy(1 + j, me, (*chip, c), src=x_ref) for j, chip in enumerate(chips)]
        for cp in first:
            cp.start()
        passed = [copy(4 + j, (*chip, c), sibling) for j, chip in enumerate(chips)]
        for j, chip in enumerate(chips):
            copy(1 + j, (*chip, c), me).wait_recv()
            passed[j].start()
        copy(0, sibling, me).wait_recv()
        for j, chip in enumerate(chips):
            copy(4 + j, (*chip, 1 - c), me).wait_recv()
        for cp in first + passed:
            cp.wait_send()
        mine.wait()
        acc = gat[0:r, :]
        for d in range(1, 8):
            acc = acc + gat[d * r:(d + 1) * r, :]
        out_ref[...] = acc

    vm = pl.BlockSpec(memory_space=pltpu.VMEM)
    return pl.pallas_call(
        body, name="allreduce_small", in_specs=[vm], out_specs=vm, out_shape=_S((r, n), v.dtype),
        scratch_shapes=[pltpu.VMEM((8 * r, n), v.dtype), pltpu.SemaphoreType.DMA((7,)), pltpu.SemaphoreType.DMA((7,)),
                        pltpu.SemaphoreType.DMA],
        compiler_params=_cp())(v)


WEIGHTS = ["ln_mix", "ln_xattn", "ln_mem", "ln_ffn", "final_norm", "m_in_w", "m_conv_w", "m_conv_b", "m_dt_bias", "m_a_log",
           "m_d", "m_norm_w", "m_out_w", "h_in_w", "h_lower_bounds", "h_norm_w", "h_out_w", "g_in_w", "g_conv_w", "g_a_log",
           "g_dt_bias", "g_norm_w", "g_out_w", "xa_q", "xa_kv", "xa_o", "f_up", "f_conv_w", "f_conv_b", "f_down"]
SHARD_AXIS = {"m_in_w": 2, "m_conv_w": 2, "m_conv_b": 1, "m_norm_w": 1, "m_out_w": 1, "h_in_w": 2, "h_out_w": 1, "g_in_w": 2,
              "g_conv_w": 2, "g_out_w": 1, "xa_q": 1, "xa_kv": 2, "xa_o": 1, "f_up": 2, "f_conv_w": 2, "f_down": 1}
MATRICES = ["m_in_w", "m_out_w", "h_in_w", "h_out_w", "g_in_w", "g_out_w", "xa_q", "xa_kv", "xa_o", "f_up", "f_down"]
SMALL_SHARDED = [n for n in WEIGHTS if n in SHARD_AXIS and n not in MATRICES]
REPLICATED = [n for n in WEIGHTS if n not in SHARD_AXIS]
SMALL_ROW = 1024


def _unstack(g, axis):
    g = jnp.moveaxis(g, 0, axis)
    return g.reshape(g.shape[:axis] + (g.shape[axis] * g.shape[axis + 1],) + g.shape[axis + 2:])


def _stack(a, axis):
    a = a.reshape(a.shape[:axis] + (NCHIP, a.shape[axis] // NCHIP) + a.shape[axis + 1:])
    return jnp.moveaxis(a, axis, 0)


def kernel(x, mem, ln_mix, ln_xattn, ln_mem, ln_ffn, final_norm, m_in_w, m_conv_w, m_conv_b, m_dt_bias, m_a_log, m_d, m_norm_w, m_out_w, h_in_w, h_lower_bounds, h_norm_w, h_out_w, g_in_w, g_conv_w, g_a_log, g_dt_bias, g_norm_w, g_out_w, xa_q, xa_kv, xa_o, f_up, f_conv_w, f_conv_b, f_down, loss_target, m_ln_mix, m_ln_xattn, m_ln_mem, m_ln_ffn, m_final_norm, m_m_in_w, m_m_conv_w, m_m_conv_b, m_m_dt_bias, m_m_a_log, m_m_d, m_m_norm_w, m_m_out_w, m_h_in_w, m_h_lower_bounds, m_h_norm_w, m_h_out_w, m_g_in_w, m_g_conv_w, m_g_a_log, m_g_dt_bias, m_g_norm_w, m_g_out_w, m_xa_q, m_xa_kv, m_xa_o, m_f_up, m_f_conv_w, m_f_conv_b, m_f_down, v_ln_mix, v_ln_xattn, v_ln_mem, v_ln_ffn, v_final_norm, v_m_in_w, v_m_conv_w, v_m_conv_b, v_m_dt_bias, v_m_a_log, v_m_d, v_m_norm_w, v_m_out_w, v_h_in_w, v_h_lower_bounds, v_h_norm_w, v_h_out_w, v_g_in_w, v_g_conv_w, v_g_a_log, v_g_dt_bias, v_g_norm_w, v_g_out_w, v_xa_q, v_xa_kv, v_xa_o, v_f_up, v_f_conv_w, v_f_conv_b, v_f_down):
    local = dict(zip(WEIGHTS, (ln_mix, ln_xattn, ln_mem, ln_ffn, final_norm, m_in_w, m_conv_w, m_conv_b, m_dt_bias, m_a_log, m_d, m_norm_w, m_out_w, h_in_w, h_lower_bounds, h_norm_w, h_out_w, g_in_w, g_conv_w, g_a_log, g_dt_bias, g_norm_w, g_out_w, xa_q, xa_kv, xa_o, f_up, f_conv_w, f_conv_b, f_down), strict=True))
    mom_m = dict(zip(WEIGHTS, (m_ln_mix, m_ln_xattn, m_ln_mem, m_ln_ffn, m_final_norm, m_m_in_w, m_m_conv_w, m_m_conv_b, m_m_dt_bias, m_m_a_log, m_m_d, m_m_norm_w, m_m_out_w, m_h_in_w, m_h_lower_bounds, m_h_norm_w, m_h_out_w, m_g_in_w, m_g_conv_w, m_g_a_log, m_g_dt_bias, m_g_norm_w, m_g_out_w, m_xa_q, m_xa_kv, m_xa_o, m_f_up, m_f_conv_w, m_f_conv_b, m_f_down), strict=True))
    mom_v = dict(zip(WEIGHTS, (v_ln_mix, v_ln_xattn, v_ln_mem, v_ln_ffn, v_final_norm, v_m_in_w, v_m_conv_w, v_m_conv_b, v_m_dt_bias, v_m_a_log, v_m_d, v_m_norm_w, v_m_out_w, v_h_in_w, v_h_lower_bounds, v_h_norm_w, v_h_out_w, v_g_in_w, v_g_conv_w, v_g_a_log, v_g_dt_bias, v_g_norm_w, v_g_out_w, v_xa_q, v_xa_kv, v_xa_o, v_f_up, v_f_conv_w, v_f_conv_b, v_f_down), strict=True))
    nb, seq, _ = x.shape
    me = 2 * lax.axis_index("x") + lax.axis_index("y")

    sharded = MATRICES + SMALL_SHARDED
    gathered = gather_shards([local[n].astype(bf16) if n in MATRICES else local[n] for n in sharded])
    full = {n: _unstack(g, SHARD_AXIS[n]) for n, g in zip(sharded, gathered, strict=True)}
    full.update({n: local[n] for n in REPLICATED})

    lb = lower_bounds_fwd(full["h_lower_bounds"])
    layers = prep_layers(full, lb)
    loss, dx, lgrads, dfinal = local_step(x.reshape(nb * seq, D), mem.reshape(nb * N_MEM, D), loss_target.reshape(nb * seq, D),
                                          layers, full["final_norm"].reshape(1, D), nb)
    grads = assemble_grads(lgrads, dfinal, full["h_lower_bounds"])

    small_names = REPLICATED + SMALL_SHARDED
    flat = jnp.concatenate([grads[n].astype(f32).reshape(-1) for n in small_names] + [loss[0, 0:1]])
    rows = -(-flat.shape[0] // (8 * SMALL_ROW)) * 8
    flat = jnp.pad(flat, (0, rows * SMALL_ROW - flat.shape[0])).reshape(rows, SMALL_ROW)
    red = allreduce_small(flat).reshape(-1)
    gsum, off = {}, 0
    for n in small_names:
        size = grads[n].size
        g = red[off:off + size].reshape(grads[n].shape)
        off += size
        if n in SHARD_AXIS:
            ax = SHARD_AXIS[n]
            w = g.shape[ax] // NCHIP
            g = lax.dynamic_slice_in_dim(g, me * w, w, axis=ax)
        gsum[n] = g
    loss_out = red[off]

    parts = [_stack(grads[n], SHARD_AXIS[n]) for n in MATRICES]
    landed = scatter_parts(parts)
    sums = []
    for n, p, l in zip(MATRICES, parts, landed, strict=True):
        own = lax.dynamic_index_in_dim(p, me, axis=0, keepdims=False)
        c = own.shape[-1]
        to2 = lambda a: a.reshape(-1, c)
        r = own.size // c
        s = rows_call(f"sum_{n}", lambda a, b, cc, d: ((a.astype(f32) + b.astype(f32) + cc.astype(f32) + d.astype(f32),), ()),
                      [to2(own), to2(l[0]), to2(l[1]), to2(l[2])], [], [(c, f32)], tm=_tile(r, (256, 128, 64, 32, 16, 8)))[0]
        sums.append(s.reshape(own.shape))
    swapped = sibling_swap(sums)

    outs = {}
    for n, s, o in zip(MATRICES, sums, swapped, strict=True):
        outs[n] = adamw(local[n], s, mom_m[n], mom_v[n], f"adamw_{n}", g2=o)
    for n in small_names:
        outs[n] = adamw(local[n], gsum[n].reshape(local[n].shape), mom_m[n], mom_v[n], f"adamw_{n}")
    res = [loss_out, dx.reshape(nb, seq, D)]
    for k in range(4):
        res += [outs[n][k] for n in WEIGHTS]
    return tuple(res)
```

```python
import functools

import jax
import jax.numpy as jnp
from jax import lax
from jax.experimental import pallas as pl
from jax.experimental.pallas import tpu as pltpu

f32 = jnp.float32
bf16 = jnp.bfloat16
HIGHEST = lax.Precision.HIGHEST
MESH = pl.DeviceIdType.MESH

D = 1024
DEPTH = 4
EPS = 1e-6
N_MEM = 256
M_INNER, M_P, M_H, M_G, M_N, M_Q = 2048, 64, 32, 8, 128, 64
M_CONV = M_INNER + 2 * M_G * M_N
M_MAIN = M_INNER + M_CONV
M_IN = M_MAIN + M_H
H_H, H_K, H_Q = 8, 128, 32
G_HV, G_HK, G_K, G_Q = 16, 8, 128, 64
G_CONV, G_VAL = 4096, 2048
G_MAIN = G_CONV + G_VAL
G_IN = G_MAIN + 2 * G_HV
X_H, X_D = 4, 256
D_FF = 2816
ADAM_LR, ADAM_B1, ADAM_B2, ADAM_EPS, ADAM_WD, ADAM_STEP = 0.001, 0.9, 0.999, 1e-08, 0.01, 10
VMEM_LIMIT = 56 * 1024 * 1024
NCHIP = 4


def _cp(**kw):
    return pltpu.CompilerParams(vmem_limit_bytes=VMEM_LIMIT, **kw)


def _S(shape, dtype):
    return jax.ShapeDtypeStruct(tuple(shape), dtype)


def _dg(a, b, ca, cb, prec=None):
    return lax.dot_general(a, b, (((ca,), (cb,)), ((), ())), precision=prec, preferred_element_type=f32)


def _hdot(a, b, ca=1, cb=0):
    return _dg(a.astype(f32), b.astype(f32), ca, cb, HIGHEST)


def _bdot_raw(a, b, ca, cb):
    return _dg(a.astype(bf16), b.astype(bf16), ca, cb)


@functools.partial(jax.custom_vjp, nondiff_argnums=(2, 3))
def _bdot(a, b, ca, cb):
    return _bdot_raw(a, b, ca, cb)


def _bdot_fwd(a, b, ca, cb):
    return _bdot_raw(a, b, ca, cb), (a, b)


def _bdot_bwd(ca, cb, res, g):
    a, b = res
    if ca == 1:
        da = _bdot_raw(g, b, 1, 1 if cb == 0 else 0)
    else:
        da = _bdot_raw(b, g, 1 if cb == 0 else 0, 1)
    if cb == 0:
        db = _bdot_raw(a, g, 0 if ca == 1 else 1, 0)
    else:
        db = _bdot_raw(g, a, 0, 0 if ca == 1 else 1)
    return da.astype(a.dtype), db.astype(b.dtype)


_bdot.defvjp(_bdot_fwd, _bdot_bwd)


def _shift_down_raw(x, k):
    r = lax.broadcasted_iota(jnp.int32, x.shape, 0)
    return jnp.where(r >= k, pltpu.roll(x, k, 0), 0.0)


def _shift_up_raw(x, k):
    n = x.shape[0]
    r = lax.broadcasted_iota(jnp.int32, x.shape, 0)
    return jnp.where(r < n - k, pltpu.roll(x, n - k, 0), 0.0)


@functools.partial(jax.custom_vjp, nondiff_argnums=(1,))
def _shift_down(x, k):
    return _shift_down_raw(x, k)


_shift_down.defvjp(lambda x, k: (_shift_down_raw(x, k), None), lambda k, _, g: (_shift_up_raw(g, k),))


def _rms(x, w):
    return x * lax.rsqrt(jnp.mean(x * x, axis=-1, keepdims=True) + EPS) * w


def _silu(x):
    return x * jax.nn.sigmoid(x)


def _masks(q):
    r = lax.broadcasted_iota(jnp.int32, (q, q), 0)
    c = lax.broadcasted_iota(jnp.int32, (q, q), 1)
    return r >= c, r > c


def _colvec(row):
    return jnp.transpose(jnp.broadcast_to(row, (8, row.shape[1])))[:, 0:1]


def _tile(n, cands):
    for c in cands:
        if n % c == 0:
            return c
    return n


def mm(a, b, *, ta=False, tb=False, bsel=None, out_stack=None, res=None, out_dtype=f32, name):
    m, k = (a.shape[1], a.shape[0]) if ta else a.shape
    ca, cb = (0 if ta else 1), (1 if tb else 0)
    tm = _tile(m, (512, 256, 128))
    if bsel is not None:
        s0, cnt = bsel
        ns = b.shape[2]
        if tb:
            n, tn, tk = b.shape[1], b.shape[1], ns
            b_spec = pl.BlockSpec((None, tn, ns), lambda i, j, kk: (s0 + kk, j, 0))
        else:
            n, tn, tk = cnt * ns, ns, k
            b_spec = pl.BlockSpec((None, tk, ns), lambda i, j, kk: (s0 + j, kk, 0))
    else:
        n = b.shape[0] if tb else b.shape[1]
        tn = out_stack if out_stack else (n if n <= 2816 else _tile(n, (2048, 1024, 512, 256, 128)))
        tk = k if (k <= 4096 and not ta) else _tile(k, (1024, 512, 256, 128))
        b_spec = pl.BlockSpec((tn, tk), lambda i, j, kk: (j, kk)) if tb else pl.BlockSpec((tk, tn), lambda i, j, kk: (kk, j))
    nk = k // tk
    if out_stack:
        out_spec = pl.BlockSpec((None, tm, tn), lambda i, j, kk: (j, i, 0))
        out_shape = _S((n // tn, m, tn), out_dtype)
    else:
        out_spec = pl.BlockSpec((tm, tn), lambda i, j, kk: (i, j))
        out_shape = _S((m, n), out_dtype)

    def body(*refs):
        if res is None:
            a_ref, b_ref, o_ref, acc = refs
            r_ref = None
        else:
            a_ref, b_ref, r_ref, o_ref, acc = refs
        kk = pl.program_id(2)

        @pl.when(kk == 0)
        def _():
            acc[...] = jnp.zeros_like(acc)

        acc[...] += _bdot_raw(a_ref[...], b_ref[...], ca, cb)

        @pl.when(kk == nk - 1)
        def _():
            v = acc[...]
            if r_ref is not None:
                v = v + r_ref[...]
            o_ref[...] = v.astype(o_ref.dtype)

    a_spec = pl.BlockSpec((tk, tm), lambda i, j, kk: (kk, i)) if ta else pl.BlockSpec((tm, tk), lambda i, j, kk: (i, kk))
    in_specs = [a_spec, b_spec]
    args = [a, b]
    if res is not None:
        in_specs.append(pl.BlockSpec((tm, tn), lambda i, j, kk: (i, j)))
        args.append(res)
    return pl.pallas_call(
        body, name=name, grid=(m // tm, n // tn, nk), in_specs=in_specs, out_specs=out_spec, out_shape=out_shape,
        scratch_shapes=[pltpu.VMEM((tm, tn), f32)], compiler_params=_cp())(*args)


def rows_call(name, fn, rows, pars, row_out, acc_out=(), tm=256):
    t = rows[0].shape[0]
    tm = min(tm, t)
    assert t % tm == 0, (name, t, tm)
    nr, npar, nro = len(rows), len(pars), len(row_out)

    def body(*refs):
        rv = [r[...] for r in refs[:nr]]
        pv = [r[...] for r in refs[nr:nr + npar]]
        ro_refs = refs[nr + npar:nr + npar + nro]
        ao_refs = refs[nr + npar + nro:]
        ro, ao = fn(*rv, *pv)
        for r, v in zip(ro_refs, ro, strict=True):
            r[...] = v.astype(r.dtype)
        if ao_refs:
            @pl.when(pl.program_id(0) == 0)
            def _():
                for r in ao_refs:
                    r[...] = jnp.zeros_like(r)
            for r, v in zip(ao_refs, ao, strict=True):
                r[...] += v.astype(r.dtype)

    in_specs = [pl.BlockSpec((tm, r.shape[1]), lambda i: (i, 0)) for r in rows]
    in_specs += [pl.BlockSpec(p.shape, lambda i: (0, 0)) for p in pars]
    out_specs = [pl.BlockSpec((tm, c), lambda i: (i, 0)) for c, _ in row_out]
    out_specs += [pl.BlockSpec(s, lambda i: (0, 0)) for s, _ in acc_out]
    out_shape = [_S((t, c), dt) for c, dt in row_out] + [_S(s, dt) for s, dt in acc_out]
    return pl.pallas_call(body, name=name, grid=(t // tm,), in_specs=in_specs, out_specs=out_specs,
                          out_shape=out_shape, compiler_params=_cp())(*rows, *pars)


def rms_fwd(x, w, name):
    return rows_call(name, lambda xv, wv: ((_rms(xv, wv),), ()), [x], [w], [(x.shape[1], bf16)])[0]


def rms_bwd(x, w, dy, dres, name):
    def fn(*a):
        if dres is None:
            xv, dyv, wv = a
        else:
            xv, dyv, drv, wv = a
        _, vjp = jax.vjp(_rms, xv, wv)
        dx, dw = vjp(dyv.astype(f32))
        if dres is not None:
            dx = dx + drv
        return (dx,), (dw,)
    rows = [x, dy] + ([] if dres is None else [dres])
    return rows_call(name, fn, rows, [w], [(x.shape[1], f32)], [(w.shape, f32)])


def cols_call(name, fn, seqs, pars, outs, *, nb, ct, ncol, dseed=None):
    ns, npar = len(seqs), len(pars)
    seq_len = seqs[0].shape[0] // nb
    nd = 0 if dseed is None else len(dseed)

    def body(*refs):
        sv = [r[...] for r in refs[:ns]]
        pv = [r[...] for r in refs[ns:ns + npar]]
        if dseed is None:
            o_refs = refs[ns + npar:]
            for r, v in zip(o_refs, fn(*sv, *pv), strict=True):
                r[...] = v.astype(r.dtype)
            return
        dv = [r[...].astype(f32) for r in refs[ns + npar:ns + npar + nd]]
        ds_refs = refs[ns + npar + nd:ns + npar + nd + ns]
        dp_refs = refs[ns + npar + nd + ns:]
        _, vjp = jax.vjp(fn, *[v.astype(f32) for v in sv], *pv)
        g = vjp(tuple(dv))
        for r, v in zip(ds_refs, g[:ns], strict=True):
            r[...] = v.astype(r.dtype)

        @pl.when(pl.program_id(1) == 0)
        def _():
            for r in dp_refs:
                r[...] = jnp.zeros_like(r)
        for r, v in zip(dp_refs, g[ns:], strict=True):
            r[...] += v

    full = pl.BlockSpec((seq_len, ct), lambda j, b: (b, j))
    in_specs = [full for _ in seqs]
    in_specs += [pl.BlockSpec((p.shape[0], ct), lambda j, b: (0, j)) for p in pars]
    args = list(seqs) + list(pars)
    if dseed is None:
        out_specs = [full for _ in outs]
        out_shape = [_S((nb * seq_len, ncol * ct), dt) for dt in outs]
    else:
        in_specs += [full for _ in dseed]
        args += list(dseed)
        out_specs = [full for _ in seqs] + [pl.BlockSpec((p.shape[0], ct), lambda j, b: (0, j)) for p in pars]
        out_shape = [_S((nb * seq_len, ncol * ct), f32) for _ in seqs] + [_S(p.shape, f32) for p in pars]
    return pl.pallas_call(body, name=name, grid=(ncol, nb), in_specs=in_specs, out_specs=out_specs,
                          out_shape=out_shape, compiler_params=_cp())(*args)


def _conv4_silu(x, w, b):
    y = x * w[3:4] + _shift_down(x, 1) * w[2:3] + _shift_down(x, 2) * w[1:2] + _shift_down(x, 3) * w[0:1] + b
    return (_silu(y),)


def _conv4_silu_nobias(x, w):
    y = x * w[3:4] + _shift_down(x, 1) * w[2:3] + _shift_down(x, 2) * w[1:2] + _shift_down(x, 3) * w[0:1]
    return (_silu(y),)


def _ffn_act(gate, up, w, b):
    y = gate * w[2:3] + _shift_down(gate, 1) * w[1:2] + _shift_down(gate, 2) * w[0:1] + b
    return (_silu(y) * up,)


def scan_call(name, chunk_fn, seqs, pars, consts, outs, *, nb, nh, q, state_shape, states=None, dseed=None):
    t = seqs[0][0].shape[0]
    nc = t // (nb * q)
    ns, npar, ncon, no = len(seqs), len(pars), len(consts), len(outs)
    s0, s1 = state_shape
    bwd = dseed is not None

    def cidx(c):
        return (nc - 1 - c) if bwd else c

    def rowblk(b, c):
        return b * nc + cidx(c)

    def seq_spec(w, colfn):
        return pl.BlockSpec((q, w), lambda b, c, h: (rowblk(b, c), colfn(h)))

    def par_spec(shape, idxfn):
        return pl.BlockSpec(shape, lambda b, c, h: idxfn(h))

    st_spec = pl.BlockSpec((s0, s1), lambda b, c, h: ((rowblk(b, c)) * nh + h, 0))
    in_specs = [seq_spec(w, cf) for _, w, cf, _ in seqs]
    in_specs += [par_spec(s, f) for _, s, f in pars] + [par_spec(s, f) for _, s, f in consts]
    args = [a for a, _, _, _ in seqs] + [a for a, _, _ in pars] + [a for a, _, _ in consts]

    if not bwd:
        def body(*refs):
            sv = [r[...] for r in refs[:ns]]
            pv = [r[...] for r in refs[ns:ns + npar]]
            cv = [r[...] for r in refs[ns + npar:ns + npar + ncon]]
            o_refs = refs[ns + npar + ncon:ns + npar + ncon + no]
            save_ref = refs[ns + npar + ncon + no]
            st = refs[-1]
            c, h = pl.program_id(1), pl.program_id(2)

            @pl.when(c == 0)
            def _():
                st[h] = jnp.zeros((s0, s1), f32)
            s_in = st[h]
            save_ref[...] = s_in
            o, s_out = chunk_fn(*sv, *pv, s_in, *cv)
            st[h] = s_out
            for r, v in zip(o_refs, o, strict=True):
                r[...] = v.astype(r.dtype)

        out_specs = [seq_spec(w, cf) for _, w, cf, _ in outs] + [st_spec]
        out_shape = [_S((t, cc), dt) for cc, _, _, dt in outs] + [_S((nb * nc * nh * s0, s1), f32)]
        return pl.pallas_call(body, name=name, grid=(nb, nc, nh), in_specs=in_specs, out_specs=out_specs,
                              out_shape=out_shape, scratch_shapes=[pltpu.VMEM((nh, s0, s1), f32)],
                              compiler_params=_cp())(*args)

    def body(*refs):
        i = 0
        sv = [r[...] for r in refs[i:i + ns]]; i += ns
        pv = [r[...] for r in refs[i:i + npar]]; i += npar
        cv = [r[...] for r in refs[i:i + ncon]]; i += ncon
        dv = [r[...].astype(f32) for r in refs[i:i + no]]; i += no
        s_in = refs[i][...]; i += 1
        ds_refs = refs[i:i + ns]; i += ns
        dp_refs = refs[i:i + npar]; i += npar
        dst = refs[-1]
        b, c, h = pl.program_id(0), pl.program_id(1), pl.program_id(2)

        @pl.when(c == 0)
        def _():
            dst[h] = jnp.zeros((s0, s1), f32)

        @pl.when((b == 0) & (c == 0) & (h == 0))
        def _():
            for r in dp_refs:
                r[...] = jnp.zeros_like(r)

        fn = lambda *a: chunk_fn(*a, *cv)
        _, vjp = jax.vjp(fn, *[v.astype(f32) for v in sv], *pv, s_in)
        g = vjp((tuple(dv), dst[h]))
        dst[h] = g[ns + npar]
        for (_, _, _, rep), r, v in zip(seqs, ds_refs, g[:ns], strict=True):
            if rep == 1:
                r[...] = v.astype(r.dtype)
            else:
                @pl.when(h % rep == 0)
                def _(r=r, v=v):
                    r[...] = v.astype(r.dtype)

                @pl.when(h % rep != 0)
                def _(r=r, v=v):
                    r[...] += v.astype(r.dtype)
        for r, v in zip(dp_refs, g[ns:ns + npar], strict=True):
            r[h] += v

    in_specs += [seq_spec(w, cf) for _, w, cf, _ in outs] + [st_spec]
    args += list(dseed) + [states]
    out_specs = [seq_spec(w, cf) for _, w, cf, _ in seqs]
    out_specs += [pl.BlockSpec((nh,) + tuple(s), lambda b, c, h: (0, 0, 0)) for _, s, _ in pars]
    out_shape = [_S(a.shape, f32) for a, _, _, _ in seqs] + [_S((nh,) + tuple(s), f32) for _, s, _ in pars]
    return pl.pallas_call(body, name=name, grid=(nb, nc, nh), in_specs=in_specs, out_specs=out_specs,
                          out_shape=out_shape, scratch_shapes=[pltpu.VMEM((nh, s0, s1), f32)],
                          compiler_params=_cp())(*args)


def _ssd_group(xs, bm, cm, z, dtr, dtb, alog, dsk, nw, st, e):
    q, gp = xs.shape[0], SSD_GP
    heads = range(4 * gp)
    sl = [slice(i * M_P, (i + 1) * M_P) for i in heads]
    gsl = [slice(g * M_N, (g + 1) * M_N) for g in range(gp)]
    incl, _ = _masks(q)
    dt = jax.nn.softplus(dtr + dtb[0:1])
    dte = _hdot(dt, e)
    ae = _hdot(-jnp.exp(alog), e)[0:1]
    de = _hdot(dsk, e)[0:1]
    xc = xs * dte
    acum = _hdot(incl.astype(f32), dte * ae)
    last = acum[q - 1:q]
    eac, eend, elast = jnp.exp(acum), jnp.exp(last - acum), jnp.exp(last)
    xe = xc * eend
    bms, cms = [bm[:, s] for s in gsl], [cm[:, s] for s in gsl]
    cb = [_bdot(cms[g], bms[g], 1, 1) for g in range(gp)]
    decs = []
    for i in heads:
        a_i = acum[:, sl[i]]
        diff = jnp.where(incl, a_i[:, 0:1] - jnp.transpose(a_i)[0:1, :], 0.0)
        decs.append(jnp.where(incl, jnp.exp(diff), 0.0))
    sts = [st[sl[i], :] for i in heads]
    yd = [_bdot(cb[i // 4] * decs[i], xc[:, sl[i]], 1, 0) for i in heads]
    yo = [_bdot(cms[i // 4], sts[i], 1, 1) for i in heads]
    ds = [_bdot(xe[:, sl[i]], bms[i // 4], 0, 0) for i in heads]
    new = [sts[i] * elast[:, i * M_P:i * M_P + 1] + ds[i] for i in heads]
    y = jnp.concatenate(yd, axis=1) + jnp.concatenate(yo, axis=1) * eac + de * xs
    y = y * _silu(z)
    yn = [_rms(y[:, g * 256:(g + 1) * 256], nw[:, g * 256:(g + 1) * 256]) for g in range(gp)]
    return (jnp.concatenate(yn, axis=1),), jnp.concatenate(new, axis=0)


def _gla_group(qr, fr, ir, gr, lb, nw, st):
    q, hp = qr.shape[0], GLA_HP
    heads = range(hp)
    sl = [slice(i * H_K, (i + 1) * H_K) for i in heads]
    incl, _ = _masks(q)
    fg = lb + (1.0 - lb) * jax.nn.sigmoid(fr)
    qq = _silu(qr) * (H_K ** -0.5)
    k = 1.0 - fg
    gc = _hdot(incl.astype(f32), jnp.log(fg))
    gl = gc[q - 1:q]
    qd, ki, ke = qq * jnp.exp(gc), k * jnp.exp(-gc), k * jnp.exp(gl - gc)
    egl = jnp.exp(gl)
    sts = [st[sl[i], :] for i in heads]
    att = [jnp.where(incl, _bdot(qd[:, sl[i]], ki[:, sl[i]], 1, 1), 0.0) for i in heads]
    o1 = [_bdot(att[i], ir[:, sl[i]], 1, 0) for i in heads]
    o2 = [_bdot(qd[:, sl[i]], sts[i], 1, 0) for i in heads]
    kv = [_bdot(ke[:, sl[i]], ir[:, sl[i]], 0, 0) for i in heads]
    new = [sts[i] * _colvec(egl[:, sl[i]]) + kv[i] for i in heads]
    on = [_rms(o1[i] + o2[i], nw) * _silu(gr[:, sl[i]]) for i in heads]
    return (jnp.concatenate(on, axis=1),), jnp.concatenate(new, axis=0)


def _tri_inv_many(ms):
    n = ms[0].shape[0]
    r = lax.broadcasted_iota(jnp.int32, (n, n), 0)
    c = lax.broadcasted_iota(jnp.int32, (n, n), 1)
    eye = (r == c).astype(f32)
    ts = [eye - m for m in ms]
    ps = list(ms)
    for _ in range(max(1, (n - 1).bit_length() - 1)):
        ps = [_hdot(p, p) for p in ps]
        ts = [t + _hdot(t, p) for t, p in zip(ts, ps)]
    return ts


def _gdn_group(qr, kr, v, z, ba, alog, dtb, nw, st, eb, ea):
    q, hp = qr.shape[0], GDN_HP
    heads = range(hp)
    sl = [slice(i * G_K, (i + 1) * G_K) for i in heads]
    incl, strict = _masks(q)
    beta = _hdot(jax.nn.sigmoid(ba), eb)
    g = _hdot(-jnp.exp(alog[0:1]) * jax.nn.softplus(ba + dtb[0:1]), ea)
    gc = _hdot(incl.astype(f32), g)
    egc = jnp.exp(gc)
    gl = gc[q - 1:q]
    eend = jnp.exp(gl - gc)
    egl = jnp.exp(gl)
    vb = v * beta
    qn, kn = [], []
    for j in range(hp // 2):
        qj, kj = qr[:, sl[j]], kr[:, sl[j]]
        qn.append(qj * lax.rsqrt(jnp.sum(qj * qj, axis=-1, keepdims=True) + EPS) * (G_K ** -0.5))
        kn.append(kj * lax.rsqrt(jnp.sum(kj * kj, axis=-1, keepdims=True) + EPS))
    qk = [_bdot(qn[j], kn[j], 1, 1) for j in range(hp // 2)]
    gcs = [gc[:, sl[i]] for i in heads]
    decs = []
    for i in heads:
        diff = jnp.where(incl, gcs[i][:, 0:1] - jnp.transpose(gcs[i])[0:1, :], 0.0)
        decs.append(jnp.where(incl, jnp.exp(diff), 0.0))
    kbs = [kn[i // 2] * beta[:, sl[i]] for i in heads]
    kk = [_bdot(kbs[i], kn[i // 2], 1, 1) for i in heads]
    tinv = _tri_inv_many([jnp.where(strict, kk[i] * decs[i], 0.0) for i in heads])
    uw = [_hdot(tinv[i], jnp.concatenate([vb[:, sl[i]], kbs[i] * egc[:, sl[i]]], axis=1)) for i in heads]
    sts = [st[sl[i], :] for i in heads]
    ws = [_bdot(jnp.concatenate([uw[i][:, G_K:], qn[i // 2] * egc[:, sl[i]]], axis=0), sts[i], 1, 0) for i in heads]
    v_new = [uw[i][:, :G_K] - ws[i][:q] for i in heads]
    o = [ws[i][q:] + _bdot(qk[i // 2] * decs[i], v_new[i], 1, 0) for i in heads]
    new = [sts[i] * egl[:, i * G_K:i * G_K + 1] + _bdot(kn[i // 2] * eend[:, sl[i]], v_new[i], 0, 0) for i in heads]
    on = [_rms(o[i], nw) * _silu(z[:, sl[i]]) for i in heads]
    return (jnp.concatenate(on, axis=1),), jnp.concatenate(new, axis=0)


def _xattn_fn(q, k, v):
    s = _bdot(q, k, 1, 1) * (X_D ** -0.5)
    return _bdot(jax.nn.softmax(s, axis=-1), v, 1, 0)


def xattn_fwd(q, k, v, nb, name, tl=512):
    t = q.shape[0]
    tl = min(tl, t // nb)
    nl = t // nb // tl

    def body(q_ref, k_ref, v_ref, o_ref):
        o_ref[...] = _xattn_fn(q_ref[...], k_ref[...], v_ref[...]).astype(o_ref.dtype)

    qs = pl.BlockSpec((tl, X_D), lambda b, i, h: (b * nl + i, h))
    ks = pl.BlockSpec((N_MEM, X_D), lambda b, i, h: (b, h))
    return pl.pallas_call(body, name=name, grid=(nb, nl, X_H), in_specs=[qs, ks, ks], out_specs=qs,
                          out_shape=_S(q.shape, bf16), compiler_params=_cp())(q, k, v)


def xattn_bwd(q, k, v, do, nb, name, tl=512):
    t = q.shape[0]
    tl = min(tl, t // nb)
    nl = t // nb // tl

    def body(q_ref, k_ref, v_ref, do_ref, dq_ref, dk_ref, dv_ref):
        _, vjp = jax.vjp(_xattn_fn, q_ref[...], k_ref[...], v_ref[...])
        dq, dk, dv = vjp(do_ref[...].astype(f32))
        dq_ref[...] = dq

        @pl.when(pl.program_id(2) == 0)
        def _():
            dk_ref[...] = jnp.zeros_like(dk_ref)
            dv_ref[...] = jnp.zeros_like(dv_ref)
        dk_ref[...] += dk
        dv_ref[...] += dv

    qs = pl.BlockSpec((tl, X_D), lambda b, h, i: (b * nl + i, h))
    ks = pl.BlockSpec((N_MEM, X_D), lambda b, h, i: (b, h))
    return pl.pallas_call(body, name=name, grid=(nb, X_H, nl), in_specs=[qs, ks, ks, qs], out_specs=[qs, ks, ks],
                          out_shape=[_S(q.shape, f32), _S(k.shape, f32), _S(v.shape, f32)],
                          compiler_params=_cp())(q, k, v, do)


def _lower_bounds(hlb):
    sm = jax.nn.softmax(hlb, axis=0)
    rows, run = [], None
    for r in range(hlb.shape[0]):
        run = sm[r:r + 1] if run is None else run + sm[r:r + 1]
        rows.append(run - sm[0:1])
    return jnp.concatenate(rows, axis=0)


def lower_bounds_fwd(hlb):
    return rows_call("lb_fwd", lambda v: ((_lower_bounds(v),), ()), [hlb], [], [(hlb.shape[1], f32)], tm=hlb.shape[0])[0]


def lower_bounds_bwd(hlb, dlb):
    def fn(v, d):
        _, vjp = jax.vjp(_lower_bounds, v)
        return (vjp(d)[0],), ()
    return rows_call("lb_bwd", fn, [hlb, dlb], [], [(hlb.shape[1], f32)], tm=hlb.shape[0])[0]


def loss_head(x, target, w):
    def fn(xv, tv, wv):
        def loss(xx, ww):
            err = _rms(xx, ww) - tv
            return 0.5 * jnp.sum(jnp.mean(err * err, axis=-1))
        val, (dx, dw) = jax.value_and_grad(loss, argnums=(0, 1))(xv, wv)
        return (dx,), (jnp.broadcast_to(val, (1, 128)), dw)
    dx, loss, dw = rows_call("loss_head", fn, [x, target], [w], [(x.shape[1], f32)], [((1, 128), f32), (w.shape, f32)])
    return dx, loss, dw


def _adamw_fn(w, g, m, v):
    m2 = ADAM_B1 * m + (1.0 - ADAM_B1) * g
    v2 = ADAM_B2 * v + (1.0 - ADAM_B2) * (g * g)
    m_hat = m2 / (1.0 - ADAM_B1 ** ADAM_STEP)
    v_hat = v2 / (1.0 - ADAM_B2 ** ADAM_STEP)
    delta = -ADAM_LR * (m_hat / (jnp.sqrt(v_hat) + ADAM_EPS) + ADAM_WD * w)
    return delta, m2, v2


def adamw(w, g, m, v, name, g2=None):
    shape = w.shape
    c = shape[-1]
    r = w.size // c
    to2 = lambda a: a.reshape(r, c)
    tm = r if r * c * 4 <= (1 << 20) else _tile(r, (256, 128, 64, 32, 16, 8))

    def fn(*a):
        if g2 is None:
            wv, gv, mv, vv = a
        else:
            wv, gv, g2v, mv, vv = a
            gv = gv + g2v
        return (gv,) + _adamw_fn(wv, gv, mv, vv), ()
    rows = [to2(w), to2(g)] + ([] if g2 is None else [to2(g2)]) + [to2(m), to2(v)]
    outs = rows_call(name, fn, rows, [], [(c, f32)] * 4, tm=tm)
    return tuple(o.reshape(shape) for o in outs)


def _expand(first_row, nheads, width):
    r = jnp.arange(128)[:, None]
    c = jnp.arange(nheads * width)[None, :]
    return (r == first_row + c // width).astype(f32)


def _pad_row(v, lane0=0):
    return jnp.pad(v.astype(f32).reshape(1, -1), ((0, 7), (lane0, 128 - lane0 - v.shape[0])))


def _pad_cols(w, n=128):
    return jnp.pad(w, ((0, 0), (0, n - w.shape[1])))


_COL = lambda h: h
_C00 = lambda h: (0, 0)
_CONV_CT = 256


def _conv(name, x, w, b, nb, dseed=None):
    fn = _conv4_silu if b is not None else _conv4_silu_nobias
    pars = [w] + ([] if b is None else [b])
    return cols_call(name, fn, [x], pars, [f32], nb=nb, ct=_CONV_CT, ncol=x.shape[1] // _CONV_CT,
                     dseed=None if dseed is None else [dseed])


SSD_GP, GLA_HP, GDN_HP = 8, 8, 8


def _ssd_scan(name, xs, bm, cm, z, dtr, p, nb, states=None, dseed=None):
    gp, ng = SSD_GP, M_G // SSD_GP
    seqs = [(xs, 256 * gp, _COL, 1), (bm, 128 * gp, _COL, 1), (cm, 128 * gp, _COL, 1), (z, 256 * gp, _COL, 1),
            (dtr, 128, lambda h: 0, ng)]
    pars = [(p["dtb"], (8, 128), _C00), (p["alog"], (8, 128), _C00), (p["dsk"], (8, 128), _C00),
            (p["nw"], (1, 256 * gp), lambda h: (0, h))]
    consts = [(_expand(0, M_H, M_P), (128, 256 * gp), lambda h: (0, h))]
    outs = [(M_INNER, 256 * gp, _COL, bf16)]
    return scan_call(name, _ssd_group, seqs, pars, consts, outs, nb=nb, nh=ng, q=M_Q, state_shape=(gp * 4 * M_P, M_N),
                     states=states, dseed=dseed)


def _gla_scan(name, qr, fr, ir, gr, p, nb, states=None, dseed=None):
    hp, ng = GLA_HP, H_H // GLA_HP
    seqs = [(a, 128 * hp, _COL, 1) for a in (qr, fr, ir, gr)]
    pars = [(p["lb"], (1, 128 * hp), lambda h: (0, h)), (p["nw"], (1, 128), _C00)]
    outs = [(D, 128 * hp, _COL, bf16)]
    return scan_call(name, _gla_group, seqs, pars, [], outs, nb=nb, nh=ng, q=H_Q, state_shape=(hp * H_K, H_K),
                     states=states, dseed=dseed)


def _gdn_scan(name, qc, kc, vc, z, ba, p, nb, states=None, dseed=None):
    hp, ng = GDN_HP, G_HV // GDN_HP
    seqs = [(qc, 64 * hp, _COL, 1), (kc, 64 * hp, _COL, 1), (vc, 128 * hp, _COL, 1), (z, 128 * hp, _COL, 1),
            (ba, 128, lambda h: 0, ng)]
    pars = [(p["alog"], (8, 128), _C00), (p["dtb"], (8, 128), _C00), (p["nw"], (1, 128), _C00)]
    consts = [(_expand(0, G_HV, G_K), (128, 128 * hp), lambda h: (0, h)),
              (_expand(G_HV, G_HV, G_K), (128, 128 * hp), lambda h: (0, h))]
    outs = [(G_VAL, 128 * hp, _COL, bf16)]
    return scan_call(name, _gdn_group, seqs, pars, consts, outs, nb=nb, nh=ng, q=G_Q, state_shape=(hp * G_K, G_K),
                     states=states, dseed=dseed)


def _w(wt):
    return wt if isinstance(wt, tuple) else (wt, None)


def _proj(a, wt, name, res=None):
    arr, bsel = _w(wt)
    return mm(a, arr, bsel=bsel, res=res, name=name)


def _proj_bwd(tag, hn, pieces):
    dhn, dws = None, []
    for i, (d, wt) in enumerate(pieces):
        arr, bsel = _w(wt)
        dws.append(mm(hn, d, ta=True, out_stack=arr.shape[2] if bsel else None, out_dtype=bf16, name=f"{tag}_dw{i}"))
        dhn = mm(d, arr, tb=True, bsel=bsel, res=dhn, name=f"{tag}_dh{i}")
    return dhn, dws


def ssd_mixer_fwd(tag, hn, w, nb):
    z, xr, br, cr, dtr = (_proj(hn, w[k], f"{tag}_in_{k}") for k in ("wz", "wx", "wb", "wc", "wdt"))
    xs = _conv(f"{tag}_convx", xr, w["cwx"], w["cbx"], nb)[0]
    bm = _conv(f"{tag}_convb", br, w["cwb"], w["cbb"], nb)[0]
    cm = _conv(f"{tag}_convc", cr, w["cwc"], w["cbc"], nb)[0]
    yn, states = _ssd_scan(f"{tag}_scan", xs, bm, cm, z, dtr, w, nb)
    return yn, (hn, z, xr, br, cr, dtr, xs, bm, cm, yn, states)


def ssd_mixer_bwd(tag, saved, dout, w, nb):
    hn, z, xr, br, cr, dtr, xs, bm, cm, yn, states = saved
    g = {"wout": mm(yn, dout, ta=True, out_dtype=bf16, name=f"{tag}_dwout")}
    dyn = mm(dout, w["wout"], tb=True, name=f"{tag}_dyn")
    dxs, dbm, dcm, dz, ddtr, ddtb, dalog, ddsk, dnw = _ssd_scan(f"{tag}_scanb", xs, bm, cm, z, dtr, w, nb, states, [dyn])
    dxr, g["cwx"], g["cbx"] = _conv(f"{tag}_convxb", xr, w["cwx"], w["cbx"], nb, dxs)
    dbr, g["cwb"], g["cbb"] = _conv(f"{tag}_convbb", br, w["cwb"], w["cbb"], nb, dbm)
    dcr, g["cwc"], g["cbc"] = _conv(f"{tag}_convcb", cr, w["cwc"], w["cbc"], nb, dcm)
    dhn, (g["wz"], g["wx"], g["wb"], g["wc"], g["wdt"]) = _proj_bwd(
        tag, hn, [(dz, w["wz"]), (dxr, w["wx"]), (dbr, w["wb"]), (dcr, w["wc"]), (ddtr, w["wdt"])])
    g["dtb"], g["alog"], g["dsk"] = (jnp.sum(a, axis=0)[0, :M_H] for a in (ddtb, dalog, ddsk))
    g["nw"] = dnw.reshape(M_INNER)
    return dhn, g


def gla_mixer_fwd(tag, hn, w, nb):
    qr, fr, ir, gr = (_proj(hn, w[k], f"{tag}_in_{k}") for k in ("wq", "wf", "wi", "wg"))
    on, states = _gla_scan(f"{tag}_scan", qr, fr, ir, gr, w, nb)
    return on, (hn, qr, fr, ir, gr, on, states)


def gla_mixer_bwd(tag, saved, dout, w, nb):
    hn, qr, fr, ir, gr, on, states = saved
    g = {"wout": mm(on, dout, ta=True, out_dtype=bf16, name=f"{tag}_dwout")}
    don = mm(dout, w["wout"], tb=True, name=f"{tag}_don")
    dq, df, di, dg, dlb, dnw = _gla_scan(f"{tag}_scanb", qr, fr, ir, gr, w, nb, states, [don])
    dhn, (g["wq"], g["wf"], g["wi"], g["wg"]) = _proj_bwd(tag, hn, [(dq, w["wq"]), (df, w["wf"]), (di, w["wi"]), (dg, w["wg"])])
    g["lb"] = dlb.reshape(1, D)
    g["nw"] = jnp.sum(dnw, axis=0).reshape(H_K)
    return dhn, g


def gdn_mixer_fwd(tag, hn, w, nb):
    qr, kr, vr, z, ba = (_proj(hn, w[k], f"{tag}_in_{k}") for k in ("wq", "wk", "wv", "wz", "wba"))
    qc = _conv(f"{tag}_convq", qr, w["cwq"], None, nb)[0]
    kc = _conv(f"{tag}_convk", kr, w["cwk"], None, nb)[0]
    vc = _conv(f"{tag}_convv", vr, w["cwv"], None, nb)[0]
    on, states = _gdn_scan(f"{tag}_scan", qc, kc, vc, z, ba, w, nb)
    return on, (hn, qr, kr, vr, z, ba, qc, kc, vc, on, states)


def gdn_mixer_bwd(tag, saved, dout, w, nb):
    hn, qr, kr, vr, z, ba, qc, kc, vc, on, states = saved
    g = {"wout": mm(on, dout, ta=True, out_dtype=bf16, name=f"{tag}_dwout")}
    don = mm(dout, w["wout"], tb=True, name=f"{tag}_don")
    dqc, dkc, dvc, dz, dba, dalog, ddtb, dnw = _gdn_scan(f"{tag}_scanb", qc, kc, vc, z, ba, w, nb, states, [don])
    dqr, g["cwq"] = _conv(f"{tag}_convqb", qr, w["cwq"], None, nb, dqc)
    dkr, g["cwk"] = _conv(f"{tag}_convkb", kr, w["cwk"], None, nb, dkc)
    dvr, g["cwv"] = _conv(f"{tag}_convvb", vr, w["cwv"], None, nb, dvc)
    dhn, (g["wq"], g["wk"], g["wv"], g["wz"], g["wba"]) = _proj_bwd(
        tag, hn, [(dqr, w["wq"]), (dkr, w["wk"]), (dvr, w["wv"]), (dz, w["wz"]), (dba, w["wba"])])
    g["alog"], g["dtb"] = (jnp.sum(a, axis=0)[0, G_HV:2 * G_HV] for a in (dalog, ddtb))
    g["nw"] = jnp.sum(dnw, axis=0).reshape(G_K)
    return dhn, g


_MIXERS = {0: (ssd_mixer_fwd, ssd_mixer_bwd), 1: (gla_mixer_fwd, gla_mixer_bwd), 2: (gdn_mixer_fwd, gdn_mixer_bwd)}


def layer_fwd(i, x, mem, w, nb):
    t = f"l{i}"
    hn = rms_fwd(x, w["ln_mix"], f"{t}_ln_mix")
    mix, s_mix = _MIXERS[i % 3][0](f"{t}_mix", hn, w["mix"], nb)
    x1 = mm(mix, w["mix"]["wout"], res=x, name=f"{t}_mix_out")
    hx = rms_fwd(x1, w["ln_xattn"], f"{t}_ln_xattn")
    mn = rms_fwd(mem, w["ln_mem"], f"{t}_ln_mem")
    q = _proj(hx, w["xq"], f"{t}_xa_q")
    k = _proj(mn, w["xk"], f"{t}_xa_k")
    v = _proj(mn, w["xv"], f"{t}_xa_v")
    o = xattn_fwd(q, k, v, nb, f"{t}_xattn")
    x2 = mm(o, w["xo"], res=x1, name=f"{t}_xa_o")
    hf = rms_fwd(x2, w["ln_ffn"], f"{t}_ln_ffn")
    gate = _proj(hf, w["fg"], f"{t}_ffn_gate")
    up = _proj(hf, w["fu"], f"{t}_ffn_up")
    act = cols_call(f"{t}_ffn_act", _ffn_act, [gate, up], [w["fcw"], w["fcb"]], [bf16], nb=nb, ct=_CONV_CT,
                    ncol=D_FF // _CONV_CT)[0]
    x3 = mm(act, w["fd"], res=x2, name=f"{t}_ffn_down")
    return x3, (x, s_mix, x1, hx, mn, q, k, v, o, x2, hf, gate, up, act)


def layer_bwd(i, saved, dx, mem, w, nb):
    t = f"l{i}b"
    x, s_mix, x1, hx, mn, q, k, v, o, x2, hf, gate, up, act = saved
    g = {}
    g["fd"] = mm(act, dx, ta=True, out_dtype=bf16, name=f"{t}_dwd")
    dact = mm(dx, w["fd"], tb=True, name=f"{t}_dact")
    dgate, dup, g["fcw"], g["fcb"] = cols_call(f"{t}_ffn_act", _ffn_act, [gate, up], [w["fcw"], w["fcb"]], [bf16], nb=nb,
                                               ct=_CONV_CT, ncol=D_FF // _CONV_CT, dseed=[dact])
    dhf, (g["fg"], g["fu"]) = _proj_bwd(f"{t}_ffn", hf, [(dgate, w["fg"]), (dup, w["fu"])])
    dx, g["ln_ffn"] = rms_bwd(x2, w["ln_ffn"], dhf, dx, f"{t}_ln_ffn")
    g["xo"] = mm(o, dx, ta=True, out_dtype=bf16, name=f"{t}_dwo")
    do = mm(dx, w["xo"], tb=True, name=f"{t}_do")
    dq, dk, dv = xattn_bwd(q, k, v, do, nb, f"{t}_xattn")
    dhx, (g["xq"],) = _proj_bwd(f"{t}_xq", hx, [(dq, w["xq"])])
    dmn, (g["xk"], g["xv"]) = _proj_bwd(f"{t}_xkv", mn, [(dk, w["xk"]), (dv, w["xv"])])
    _, g["ln_mem"] = rms_bwd(mem, w["ln_mem"], dmn, None, f"{t}_ln_mem")
    dx, g["ln_xattn"] = rms_bwd(x1, w["ln_xattn"], dhx, dx, f"{t}_ln_xattn")
    dhn, g["mix"] = _MIXERS[i % 3][1](f"{t}_mix", s_mix, dx, w["mix"], nb)
    dx, g["ln_mix"] = rms_bwd(x, w["ln_mix"], dhn, dx, f"{t}_ln_mix")
    return dx, g


def local_step(x, mem, target, layers, final_norm, nb):
    saved = []
    for i, w in enumerate(layers):
        x, s = layer_fwd(i, x, mem, w, nb)
        saved.append(s)
    dx, loss, dfinal = loss_head(x, target, final_norm)
    grads = [None] * len(layers)
    for i in reversed(range(len(layers))):
        dx, grads[i] = layer_bwd(i, saved[i], dx, mem, layers[i], nb)
    return loss, dx, grads, dfinal


WEIGHTS = ["ln_mix", "ln_xattn", "ln_mem", "ln_ffn", "final_norm", "m_in_w", "m_conv_w", "m_conv_b", "m_dt_bias", "m_a_log",
           "m_d", "m_norm_w", "m_out_w", "h_in_w", "h_lower_bounds", "h_norm_w", "h_out_w", "g_in_w", "g_conv_w", "g_a_log",
           "g_dt_bias", "g_norm_w", "g_out_w", "xa_q", "xa_kv", "xa_o", "f_up", "f_conv_w", "f_conv_b", "f_down"]
SHARD_AXIS = {"m_in_w": 2, "m_conv_w": 2, "m_conv_b": 1, "m_norm_w": 1, "m_out_w": 1, "h_in_w": 2, "h_out_w": 1, "g_in_w": 2,
              "g_conv_w": 2, "g_out_w": 1, "xa_q": 1, "xa_kv": 2, "xa_o": 1, "f_up": 2, "f_conv_w": 2, "f_down": 1}
MATRICES = ["m_in_w", "m_out_w", "h_in_w", "h_out_w", "g_in_w", "g_out_w", "xa_q", "xa_kv", "xa_o", "f_up", "f_down"]
SMALL_SHARDED = [n for n in WEIGHTS if n in SHARD_AXIS and n not in MATRICES]
REPLICATED = [n for n in WEIGHTS if n not in SHARD_AXIS]
_MIXER_PREFIX = {0: "m", 1: "h", 2: "g"}


def layer_weight_names(i):
    p = _MIXER_PREFIX[i % 3]
    k = i // 3
    return [(n, k) for n in WEIGHTS if n in SHARD_AXIS and n.startswith(p + "_")] + \
           [(n, i) for n in ("xa_q", "xa_kv", "xa_o", "f_up", "f_conv_w", "f_down")]


def _cols(st, lo, hi):
    ns = st.shape[-1]
    parts = []
    for j in range(NCHIP):
        a, b = max(lo, j * ns), min(hi, (j + 1) * ns)
        if a < b:
            parts.append(st[j][..., a - j * ns:b - j * ns])
    return parts[0] if len(parts) == 1 else jnp.concatenate(parts, axis=-1)


def _col_shards(pieces, ns):
    full = jnp.concatenate(pieces, axis=-1)
    return [full[..., j * ns:(j + 1) * ns] for j in range(NCHIP)]


def _rows(st):
    return st.reshape(st.shape[0] * st.shape[1], st.shape[2])


def prep_layer(i, G, R, lb):
    row = lambda a: a.reshape(1, -1)
    p, k = _MIXER_PREFIX[i % 3], i // 3
    kv, fup = G["xa_kv"], G["f_up"]
    layer = dict(ln_mix=R["ln_mix"][i:i + 1], ln_xattn=R["ln_xattn"][i:i + 1], ln_mem=R["ln_mem"][i:i + 1],
                 ln_ffn=R["ln_ffn"][i:i + 1], xq=_rows(G["xa_q"]), xk=(kv, (0, 2)), xv=(kv, (2, 2)), xo=_rows(G["xa_o"]),
                 fg=(fup, (0, 2)), fu=(fup, (2, 2)), fcw=_cols(G["f_conv_w"], 0, D_FF), fcb=R["f_conv_b"][i:i + 1],
                 fd=_rows(G["f_down"]))
    inw, wout = G[p + "_in_w"], _rows(G[p + "_out_w"])
    if p == "m":
        cw, cb = G["m_conv_w"], G["m_conv_b"]
        a, b, c = M_INNER, M_INNER + M_G * M_N, M_CONV
        layer["mix"] = dict(
            wz=_cols(inw, 0, M_INNER), wx=_cols(inw, M_INNER, M_INNER + a), wb=_cols(inw, M_INNER + a, M_INNER + b),
            wc=_cols(inw, M_INNER + b, M_MAIN), wdt=_pad_cols(_cols(inw, M_MAIN, M_IN)),
            cwx=_cols(cw, 0, a), cwb=_cols(cw, a, b), cwc=_cols(cw, b, c),
            cbx=row(_cols(cb, 0, a)), cbb=row(_cols(cb, a, b)), cbc=row(_cols(cb, b, c)),
            dtb=_pad_row(R["m_dt_bias"][k]), alog=_pad_row(R["m_a_log"][k]), dsk=_pad_row(R["m_d"][k]),
            nw=row(_cols(G["m_norm_w"], 0, M_INNER)), wout=wout)
    elif p == "h":
        layer["mix"] = dict(wq=(inw, (0, 1)), wf=(inw, (1, 1)), wi=(inw, (2, 1)), wg=(inw, (3, 1)),
                            lb=lb[i:i + 1], nw=row(R["h_norm_w"][k]), wout=wout)
    else:
        cw = G["g_conv_w"]
        layer["mix"] = dict(
            wq=_cols(inw, 0, D), wk=_cols(inw, D, 2 * D), wv=_cols(inw, 2 * D, G_CONV), wz=_cols(inw, G_CONV, G_MAIN),
            wba=_pad_cols(_cols(inw, G_MAIN, G_IN)), cwq=_cols(cw, 0, D), cwk=_cols(cw, D, 2 * D), cwv=_cols(cw, 2 * D, G_CONV),
            alog=_pad_row(R["g_a_log"][k], G_HV), dtb=_pad_row(R["g_dt_bias"][k], G_HV),
            nw=row(R["g_norm_w"][k]), wout=wout)
    return layer


def matrix_grad_parts(i, g):
    p = _MIXER_PREFIX[i % 3]
    m = g["mix"]
    by_rows = lambda a: [(a.reshape(NCHIP, a.shape[0] // NCHIP, a.shape[1]), j) for j in range(NCHIP)]
    out = {"xa_q": by_rows(g["xq"]), "xa_kv": [(g["xk"], 0), (g["xk"], 1), (g["xv"], 0), (g["xv"], 1)], "xa_o": by_rows(g["xo"]),
           "f_up": [(g["fg"], 0), (g["fg"], 1), (g["fu"], 0), (g["fu"], 1)], "f_down": by_rows(g["fd"]),
           p + "_out_w": by_rows(m["wout"])}
    if p == "m":
        out["m_in_w"] = [(a, None) for a in _col_shards([m["wz"], m["wx"], m["wb"], m["wc"], m["wdt"]], M_IN // NCHIP)]
    elif p == "h":
        out["h_in_w"] = [(m["wq"], 0), (m["wf"], 0), (m["wi"], 0), (m["wg"], 0)]
    else:
        out["g_in_w"] = [(a, None) for a in _col_shards([m["wq"], m["wk"], m["wv"], m["wz"], m["wba"]], G_IN // NCHIP)]
    return out


def small_grads(grads, dfinal, hlb):
    cat = lambda xs: jnp.concatenate(xs, axis=1)
    out = {k: jnp.concatenate([g[k] for g in grads], axis=0) for k in ("ln_mix", "ln_xattn", "ln_mem", "ln_ffn")}
    out["final_norm"] = dfinal.reshape(D)
    out["f_conv_w"] = jnp.stack([g["fcw"] for g in grads])
    out["f_conv_b"] = jnp.concatenate([g["fcb"] for g in grads], axis=0)
    ms = [g["mix"] for i, g in enumerate(grads) if i % 3 == 0]
    out["m_conv_w"] = jnp.stack([cat([m["cwx"], m["cwb"], m["cwc"]]) for m in ms])
    out["m_conv_b"] = jnp.concatenate([cat([m["cbx"], m["cbb"], m["cbc"]]) for m in ms], axis=0)
    out["m_dt_bias"] = jnp.stack([m["dtb"] for m in ms])
    out["m_a_log"] = jnp.stack([m["alog"] for m in ms])
    out["m_d"] = jnp.stack([m["dsk"] for m in ms])
    out["m_norm_w"] = jnp.stack([m["nw"] for m in ms])
    hs = [(i, g["mix"]) for i, g in enumerate(grads) if i % 3 == 1]
    lb_rows = dict(hs)
    dlb = jnp.concatenate([lb_rows[i]["lb"] if i in lb_rows else jnp.zeros((1, D), f32) for i in range(DEPTH)], axis=0)
    out["h_lower_bounds"] = lower_bounds_bwd(hlb, dlb)
    out["h_norm_w"] = jnp.stack([m["nw"] for _, m in hs])
    gs = [g["mix"] for i, g in enumerate(grads) if i % 3 == 2]
    out["g_conv_w"] = jnp.stack([cat([m["cwq"], m["cwk"], m["cwv"]]) for m in gs])
    out["g_a_log"] = jnp.stack([m["alog"] for m in gs])
    out["g_dt_bias"] = jnp.stack([m["dtb"] for m in gs])
    out["g_norm_w"] = jnp.stack([m["nw"] for m in gs])
    return out


_HBM = pl.BlockSpec(memory_space=pltpu.HBM)


def _place():
    x, y, c = lax.axis_index("x"), lax.axis_index("y"), lax.axis_index("c")
    chips = [(1 - x, y), (x, 1 - y), (1 - x, 1 - y)]
    return x, y, c, chips


def gather_shards(name, tensors):
    n = len(tensors)

    def body(*refs):
        ins, outs = refs[:n], refs[n:2 * n]
        send_sems, recv_sems, loc_sems = refs[2 * n:]
        x, y, c, chips = _place()
        me = 2 * x + y
        local_copies, sends = [], []
        for t in range(n):
            loc = pltpu.make_async_copy(ins[t], outs[t].at[me], loc_sems.at[t])
            loc.start()
            local_copies.append(loc)
            for j, (px, py) in enumerate(chips):
                cp = pltpu.make_async_remote_copy(src_ref=ins[t], dst_ref=outs[t].at[me], send_sem=send_sems.at[3 * t + j],
                                                  recv_sem=recv_sems.at[3 * t + j], device_id=(px, py, c), device_id_type=MESH)
                cp.start()
                sends.append(cp)
        for t in range(n):
            for j, (px, py) in enumerate(chips):
                pltpu.make_async_remote_copy(src_ref=ins[t], dst_ref=outs[t].at[2 * px + py], send_sem=send_sems.at[3 * t + j],
                                             recv_sem=recv_sems.at[3 * t + j], device_id=(px, py, c),
                                             device_id_type=MESH).wait_recv()
        for cp in sends:
            cp.wait_send()
        for cp in local_copies:
            cp.wait()

    return pl.pallas_call(
        body, name=name, in_specs=[_HBM] * n, out_specs=[_HBM] * n,
        out_shape=[_S((NCHIP,) + a.shape, a.dtype) for a in tensors],
        scratch_shapes=[pltpu.SemaphoreType.DMA((3 * n,)), pltpu.SemaphoreType.DMA((3 * n,)), pltpu.SemaphoreType.DMA((n,))])(*tensors)


def scatter_parts(name, parts):
    n = len(parts)
    arrs, pos = [], {}
    for srcs in parts:
        for a, _ in srcs:
            if id(a) not in pos:
                pos[id(a)] = len(arrs)
                arrs.append(a)
    na = len(arrs)
    shapes = [(a.shape if idx is None else a.shape[1:], a.dtype) for (a, idx) in (srcs[0] for srcs in parts)]

    def body(*refs):
        ins, outs = refs[:na], refs[na:na + n]
        send_sems, recv_sems, loc_sems = refs[na + n:]
        x, y, c, chips = _place()
        me = 2 * x + y

        def src(t, s):
            a, idx = parts[t][s]
            r = ins[pos[id(a)]]
            return r if idx is None else r.at[idx]

        def remote(t, s, j):
            return pltpu.make_async_remote_copy(src_ref=src(t, s), dst_ref=outs[t].at[j], send_sem=send_sems.at[3 * t + j],
                                                recv_sem=recv_sems.at[3 * t + j], device_id=(*chips[j], c), device_id_type=MESH)

        def own(t, s):
            return pltpu.make_async_copy(src(t, s), outs[t].at[3], loc_sems.at[t])

        for t in range(n):
            for s in range(NCHIP):
                for j, (px, py) in enumerate(chips):
                    @pl.when(2 * px + py == s)
                    def _(t=t, s=s, j=j):
                        remote(t, s, j).start()

                @pl.when(me == s)
                def _(t=t, s=s):
                    own(t, s).start()
        for t in range(n):
            for j in range(3):
                remote(t, 0, j).wait_recv()
        for t in range(n):
            for j in range(3):
                remote(t, 0, j).wait_send()
            own(t, 0).wait()

    return pl.pallas_call(
        body, name=name, in_specs=[_HBM] * na, out_specs=[_HBM] * n,
        out_shape=[_S((NCHIP,) + tuple(s), dt) for s, dt in shapes],
        scratch_shapes=[pltpu.SemaphoreType.DMA((3 * n,)), pltpu.SemaphoreType.DMA((3 * n,)), pltpu.SemaphoreType.DMA((n,))])(*arrs)


def sum_lead(name, land):
    shape = land.shape[1:]
    c = shape[-1]
    r = land.size // (NCHIP * c)
    tm = _tile(r, (256, 128, 64, 32, 16, 8))

    def body(l_ref, o_ref):
        o_ref[...] = l_ref[3].astype(f32) + l_ref[0].astype(f32) + l_ref[1].astype(f32) + l_ref[2].astype(f32)

    out = pl.pallas_call(body, name=name, grid=(r // tm,), in_specs=[pl.BlockSpec((NCHIP, tm, c), lambda i: (0, i, 0))],
                         out_specs=pl.BlockSpec((tm, c), lambda i: (i, 0)), out_shape=_S((r, c), f32),
                         compiler_params=_cp())(land.reshape(NCHIP, r, c))
    return out.reshape(shape)


def sibling_swap(tensors):
    n = len(tensors)

    def body(*refs):
        ins, outs = refs[:n], refs[n:2 * n]
        send_sems, recv_sems = refs[2 * n:]
        x, y, c, _ = _place()
        started = []
        for t in range(n):
            cp = pltpu.make_async_remote_copy(src_ref=ins[t], dst_ref=outs[t], send_sem=send_sems.at[t], recv_sem=recv_sems.at[t],
                                              device_id=(x, y, 1 - c), device_id_type=MESH)
            cp.start()
            started.append(cp)
        for cp in started:
            cp.wait_recv()
        for cp in started:
            cp.wait_send()

    return pl.pallas_call(
        body, name="sibling_swap", in_specs=[_HBM] * n, out_specs=[_HBM] * n,
        out_shape=[_S(a.shape, a.dtype) for a in tensors],
        scratch_shapes=[pltpu.SemaphoreType.DMA((n,)), pltpu.SemaphoreType.DMA((n,))])(*tensors)


def allreduce_small(v):
    r, n = v.shape

    def body(x_ref, out_ref, gat, send_sems, recv_sems, local_sem):
        x, y, c, chips = _place()
        me, sibling = (x, y, c), (x, y, 1 - c)

        def rows(px, py, pc):
            return gat.at[pl.ds((4 * px + 2 * py + pc) * r, r), :]

        def copy(k, block, to, src=None):
            return pltpu.make_async_remote_copy(src_ref=rows(*block) if src is None else src, dst_ref=rows(*block),
                                                send_sem=send_sems.at[k], recv_sem=recv_sems.at[k], device_id=to,
                                                device_id_type=MESH)

        mine = pltpu.make_async_copy(x_ref, rows(*me), local_sem)
        mine.start()
        first = [copy(0, me, sibling, src=x_ref)] + [copy(1 + j, me, (*chip, c), src=x_ref) for j, chip in enumerate(chips)]
        for cp in first:
            cp.start()
        passed = [copy(4 + j, (*chip, c), sibling) for j, chip in enumerate(chips)]
        for j, chip in enumerate(chips):
            copy(1 + j, (*chip, c), me).wait_recv()
            passed[j].start()
        copy(0, sibling, me).wait_recv()
        for j, chip in enumerate(chips):
            copy(4 + j, (*chip, 1 - c), me).wait_recv()
        for cp in first + passed:
            cp.wait_send()
        mine.wait()
        acc = gat[0:r, :]
        for d in range(1, 8):
            acc = acc + gat[d * r:(d + 1) * r, :]
        out_ref[...] = acc

    vm = pl.BlockSpec(memory_space=pltpu.VMEM)
    return pl.pallas_call(
        body, name="allreduce_small", in_specs=[vm], out_specs=vm, out_shape=_S((r, n), v.dtype),
        scratch_shapes=[pltpu.VMEM((8 * r, n), v.dtype), pltpu.SemaphoreType.DMA((7,)), pltpu.SemaphoreType.DMA((7,)),
                        pltpu.SemaphoreType.DMA],
        compiler_params=_cp())(v)


SMALL_ROW = 1024


def kernel(x, mem, ln_mix, ln_xattn, ln_mem, ln_ffn, final_norm, m_in_w, m_conv_w, m_conv_b, m_dt_bias, m_a_log, m_d, m_norm_w, m_out_w, h_in_w, h_lower_bounds, h_norm_w, h_out_w, g_in_w, g_conv_w, g_a_log, g_dt_bias, g_norm_w, g_out_w, xa_q, xa_kv, xa_o, f_up, f_conv_w, f_conv_b, f_down, loss_target, m_ln_mix, m_ln_xattn, m_ln_mem, m_ln_ffn, m_final_norm, m_m_in_w, m_m_conv_w, m_m_conv_b, m_m_dt_bias, m_m_a_log, m_m_d, m_m_norm_w, m_m_out_w, m_h_in_w, m_h_lower_bounds, m_h_norm_w, m_h_out_w, m_g_in_w, m_g_conv_w, m_g_a_log, m_g_dt_bias, m_g_norm_w, m_g_out_w, m_xa_q, m_xa_kv, m_xa_o, m_f_up, m_f_conv_w, m_f_conv_b, m_f_down, v_ln_mix, v_ln_xattn, v_ln_mem, v_ln_ffn, v_final_norm, v_m_in_w, v_m_conv_w, v_m_conv_b, v_m_dt_bias, v_m_a_log, v_m_d, v_m_norm_w, v_m_out_w, v_h_in_w, v_h_lower_bounds, v_h_norm_w, v_h_out_w, v_g_in_w, v_g_conv_w, v_g_a_log, v_g_dt_bias, v_g_norm_w, v_g_out_w, v_xa_q, v_xa_kv, v_xa_o, v_f_up, v_f_conv_w, v_f_conv_b, v_f_down):
    local = dict(zip(WEIGHTS, (ln_mix, ln_xattn, ln_mem, ln_ffn, final_norm, m_in_w, m_conv_w, m_conv_b, m_dt_bias, m_a_log, m_d, m_norm_w, m_out_w, h_in_w, h_lower_bounds, h_norm_w, h_out_w, g_in_w, g_conv_w, g_a_log, g_dt_bias, g_norm_w, g_out_w, xa_q, xa_kv, xa_o, f_up, f_conv_w, f_conv_b, f_down), strict=True))
    mom_m = dict(zip(WEIGHTS, (m_ln_mix, m_ln_xattn, m_ln_mem, m_ln_ffn, m_final_norm, m_m_in_w, m_m_conv_w, m_m_conv_b, m_m_dt_bias, m_m_a_log, m_m_d, m_m_norm_w, m_m_out_w, m_h_in_w, m_h_lower_bounds, m_h_norm_w, m_h_out_w, m_g_in_w, m_g_conv_w, m_g_a_log, m_g_dt_bias, m_g_norm_w, m_g_out_w, m_xa_q, m_xa_kv, m_xa_o, m_f_up, m_f_conv_w, m_f_conv_b, m_f_down), strict=True))
    mom_v = dict(zip(WEIGHTS, (v_ln_mix, v_ln_xattn, v_ln_mem, v_ln_ffn, v_final_norm, v_m_in_w, v_m_conv_w, v_m_conv_b, v_m_dt_bias, v_m_a_log, v_m_d, v_m_norm_w, v_m_out_w, v_h_in_w, v_h_lower_bounds, v_h_norm_w, v_h_out_w, v_g_in_w, v_g_conv_w, v_g_a_log, v_g_dt_bias, v_g_norm_w, v_g_out_w, v_xa_q, v_xa_kv, v_xa_o, v_f_up, v_f_conv_w, v_f_conv_b, v_f_down), strict=True))
    nb, seq, _ = x.shape
    me = 2 * lax.axis_index("x") + lax.axis_index("y")

    repl = {n: local[n] for n in REPLICATED}
    lb = lower_bounds_fwd(repl["h_lower_bounds"])
    layers = []
    for i in range(DEPTH):
        names = layer_weight_names(i)
        shards = [local[n][k].astype(bf16) if n in MATRICES else local[n][k] for n, k in names]
        gathered = gather_shards(f"gather_l{i}", shards)
        layers.append(prep_layer(i, {n: g for (n, _), g in zip(names, gathered, strict=True)}, repl, lb))

    loss, dx, lgrads, dfinal = local_step(x.reshape(nb * seq, D), mem.reshape(nb * N_MEM, D), loss_target.reshape(nb * seq, D),
                                          layers, repl["final_norm"].reshape(1, D), nb)
    grads = small_grads(lgrads, dfinal, repl["h_lower_bounds"])

    small_names = REPLICATED + SMALL_SHARDED
    flat = jnp.concatenate([grads[n].astype(f32).reshape(-1) for n in small_names] + [loss[0, 0:1]])
    rows = -(-flat.shape[0] // (8 * SMALL_ROW)) * 8
    flat = jnp.pad(flat, (0, rows * SMALL_ROW - flat.shape[0])).reshape(rows, SMALL_ROW)
    red = allreduce_small(flat).reshape(-1)
    gsum, off = {}, 0
    for n in small_names:
        size = grads[n].size
        g = red[off:off + size].reshape(grads[n].shape)
        off += size
        if n in SHARD_AXIS:
            ax = SHARD_AXIS[n]
            w = g.shape[ax] // NCHIP
            g = lax.dynamic_slice_in_dim(g, me * w, w, axis=ax)
        gsum[n] = g
    loss_out = red[off]

    layer_sums = {n: [] for n in MATRICES}
    for i in range(DEPTH):
        parts = matrix_grad_parts(i, lgrads[i])
        landed = scatter_parts(f"scatter_l{i}", list(parts.values()))
        for n, land in zip(parts, landed, strict=True):
            layer_sums[n].append(sum_lead(f"sum_l{i}_{n}", land))
    sums = [jnp.stack(layer_sums[n]) for n in MATRICES]
    swapped = sibling_swap(sums)

    outs = {}
    for n, s, o in zip(MATRICES, sums, swapped, strict=True):
        outs[n] = adamw(local[n], s, mom_m[n], mom_v[n], f"adamw_{n}", g2=o)
    for n in small_names:
        outs[n] = adamw(local[n], gsum[n].reshape(local[n].shape), mom_m[n], mom_v[n], f"adamw_{n}")
    res = [loss_out, dx.reshape(nb, seq, D)]
    for k in range(4):
        res += [outs[n][k] for n in WEIGHTS]
    return tuple(res)
```

```python
import functools

import jax
import jax.numpy as jnp
from jax import lax
from jax.experimental import pallas as pl
from jax.experimental.pallas import tpu as pltpu

f32 = jnp.float32
bf16 = jnp.bfloat16
HIGHEST = lax.Precision.HIGHEST
MESH = pl.DeviceIdType.MESH

D = 1024
DEPTH = 4
EPS = 1e-6
N_MEM = 256
M_INNER, M_P, M_H, M_G, M_N, M_Q = 2048, 64, 32, 8, 128, 64
M_CONV = M_INNER + 2 * M_G * M_N
M_MAIN = M_INNER + M_CONV
M_IN = M_MAIN + M_H
H_H, H_K, H_Q = 8, 128, 32
G_HV, G_HK, G_K, G_Q = 16, 8, 128, 64
G_CONV, G_VAL = 4096, 2048
G_MAIN = G_CONV + G_VAL
G_IN = G_MAIN + 2 * G_HV
X_H, X_D = 4, 256
D_FF = 2816
ADAM_LR, ADAM_B1, ADAM_B2, ADAM_EPS, ADAM_WD, ADAM_STEP = 0.001, 0.9, 0.999, 1e-08, 0.01, 10
VMEM_LIMIT = 56 * 1024 * 1024
NCHIP = 4


def _cp(**kw):
    return pltpu.CompilerParams(vmem_limit_bytes=VMEM_LIMIT, **kw)


def _S(shape, dtype):
    return jax.ShapeDtypeStruct(tuple(shape), dtype)


def _dg(a, b, ca, cb, prec=None):
    return lax.dot_general(a, b, (((ca,), (cb,)), ((), ())), precision=prec, preferred_element_type=f32)


def _hdot(a, b, ca=1, cb=0):
    return _dg(a.astype(f32), b.astype(f32), ca, cb, HIGHEST)


def _bdot_raw(a, b, ca, cb):
    return _dg(a.astype(bf16), b.astype(bf16), ca, cb)


@functools.partial(jax.custom_vjp, nondiff_argnums=(2, 3))
def _bdot(a, b, ca, cb):
    return _bdot_raw(a, b, ca, cb)


def _bdot_fwd(a, b, ca, cb):
    return _bdot_raw(a, b, ca, cb), (a, b)


def _bdot_bwd(ca, cb, res, g):
    a, b = res
    if ca == 1:
        da = _bdot_raw(g, b, 1, 1 if cb == 0 else 0)
    else:
        da = _bdot_raw(b, g, 1 if cb == 0 else 0, 1)
    if cb == 0:
        db = _bdot_raw(a, g, 0 if ca == 1 else 1, 0)
    else:
        db = _bdot_raw(g, a, 0, 0 if ca == 1 else 1)
    return da.astype(a.dtype), db.astype(b.dtype)


_bdot.defvjp(_bdot_fwd, _bdot_bwd)


def _shift_down_raw(x, k):
    r = lax.broadcasted_iota(jnp.int32, x.shape, 0)
    return jnp.where(r >= k, pltpu.roll(x, k, 0), 0.0)


def _shift_up_raw(x, k):
    n = x.shape[0]
    r = lax.broadcasted_iota(jnp.int32, x.shape, 0)
    return jnp.where(r < n - k, pltpu.roll(x, n - k, 0), 0.0)


@functools.partial(jax.custom_vjp, nondiff_argnums=(1,))
def _shift_down(x, k):
    return _shift_down_raw(x, k)


_shift_down.defvjp(lambda x, k: (_shift_down_raw(x, k), None), lambda k, _, g: (_shift_up_raw(g, k),))


def _rms(x, w):
    return x * lax.rsqrt(jnp.mean(x * x, axis=-1, keepdims=True) + EPS) * w


def _silu(x):
    return x * jax.nn.sigmoid(x)


def _masks(q):
    r = lax.broadcasted_iota(jnp.int32, (q, q), 0)
    c = lax.broadcasted_iota(jnp.int32, (q, q), 1)
    return r >= c, r > c


def _colvec(row):
    return jnp.transpose(jnp.broadcast_to(row, (8, row.shape[1])))[:, 0:1]


def _tile(n, cands):
    for c in cands:
        if n % c == 0:
            return c
    return n


def mm(a, b, *, ta=False, tb=False, bsel=None, out_stack=None, out_slots=None, into=None, res=None, out_dtype=f32, name):
    m, k = (a.shape[1], a.shape[0]) if ta else a.shape
    ca, cb = (0 if ta else 1), (1 if tb else 0)
    tm = _tile(m, (512, 256, 128))
    if bsel is not None:
        s0, cnt = bsel
        ns = b.shape[2]
        if tb:
            n, tn, tk = b.shape[1], b.shape[1], ns
            b_spec = pl.BlockSpec((None, tn, ns), lambda i, j, kk: (s0 + kk, j, 0))
        else:
            n, tn, tk = cnt * ns, ns, k
            b_spec = pl.BlockSpec((None, tk, ns), lambda i, j, kk: (s0 + j, kk, 0))
    else:
        n = b.shape[0] if tb else b.shape[1]
        tn = out_stack if out_stack else (n if n <= 2816 else _tile(n, (2048, 1024, 512, 256, 128)))
        tk = k if (k <= 4096 and not ta) else _tile(k, (1024, 512, 256, 128))
        b_spec = pl.BlockSpec((tn, tk), lambda i, j, kk: (j, kk)) if tb else pl.BlockSpec((tk, tn), lambda i, j, kk: (kk, j))
    nk = k // tk
    if out_stack:
        total, first = out_slots if out_slots else (n // tn, 0)
        out_spec = pl.BlockSpec((None, tm, tn), lambda i, j, kk: (first + j, i, 0))
        out_shape = _S((total, m, tn), out_dtype)
    else:
        out_spec = pl.BlockSpec((tm, tn), lambda i, j, kk: (i, j))
        out_shape = _S((m, n), out_dtype)

    def body(*refs):
        a_ref, b_ref = refs[:2]
        r_ref = refs[2] if res is not None else None
        o_ref, acc = refs[-2:]
        kk = pl.program_id(2)

        @pl.when(kk == 0)
        def _():
            acc[...] = jnp.zeros_like(acc)

        acc[...] += _bdot_raw(a_ref[...], b_ref[...], ca, cb)

        @pl.when(kk == nk - 1)
        def _():
            v = acc[...]
            if r_ref is not None:
                v = v + r_ref[...]
            o_ref[...] = v.astype(o_ref.dtype)

    a_spec = pl.BlockSpec((tk, tm), lambda i, j, kk: (kk, i)) if ta else pl.BlockSpec((tm, tk), lambda i, j, kk: (i, kk))
    in_specs = [a_spec, b_spec]
    args = [a, b]
    if res is not None:
        in_specs.append(pl.BlockSpec((tm, tn), lambda i, j, kk: (i, j)))
        args.append(res)
    aliases = {}
    if into is not None:
        aliases = {len(args): 0}
        in_specs.append(pl.BlockSpec(memory_space=pl.ANY))
        args.append(into)
    return pl.pallas_call(
        body, name=name, grid=(m // tm, n // tn, nk), in_specs=in_specs, out_specs=out_spec, out_shape=out_shape,
        scratch_shapes=[pltpu.VMEM((tm, tn), f32)], input_output_aliases=aliases, compiler_params=_cp())(*args)


def rows_call(name, fn, rows, pars, row_out, acc_out=(), tm=256):
    t = rows[0].shape[0]
    tm = min(tm, t)
    assert t % tm == 0, (name, t, tm)
    nr, npar, nro = len(rows), len(pars), len(row_out)

    def body(*refs):
        rv = [r[...] for r in refs[:nr]]
        pv = [r[...] for r in refs[nr:nr + npar]]
        ro_refs = refs[nr + npar:nr + npar + nro]
        ao_refs = refs[nr + npar + nro:]
        ro, ao = fn(*rv, *pv)
        for r, v in zip(ro_refs, ro, strict=True):
            r[...] = v.astype(r.dtype)
        if ao_refs:
            @pl.when(pl.program_id(0) == 0)
            def _():
                for r in ao_refs:
                    r[...] = jnp.zeros_like(r)
            for r, v in zip(ao_refs, ao, strict=True):
                r[...] += v.astype(r.dtype)

    in_specs = [pl.BlockSpec((tm, r.shape[1]), lambda i: (i, 0)) for r in rows]
    in_specs += [pl.BlockSpec(p.shape, lambda i: (0, 0)) for p in pars]
    out_specs = [pl.BlockSpec((tm, c), lambda i: (i, 0)) for c, _ in row_out]
    out_specs += [pl.BlockSpec(s, lambda i: (0, 0)) for s, _ in acc_out]
    out_shape = [_S((t, c), dt) for c, dt in row_out] + [_S(s, dt) for s, dt in acc_out]
    return pl.pallas_call(body, name=name, grid=(t // tm,), in_specs=in_specs, out_specs=out_specs,
                          out_shape=out_shape, compiler_params=_cp())(*rows, *pars)


def rms_fwd(x, w, name):
    return rows_call(name, lambda xv, wv: ((_rms(xv, wv),), ()), [x], [w], [(x.shape[1], bf16)])[0]


def rms_bwd(x, w, dy, dres, name):
    def fn(*a):
        if dres is None:
            xv, dyv, wv = a
        else:
            xv, dyv, drv, wv = a
        _, vjp = jax.vjp(_rms, xv, wv)
        dx, dw = vjp(dyv.astype(f32))
        if dres is not None:
            dx = dx + drv
        return (dx,), (dw,)
    rows = [x, dy] + ([] if dres is None else [dres])
    return rows_call(name, fn, rows, [w], [(x.shape[1], f32)], [(w.shape, f32)])


def cols_call(name, fn, seqs, pars, outs, *, nb, ct, ncol, dseed=None):
    ns, npar = len(seqs), len(pars)
    seq_len = seqs[0].shape[0] // nb
    nd = 0 if dseed is None else len(dseed)

    def body(*refs):
        sv = [r[...] for r in refs[:ns]]
        pv = [r[...] for r in refs[ns:ns + npar]]
        if dseed is None:
            o_refs = refs[ns + npar:]
            for r, v in zip(o_refs, fn(*sv, *pv), strict=True):
                r[...] = v.astype(r.dtype)
            return
        dv = [r[...].astype(f32) for r in refs[ns + npar:ns + npar + nd]]
        ds_refs = refs[ns + npar + nd:ns + npar + nd + ns]
        dp_refs = refs[ns + npar + nd + ns:]
        _, vjp = jax.vjp(fn, *[v.astype(f32) for v in sv], *pv)
        g = vjp(tuple(dv))
        for r, v in zip(ds_refs, g[:ns], strict=True):
            r[...] = v.astype(r.dtype)

        @pl.when(pl.program_id(1) == 0)
        def _():
            for r in dp_refs:
                r[...] = jnp.zeros_like(r)
        for r, v in zip(dp_refs, g[ns:], strict=True):
            r[...] += v

    full = pl.BlockSpec((seq_len, ct), lambda j, b: (b, j))
    in_specs = [full for _ in seqs]
    in_specs += [pl.BlockSpec((p.shape[0], ct), lambda j, b: (0, j)) for p in pars]
    args = list(seqs) + list(pars)
    if dseed is None:
        out_specs = [full for _ in outs]
        out_shape = [_S((nb * seq_len, ncol * ct), dt) for dt in outs]
    else:
        in_specs += [full for _ in dseed]
        args += list(dseed)
        out_specs = [full for _ in seqs] + [pl.BlockSpec((p.shape[0], ct), lambda j, b: (0, j)) for p in pars]
        out_shape = [_S((nb * seq_len, ncol * ct), f32) for _ in seqs] + [_S(p.shape, f32) for p in pars]
    return pl.pallas_call(body, name=name, grid=(ncol, nb), in_specs=in_specs, out_specs=out_specs,
                          out_shape=out_shape, compiler_params=_cp())(*args)


def _conv4_silu(x, w, b):
    y = x * w[3:4] + _shift_down(x, 1) * w[2:3] + _shift_down(x, 2) * w[1:2] + _shift_down(x, 3) * w[0:1] + b
    return (_silu(y),)


def _conv4_silu_nobias(x, w):
    y = x * w[3:4] + _shift_down(x, 1) * w[2:3] + _shift_down(x, 2) * w[1:2] + _shift_down(x, 3) * w[0:1]
    return (_silu(y),)


def _ffn_act(gate, up, w, b):
    y = gate * w[2:3] + _shift_down(gate, 1) * w[1:2] + _shift_down(gate, 2) * w[0:1] + b
    return (_silu(y) * up,)


def scan_call(name, chunk_fn, seqs, pars, consts, outs, *, nb, nh, q, state_shape, states=None, dseed=None):
    t = seqs[0][0].shape[0]
    nc = t // (nb * q)
    ns, npar, ncon, no = len(seqs), len(pars), len(consts), len(outs)
    s0, s1 = state_shape
    bwd = dseed is not None

    def cidx(c):
        return (nc - 1 - c) if bwd else c

    def rowblk(b, c):
        return b * nc + cidx(c)

    def seq_spec(w, colfn):
        return pl.BlockSpec((q, w), lambda b, c, h: (rowblk(b, c), colfn(h)))

    def par_spec(shape, idxfn):
        return pl.BlockSpec(shape, lambda b, c, h: idxfn(h))

    st_spec = pl.BlockSpec((s0, s1), lambda b, c, h: ((rowblk(b, c)) * nh + h, 0))
    in_specs = [seq_spec(w, cf) for _, w, cf, _ in seqs]
    in_specs += [par_spec(s, f) for _, s, f in pars] + [par_spec(s, f) for _, s, f in consts]
    args = [a for a, _, _, _ in seqs] + [a for a, _, _ in pars] + [a for a, _, _ in consts]

    if not bwd:
        def body(*refs):
            sv = [r[...] for r in refs[:ns]]
            pv = [r[...] for r in refs[ns:ns + npar]]
            cv = [r[...] for r in refs[ns + npar:ns + npar + ncon]]
            o_refs = refs[ns + npar + ncon:ns + npar + ncon + no]
            save_ref = refs[ns + npar + ncon + no]
            st = refs[-1]
            c, h = pl.program_id(1), pl.program_id(2)

            @pl.when(c == 0)
            def _():
                st[h] = jnp.zeros((s0, s1), f32)
            s_in = st[h]
            save_ref[...] = s_in
            o, s_out = chunk_fn(*sv, *pv, s_in, *cv)
            st[h] = s_out
            for r, v in zip(o_refs, o, strict=True):
                r[...] = v.astype(r.dtype)

        out_specs = [seq_spec(w, cf) for _, w, cf, _ in outs] + [st_spec]
        out_shape = [_S((t, cc), dt) for cc, _, _, dt in outs] + [_S((nb * nc * nh * s0, s1), f32)]
        return pl.pallas_call(body, name=name, grid=(nb, nc, nh), in_specs=in_specs, out_specs=out_specs,
                              out_shape=out_shape, scratch_shapes=[pltpu.VMEM((nh, s0, s1), f32)],
                              compiler_params=_cp())(*args)

    def body(*refs):
        i = 0
        sv = [r[...] for r in refs[i:i + ns]]; i += ns
        pv = [r[...] for r in refs[i:i + npar]]; i += npar
        cv = [r[...] for r in refs[i:i + ncon]]; i += ncon
        dv = [r[...].astype(f32) for r in refs[i:i + no]]; i += no
        s_in = refs[i][...]; i += 1
        ds_refs = refs[i:i + ns]; i += ns
        dp_refs = refs[i:i + npar]; i += npar
        dst = refs[-1]
        b, c, h = pl.program_id(0), pl.program_id(1), pl.program_id(2)

        @pl.when(c == 0)
        def _():
            dst[h] = jnp.zeros((s0, s1), f32)

        @pl.when((b == 0) & (c == 0) & (h == 0))
        def _():
            for r in dp_refs:
                r[...] = jnp.zeros_like(r)

        fn = lambda *a: chunk_fn(*a, *cv)
        _, vjp = jax.vjp(fn, *[v.astype(f32) for v in sv], *pv, s_in)
        g = vjp((tuple(dv), dst[h]))
        dst[h] = g[ns + npar]
        for (_, _, _, rep), r, v in zip(seqs, ds_refs, g[:ns], strict=True):
            if rep == 1:
                r[...] = v.astype(r.dtype)
            else:
                @pl.when(h % rep == 0)
                def _(r=r, v=v):
                    r[...] = v.astype(r.dtype)

                @pl.when(h % rep != 0)
                def _(r=r, v=v):
                    r[...] += v.astype(r.dtype)
        for r, v in zip(dp_refs, g[ns:ns + npar], strict=True):
            r[h] += v

    in_specs += [seq_spec(w, cf) for _, w, cf, _ in outs] + [st_spec]
    args += list(dseed) + [states]
    out_specs = [seq_spec(w, cf) for _, w, cf, _ in seqs]
    out_specs += [pl.BlockSpec((nh,) + tuple(s), lambda b, c, h: (0, 0, 0)) for _, s, _ in pars]
    out_shape = [_S(a.shape, f32) for a, _, _, _ in seqs] + [_S((nh,) + tuple(s), f32) for _, s, _ in pars]
    return pl.pallas_call(body, name=name, grid=(nb, nc, nh), in_specs=in_specs, out_specs=out_specs,
                          out_shape=out_shape, scratch_shapes=[pltpu.VMEM((nh, s0, s1), f32)],
                          compiler_params=_cp())(*args)


def _ssd_group(xs, bm, cm, z, dtr, dtb, alog, dsk, nw, st, e):
    q, gp = xs.shape[0], SSD_GP
    heads = range(4 * gp)
    sl = [slice(i * M_P, (i + 1) * M_P) for i in heads]
    gsl = [slice(g * M_N, (g + 1) * M_N) for g in range(gp)]
    incl, _ = _masks(q)
    dt = jax.nn.softplus(dtr + dtb[0:1])
    dte = _hdot(dt, e)
    ae = _hdot(-jnp.exp(alog), e)[0:1]
    de = _hdot(dsk, e)[0:1]
    xc = xs * dte
    acum = _hdot(incl.astype(f32), dte * ae)
    last = acum[q - 1:q]
    eac, eend, elast = jnp.exp(acum), jnp.exp(last - acum), jnp.exp(last)
    xe = xc * eend
    bms, cms = [bm[:, s] for s in gsl], [cm[:, s] for s in gsl]
    cb = [_bdot(cms[g], bms[g], 1, 1) for g in range(gp)]
    decs = []
    for i in heads:
        a_i = acum[:, sl[i]]
        diff = jnp.where(incl, a_i[:, 0:1] - jnp.transpose(a_i)[0:1, :], 0.0)
        decs.append(jnp.where(incl, jnp.exp(diff), 0.0))
    sts = [st[sl[i], :] for i in heads]
    yd = [_bdot(cb[i // 4] * decs[i], xc[:, sl[i]], 1, 0) for i in heads]
    yo = [_bdot(cms[i // 4], sts[i], 1, 1) for i in heads]
    ds = [_bdot(xe[:, sl[i]], bms[i // 4], 0, 0) for i in heads]
    new = [sts[i] * elast[:, i * M_P:i * M_P + 1] + ds[i] for i in heads]
    y = jnp.concatenate(yd, axis=1) + jnp.concatenate(yo, axis=1) * eac + de * xs
    y = y * _silu(z)
    yn = [_rms(y[:, g * 256:(g + 1) * 256], nw[:, g * 256:(g + 1) * 256]) for g in range(gp)]
    return (jnp.concatenate(yn, axis=1),), jnp.concatenate(new, axis=0)


def _gla_group(qr, fr, ir, gr, lb, nw, st):
    q, hp = qr.shape[0], GLA_HP
    heads = range(hp)
    sl = [slice(i * H_K, (i + 1) * H_K) for i in heads]
    incl, _ = _masks(q)
    fg = lb + (1.0 - lb) * jax.nn.sigmoid(fr)
    qq = _silu(qr) * (H_K ** -0.5)
    k = 1.0 - fg
    gc = _hdot(incl.astype(f32), jnp.log(fg))
    gl = gc[q - 1:q]
    qd, ki, ke = qq * jnp.exp(gc), k * jnp.exp(-gc), k * jnp.exp(gl - gc)
    egl = jnp.exp(gl)
    sts = [st[sl[i], :] for i in heads]
    att = [jnp.where(incl, _bdot(qd[:, sl[i]], ki[:, sl[i]], 1, 1), 0.0) for i in heads]
    o1 = [_bdot(att[i], ir[:, sl[i]], 1, 0) for i in heads]
    o2 = [_bdot(qd[:, sl[i]], sts[i], 1, 0) for i in heads]
    kv = [_bdot(ke[:, sl[i]], ir[:, sl[i]], 0, 0) for i in heads]
    new = [sts[i] * _colvec(egl[:, sl[i]]) + kv[i] for i in heads]
    on = [_rms(o1[i] + o2[i], nw) * _silu(gr[:, sl[i]]) for i in heads]
    return (jnp.concatenate(on, axis=1),), jnp.concatenate(new, axis=0)


def _tri_inv_many(ms):
    n = ms[0].shape[0]
    r = lax.broadcasted_iota(jnp.int32, (n, n), 0)
    c = lax.broadcasted_iota(jnp.int32, (n, n), 1)
    eye = (r == c).astype(f32)
    ts = [eye - m for m in ms]
    ps = list(ms)
    for _ in range(max(1, (n - 1).bit_length() - 1)):
        ps = [_hdot(p, p) for p in ps]
        ts = [t + _hdot(t, p) for t, p in zip(ts, ps)]
    return ts


def _gdn_group(qr, kr, v, z, ba, alog, dtb, nw, st, eb, ea):
    q, hp = qr.shape[0], GDN_HP
    heads = range(hp)
    sl = [slice(i * G_K, (i + 1) * G_K) for i in heads]
    incl, strict = _masks(q)
    beta = _hdot(jax.nn.sigmoid(ba), eb)
    g = _hdot(-jnp.exp(alog[0:1]) * jax.nn.softplus(ba + dtb[0:1]), ea)
    gc = _hdot(incl.astype(f32), g)
    egc = jnp.exp(gc)
    gl = gc[q - 1:q]
    eend = jnp.exp(gl - gc)
    egl = jnp.exp(gl)
    vb = v * beta
    qn, kn = [], []
    for j in range(hp // 2):
        qj, kj = qr[:, sl[j]], kr[:, sl[j]]
        qn.append(qj * lax.rsqrt(jnp.sum(qj * qj, axis=-1, keepdims=True) + EPS) * (G_K ** -0.5))
        kn.append(kj * lax.rsqrt(jnp.sum(kj * kj, axis=-1, keepdims=True) + EPS))
    qk = [_bdot(qn[j], kn[j], 1, 1) for j in range(hp // 2)]
    gcs = [gc[:, sl[i]] for i in heads]
    decs = []
    for i in heads:
        diff = jnp.where(incl, gcs[i][:, 0:1] - jnp.transpose(gcs[i])[0:1, :], 0.0)
        decs.append(jnp.where(incl, jnp.exp(diff), 0.0))
    kbs = [kn[i // 2] * beta[:, sl[i]] for i in heads]
    kk = [_bdot(kbs[i], kn[i // 2], 1, 1) for i in heads]
    tinv = _tri_inv_many([jnp.where(strict, kk[i] * decs[i], 0.0) for i in heads])
    uw = [_hdot(tinv[i], jnp.concatenate([vb[:, sl[i]], kbs[i] * egc[:, sl[i]]], axis=1)) for i in heads]
    sts = [st[sl[i], :] for i in heads]
    ws = [_bdot(jnp.concatenate([uw[i][:, G_K:], qn[i // 2] * egc[:, sl[i]]], axis=0), sts[i], 1, 0) for i in heads]
    v_new = [uw[i][:, :G_K] - ws[i][:q] for i in heads]
    o = [ws[i][q:] + _bdot(qk[i // 2] * decs[i], v_new[i], 1, 0) for i in heads]
    new = [sts[i] * egl[:, i * G_K:i * G_K + 1] + _bdot(kn[i // 2] * eend[:, sl[i]], v_new[i], 0, 0) for i in heads]
    on = [_rms(o[i], nw) * _silu(z[:, sl[i]]) for i in heads]
    return (jnp.concatenate(on, axis=1),), jnp.concatenate(new, axis=0)


def _xattn_fn(q, k, v):
    s = _bdot(q, k, 1, 1) * (X_D ** -0.5)
    return _bdot(jax.nn.softmax(s, axis=-1), v, 1, 0)


def xattn_fwd(q, k, v, nb, name, tl=512):
    t = q.shape[0]
    tl = min(tl, t // nb)
    nl = t // nb // tl

    def body(q_ref, k_ref, v_ref, o_ref):
        o_ref[...] = _xattn_fn(q_ref[...], k_ref[...], v_ref[...]).astype(o_ref.dtype)

    qs = pl.BlockSpec((tl, X_D), lambda b, i, h: (b * nl + i, h))
    ks = pl.BlockSpec((N_MEM, X_D), lambda b, i, h: (b, h))
    return pl.pallas_call(body, name=name, grid=(nb, nl, X_H), in_specs=[qs, ks, ks], out_specs=qs,
                          out_shape=_S(q.shape, bf16), compiler_params=_cp())(q, k, v)


def xattn_bwd(q, k, v, do, nb, name, tl=512):
    t = q.shape[0]
    tl = min(tl, t // nb)
    nl = t // nb // tl

    def body(q_ref, k_ref, v_ref, do_ref, dq_ref, dk_ref, dv_ref):
        _, vjp = jax.vjp(_xattn_fn, q_ref[...], k_ref[...], v_ref[...])
        dq, dk, dv = vjp(do_ref[...].astype(f32))
        dq_ref[...] = dq

        @pl.when(pl.program_id(2) == 0)
        def _():
            dk_ref[...] = jnp.zeros_like(dk_ref)
            dv_ref[...] = jnp.zeros_like(dv_ref)
        dk_ref[...] += dk
        dv_ref[...] += dv

    qs = pl.BlockSpec((tl, X_D), lambda b, h, i: (b * nl + i, h))
    ks = pl.BlockSpec((N_MEM, X_D), lambda b, h, i: (b, h))
    return pl.pallas_call(body, name=name, grid=(nb, X_H, nl), in_specs=[qs, ks, ks, qs], out_specs=[qs, ks, ks],
                          out_shape=[_S(q.shape, f32), _S(k.shape, f32), _S(v.shape, f32)],
                          compiler_params=_cp())(q, k, v, do)


def _lower_bounds(hlb):
    sm = jax.nn.softmax(hlb, axis=0)
    rows, run = [], None
    for r in range(hlb.shape[0]):
        run = sm[r:r + 1] if run is None else run + sm[r:r + 1]
        rows.append(run - sm[0:1])
    return jnp.concatenate(rows, axis=0)


def lower_bounds_fwd(hlb):
    return rows_call("lb_fwd", lambda v: ((_lower_bounds(v),), ()), [hlb], [], [(hlb.shape[1], f32)], tm=hlb.shape[0])[0]


def lower_bounds_bwd(hlb, dlb):
    def fn(v, d):
        _, vjp = jax.vjp(_lower_bounds, v)
        return (vjp(d)[0],), ()
    return rows_call("lb_bwd", fn, [hlb, dlb], [], [(hlb.shape[1], f32)], tm=hlb.shape[0])[0]


def loss_head(x, target, w):
    def fn(xv, tv, wv):
        def loss(xx, ww):
            err = _rms(xx, ww) - tv
            return 0.5 * jnp.sum(jnp.mean(err * err, axis=-1))
        val, (dx, dw) = jax.value_and_grad(loss, argnums=(0, 1))(xv, wv)
        return (dx,), (jnp.broadcast_to(val, (1, 128)), dw)
    dx, loss, dw = rows_call("loss_head", fn, [x, target], [w], [(x.shape[1], f32)], [((1, 128), f32), (w.shape, f32)])
    return dx, loss, dw


def _adamw_fn(w, g, m, v):
    m2 = ADAM_B1 * m + (1.0 - ADAM_B1) * g
    v2 = ADAM_B2 * v + (1.0 - ADAM_B2) * (g * g)
    m_hat = m2 / (1.0 - ADAM_B1 ** ADAM_STEP)
    v_hat = v2 / (1.0 - ADAM_B2 ** ADAM_STEP)
    delta = -ADAM_LR * (m_hat / (jnp.sqrt(v_hat) + ADAM_EPS) + ADAM_WD * w)
    return delta, m2, v2


def adamw(w, g, m, v, name, g2=None):
    shape = w.shape
    c = shape[-1]
    r = w.size // c
    to2 = lambda a: a.reshape(r, c)
    tm = r if r * c * 4 <= (1 << 20) else _tile(r, (256, 128, 64, 32, 16, 8))

    def fn(*a):
        if g2 is None:
            wv, gv, mv, vv = a
        else:
            wv, gv, g2v, mv, vv = a
            gv = gv + g2v
        return (gv,) + _adamw_fn(wv, gv, mv, vv), ()
    rows = [to2(w), to2(g)] + ([] if g2 is None else [to2(g2)]) + [to2(m), to2(v)]
    outs = rows_call(name, fn, rows, [], [(c, f32)] * 4, tm=tm)
    return tuple(o.reshape(shape) for o in outs)


def _expand(first_row, nheads, width):
    r = jnp.arange(128)[:, None]
    c = jnp.arange(nheads * width)[None, :]
    return (r == first_row + c // width).astype(f32)


def _pad_row(v, lane0=0):
    return jnp.pad(v.astype(f32).reshape(1, -1), ((0, 7), (lane0, 128 - lane0 - v.shape[0])))


def _pad_cols(w, n=128):
    return jnp.pad(w, ((0, 0), (0, n - w.shape[1])))


_COL = lambda h: h
_C00 = lambda h: (0, 0)
_CONV_CT = 256


def _conv(name, x, w, b, nb, dseed=None):
    fn = _conv4_silu if b is not None else _conv4_silu_nobias
    pars = [w] + ([] if b is None else [b])
    return cols_call(name, fn, [x], pars, [f32], nb=nb, ct=_CONV_CT, ncol=x.shape[1] // _CONV_CT,
                     dseed=None if dseed is None else [dseed])


SSD_GP, GLA_HP, GDN_HP = 8, 8, 8


def _ssd_scan(name, xs, bm, cm, z, dtr, p, nb, states=None, dseed=None):
    gp, ng = SSD_GP, M_G // SSD_GP
    seqs = [(xs, 256 * gp, _COL, 1), (bm, 128 * gp, _COL, 1), (cm, 128 * gp, _COL, 1), (z, 256 * gp, _COL, 1),
            (dtr, 128, lambda h: 0, ng)]
    pars = [(p["dtb"], (8, 128), _C00), (p["alog"], (8, 128), _C00), (p["dsk"], (8, 128), _C00),
            (p["nw"], (1, 256 * gp), lambda h: (0, h))]
    consts = [(_expand(0, M_H, M_P), (128, 256 * gp), lambda h: (0, h))]
    outs = [(M_INNER, 256 * gp, _COL, bf16)]
    return scan_call(name, _ssd_group, seqs, pars, consts, outs, nb=nb, nh=ng, q=M_Q, state_shape=(gp * 4 * M_P, M_N),
                     states=states, dseed=dseed)


def _gla_scan(name, qr, fr, ir, gr, p, nb, states=None, dseed=None):
    hp, ng = GLA_HP, H_H // GLA_HP
    seqs = [(a, 128 * hp, _COL, 1) for a in (qr, fr, ir, gr)]
    pars = [(p["lb"], (1, 128 * hp), lambda h: (0, h)), (p["nw"], (1, 128), _C00)]
    outs = [(D, 128 * hp, _COL, bf16)]
    return scan_call(name, _gla_group, seqs, pars, [], outs, nb=nb, nh=ng, q=H_Q, state_shape=(hp * H_K, H_K),
                     states=states, dseed=dseed)


def _gdn_scan(name, qc, kc, vc, z, ba, p, nb, states=None, dseed=None):
    hp, ng = GDN_HP, G_HV // GDN_HP
    seqs = [(qc, 64 * hp, _COL, 1), (kc, 64 * hp, _COL, 1), (vc, 128 * hp, _COL, 1), (z, 128 * hp, _COL, 1),
            (ba, 128, lambda h: 0, ng)]
    pars = [(p["alog"], (8, 128), _C00), (p["dtb"], (8, 128), _C00), (p["nw"], (1, 128), _C00)]
    consts = [(_expand(0, G_HV, G_K), (128, 128 * hp), lambda h: (0, h)),
              (_expand(G_HV, G_HV, G_K), (128, 128 * hp), lambda h: (0, h))]
    outs = [(G_VAL, 128 * hp, _COL, bf16)]
    return scan_call(name, _gdn_group, seqs, pars, consts, outs, nb=nb, nh=ng, q=G_Q, state_shape=(hp * G_K, G_K),
                     states=states, dseed=dseed)


def _w(wt):
    return wt if isinstance(wt, tuple) else (wt, None)


def _proj(a, wt, name, res=None):
    arr, bsel = _w(wt)
    return mm(a, arr, bsel=bsel, res=res, name=name)


def _proj_bwd(tag, hn, pieces):
    dhn, dws, bufs = None, [], {}
    for i, (d, wt) in enumerate(pieces):
        arr, bsel = _w(wt)
        if bsel is None:
            dws.append(mm(hn, d, ta=True, out_dtype=bf16, name=f"{tag}_dw{i}"))
        else:
            bufs[id(arr)] = mm(hn, d, ta=True, out_stack=arr.shape[2], out_slots=(arr.shape[0], bsel[0]),
                               into=bufs.get(id(arr)), out_dtype=bf16, name=f"{tag}_dw{i}")
            dws.append(None)
        dhn = mm(d, arr, tb=True, bsel=bsel, res=dhn, name=f"{tag}_dh{i}")
    dws = [dw if dw is not None else bufs[id(_w(wt)[0])] for dw, (_, wt) in zip(dws, pieces, strict=True)]
    return dhn, dws


def ssd_mixer_fwd(tag, hn, w, nb):
    z, xr, br, cr, dtr = (_proj(hn, w[k], f"{tag}_in_{k}") for k in ("wz", "wx", "wb", "wc", "wdt"))
    xs = _conv(f"{tag}_convx", xr, w["cwx"], w["cbx"], nb)[0]
    bm = _conv(f"{tag}_convb", br, w["cwb"], w["cbb"], nb)[0]
    cm = _conv(f"{tag}_convc", cr, w["cwc"], w["cbc"], nb)[0]
    yn, states = _ssd_scan(f"{tag}_scan", xs, bm, cm, z, dtr, w, nb)
    return yn, (hn, z, xr, br, cr, dtr, xs, bm, cm, yn, states)


def ssd_mixer_bwd(tag, saved, dout, w, nb):
    hn, z, xr, br, cr, dtr, xs, bm, cm, yn, states = saved
    g = {"wout": mm(yn, dout, ta=True, out_dtype=bf16, name=f"{tag}_dwout")}
    dyn = mm(dout, w["wout"], tb=True, name=f"{tag}_dyn")
    dxs, dbm, dcm, dz, ddtr, ddtb, dalog, ddsk, dnw = _ssd_scan(f"{tag}_scanb", xs, bm, cm, z, dtr, w, nb, states, [dyn])
    dxr, g["cwx"], g["cbx"] = _conv(f"{tag}_convxb", xr, w["cwx"], w["cbx"], nb, dxs)
    dbr, g["cwb"], g["cbb"] = _conv(f"{tag}_convbb", br, w["cwb"], w["cbb"], nb, dbm)
    dcr, g["cwc"], g["cbc"] = _conv(f"{tag}_convcb", cr, w["cwc"], w["cbc"], nb, dcm)
    dhn, (g["wz"], g["wx"], g["wb"], g["wc"], g["wdt"]) = _proj_bwd(
        tag, hn, [(dz, w["wz"]), (dxr, w["wx"]), (dbr, w["wb"]), (dcr, w["wc"]), (ddtr, w["wdt"])])
    g["dtb"], g["alog"], g["dsk"] = (jnp.sum(a, axis=0)[0, :M_H] for a in (ddtb, dalog, ddsk))
    g["nw"] = dnw.reshape(M_INNER)
    return dhn, g


def gla_mixer_fwd(tag, hn, w, nb):
    qr, fr, ir, gr = (_proj(hn, w[k], f"{tag}_in_{k}") for k in ("wq", "wf", "wi", "wg"))
    on, states = _gla_scan(f"{tag}_scan", qr, fr, ir, gr, w, nb)
    return on, (hn, qr, fr, ir, gr, on, states)


def gla_mixer_bwd(tag, saved, dout, w, nb):
    hn, qr, fr, ir, gr, on, states = saved
    g = {"wout": mm(on, dout, ta=True, out_dtype=bf16, name=f"{tag}_dwout")}
    don = mm(dout, w["wout"], tb=True, name=f"{tag}_don")
    dq, df, di, dg, dlb, dnw = _gla_scan(f"{tag}_scanb", qr, fr, ir, gr, w, nb, states, [don])
    dhn, (g["wq"], g["wf"], g["wi"], g["wg"]) = _proj_bwd(tag, hn, [(dq, w["wq"]), (df, w["wf"]), (di, w["wi"]), (dg, w["wg"])])
    g["lb"] = dlb.reshape(1, D)
    g["nw"] = jnp.sum(dnw, axis=0).reshape(H_K)
    return dhn, g


def gdn_mixer_fwd(tag, hn, w, nb):
    qr, kr, vr, z, ba = (_proj(hn, w[k], f"{tag}_in_{k}") for k in ("wq", "wk", "wv", "wz", "wba"))
    qc = _conv(f"{tag}_convq", qr, w["cwq"], None, nb)[0]
    kc = _conv(f"{tag}_convk", kr, w["cwk"], None, nb)[0]
    vc = _conv(f"{tag}_convv", vr, w["cwv"], None, nb)[0]
    on, states = _gdn_scan(f"{tag}_scan", qc, kc, vc, z, ba, w, nb)
    return on, (hn, qr, kr, vr, z, ba, qc, kc, vc, on, states)


def gdn_mixer_bwd(tag, saved, dout, w, nb):
    hn, qr, kr, vr, z, ba, qc, kc, vc, on, states = saved
    g = {"wout": mm(on, dout, ta=True, out_dtype=bf16, name=f"{tag}_dwout")}
    don = mm(dout, w["wout"], tb=True, name=f"{tag}_don")
    dqc, dkc, dvc, dz, dba, dalog, ddtb, dnw = _gdn_scan(f"{tag}_scanb", qc, kc, vc, z, ba, w, nb, states, [don])
    dqr, g["cwq"] = _conv(f"{tag}_convqb", qr, w["cwq"], None, nb, dqc)
    dkr, g["cwk"] = _conv(f"{tag}_convkb", kr, w["cwk"], None, nb, dkc)
    dvr, g["cwv"] = _conv(f"{tag}_convvb", vr, w["cwv"], None, nb, dvc)
    dhn, (g["wq"], g["wk"], g["wv"], g["wz"], g["wba"]) = _proj_bwd(
        tag, hn, [(dqr, w["wq"]), (dkr, w["wk"]), (dvr, w["wv"]), (dz, w["wz"]), (dba, w["wba"])])
    g["alog"], g["dtb"] = (jnp.sum(a, axis=0)[0, G_HV:2 * G_HV] for a in (dalog, ddtb))
    g["nw"] = jnp.sum(dnw, axis=0).reshape(G_K)
    return dhn, g


_MIXERS = {0: (ssd_mixer_fwd, ssd_mixer_bwd), 1: (gla_mixer_fwd, gla_mixer_bwd), 2: (gdn_mixer_fwd, gdn_mixer_bwd)}


def layer_fwd(i, x, mem, w, nb):
    t = f"l{i}"
    hn = rms_fwd(x, w["ln_mix"], f"{t}_ln_mix")
    mix, s_mix = _MIXERS[i % 3][0](f"{t}_mix", hn, w["mix"], nb)
    x1 = mm(mix, w["mix"]["wout"], res=x, name=f"{t}_mix_out")
    hx = rms_fwd(x1, w["ln_xattn"], f"{t}_ln_xattn")
    mn = rms_fwd(mem, w["ln_mem"], f"{t}_ln_mem")
    q = _proj(hx, w["xq"], f"{t}_xa_q")
    k = _proj(mn, w["xk"], f"{t}_xa_k")
    v = _proj(mn, w["xv"], f"{t}_xa_v")
    o = xattn_fwd(q, k, v, nb, f"{t}_xattn")
    x2 = mm(o, w["xo"], res=x1, name=f"{t}_xa_o")
    hf = rms_fwd(x2, w["ln_ffn"], f"{t}_ln_ffn")
    gate = _proj(hf, w["fg"], f"{t}_ffn_gate")
    up = _proj(hf, w["fu"], f"{t}_ffn_up")
    act = cols_call(f"{t}_ffn_act", _ffn_act, [gate, up], [w["fcw"], w["fcb"]], [bf16], nb=nb, ct=_CONV_CT,
                    ncol=D_FF // _CONV_CT)[0]
    x3 = mm(act, w["fd"], res=x2, name=f"{t}_ffn_down")
    return x3, (x, s_mix, x1, hx, mn, q, k, v, o, x2, hf, gate, up, act)


def layer_bwd(i, saved, dx, mem, w, nb):
    t = f"l{i}b"
    x, s_mix, x1, hx, mn, q, k, v, o, x2, hf, gate, up, act = saved
    g = {}
    g["fd"] = mm(act, dx, ta=True, out_dtype=bf16, name=f"{t}_dwd")
    dact = mm(dx, w["fd"], tb=True, name=f"{t}_dact")
    dgate, dup, g["fcw"], g["fcb"] = cols_call(f"{t}_ffn_act", _ffn_act, [gate, up], [w["fcw"], w["fcb"]], [bf16], nb=nb,
                                               ct=_CONV_CT, ncol=D_FF // _CONV_CT, dseed=[dact])
    dhf, (g["fg"], g["fu"]) = _proj_bwd(f"{t}_ffn", hf, [(dgate, w["fg"]), (dup, w["fu"])])
    dx, g["ln_ffn"] = rms_bwd(x2, w["ln_ffn"], dhf, dx, f"{t}_ln_ffn")
    g["xo"] = mm(o, dx, ta=True, out_dtype=bf16, name=f"{t}_dwo")
    do = mm(dx, w["xo"], tb=True, name=f"{t}_do")
    dq, dk, dv = xattn_bwd(q, k, v, do, nb, f"{t}_xattn")
    dhx, (g["xq"],) = _proj_bwd(f"{t}_xq", hx, [(dq, w["xq"])])
    dmn, (g["xk"], g["xv"]) = _proj_bwd(f"{t}_xkv", mn, [(dk, w["xk"]), (dv, w["xv"])])
    _, g["ln_mem"] = rms_bwd(mem, w["ln_mem"], dmn, None, f"{t}_ln_mem")
    dx, g["ln_xattn"] = rms_bwd(x1, w["ln_xattn"], dhx, dx, f"{t}_ln_xattn")
    dhn, g["mix"] = _MIXERS[i % 3][1](f"{t}_mix", s_mix, dx, w["mix"], nb)
    dx, g["ln_mix"] = rms_bwd(x, w["ln_mix"], dhn, dx, f"{t}_ln_mix")
    return dx, g


def local_step(x, mem, target, weights_of, final_norm, nb, grads_done=None):
    saved, layers = [], []
    for i in range(DEPTH):
        layers.append(weights_of(i, x))
        x, s = layer_fwd(i, x, mem, layers[i], nb)
        saved.append(s)
    dx, loss, dfinal = loss_head(x, target, final_norm)
    grads = [None] * DEPTH
    token = None
    for i in reversed(range(DEPTH)):
        w = layers[i] if token is None else dict(layers[i], fd=layers[i]["fd"] + token[0, 0].astype(bf16))
        dx, grads[i] = layer_bwd(i, saved[i], dx, mem, w, nb)
        token = grads_done(i, grads[i], dx) if grads_done else None
    return loss, dx, grads, dfinal


WEIGHTS = ["ln_mix", "ln_xattn", "ln_mem", "ln_ffn", "final_norm", "m_in_w", "m_conv_w", "m_conv_b", "m_dt_bias", "m_a_log",
           "m_d", "m_norm_w", "m_out_w", "h_in_w", "h_lower_bounds", "h_norm_w", "h_out_w", "g_in_w", "g_conv_w", "g_a_log",
           "g_dt_bias", "g_norm_w", "g_out_w", "xa_q", "xa_kv", "xa_o", "f_up", "f_conv_w", "f_conv_b", "f_down"]
SHARD_AXIS = {"m_in_w": 2, "m_conv_w": 2, "m_conv_b": 1, "m_norm_w": 1, "m_out_w": 1, "h_in_w": 2, "h_out_w": 1, "g_in_w": 2,
              "g_conv_w": 2, "g_out_w": 1, "xa_q": 1, "xa_kv": 2, "xa_o": 1, "f_up": 2, "f_conv_w": 2, "f_down": 1}
MATRICES = ["m_in_w", "m_out_w", "h_in_w", "h_out_w", "g_in_w", "g_out_w", "xa_q", "xa_kv", "xa_o", "f_up", "f_down"]
SMALL_SHARDED = [n for n in WEIGHTS if n in SHARD_AXIS and n not in MATRICES]
REPLICATED = [n for n in WEIGHTS if n not in SHARD_AXIS]
_MIXER_PREFIX = {0: "m", 1: "h", 2: "g"}


def layer_weight_names(i):
    p = _MIXER_PREFIX[i % 3]
    k = i // 3
    return [(n, k) for n in WEIGHTS if n in SHARD_AXIS and n.startswith(p + "_")] + \
           [(n, i) for n in ("xa_q", "xa_kv", "xa_o", "f_up", "f_conv_w", "f_down")]


def _cols(st, lo, hi):
    ns = st.shape[-1]
    parts = []
    for j in range(NCHIP):
        a, b = max(lo, j * ns), min(hi, (j + 1) * ns)
        if a < b:
            parts.append(st[j][..., a - j * ns:b - j * ns])
    return parts[0] if len(parts) == 1 else jnp.concatenate(parts, axis=-1)


def _col_shards(pieces, ns):
    full = jnp.concatenate(pieces, axis=-1)
    return [full[..., j * ns:(j + 1) * ns] for j in range(NCHIP)]


def _rows(st):
    return st.reshape(st.shape[0] * st.shape[1], st.shape[2])


def prep_layer(i, G, R, lb):
    row = lambda a: a.reshape(1, -1)
    p, k = _MIXER_PREFIX[i % 3], i // 3
    kv, fup = G["xa_kv"], G["f_up"]
    layer = dict(ln_mix=R["ln_mix"][i:i + 1], ln_xattn=R["ln_xattn"][i:i + 1], ln_mem=R["ln_mem"][i:i + 1],
                 ln_ffn=R["ln_ffn"][i:i + 1], xq=_rows(G["xa_q"]), xk=(kv, (0, 2)), xv=(kv, (2, 2)), xo=_rows(G["xa_o"]),
                 fg=(fup, (0, 2)), fu=(fup, (2, 2)), fcw=_cols(G["f_conv_w"], 0, D_FF), fcb=R["f_conv_b"][i:i + 1],
                 fd=_rows(G["f_down"]))
    inw, wout = G[p + "_in_w"], _rows(G[p + "_out_w"])
    if p == "m":
        cw, cb = G["m_conv_w"], G["m_conv_b"]
        a, b, c = M_INNER, M_INNER + M_G * M_N, M_CONV
        layer["mix"] = dict(
            wz=_cols(inw, 0, M_INNER), wx=_cols(inw, M_INNER, M_INNER + a), wb=_cols(inw, M_INNER + a, M_INNER + b),
            wc=_cols(inw, M_INNER + b, M_MAIN), wdt=_pad_cols(_cols(inw, M_MAIN, M_IN)),
            cwx=_cols(cw, 0, a), cwb=_cols(cw, a, b), cwc=_cols(cw, b, c),
            cbx=row(_cols(cb, 0, a)), cbb=row(_cols(cb, a, b)), cbc=row(_cols(cb, b, c)),
            dtb=_pad_row(R["m_dt_bias"][k]), alog=_pad_row(R["m_a_log"][k]), dsk=_pad_row(R["m_d"][k]),
            nw=row(_cols(G["m_norm_w"], 0, M_INNER)), wout=wout)
    elif p == "h":
        layer["mix"] = dict(wq=(inw, (0, 1)), wf=(inw, (1, 1)), wi=(inw, (2, 1)), wg=(inw, (3, 1)),
                            lb=lb[i:i + 1], nw=row(R["h_norm_w"][k]), wout=wout)
    else:
        cw = G["g_conv_w"]
        layer["mix"] = dict(
            wq=_cols(inw, 0, D), wk=_cols(inw, D, 2 * D), wv=_cols(inw, 2 * D, G_CONV), wz=_cols(inw, G_CONV, G_MAIN),
            wba=_pad_cols(_cols(inw, G_MAIN, G_IN)), cwq=_cols(cw, 0, D), cwk=_cols(cw, D, 2 * D), cwv=_cols(cw, 2 * D, G_CONV),
            alog=_pad_row(R["g_a_log"][k], G_HV), dtb=_pad_row(R["g_dt_bias"][k], G_HV),
            nw=row(R["g_norm_w"][k]), wout=wout)
    return layer


def matrix_grad_parts(i, g):
    p = _MIXER_PREFIX[i % 3]
    m = g["mix"]
    by_rows = lambda a: a.reshape(NCHIP, a.shape[0] // NCHIP, a.shape[1])
    out = {"xa_q": by_rows(g["xq"]), "xa_kv": g["xk"], "xa_o": by_rows(g["xo"]), "f_up": g["fg"], "f_down": by_rows(g["fd"]),
           p + "_out_w": by_rows(m["wout"])}
    if p == "m":
        out["m_in_w"] = jnp.stack(_col_shards([m["wz"], m["wx"], m["wb"], m["wc"], m["wdt"]], M_IN // NCHIP))
    elif p == "h":
        out["h_in_w"] = m["wq"]
    else:
        out["g_in_w"] = jnp.stack(_col_shards([m["wq"], m["wk"], m["wv"], m["wz"], m["wba"]], G_IN // NCHIP))
    return out


def small_grads(grads, dfinal, hlb):
    cat = lambda xs: jnp.concatenate(xs, axis=1)
    out = {k: jnp.concatenate([g[k] for g in grads], axis=0) for k in ("ln_mix", "ln_xattn", "ln_mem", "ln_ffn")}
    out["final_norm"] = dfinal.reshape(D)
    out["f_conv_w"] = jnp.stack([g["fcw"] for g in grads])
    out["f_conv_b"] = jnp.concatenate([g["fcb"] for g in grads], axis=0)
    ms = [g["mix"] for i, g in enumerate(grads) if i % 3 == 0]
    out["m_conv_w"] = jnp.stack([cat([m["cwx"], m["cwb"], m["cwc"]]) for m in ms])
    out["m_conv_b"] = jnp.concatenate([cat([m["cbx"], m["cbb"], m["cbc"]]) for m in ms], axis=0)
    out["m_dt_bias"] = jnp.stack([m["dtb"] for m in ms])
    out["m_a_log"] = jnp.stack([m["alog"] for m in ms])
    out["m_d"] = jnp.stack([m["dsk"] for m in ms])
    out["m_norm_w"] = jnp.stack([m["nw"] for m in ms])
    hs = [(i, g["mix"]) for i, g in enumerate(grads) if i % 3 == 1]
    lb_rows = dict(hs)
    dlb = jnp.concatenate([lb_rows[i]["lb"] if i in lb_rows else jnp.zeros((1, D), f32) for i in range(DEPTH)], axis=0)
    out["h_lower_bounds"] = lower_bounds_bwd(hlb, dlb)
    out["h_norm_w"] = jnp.stack([m["nw"] for _, m in hs])
    gs = [g["mix"] for i, g in enumerate(grads) if i % 3 == 2]
    out["g_conv_w"] = jnp.stack([cat([m["cwq"], m["cwk"], m["cwv"]]) for m in gs])
    out["g_a_log"] = jnp.stack([m["alog"] for m in gs])
    out["g_dt_bias"] = jnp.stack([m["dtb"] for m in gs])
    out["g_norm_w"] = jnp.stack([m["nw"] for m in gs])
    return out


_HBM = pl.BlockSpec(memory_space=pltpu.HBM)


def _place():
    x, y, c = lax.axis_index("x"), lax.axis_index("y"), lax.axis_index("c")
    chips = [(1 - x, y), (x, 1 - y), (1 - x, 1 - y)]
    return x, y, c, chips


def gather_shards(name, tensors):
    n = len(tensors)

    def body(*refs):
        ins, outs = refs[:n], refs[n:2 * n]
        send_sems, recv_sems, loc_sems = refs[2 * n:]
        x, y, c, chips = _place()
        me = 2 * x + y
        local_copies, sends = [], []
        for t in range(n):
            loc = pltpu.make_async_copy(ins[t], outs[t].at[me], loc_sems.at[t])
            loc.start()
            local_copies.append(loc)
            for j, (px, py) in enumerate(chips):
                cp = pltpu.make_async_remote_copy(src_ref=ins[t], dst_ref=outs[t].at[me], send_sem=send_sems.at[3 * t + j],
                                                  recv_sem=recv_sems.at[3 * t + j], device_id=(px, py, c), device_id_type=MESH)
                cp.start()
                sends.append(cp)
        for t in range(n):
            for j, (px, py) in enumerate(chips):
                pltpu.make_async_remote_copy(src_ref=ins[t], dst_ref=outs[t].at[2 * px + py], send_sem=send_sems.at[3 * t + j],
                                             recv_sem=recv_sems.at[3 * t + j], device_id=(px, py, c),
                                             device_id_type=MESH).wait_recv()
        for cp in sends:
            cp.wait_send()
        for cp in local_copies:
            cp.wait()

    return pl.pallas_call(
        body, name=name, in_specs=[_HBM] * n, out_specs=[_HBM] * n,
        out_shape=[_S((NCHIP,) + a.shape, a.dtype) for a in tensors],
        scratch_shapes=[pltpu.SemaphoreType.DMA((3 * n,)), pltpu.SemaphoreType.DMA((3 * n,)), pltpu.SemaphoreType.DMA((n,))])(*tensors)


_SEM = pl.BlockSpec(memory_space=pltpu.SEMAPHORE)
_ANY = pl.BlockSpec(memory_space=pl.ANY)
_SPLIT = pltpu.CompilerParams(has_side_effects=pltpu.SideEffectType.DATAFLOW_SIDE_EFFECTING)


def _hbm(a):
    return pltpu.with_memory_space_constraint(a, pltpu.HBM)


def _split_start(name, srcs, lands, dep, copies):
    n = len(srcs)

    def body(*refs):
        src_refs, land_refs = refs[:n], refs[n:2 * n]
        send_sems, recv_sems = refs[2 * n + 1], refs[2 * n + 2]
        token = refs[-1]
        for cp in copies(src_refs, land_refs, send_sems, recv_sems):
            cp.start()
        token[...] = jnp.zeros_like(token)

    thru = [pltpu.HBM(a.shape, a.dtype) for a in list(srcs) + list(lands)]
    out = pl.pallas_call(
        body, name=name, in_specs=[_HBM] * (2 * n) + [_ANY],
        out_specs=[_SEM, _SEM] + [_HBM] * (2 * n) + [pl.BlockSpec(memory_space=pltpu.VMEM)],
        out_shape=[pltpu.SemaphoreType.DMA((3 * n,)), pltpu.SemaphoreType.DMA((3 * n,))] + thru + [_S((8, 128), f32)],
        input_output_aliases={t: 2 + t for t in range(2 * n)}, compiler_params=_SPLIT,
    )(*[_hbm(a) for a in srcs], *[_hbm(a) for a in lands], dep)
    return out[0], out[1], out[2:2 + n], out[2 + n:2 + 2 * n], out[-1]


def _split_wait(name, started, after, copies):
    send_sems, recv_sems, srcs, lands, _ = started
    n = len(srcs)

    def body(*refs):
        src_refs, land_refs = refs[:n], refs[n:2 * n]
        s_sems, r_sems = refs[2 * n], refs[2 * n + 1]
        for cp in copies(src_refs, land_refs, s_sems, r_sems):
            cp.wait_send()
            cp.wait_recv()

    out = pl.pallas_call(
        body, name=name, in_specs=[_HBM] * (2 * n) + [_SEM, _SEM, _ANY], out_specs=[_HBM] * (2 * n),
        out_shape=[pltpu.HBM(a.shape, a.dtype) for a in list(srcs) + list(lands)],
        input_output_aliases={t: t for t in range(2 * n)}, compiler_params=_SPLIT,
    )(*srcs, *lands, send_sems, recv_sems, after)
    return out[:n], out[n:]


def _gather_copies(arrive):
    def copies(src_refs, land_refs, send_sems, recv_sems):
        x, y, c, chips = _place()
        out = []
        for t, (s, l) in enumerate(zip(src_refs, land_refs, strict=True)):
            for j, (px, py) in enumerate(chips):
                slot = 2 * px + py if arrive else 2 * x + y
                out.append(pltpu.make_async_remote_copy(src_ref=s, dst_ref=l.at[slot], send_sem=send_sems.at[3 * t + j],
                                                        recv_sem=recv_sems.at[3 * t + j], device_id=(px, py, c), device_id_type=MESH))
        return out
    return copies


def gather_start(name, tensors, me, dep):
    lands = [lax.dynamic_update_index_in_dim(jnp.zeros((NCHIP,) + a.shape, a.dtype), a, me, 0) for a in tensors]
    return _split_start(name, tensors, lands, dep, _gather_copies(False))


def gather_wait(name, started, after):
    return _split_wait(name, started, after, _gather_copies(True))


def _scatter_copies(src_refs, land_refs, send_sems, recv_sems):
    x, y, c, chips = _place()
    out = []
    for t, (s, l) in enumerate(zip(src_refs, land_refs, strict=True)):
        for j, (px, py) in enumerate(chips):
            out.append(pltpu.make_async_remote_copy(src_ref=s.at[2 * px + py], dst_ref=l.at[j], send_sem=send_sems.at[3 * t + j],
                                                    recv_sem=recv_sems.at[3 * t + j], device_id=(px, py, c), device_id_type=MESH))
    return out


def scatter_start(name, parts, dep):
    lands = [lax.empty((3,) + a.shape[1:], a.dtype) for a in parts]
    return _split_start(name, parts, lands, dep, _scatter_copies)


def scatter_wait(name, started, after):
    return _split_wait(name, started, after, _scatter_copies)


def sum_parts(name, part, land, me):
    shape = land.shape[1:]
    c = shape[-1]
    r = land.size // (3 * c)
    tm = _tile(r, (256, 128, 64, 32, 16, 8))

    def body(me_ref, p_ref, l_ref, o_ref):
        o_ref[...] = p_ref[...].astype(f32) + l_ref[0].astype(f32) + l_ref[1].astype(f32) + l_ref[2].astype(f32)

    grid_spec = pltpu.PrefetchScalarGridSpec(
        num_scalar_prefetch=1, grid=(r // tm,),
        in_specs=[pl.BlockSpec((None, tm, c), lambda i, me_ref: (me_ref[0], i, 0)),
                  pl.BlockSpec((3, tm, c), lambda i, me_ref: (0, i, 0))],
        out_specs=pl.BlockSpec((tm, c), lambda i, me_ref: (i, 0)))
    out = pl.pallas_call(body, name=name, grid_spec=grid_spec, out_shape=_S((r, c), f32), compiler_params=_cp())(
        me.reshape(1).astype(jnp.int32), part.reshape(NCHIP, r, c), land.reshape(3, r, c))
    return out.reshape(shape)


def sibling_swap(tensors):
    n = len(tensors)

    def body(*refs):
        ins, outs = refs[:n], refs[n:2 * n]
        send_sems, recv_sems = refs[2 * n:]
        x, y, c, _ = _place()
        started = []
        for t in range(n):
            cp = pltpu.make_async_remote_copy(src_ref=ins[t], dst_ref=outs[t], send_sem=send_sems.at[t], recv_sem=recv_sems.at[t],
                                              device_id=(x, y, 1 - c), device_id_type=MESH)
            cp.start()
            started.append(cp)
        for cp in started:
            cp.wait_recv()
        for cp in started:
            cp.wait_send()

    return pl.pallas_call(
        body, name="sibling_swap", in_specs=[_HBM] * n, out_specs=[_HBM] * n,
        out_shape=[_S(a.shape, a.dtype) for a in tensors],
        scratch_shapes=[pltpu.SemaphoreType.DMA((n,)), pltpu.SemaphoreType.DMA((n,))])(*tensors)


def allreduce_small(v):
    r, n = v.shape

    def body(x_ref, out_ref, gat, send_sems, recv_sems, local_sem):
        x, y, c, chips = _place()
        me, sibling = (x, y, c), (x, y, 1 - c)

        def rows(px, py, pc):
            return gat.at[pl.ds((4 * px + 2 * py + pc) * r, r), :]

        def copy(k, block, to, src=None):
            return pltpu.make_async_remote_copy(src_ref=rows(*block) if src is None else src, dst_ref=rows(*block),
                                                send_sem=send_sems.at[k], recv_sem=recv_sems.at[k], device_id=to,
                                                device_id_type=MESH)

        mine = pltpu.make_async_copy(x_ref, rows(*me), local_sem)
        mine.start()
        first = [copy(0, me, sibling, src=x_ref)] + [copy(1 + j, me, (*chip, c), src=x_ref) for j, chip in enumerate(chips)]
        for cp in first:
            cp.start()
        passed = [copy(4 + j, (*chip, c), sibling) for j, chip in enumerate(chips)]
        for j, chip in enumerate(chips):
            copy(1 + j, (*chip, c), me).wait_recv()
            passed[j].start()
        copy(0, sibling, me).wait_recv()
        for j, chip in enumerate(chips):
            copy(4 + j, (*chip, 1 - c), me).wait_recv()
        for cp in first + passed:
            cp.wait_send()
        mine.wait()
        acc = gat[0:r, :]
        for d in range(1, 8):
            acc = acc + gat[d * r:(d + 1) * r, :]
        out_ref[...] = acc

    vm = pl.BlockSpec(memory_space=pltpu.VMEM)
    return pl.pallas_call(
        body, name="allreduce_small", in_specs=[vm], out_specs=vm, out_shape=_S((r, n), v.dtype),
        scratch_shapes=[pltpu.VMEM((8 * r, n), v.dtype), pltpu.SemaphoreType.DMA((7,)), pltpu.SemaphoreType.DMA((7,)),
                        pltpu.SemaphoreType.DMA],
        compiler_params=_cp())(v)


SMALL_ROW = 1024


def kernel(x, mem, ln_mix, ln_xattn, ln_mem, ln_ffn, final_norm, m_in_w, m_conv_w, m_conv_b, m_dt_bias, m_a_log, m_d, m_norm_w, m_out_w, h_in_w, h_lower_bounds, h_norm_w, h_out_w, g_in_w, g_conv_w, g_a_log, g_dt_bias, g_norm_w, g_out_w, xa_q, xa_kv, xa_o, f_up, f_conv_w, f_conv_b, f_down, loss_target, m_ln_mix, m_ln_xattn, m_ln_mem, m_ln_ffn, m_final_norm, m_m_in_w, m_m_conv_w, m_m_conv_b, m_m_dt_bias, m_m_a_log, m_m_d, m_m_norm_w, m_m_out_w, m_h_in_w, m_h_lower_bounds, m_h_norm_w, m_h_out_w, m_g_in_w, m_g_conv_w, m_g_a_log, m_g_dt_bias, m_g_norm_w, m_g_out_w, m_xa_q, m_xa_kv, m_xa_o, m_f_up, m_f_conv_w, m_f_conv_b, m_f_down, v_ln_mix, v_ln_xattn, v_ln_mem, v_ln_ffn, v_final_norm, v_m_in_w, v_m_conv_w, v_m_conv_b, v_m_dt_bias, v_m_a_log, v_m_d, v_m_norm_w, v_m_out_w, v_h_in_w, v_h_lower_bounds, v_h_norm_w, v_h_out_w, v_g_in_w, v_g_conv_w, v_g_a_log, v_g_dt_bias, v_g_norm_w, v_g_out_w, v_xa_q, v_xa_kv, v_xa_o, v_f_up, v_f_conv_w, v_f_conv_b, v_f_down):
    local = dict(zip(WEIGHTS, (ln_mix, ln_xattn, ln_mem, ln_ffn, final_norm, m_in_w, m_conv_w, m_conv_b, m_dt_bias, m_a_log, m_d, m_norm_w, m_out_w, h_in_w, h_lower_bounds, h_norm_w, h_out_w, g_in_w, g_conv_w, g_a_log, g_dt_bias, g_norm_w, g_out_w, xa_q, xa_kv, xa_o, f_up, f_conv_w, f_conv_b, f_down), strict=True))
    mom_m = dict(zip(WEIGHTS, (m_ln_mix, m_ln_xattn, m_ln_mem, m_ln_ffn, m_final_norm, m_m_in_w, m_m_conv_w, m_m_conv_b, m_m_dt_bias, m_m_a_log, m_m_d, m_m_norm_w, m_m_out_w, m_h_in_w, m_h_lower_bounds, m_h_norm_w, m_h_out_w, m_g_in_w, m_g_conv_w, m_g_a_log, m_g_dt_bias, m_g_norm_w, m_g_out_w, m_xa_q, m_xa_kv, m_xa_o, m_f_up, m_f_conv_w, m_f_conv_b, m_f_down), strict=True))
    mom_v = dict(zip(WEIGHTS, (v_ln_mix, v_ln_xattn, v_ln_mem, v_ln_ffn, v_final_norm, v_m_in_w, v_m_conv_w, v_m_conv_b, v_m_dt_bias, v_m_a_log, v_m_d, v_m_norm_w, v_m_out_w, v_h_in_w, v_h_lower_bounds, v_h_norm_w, v_h_out_w, v_g_in_w, v_g_conv_w, v_g_a_log, v_g_dt_bias, v_g_norm_w, v_g_out_w, v_xa_q, v_xa_kv, v_xa_o, v_f_up, v_f_conv_w, v_f_conv_b, v_f_down), strict=True))
    nb, seq, _ = x.shape
    me = 2 * lax.axis_index("x") + lax.axis_index("y")

    repl = {n: local[n] for n in REPLICATED}
    lb = lower_bounds_fwd(repl["h_lower_bounds"])
    names = [layer_weight_names(i) for i in range(DEPTH)]
    shards = [[local[n][k].astype(bf16) if n in MATRICES else local[n][k] for n, k in names[i]] for i in range(DEPTH)]
    flying = {}

    def weights_of(i, x_in):
        gathered = gather_shards("gather_l0", shards[0]) if i == 0 else gather_wait(f"gather_wait_l{i}", flying.pop(i), x_in)[1]
        layer = prep_layer(i, {n: g for (n, _), g in zip(names[i], gathered, strict=True)}, repl, lb)
        if i + 1 < DEPTH:
            flying[i + 1] = gather_start(f"gather_start_l{i + 1}", shards[i + 1], me, gathered[0])
            layer["ln_mix"] = layer["ln_mix"] + flying[i + 1][4][0, 0]
        return layer

    scattering = {}

    def grads_done(i, g, dx_i):
        parts = matrix_grad_parts(i, g)
        scattering[i] = (list(parts), scatter_start(f"scatter_start_l{i}", list(parts.values()), dx_i))
        return scattering[i][1][4]

    loss, dx, lgrads, dfinal = local_step(x.reshape(nb * seq, D), mem.reshape(nb * N_MEM, D), loss_target.reshape(nb * seq, D),
                                          weights_of, repl["final_norm"].reshape(1, D), nb, grads_done)
    grads = small_grads(lgrads, dfinal, repl["h_lower_bounds"])

    small_names = REPLICATED + SMALL_SHARDED
    flat = jnp.concatenate([grads[n].astype(f32).reshape(-1) for n in small_names] + [loss[0, 0:1]])
    rows = -(-flat.shape[0] // (8 * SMALL_ROW)) * 8
    flat = jnp.pad(flat, (0, rows * SMALL_ROW - flat.shape[0])).reshape(rows, SMALL_ROW)
    red = allreduce_small(flat).reshape(-1)
    gsum, off = {}, 0
    for n in small_names:
        size = grads[n].size
        g = red[off:off + size].reshape(grads[n].shape)
        off += size
        if n in SHARD_AXIS:
            ax = SHARD_AXIS[n]
            w = g.shape[ax] // NCHIP
            g = lax.dynamic_slice_in_dim(g, me * w, w, axis=ax)
        gsum[n] = g
    loss_out = red[off]

    layer_sums = {n: [] for n in MATRICES}
    for i in range(DEPTH):
        part_names, started = scattering[i]
        sent, landed = scatter_wait(f"scatter_wait_l{i}", started, dx)
        for n, part, land in zip(part_names, sent, landed, strict=True):
            layer_sums[n].append(sum_parts(f"sum_l{i}_{n}", part, land, me))
    sums = [jnp.stack(layer_sums[n]) for n in MATRICES]
    swapped = sibling_swap(sums)

    outs = {}
    for n, s, o in zip(MATRICES, sums, swapped, strict=True):
        outs[n] = adamw(local[n], s, mom_m[n], mom_v[n], f"adamw_{n}", g2=o)
    for n in small_names:
        outs[n] = adamw(local[n], gsum[n].reshape(local[n].shape), mom_m[n], mom_v[n], f"adamw_{n}")
    res = [loss_out, dx.reshape(nb, seq, D)]
    for k in range(4):
        res += [outs[n][k] for n in WEIGHTS]
    return tuple(res)
```

```python
import functools

import jax
import jax.numpy as jnp
from jax import lax
from jax.experimental import pallas as pl
from jax.experimental.pallas import tpu as pltpu

f32 = jnp.float32
bf16 = jnp.bfloat16
HIGHEST = lax.Precision.HIGHEST
MESH = pl.DeviceIdType.MESH

D = 1024
DEPTH = 4
EPS = 1e-6
N_MEM = 256
M_INNER, M_P, M_H, M_G, M_N, M_Q = 2048, 64, 32, 8, 128, 64
M_CONV = M_INNER + 2 * M_G * M_N
M_MAIN = M_INNER + M_CONV
M_IN = M_MAIN + M_H
H_H, H_K, H_Q = 8, 128, 32
G_HV, G_HK, G_K, G_Q = 16, 8, 128, 64
G_CONV, G_VAL = 4096, 2048
G_MAIN = G_CONV + G_VAL
G_IN = G_MAIN + 2 * G_HV
X_H, X_D = 4, 256
D_FF = 2816
ADAM_LR, ADAM_B1, ADAM_B2, ADAM_EPS, ADAM_WD, ADAM_STEP = 0.001, 0.9, 0.999, 1e-08, 0.01, 10
VMEM_LIMIT = 56 * 1024 * 1024
NCHIP = 4


def _cp(**kw):
    return pltpu.CompilerParams(vmem_limit_bytes=VMEM_LIMIT, **kw)


def _S(shape, dtype):
    return jax.ShapeDtypeStruct(tuple(shape), dtype)


def _dg(a, b, ca, cb, prec=None):
    return lax.dot_general(a, b, (((ca,), (cb,)), ((), ())), precision=prec, preferred_element_type=f32)


def _hdot(a, b, ca=1, cb=0, prec=lax.Precision.HIGH):
    return _dg(a.astype(f32), b.astype(f32), ca, cb, prec)


def _bdot_raw(a, b, ca, cb):
    return _dg(a.astype(bf16), b.astype(bf16), ca, cb)


@functools.partial(jax.custom_vjp, nondiff_argnums=(2, 3))
def _bdot(a, b, ca, cb):
    return _bdot_raw(a, b, ca, cb)


def _bdot_fwd(a, b, ca, cb):
    return _bdot_raw(a, b, ca, cb), (a, b)


def _bdot_bwd(ca, cb, res, g):
    a, b = res
    if ca == 1:
        da = _bdot_raw(g, b, 1, 1 if cb == 0 else 0)
    else:
        da = _bdot_raw(b, g, 1 if cb == 0 else 0, 1)
    if cb == 0:
        db = _bdot_raw(a, g, 0 if ca == 1 else 1, 0)
    else:
        db = _bdot_raw(g, a, 0, 0 if ca == 1 else 1)
    return da.astype(a.dtype), db.astype(b.dtype)


_bdot.defvjp(_bdot_fwd, _bdot_bwd)


def _shift_down_raw(x, k):
    r = lax.broadcasted_iota(jnp.int32, x.shape, 0)
    return jnp.where(r >= k, pltpu.roll(x, k, 0), 0.0)


def _shift_up_raw(x, k):
    n = x.shape[0]
    r = lax.broadcasted_iota(jnp.int32, x.shape, 0)
    return jnp.where(r < n - k, pltpu.roll(x, n - k, 0), 0.0)


@functools.partial(jax.custom_vjp, nondiff_argnums=(1,))
def _shift_down(x, k):
    return _shift_down_raw(x, k)


_shift_down.defvjp(lambda x, k: (_shift_down_raw(x, k), None), lambda k, _, g: (_shift_up_raw(g, k),))


def _rms(x, w):
    return x * lax.rsqrt(jnp.mean(x * x, axis=-1, keepdims=True) + EPS) * w


def _silu(x):
    return x * jax.nn.sigmoid(x)


def _masks(q):
    r = lax.broadcasted_iota(jnp.int32, (q, q), 0)
    c = lax.broadcasted_iota(jnp.int32, (q, q), 1)
    return r >= c, r > c


def _colvec(row):
    return jnp.transpose(jnp.broadcast_to(row, (8, row.shape[1])))[:, 0:1]


def _tile(n, cands):
    for c in cands:
        if n % c == 0:
            return c
    return n


def mm(a, b, *, ta=False, tb=False, bsel=None, out_stack=None, out_slots=None, into=None, res=None, out_dtype=f32, name):
    m, k = (a.shape[1], a.shape[0]) if ta else a.shape
    ca, cb = (0 if ta else 1), (1 if tb else 0)
    tm = _tile(m, (512, 256, 128))
    if bsel is not None:
        s0, cnt = bsel
        ns = b.shape[2]
        if tb:
            n, tn, tk = b.shape[1], b.shape[1], ns
            b_spec = pl.BlockSpec((None, tn, ns), lambda i, j, kk: (s0 + kk, j, 0))
        else:
            n, tn, tk = cnt * ns, ns, k
            b_spec = pl.BlockSpec((None, tk, ns), lambda i, j, kk: (s0 + j, kk, 0))
    else:
        n = b.shape[0] if tb else b.shape[1]
        tn = out_stack if out_stack else (n if n <= 2816 else _tile(n, (2048, 1024, 512, 256, 128)))
        tk = k if (k <= 4096 and not ta) else _tile(k, (1024, 512, 256, 128))
        b_spec = pl.BlockSpec((tn, tk), lambda i, j, kk: (j, kk)) if tb else pl.BlockSpec((tk, tn), lambda i, j, kk: (kk, j))
    nk = k // tk
    if out_stack:
        total, first = out_slots if out_slots else (n // tn, 0)
        out_spec = pl.BlockSpec((None, tm, tn), lambda i, j, kk: (first + j, i, 0))
        out_shape = _S((total, m, tn), out_dtype)
    else:
        out_spec = pl.BlockSpec((tm, tn), lambda i, j, kk: (i, j))
        out_shape = _S((m, n), out_dtype)

    def body(*refs):
        a_ref, b_ref = refs[:2]
        r_ref = refs[2] if res is not None else None
        o_ref, acc = refs[-2:]
        kk = pl.program_id(2)

        @pl.when(kk == 0)
        def _():
            acc[...] = jnp.zeros_like(acc)

        acc[...] += _bdot_raw(a_ref[...], b_ref[...], ca, cb)

        @pl.when(kk == nk - 1)
        def _():
            v = acc[...]
            if r_ref is not None:
                v = v + r_ref[...]
            o_ref[...] = v.astype(o_ref.dtype)

    a_spec = pl.BlockSpec((tk, tm), lambda i, j, kk: (kk, i)) if ta else pl.BlockSpec((tm, tk), lambda i, j, kk: (i, kk))
    in_specs = [a_spec, b_spec]
    args = [a, b]
    if res is not None:
        in_specs.append(pl.BlockSpec((tm, tn), lambda i, j, kk: (i, j)))
        args.append(res)
    aliases = {}
    if into is not None:
        aliases = {len(args): 0}
        in_specs.append(pl.BlockSpec(memory_space=pl.ANY))
        args.append(into)
    return pl.pallas_call(
        body, name=name, grid=(m // tm, n // tn, nk), in_specs=in_specs, out_specs=out_spec, out_shape=out_shape,
        scratch_shapes=[pltpu.VMEM((tm, tn), f32)], input_output_aliases=aliases, compiler_params=_cp())(*args)


def rows_call(name, fn, rows, pars, row_out, acc_out=(), tm=512):
    t = rows[0].shape[0]
    tm = min(tm, t)
    assert t % tm == 0, (name, t, tm)
    nr, npar, nro = len(rows), len(pars), len(row_out)

    def body(*refs):
        rv = [r[...] for r in refs[:nr]]
        pv = [r[...] for r in refs[nr:nr + npar]]
        ro_refs = refs[nr + npar:nr + npar + nro]
        ao_refs = refs[nr + npar + nro:]
        ro, ao = fn(*rv, *pv)
        for r, v in zip(ro_refs, ro, strict=True):
            r[...] = v.astype(r.dtype)
        if ao_refs:
            @pl.when(pl.program_id(0) == 0)
            def _():
                for r in ao_refs:
                    r[...] = jnp.zeros_like(r)
            for r, v in zip(ao_refs, ao, strict=True):
                r[...] += v.astype(r.dtype)

    in_specs = [pl.BlockSpec((tm, r.shape[1]), lambda i: (i, 0)) for r in rows]
    in_specs += [pl.BlockSpec(p.shape, lambda i: (0, 0)) for p in pars]
    out_specs = [pl.BlockSpec((tm, c), lambda i: (i, 0)) for c, _ in row_out]
    out_specs += [pl.BlockSpec(s, lambda i: (0, 0)) for s, _ in acc_out]
    out_shape = [_S((t, c), dt) for c, dt in row_out] + [_S(s, dt) for s, dt in acc_out]
    return pl.pallas_call(body, name=name, grid=(t // tm,), in_specs=in_specs, out_specs=out_specs,
                          out_shape=out_shape, compiler_params=_cp())(*rows, *pars)


def rms_fwd(x, w, name):
    return rows_call(name, lambda xv, wv: ((_rms(xv, wv),), ()), [x], [w], [(x.shape[1], bf16)])[0]


def rms_bwd(x, w, dy, dres, name):
    def fn(*a):
        if dres is None:
            xv, dyv, wv = a
        else:
            xv, dyv, drv, wv = a
        _, vjp = jax.vjp(_rms, xv, wv)
        dx, dw = vjp(dyv.astype(f32))
        if dres is not None:
            dx = dx + drv
        return (dx,), (dw,)
    rows = [x, dy] + ([] if dres is None else [dres])
    return rows_call(name, fn, rows, [w], [(x.shape[1], f32)], [(w.shape, f32)])


def cols_call(name, fn, seqs, pars, outs, *, nb, ct, ncol, dseed=None):
    ns, npar = len(seqs), len(pars)
    seq_len = seqs[0].shape[0] // nb
    nd = 0 if dseed is None else len(dseed)

    def body(*refs):
        sv = [r[...] for r in refs[:ns]]
        pv = [r[...] for r in refs[ns:ns + npar]]
        if dseed is None:
            o_refs = refs[ns + npar:]
            for r, v in zip(o_refs, fn(*sv, *pv), strict=True):
                r[...] = v.astype(r.dtype)
            return
        dv = [r[...].astype(f32) for r in refs[ns + npar:ns + npar + nd]]
        ds_refs = refs[ns + npar + nd:ns + npar + nd + ns]
        dp_refs = refs[ns + npar + nd + ns:]
        _, vjp = jax.vjp(fn, *[v.astype(f32) for v in sv], *pv)
        g = vjp(tuple(dv))
        for r, v in zip(ds_refs, g[:ns], strict=True):
            r[...] = v.astype(r.dtype)

        @pl.when(pl.program_id(1) == 0)
        def _():
            for r in dp_refs:
                r[...] = jnp.zeros_like(r)
        for r, v in zip(dp_refs, g[ns:], strict=True):
            r[...] += v

    full = pl.BlockSpec((seq_len, ct), lambda j, b: (b, j))
    in_specs = [full for _ in seqs]
    in_specs += [pl.BlockSpec((p.shape[0], ct), lambda j, b: (0, j)) for p in pars]
    args = list(seqs) + list(pars)
    if dseed is None:
        out_specs = [full for _ in outs]
        out_shape = [_S((nb * seq_len, ncol * ct), dt) for dt in outs]
    else:
        in_specs += [full for _ in dseed]
        args += list(dseed)
        out_specs = [full for _ in seqs] + [pl.BlockSpec((p.shape[0], ct), lambda j, b: (0, j)) for p in pars]
        out_shape = [_S((nb * seq_len, ncol * ct), f32) for _ in seqs] + [_S(p.shape, f32) for p in pars]
    return pl.pallas_call(body, name=name, grid=(ncol, nb), in_specs=in_specs, out_specs=out_specs,
                          out_shape=out_shape, compiler_params=_cp())(*args)


def _conv4_silu(x, w, b):
    y = x * w[3:4] + _shift_down(x, 1) * w[2:3] + _shift_down(x, 2) * w[1:2] + _shift_down(x, 3) * w[0:1] + b
    return (_silu(y),)


def _conv4_silu_nobias(x, w):
    y = x * w[3:4] + _shift_down(x, 1) * w[2:3] + _shift_down(x, 2) * w[1:2] + _shift_down(x, 3) * w[0:1]
    return (_silu(y),)


def _ffn_act(gate, up, w, b):
    y = gate * w[2:3] + _shift_down(gate, 1) * w[1:2] + _shift_down(gate, 2) * w[0:1] + b
    return (_silu(y) * up,)


def scan_call(name, chunk_fn, seqs, pars, consts, outs, *, nb, nh, q, state_shape, states=None, dseed=None):
    t = seqs[0][0].shape[0]
    nc = t // (nb * q)
    ns, npar, ncon, no = len(seqs), len(pars), len(consts), len(outs)
    s0, s1 = state_shape
    bwd = dseed is not None

    def cidx(c):
        return (nc - 1 - c) if bwd else c

    def rowblk(b, c):
        return b * nc + cidx(c)

    def seq_spec(w, colfn):
        return pl.BlockSpec((q, w), lambda b, c, h: (rowblk(b, c), colfn(h)))

    def par_spec(shape, idxfn):
        return pl.BlockSpec(shape, lambda b, c, h: idxfn(h))

    st_spec = pl.BlockSpec((s0, s1), lambda b, c, h: ((rowblk(b, c)) * nh + h, 0))
    in_specs = [seq_spec(w, cf) for _, w, cf, _ in seqs]
    in_specs += [par_spec(s, f) for _, s, f in pars] + [par_spec(s, f) for _, s, f in consts]
    args = [a for a, _, _, _ in seqs] + [a for a, _, _ in pars] + [a for a, _, _ in consts]

    if not bwd:
        def body(*refs):
            sv = [r[...] for r in refs[:ns]]
            pv = [r[...] for r in refs[ns:ns + npar]]
            cv = [r[...] for r in refs[ns + npar:ns + npar + ncon]]
            o_refs = refs[ns + npar + ncon:ns + npar + ncon + no]
            save_ref = refs[ns + npar + ncon + no]
            st = refs[-1]
            c, h = pl.program_id(1), pl.program_id(2)

            @pl.when(c == 0)
            def _():
                st[h] = jnp.zeros((s0, s1), f32)
            s_in = st[h]
            save_ref[...] = s_in
            o, s_out = chunk_fn(*sv, *pv, s_in, *cv)
            st[h] = s_out
            for r, v in zip(o_refs, o, strict=True):
                r[...] = v.astype(r.dtype)

        out_specs = [seq_spec(w, cf) for _, w, cf, _ in outs] + [st_spec]
        out_shape = [_S((t, cc), dt) for cc, _, _, dt in outs] + [_S((nb * nc * nh * s0, s1), f32)]
        return pl.pallas_call(body, name=name, grid=(nb, nc, nh), in_specs=in_specs, out_specs=out_specs,
                              out_shape=out_shape, scratch_shapes=[pltpu.VMEM((nh, s0, s1), f32)],
                              compiler_params=_cp())(*args)

    def body(*refs):
        i = 0
        sv = [r[...] for r in refs[i:i + ns]]; i += ns
        pv = [r[...] for r in refs[i:i + npar]]; i += npar
        cv = [r[...] for r in refs[i:i + ncon]]; i += ncon
        dv = [r[...].astype(f32) for r in refs[i:i + no]]; i += no
        s_in = refs[i][...]; i += 1
        ds_refs = refs[i:i + ns]; i += ns
        dp_refs = refs[i:i + npar]; i += npar
        dst = refs[-1]
        b, c, h = pl.program_id(0), pl.program_id(1), pl.program_id(2)

        @pl.when(c == 0)
        def _():
            dst[h] = jnp.zeros((s0, s1), f32)

        @pl.when((b == 0) & (c == 0) & (h == 0))
        def _():
            for r in dp_refs:
                r[...] = jnp.zeros_like(r)

        fn = lambda *a: chunk_fn(*a, *cv)
        _, vjp = jax.vjp(fn, *[v.astype(f32) for v in sv], *pv, s_in)
        g = vjp((tuple(dv), dst[h]))
        dst[h] = g[ns + npar]
        for (_, _, _, rep), r, v in zip(seqs, ds_refs, g[:ns], strict=True):
            if rep == 1:
                r[...] = v.astype(r.dtype)
            else:
                @pl.when(h % rep == 0)
                def _(r=r, v=v):
                    r[...] = v.astype(r.dtype)

                @pl.when(h % rep != 0)
                def _(r=r, v=v):
                    r[...] += v.astype(r.dtype)
        for r, v in zip(dp_refs, g[ns:ns + npar], strict=True):
            r[h] += v

    in_specs += [seq_spec(w, cf) for _, w, cf, _ in outs] + [st_spec]
    args += list(dseed) + [states]
    out_specs = [seq_spec(w, cf) for _, w, cf, _ in seqs]
    out_specs += [pl.BlockSpec((nh,) + tuple(s), lambda b, c, h: (0, 0, 0)) for _, s, _ in pars]
    out_shape = [_S(a.shape, f32) for a, _, _, _ in seqs] + [_S((nh,) + tuple(s), f32) for _, s, _ in pars]
    return pl.pallas_call(body, name=name, grid=(nb, nc, nh), in_specs=in_specs, out_specs=out_specs,
                          out_shape=out_shape, scratch_shapes=[pltpu.VMEM((nh, s0, s1), f32)],
                          compiler_params=_cp())(*args)


def _ssd_group(xs, bm, cm, z, dtr, dtb, alog, dsk, nw, st, e):
    q, gp = xs.shape[0], SSD_GP
    heads = range(4 * gp)
    sl = [slice(i * M_P, (i + 1) * M_P) for i in heads]
    gsl = [slice(g * M_N, (g + 1) * M_N) for g in range(gp)]
    incl, _ = _masks(q)
    dt = jax.nn.softplus(dtr + dtb[0:1])
    dte = _hdot(dt, e, prec=HIGHEST)
    ae = _hdot(-jnp.exp(alog), e, prec=HIGHEST)[0:1]
    de = _hdot(dsk, e, prec=HIGHEST)[0:1]
    xc = xs * dte
    acum = _hdot(incl.astype(f32), dte * ae, prec=HIGHEST)
    last = acum[q - 1:q]
    eac, eend, elast = jnp.exp(acum), jnp.exp(last - acum), jnp.exp(last)
    xe = xc * eend
    bms, cms = [bm[:, s] for s in gsl], [cm[:, s] for s in gsl]
    cb = [_bdot(cms[g], bms[g], 1, 1) for g in range(gp)]
    decs = []
    for i in heads:
        a_i = acum[:, sl[i]]
        diff = jnp.where(incl, a_i[:, 0:1] - jnp.transpose(a_i)[0:1, :], 0.0)
        decs.append(jnp.where(incl, jnp.exp(diff), 0.0))
    sts = [st[sl[i], :] for i in heads]
    yd = [_bdot(cb[i // 4] * decs[i], xc[:, sl[i]], 1, 0) for i in heads]
    yo = [_bdot(cms[i // 4], sts[i], 1, 1) for i in heads]
    ds = [_bdot(xe[:, sl[i]], bms[i // 4], 0, 0) for i in heads]
    new = [sts[i] * elast[:, i * M_P:i * M_P + 1] + ds[i] for i in heads]
    y = jnp.concatenate(yd, axis=1) + jnp.concatenate(yo, axis=1) * eac + de * xs
    y = y * _silu(z)
    yn = [_rms(y[:, g * 256:(g + 1) * 256], nw[:, g * 256:(g + 1) * 256]) for g in range(gp)]
    return (jnp.concatenate(yn, axis=1),), jnp.concatenate(new, axis=0)


def _gla_group(qr, fr, ir, gr, lb, nw, st):
    q, hp = qr.shape[0], GLA_HP
    heads = range(hp)
    sl = [slice(i * H_K, (i + 1) * H_K) for i in heads]
    incl, _ = _masks(q)
    fg = lb + (1.0 - lb) * jax.nn.sigmoid(fr)
    qq = _silu(qr) * (H_K ** -0.5)
    k = 1.0 - fg
    gc = _hdot(incl.astype(f32), jnp.log(fg))
    gl = gc[q - 1:q]
    qd, ki, ke = qq * jnp.exp(gc), k * jnp.exp(-gc), k * jnp.exp(gl - gc)
    egl = jnp.exp(gl)
    sts = [st[sl[i], :] for i in heads]
    att = [jnp.where(incl, _bdot(qd[:, sl[i]], ki[:, sl[i]], 1, 1), 0.0) for i in heads]
    o1 = [_bdot(att[i], ir[:, sl[i]], 1, 0) for i in heads]
    o2 = [_bdot(qd[:, sl[i]], sts[i], 1, 0) for i in heads]
    kv = [_bdot(ke[:, sl[i]], ir[:, sl[i]], 0, 0) for i in heads]
    new = [sts[i] * _colvec(egl[:, sl[i]]) + kv[i] for i in heads]
    on = [_rms(o1[i] + o2[i], nw) * _silu(gr[:, sl[i]]) for i in heads]
    return (jnp.concatenate(on, axis=1),), jnp.concatenate(new, axis=0)


def _tri_inv_many(ms):
    n = ms[0].shape[0]
    r = lax.broadcasted_iota(jnp.int32, (n, n), 0)
    c = lax.broadcasted_iota(jnp.int32, (n, n), 1)
    eye = (r == c).astype(f32)
    ts = [eye - m for m in ms]
    ps = list(ms)
    for _ in range(max(1, (n - 1).bit_length() - 1)):
        ps = [_hdot(p, p) for p in ps]
        ts = [t + _hdot(t, p) for t, p in zip(ts, ps)]
    return ts


def _gdn_group(qr, kr, v, z, ba, alog, dtb, nw, st, eb, ea):
    q, hp = qr.shape[0], GDN_HP
    heads = range(hp)
    sl = [slice(i * G_K, (i + 1) * G_K) for i in heads]
    incl, strict = _masks(q)
    beta = _hdot(jax.nn.sigmoid(ba), eb)
    g = _hdot(-jnp.exp(alog[0:1]) * jax.nn.softplus(ba + dtb[0:1]), ea)
    gc = _hdot(incl.astype(f32), g)
    egc = jnp.exp(gc)
    gl = gc[q - 1:q]
    eend = jnp.exp(gl - gc)
    egl = jnp.exp(gl)
    vb = v * beta
    qn, kn = [], []
    for j in range(hp // 2):
        qj, kj = qr[:, sl[j]], kr[:, sl[j]]
        qn.append(qj * lax.rsqrt(jnp.sum(qj * qj, axis=-1, keepdims=True) + EPS) * (G_K ** -0.5))
        kn.append(kj * lax.rsqrt(jnp.sum(kj * kj, axis=-1, keepdims=True) + EPS))
    qk = [_bdot(qn[j], kn[j], 1, 1) for j in range(hp // 2)]
    gcs = [gc[:, sl[i]] for i in heads]
    decs = []
    for i in heads:
        diff = jnp.where(incl, gcs[i][:, 0:1] - jnp.transpose(gcs[i])[0:1, :], 0.0)
        decs.append(jnp.where(incl, jnp.exp(diff), 0.0))
    kbs = [kn[i // 2] * beta[:, sl[i]] for i in heads]
    kk = [_bdot(kbs[i], kn[i // 2], 1, 1) for i in heads]
    tinv = _tri_inv_many([jnp.where(strict, kk[i] * decs[i], 0.0) for i in heads])
    uw = [_hdot(tinv[i], jnp.concatenate([vb[:, sl[i]], kbs[i] * egc[:, sl[i]]], axis=1)) for i in heads]
    sts = [st[sl[i], :] for i in heads]
    ws = [_bdot(jnp.concatenate([uw[i][:, G_K:], qn[i // 2] * egc[:, sl[i]]], axis=0), sts[i], 1, 0) for i in heads]
    v_new = [uw[i][:, :G_K] - ws[i][:q] for i in heads]
    o = [ws[i][q:] + _bdot(qk[i // 2] * decs[i], v_new[i], 1, 0) for i in heads]
    new = [sts[i] * egl[:, i * G_K:i * G_K + 1] + _bdot(kn[i // 2] * eend[:, sl[i]], v_new[i], 0, 0) for i in heads]
    on = [_rms(o[i], nw) * _silu(z[:, sl[i]]) for i in heads]
    return (jnp.concatenate(on, axis=1),), jnp.concatenate(new, axis=0)


def _xattn_fn(q, k, v):
    s = _bdot(q, k, 1, 1) * (X_D ** -0.5)
    return _bdot(jax.nn.softmax(s, axis=-1), v, 1, 0)


def xattn_fwd(q, k, v, nb, name, tl=512):
    t = q.shape[0]
    tl = min(tl, t // nb)
    nl = t // nb // tl

    def body(q_ref, k_ref, v_ref, o_ref):
        o_ref[...] = _xattn_fn(q_ref[...], k_ref[...], v_ref[...]).astype(o_ref.dtype)

    qs = pl.BlockSpec((tl, X_D), lambda b, i, h: (b * nl + i, h))
    ks = pl.BlockSpec((N_MEM, X_D), lambda b, i, h: (b, h))
    return pl.pallas_call(body, name=name, grid=(nb, nl, X_H), in_specs=[qs, ks, ks], out_specs=qs,
                          out_shape=_S(q.shape, bf16), compiler_params=_cp())(q, k, v)


def xattn_bwd(q, k, v, do, nb, name, tl=512):
    t = q.shape[0]
    tl = min(tl, t // nb)
    nl = t // nb // tl

    def body(q_ref, k_ref, v_ref, do_ref, dq_ref, dk_ref, dv_ref):
        _, vjp = jax.vjp(_xattn_fn, q_ref[...], k_ref[...], v_ref[...])
        dq, dk, dv = vjp(do_ref[...].astype(f32))
        dq_ref[...] = dq

        @pl.when(pl.program_id(2) == 0)
        def _():
            dk_ref[...] = jnp.zeros_like(dk_ref)
            dv_ref[...] = jnp.zeros_like(dv_ref)
        dk_ref[...] += dk
        dv_ref[...] += dv

    qs = pl.BlockSpec((tl, X_D), lambda b, h, i: (b * nl + i, h))
    ks = pl.BlockSpec((N_MEM, X_D), lambda b, h, i: (b, h))
    return pl.pallas_call(body, name=name, grid=(nb, X_H, nl), in_specs=[qs, ks, ks, qs], out_specs=[qs, ks, ks],
                          out_shape=[_S(q.shape, f32), _S(k.shape, f32), _S(v.shape, f32)],
                          compiler_params=_cp())(q, k, v, do)


def _lower_bounds(hlb):
    sm = jax.nn.softmax(hlb, axis=0)
    rows, run = [], None
    for r in range(hlb.shape[0]):
        run = sm[r:r + 1] if run is None else run + sm[r:r + 1]
        rows.append(run - sm[0:1])
    return jnp.concatenate(rows, axis=0)


def lower_bounds_fwd(hlb):
    return rows_call("lb_fwd", lambda v: ((_lower_bounds(v),), ()), [hlb], [], [(hlb.shape[1], f32)], tm=hlb.shape[0])[0]


def lower_bounds_bwd(hlb, dlb):
    def fn(v, d):
        _, vjp = jax.vjp(_lower_bounds, v)
        return (vjp(d)[0],), ()
    return rows_call("lb_bwd", fn, [hlb, dlb], [], [(hlb.shape[1], f32)], tm=hlb.shape[0])[0]


def loss_head(x, target, w):
    def fn(xv, tv, wv):
        def loss(xx, ww):
            err = _rms(xx, ww) - tv
            return 0.5 * jnp.sum(jnp.mean(err * err, axis=-1))
        val, (dx, dw) = jax.value_and_grad(loss, argnums=(0, 1))(xv, wv)
        return (dx,), (jnp.broadcast_to(val, (1, 128)), dw)
    dx, loss, dw = rows_call("loss_head", fn, [x, target], [w], [(x.shape[1], f32)], [((1, 128), f32), (w.shape, f32)])
    return dx, loss, dw


def _adamw_fn(w, g, m, v):
    m2 = ADAM_B1 * m + (1.0 - ADAM_B1) * g
    v2 = ADAM_B2 * v + (1.0 - ADAM_B2) * (g * g)
    m_hat = m2 / (1.0 - ADAM_B1 ** ADAM_STEP)
    v_hat = v2 / (1.0 - ADAM_B2 ** ADAM_STEP)
    delta = -ADAM_LR * (m_hat / (jnp.sqrt(v_hat) + ADAM_EPS) + ADAM_WD * w)
    return delta, m2, v2


def adamw(w, g, m, v, name, g2=None):
    shape = w.shape
    c = shape[-1]
    r = w.size // c
    to2 = lambda a: a.reshape(r, c)
    tm = r if r * c * 4 <= (1 << 20) else _tile(r, (256, 128, 64, 32, 16, 8))

    def fn(*a):
        if g2 is None:
            wv, gv, mv, vv = a
        else:
            wv, gv, g2v, mv, vv = a
            gv = gv + g2v
        return (gv,) + _adamw_fn(wv, gv, mv, vv), ()
    rows = [to2(w), to2(g)] + ([] if g2 is None else [to2(g2)]) + [to2(m), to2(v)]
    outs = rows_call(name, fn, rows, [], [(c, f32)] * 4, tm=tm)
    return tuple(o.reshape(shape) for o in outs)


def _expand(first_row, nheads, width):
    r = jnp.arange(128)[:, None]
    c = jnp.arange(nheads * width)[None, :]
    return (r == first_row + c // width).astype(f32)


def _pad_row(v, lane0=0):
    return jnp.pad(v.astype(f32).reshape(1, -1), ((0, 7), (lane0, 128 - lane0 - v.shape[0])))


def _pad_cols(w, n=128):
    return jnp.pad(w, ((0, 0), (0, n - w.shape[1])))


_COL = lambda h: h
_C00 = lambda h: (0, 0)
_CONV_CT = 256


def _conv(name, x, w, b, nb, dseed=None):
    fn = _conv4_silu if b is not None else _conv4_silu_nobias
    pars = [w] + ([] if b is None else [b])
    return cols_call(name, fn, [x], pars, [f32], nb=nb, ct=_CONV_CT, ncol=x.shape[1] // _CONV_CT,
                     dseed=None if dseed is None else [dseed])


SSD_GP, GLA_HP, GDN_HP = 8, 8, 8


def _ssd_scan(name, xs, bm, cm, z, dtr, p, nb, states=None, dseed=None):
    gp, ng = SSD_GP, M_G // SSD_GP
    seqs = [(xs, 256 * gp, _COL, 1), (bm, 128 * gp, _COL, 1), (cm, 128 * gp, _COL, 1), (z, 256 * gp, _COL, 1),
            (dtr, 128, lambda h: 0, ng)]
    pars = [(p["dtb"], (8, 128), _C00), (p["alog"], (8, 128), _C00), (p["dsk"], (8, 128), _C00),
            (p["nw"], (1, 256 * gp), lambda h: (0, h))]
    consts = [(_expand(0, M_H, M_P), (128, 256 * gp), lambda h: (0, h))]
    outs = [(M_INNER, 256 * gp, _COL, bf16)]
    return scan_call(name, _ssd_group, seqs, pars, consts, outs, nb=nb, nh=ng, q=M_Q, state_shape=(gp * 4 * M_P, M_N),
                     states=states, dseed=dseed)


def _gla_scan(name, qr, fr, ir, gr, p, nb, states=None, dseed=None):
    hp, ng = GLA_HP, H_H // GLA_HP
    seqs = [(a, 128 * hp, _COL, 1) for a in (qr, fr, ir, gr)]
    pars = [(p["lb"], (1, 128 * hp), lambda h: (0, h)), (p["nw"], (1, 128), _C00)]
    outs = [(D, 128 * hp, _COL, bf16)]
    return scan_call(name, _gla_group, seqs, pars, [], outs, nb=nb, nh=ng, q=H_Q, state_shape=(hp * H_K, H_K),
                     states=states, dseed=dseed)


def _gdn_scan(name, qc, kc, vc, z, ba, p, nb, states=None, dseed=None):
    hp, ng = GDN_HP, G_HV // GDN_HP
    seqs = [(qc, 64 * hp, _COL, 1), (kc, 64 * hp, _COL, 1), (vc, 128 * hp, _COL, 1), (z, 128 * hp, _COL, 1),
            (ba, 128, lambda h: 0, ng)]
    pars = [(p["alog"], (8, 128), _C00), (p["dtb"], (8, 128), _C00), (p["nw"], (1, 128), _C00)]
    consts = [(_expand(0, G_HV, G_K), (128, 128 * hp), lambda h: (0, h)),
              (_expand(G_HV, G_HV, G_K), (128, 128 * hp), lambda h: (0, h))]
    outs = [(G_VAL, 128 * hp, _COL, bf16)]
    return scan_call(name, _gdn_group, seqs, pars, consts, outs, nb=nb, nh=ng, q=G_Q, state_shape=(hp * G_K, G_K),
                     states=states, dseed=dseed)


def _w(wt):
    return wt if isinstance(wt, tuple) else (wt, None)


def _proj(a, wt, name, res=None):
    arr, bsel = _w(wt)
    return mm(a, arr, bsel=bsel, res=res, name=name)


def _proj_bwd(tag, hn, pieces):
    dhn, dws, bufs = None, [], {}
    for i, (d, wt) in enumerate(pieces):
        arr, bsel = _w(wt)
        if bsel is None:
            dws.append(mm(hn, d, ta=True, out_dtype=bf16, name=f"{tag}_dw{i}"))
        else:
            bufs[id(arr)] = mm(hn, d, ta=True, out_stack=arr.shape[2], out_slots=(arr.shape[0], bsel[0]),
                               into=bufs.get(id(arr)), out_dtype=bf16, name=f"{tag}_dw{i}")
            dws.append(None)
        dhn = mm(d, arr, tb=True, bsel=bsel, res=dhn, name=f"{tag}_dh{i}")
    dws = [dw if dw is not None else bufs[id(_w(wt)[0])] for dw, (_, wt) in zip(dws, pieces, strict=True)]
    return dhn, dws


def ssd_mixer_fwd(tag, hn, w, nb):
    z, xr, br, cr, dtr = (_proj(hn, w[k], f"{tag}_in_{k}") for k in ("wz", "wx", "wb", "wc", "wdt"))
    xs = _conv(f"{tag}_convx", xr, w["cwx"], w["cbx"], nb)[0]
    bm = _conv(f"{tag}_convb", br, w["cwb"], w["cbb"], nb)[0]
    cm = _conv(f"{tag}_convc", cr, w["cwc"], w["cbc"], nb)[0]
    yn, states = _ssd_scan(f"{tag}_scan", xs, bm, cm, z, dtr, w, nb)
    return yn, (hn, z, xr, br, cr, dtr, xs, bm, cm, yn, states)


def ssd_mixer_bwd(tag, saved, dout, w, nb):
    hn, z, xr, br, cr, dtr, xs, bm, cm, yn, states = saved
    g = {"wout": mm(yn, dout, ta=True, out_dtype=bf16, name=f"{tag}_dwout")}
    dyn = mm(dout, w["wout"], tb=True, name=f"{tag}_dyn")
    dxs, dbm, dcm, dz, ddtr, ddtb, dalog, ddsk, dnw = _ssd_scan(f"{tag}_scanb", xs, bm, cm, z, dtr, w, nb, states, [dyn])
    dxr, g["cwx"], g["cbx"] = _conv(f"{tag}_convxb", xr, w["cwx"], w["cbx"], nb, dxs)
    dbr, g["cwb"], g["cbb"] = _conv(f"{tag}_convbb", br, w["cwb"], w["cbb"], nb, dbm)
    dcr, g["cwc"], g["cbc"] = _conv(f"{tag}_convcb", cr, w["cwc"], w["cbc"], nb, dcm)
    dhn, (g["wz"], g["wx"], g["wb"], g["wc"], g["wdt"]) = _proj_bwd(
        tag, hn, [(dz, w["wz"]), (dxr, w["wx"]), (dbr, w["wb"]), (dcr, w["wc"]), (ddtr, w["wdt"])])
    g["dtb"], g["alog"], g["dsk"] = (jnp.sum(a, axis=0)[0, :M_H] for a in (ddtb, dalog, ddsk))
    g["nw"] = dnw.reshape(M_INNER)
    return dhn, g


def gla_mixer_fwd(tag, hn, w, nb):
    qr, fr, ir, gr = (_proj(hn, w[k], f"{tag}_in_{k}") for k in ("wq", "wf", "wi", "wg"))
    on, states = _gla_scan(f"{tag}_scan", qr, fr, ir, gr, w, nb)
    return on, (hn, qr, fr, ir, gr, on, states)


def gla_mixer_bwd(tag, saved, dout, w, nb):
    hn, qr, fr, ir, gr, on, states = saved
    g = {"wout": mm(on, dout, ta=True, out_dtype=bf16, name=f"{tag}_dwout")}
    don = mm(dout, w["wout"], tb=True, name=f"{tag}_don")
    dq, df, di, dg, dlb, dnw = _gla_scan(f"{tag}_scanb", qr, fr, ir, gr, w, nb, states, [don])
    dhn, (g["wq"], g["wf"], g["wi"], g["wg"]) = _proj_bwd(tag, hn, [(dq, w["wq"]), (df, w["wf"]), (di, w["wi"]), (dg, w["wg"])])
    g["lb"] = dlb.reshape(1, D)
    g["nw"] = jnp.sum(dnw, axis=0).reshape(H_K)
    return dhn, g


def gdn_mixer_fwd(tag, hn, w, nb):
    qr, kr, vr, z, ba = (_proj(hn, w[k], f"{tag}_in_{k}") for k in ("wq", "wk", "wv", "wz", "wba"))
    qc = _conv(f"{tag}_convq", qr, w["cwq"], None, nb)[0]
    kc = _conv(f"{tag}_convk", kr, w["cwk"], None, nb)[0]
    vc = _conv(f"{tag}_convv", vr, w["cwv"], None, nb)[0]
    on, states = _gdn_scan(f"{tag}_scan", qc, kc, vc, z, ba, w, nb)
    return on, (hn, qr, kr, vr, z, ba, qc, kc, vc, on, states)


def gdn_mixer_bwd(tag, saved, dout, w, nb):
    hn, qr, kr, vr, z, ba, qc, kc, vc, on, states = saved
    g = {"wout": mm(on, dout, ta=True, out_dtype=bf16, name=f"{tag}_dwout")}
    don = mm(dout, w["wout"], tb=True, name=f"{tag}_don")
    dqc, dkc, dvc, dz, dba, dalog, ddtb, dnw = _gdn_scan(f"{tag}_scanb", qc, kc, vc, z, ba, w, nb, states, [don])
    dqr, g["cwq"] = _conv(f"{tag}_convqb", qr, w["cwq"], None, nb, dqc)
    dkr, g["cwk"] = _conv(f"{tag}_convkb", kr, w["cwk"], None, nb, dkc)
    dvr, g["cwv"] = _conv(f"{tag}_convvb", vr, w["cwv"], None, nb, dvc)
    dhn, (g["wq"], g["wk"], g["wv"], g["wz"], g["wba"]) = _proj_bwd(
        tag, hn, [(dqr, w["wq"]), (dkr, w["wk"]), (dvr, w["wv"]), (dz, w["wz"]), (dba, w["wba"])])
    g["alog"], g["dtb"] = (jnp.sum(a, axis=0)[0, G_HV:2 * G_HV] for a in (dalog, ddtb))
    g["nw"] = jnp.sum(dnw, axis=0).reshape(G_K)
    return dhn, g


_MIXERS = {0: (ssd_mixer_fwd, ssd_mixer_bwd), 1: (gla_mixer_fwd, gla_mixer_bwd), 2: (gdn_mixer_fwd, gdn_mixer_bwd)}


def layer_fwd(i, x, mem, weights_of, nb):
    t = f"l{i}"
    wm = weights_of(i, 0, x)
    hn = rms_fwd(x, wm["ln_mix"], f"{t}_ln_mix")
    mix, s_mix = _MIXERS[i % 3][0](f"{t}_mix", hn, wm["mix"], nb)
    x1 = mm(mix, wm["mix"]["wout"], res=x, name=f"{t}_mix_out")
    w = weights_of(i, 1, x1)
    hx = rms_fwd(x1, w["ln_xattn"], f"{t}_ln_xattn")
    mn = rms_fwd(mem, w["ln_mem"], f"{t}_ln_mem")
    q = _proj(hx, w["xq"], f"{t}_xa_q")
    k = _proj(mn, w["xk"], f"{t}_xa_k")
    v = _proj(mn, w["xv"], f"{t}_xa_v")
    o = xattn_fwd(q, k, v, nb, f"{t}_xattn")
    x2 = mm(o, w["xo"], res=x1, name=f"{t}_xa_o")
    hf = rms_fwd(x2, w["ln_ffn"], f"{t}_ln_ffn")
    gate = _proj(hf, w["fg"], f"{t}_ffn_gate")
    up = _proj(hf, w["fu"], f"{t}_ffn_up")
    act = cols_call(f"{t}_ffn_act", _ffn_act, [gate, up], [w["fcw"], w["fcb"]], [bf16], nb=nb, ct=_CONV_CT,
                    ncol=D_FF // _CONV_CT)[0]
    x3 = mm(act, w["fd"], res=x2, name=f"{t}_ffn_down")
    return x3, (wm, w, x, s_mix, x1, hx, mn, q, k, v, o, x2, hf, gate, up, act)


def layer_bwd(i, saved, dx, mem, nb, token, grads_done):
    t = f"l{i}b"
    wm, w, x, s_mix, x1, hx, mn, q, k, v, o, x2, hf, gate, up, act = saved
    if token is not None:
        w = dict(w, fd=w["fd"] + token[0, 0].astype(w["fd"].dtype))
    g = {}
    g["fd"] = mm(act, dx, ta=True, out_dtype=bf16, name=f"{t}_dwd")
    dact = mm(dx, w["fd"], tb=True, name=f"{t}_dact")
    dgate, dup, g["fcw"], g["fcb"] = cols_call(f"{t}_ffn_act", _ffn_act, [gate, up], [w["fcw"], w["fcb"]], [bf16], nb=nb,
                                               ct=_CONV_CT, ncol=D_FF // _CONV_CT, dseed=[dact])
    dhf, (g["fg"], g["fu"]) = _proj_bwd(f"{t}_ffn", hf, [(dgate, w["fg"]), (dup, w["fu"])])
    dx, g["ln_ffn"] = rms_bwd(x2, w["ln_ffn"], dhf, dx, f"{t}_ln_ffn")
    g["xo"] = mm(o, dx, ta=True, out_dtype=bf16, name=f"{t}_dwo")
    do = mm(dx, w["xo"], tb=True, name=f"{t}_do")
    dq, dk, dv = xattn_bwd(q, k, v, do, nb, f"{t}_xattn")
    dhx, (g["xq"],) = _proj_bwd(f"{t}_xq", hx, [(dq, w["xq"])])
    dmn, (g["xk"], g["xv"]) = _proj_bwd(f"{t}_xkv", mn, [(dk, w["xk"]), (dv, w["xv"])])
    _, g["ln_mem"] = rms_bwd(mem, w["ln_mem"], dmn, None, f"{t}_ln_mem")
    dx, g["ln_xattn"] = rms_bwd(x1, w["ln_xattn"], dhx, dx, f"{t}_ln_xattn")
    token = grads_done(i, 1, g, dx) if grads_done else None
    mixw = wm["mix"] if token is None else dict(wm["mix"], wout=wm["mix"]["wout"] + token[0, 0].astype(wm["mix"]["wout"].dtype))
    dhn, g["mix"] = _MIXERS[i % 3][1](f"{t}_mix", s_mix, dx, mixw, nb)
    dx, g["ln_mix"] = rms_bwd(x, wm["ln_mix"], dhn, dx, f"{t}_ln_mix")
    token = grads_done(i, 0, g, dx) if grads_done else None
    return dx, g, token


def local_step(x, mem, target, weights_of, final_norm, nb, grads_done=None):
    saved = []
    for i in range(DEPTH):
        x, s = layer_fwd(i, x, mem, weights_of, nb)
        saved.append(s)
    dx, loss, dfinal = loss_head(x, target, final_norm)
    grads = [None] * DEPTH
    token = None
    for i in reversed(range(DEPTH)):
        dx, grads[i], token = layer_bwd(i, saved[i], dx, mem, nb, token, grads_done)
    return loss, dx, grads, dfinal


WEIGHTS = ["ln_mix", "ln_xattn", "ln_mem", "ln_ffn", "final_norm", "m_in_w", "m_conv_w", "m_conv_b", "m_dt_bias", "m_a_log",
           "m_d", "m_norm_w", "m_out_w", "h_in_w", "h_lower_bounds", "h_norm_w", "h_out_w", "g_in_w", "g_conv_w", "g_a_log",
           "g_dt_bias", "g_norm_w", "g_out_w", "xa_q", "xa_kv", "xa_o", "f_up", "f_conv_w", "f_conv_b", "f_down"]
SHARD_AXIS = {"m_in_w": 2, "m_conv_w": 2, "m_conv_b": 1, "m_norm_w": 1, "m_out_w": 1, "h_in_w": 2, "h_out_w": 1, "g_in_w": 2,
              "g_conv_w": 2, "g_out_w": 1, "xa_q": 1, "xa_kv": 2, "xa_o": 1, "f_up": 2, "f_conv_w": 2, "f_down": 1}
MATRICES = ["m_in_w", "m_out_w", "h_in_w", "h_out_w", "g_in_w", "g_out_w", "xa_q", "xa_kv", "xa_o", "f_up", "f_down"]
SMALL_SHARDED = [n for n in WEIGHTS if n in SHARD_AXIS and n not in MATRICES]
REPLICATED = [n for n in WEIGHTS if n not in SHARD_AXIS]
_MIXER_PREFIX = {0: "m", 1: "h", 2: "g"}


def layer_weight_names(i, part):
    if part == 0:
        p = _MIXER_PREFIX[i % 3]
        return [(n, i // 3) for n in WEIGHTS if n in SHARD_AXIS and n.startswith(p + "_")]
    return [(n, i) for n in ("xa_q", "xa_kv", "xa_o", "f_up", "f_conv_w", "f_down")]


def _cols(st, lo, hi):
    ns = st.shape[-1]
    parts = []
    for j in range(NCHIP):
        a, b = max(lo, j * ns), min(hi, (j + 1) * ns)
        if a < b:
            parts.append(st[j][..., a - j * ns:b - j * ns])
    return parts[0] if len(parts) == 1 else jnp.concatenate(parts, axis=-1)


def _col_shards(pieces, ns):
    full = jnp.concatenate(pieces, axis=-1)
    return [full[..., j * ns:(j + 1) * ns] for j in range(NCHIP)]


def _rows(st):
    return st.reshape(st.shape[0] * st.shape[1], st.shape[2])


def prep_layer(i, part, G, R, lb):
    row = lambda a: a.reshape(1, -1)
    p, k = _MIXER_PREFIX[i % 3], i // 3
    if part == 1:
        kv, fup = G["xa_kv"], G["f_up"]
        return dict(ln_xattn=R["ln_xattn"][i:i + 1], ln_mem=R["ln_mem"][i:i + 1], ln_ffn=R["ln_ffn"][i:i + 1],
                    xq=_rows(G["xa_q"]), xk=(kv, (0, 2)), xv=(kv, (2, 2)), xo=_rows(G["xa_o"]), fg=(fup, (0, 2)), fu=(fup, (2, 2)),
                    fcw=_cols(G["f_conv_w"], 0, D_FF), fcb=R["f_conv_b"][i:i + 1], fd=_rows(G["f_down"]))
    layer = dict(ln_mix=R["ln_mix"][i:i + 1])
    inw, wout = G[p + "_in_w"], _rows(G[p + "_out_w"])
    if p == "m":
        cw, cb = G["m_conv_w"], G["m_conv_b"]
        a, b, c = M_INNER, M_INNER + M_G * M_N, M_CONV
        layer["mix"] = dict(
            wz=_cols(inw, 0, M_INNER), wx=_cols(inw, M_INNER, M_INNER + a), wb=_cols(inw, M_INNER + a, M_INNER + b),
            wc=_cols(inw, M_INNER + b, M_MAIN), wdt=_pad_cols(_cols(inw, M_MAIN, M_IN)),
            cwx=_cols(cw, 0, a), cwb=_cols(cw, a, b), cwc=_cols(cw, b, c),
            cbx=row(_cols(cb, 0, a)), cbb=row(_cols(cb, a, b)), cbc=row(_cols(cb, b, c)),
            dtb=_pad_row(R["m_dt_bias"][k]), alog=_pad_row(R["m_a_log"][k]), dsk=_pad_row(R["m_d"][k]),
            nw=row(_cols(G["m_norm_w"], 0, M_INNER)), wout=wout)
    elif p == "h":
        layer["mix"] = dict(wq=(inw, (0, 1)), wf=(inw, (1, 1)), wi=(inw, (2, 1)), wg=(inw, (3, 1)),
                            lb=lb[i:i + 1], nw=row(R["h_norm_w"][k]), wout=wout)
    else:
        cw = G["g_conv_w"]
        layer["mix"] = dict(
            wq=_cols(inw, 0, D), wk=_cols(inw, D, 2 * D), wv=_cols(inw, 2 * D, G_CONV), wz=_cols(inw, G_CONV, G_MAIN),
            wba=_pad_cols(_cols(inw, G_MAIN, G_IN)), cwq=_cols(cw, 0, D), cwk=_cols(cw, D, 2 * D), cwv=_cols(cw, 2 * D, G_CONV),
            alog=_pad_row(R["g_a_log"][k], G_HV), dtb=_pad_row(R["g_dt_bias"][k], G_HV),
            nw=row(R["g_norm_w"][k]), wout=wout)
    return layer


def matrix_grad_parts(i, part, g):
    by_rows = lambda a: a.reshape(NCHIP, a.shape[0] // NCHIP, a.shape[1])
    if part == 1:
        return {"xa_q": by_rows(g["xq"]), "xa_kv": g["xk"], "xa_o": by_rows(g["xo"]), "f_up": g["fg"], "f_down": by_rows(g["fd"])}
    p = _MIXER_PREFIX[i % 3]
    m = g["mix"]
    out = {p + "_out_w": by_rows(m["wout"])}
    if p == "m":
        out["m_in_w"] = jnp.stack(_col_shards([m["wz"], m["wx"], m["wb"], m["wc"], m["wdt"]], M_IN // NCHIP))
    elif p == "h":
        out["h_in_w"] = m["wq"]
    else:
        out["g_in_w"] = jnp.stack(_col_shards([m["wq"], m["wk"], m["wv"], m["wz"], m["wba"]], G_IN // NCHIP))
    return out


def small_grads(grads, dfinal, hlb):
    cat = lambda xs: jnp.concatenate(xs, axis=1)
    out = {k: jnp.concatenate([g[k] for g in grads], axis=0) for k in ("ln_mix", "ln_xattn", "ln_mem", "ln_ffn")}
    out["final_norm"] = dfinal.reshape(D)
    out["f_conv_w"] = jnp.stack([g["fcw"] for g in grads])
    out["f_conv_b"] = jnp.concatenate([g["fcb"] for g in grads], axis=0)
    ms = [g["mix"] for i, g in enumerate(grads) if i % 3 == 0]
    out["m_conv_w"] = jnp.stack([cat([m["cwx"], m["cwb"], m["cwc"]]) for m in ms])
    out["m_conv_b"] = jnp.concatenate([cat([m["cbx"], m["cbb"], m["cbc"]]) for m in ms], axis=0)
    out["m_dt_bias"] = jnp.stack([m["dtb"] for m in ms])
    out["m_a_log"] = jnp.stack([m["alog"] for m in ms])
    out["m_d"] = jnp.stack([m["dsk"] for m in ms])
    out["m_norm_w"] = jnp.stack([m["nw"] for m in ms])
    hs = [(i, g["mix"]) for i, g in enumerate(grads) if i % 3 == 1]
    lb_rows = dict(hs)
    dlb = jnp.concatenate([lb_rows[i]["lb"] if i in lb_rows else jnp.zeros((1, D), f32) for i in range(DEPTH)], axis=0)
    out["h_lower_bounds"] = lower_bounds_bwd(hlb, dlb)
    out["h_norm_w"] = jnp.stack([m["nw"] for _, m in hs])
    gs = [g["mix"] for i, g in enumerate(grads) if i % 3 == 2]
    out["g_conv_w"] = jnp.stack([cat([m["cwq"], m["cwk"], m["cwv"]]) for m in gs])
    out["g_a_log"] = jnp.stack([m["alog"] for m in gs])
    out["g_dt_bias"] = jnp.stack([m["dtb"] for m in gs])
    out["g_norm_w"] = jnp.stack([m["nw"] for m in gs])
    return out


_HBM = pl.BlockSpec(memory_space=pltpu.HBM)


def _place():
    x, y, c = lax.axis_index("x"), lax.axis_index("y"), lax.axis_index("c")
    chips = [(1 - x, y), (x, 1 - y), (1 - x, 1 - y)]
    return x, y, c, chips


def gather_shards(name, tensors):
    n = len(tensors)

    def body(*refs):
        ins, outs = refs[:n], refs[n:2 * n]
        send_sems, recv_sems, loc_sems = refs[2 * n:]
        x, y, c, chips = _place()
        me = 2 * x + y
        local_copies, sends = [], []
        for t in range(n):
            loc = pltpu.make_async_copy(ins[t], outs[t].at[me], loc_sems.at[t])
            loc.start()
            local_copies.append(loc)
            for j, (px, py) in enumerate(chips):
                cp = pltpu.make_async_remote_copy(src_ref=ins[t], dst_ref=outs[t].at[me], send_sem=send_sems.at[3 * t + j],
                                                  recv_sem=recv_sems.at[3 * t + j], device_id=(px, py, c), device_id_type=MESH)
                cp.start()
                sends.append(cp)
        for t in range(n):
            for j, (px, py) in enumerate(chips):
                pltpu.make_async_remote_copy(src_ref=ins[t], dst_ref=outs[t].at[2 * px + py], send_sem=send_sems.at[3 * t + j],
                                             recv_sem=recv_sems.at[3 * t + j], device_id=(px, py, c),
                                             device_id_type=MESH).wait_recv()
        for cp in sends:
            cp.wait_send()
        for cp in local_copies:
            cp.wait()

    return pl.pallas_call(
        body, name=name, in_specs=[_HBM] * n, out_specs=[_HBM] * n,
        out_shape=[_S((NCHIP,) + a.shape, a.dtype) for a in tensors],
        scratch_shapes=[pltpu.SemaphoreType.DMA((3 * n,)), pltpu.SemaphoreType.DMA((3 * n,)), pltpu.SemaphoreType.DMA((n,))])(*tensors)


_SEM = pl.BlockSpec(memory_space=pltpu.SEMAPHORE)
_ANY = pl.BlockSpec(memory_space=pl.ANY)
_SPLIT = pltpu.CompilerParams(has_side_effects=pltpu.SideEffectType.DATAFLOW_SIDE_EFFECTING)


def _hbm(a):
    return pltpu.with_memory_space_constraint(a, pltpu.HBM)


def _split_start(name, srcs, lands, dep, copies):
    n = len(srcs)

    def body(*refs):
        src_refs, land_refs = refs[:n], refs[n:2 * n]
        send_sems, recv_sems = refs[2 * n + 1], refs[2 * n + 2]
        token = refs[-1]
        for cp in copies(src_refs, land_refs, send_sems, recv_sems):
            cp.start()
        token[...] = jnp.zeros_like(token)

    thru = [pltpu.HBM(a.shape, a.dtype) for a in list(srcs) + list(lands)]
    out = pl.pallas_call(
        body, name=name, in_specs=[_HBM] * (2 * n) + [_ANY],
        out_specs=[_SEM, _SEM] + [_HBM] * (2 * n) + [pl.BlockSpec(memory_space=pltpu.VMEM)],
        out_shape=[pltpu.SemaphoreType.DMA((3 * n,)), pltpu.SemaphoreType.DMA((3 * n,))] + thru + [_S((8, 128), f32)],
        input_output_aliases={t: 2 + t for t in range(2 * n)}, compiler_params=_SPLIT,
    )(*[_hbm(a) for a in srcs], *[_hbm(a) for a in lands], dep)
    return out[0], out[1], out[2:2 + n], out[2 + n:2 + 2 * n], out[-1]


def _split_wait(name, started, after, copies):
    send_sems, recv_sems, srcs, lands, _ = started
    n = len(srcs)

    def body(*refs):
        src_refs, land_refs = refs[:n], refs[n:2 * n]
        s_sems, r_sems = refs[2 * n], refs[2 * n + 1]
        for cp in copies(src_refs, land_refs, s_sems, r_sems):
            cp.wait_send()
            cp.wait_recv()

    out = pl.pallas_call(
        body, name=name, in_specs=[_HBM] * (2 * n) + [_SEM, _SEM, _ANY], out_specs=[_HBM] * (2 * n),
        out_shape=[pltpu.HBM(a.shape, a.dtype) for a in list(srcs) + list(lands)],
        input_output_aliases={t: t for t in range(2 * n)}, compiler_params=_SPLIT,
    )(*srcs, *lands, send_sems, recv_sems, after)
    return out[:n], out[n:]


def _gather_copies(arrive):
    def copies(src_refs, land_refs, send_sems, recv_sems):
        x, y, c, chips = _place()
        out = []
        for t, (s, l) in enumerate(zip(src_refs, land_refs, strict=True)):
            for j, (px, py) in enumerate(chips):
                slot = 2 * px + py if arrive else 2 * x + y
                out.append(pltpu.make_async_remote_copy(src_ref=s, dst_ref=l.at[slot], send_sem=send_sems.at[3 * t + j],
                                                        recv_sem=recv_sems.at[3 * t + j], device_id=(px, py, c), device_id_type=MESH))
        return out
    return copies


def gather_start(name, tensors, me, dep):
    lands = [lax.dynamic_update_index_in_dim(jnp.zeros((NCHIP,) + a.shape, a.dtype), a, me, 0) for a in tensors]
    return _split_start(name, tensors, lands, dep, _gather_copies(False))


def gather_wait(name, started, after):
    return _split_wait(name, started, after, _gather_copies(True))


def _scatter_copies(src_refs, land_refs, send_sems, recv_sems):
    x, y, c, chips = _place()
    out = []
    for t, (s, l) in enumerate(zip(src_refs, land_refs, strict=True)):
        for j, (px, py) in enumerate(chips):
            out.append(pltpu.make_async_remote_copy(src_ref=s.at[2 * px + py], dst_ref=l.at[j], send_sem=send_sems.at[3 * t + j],
                                                    recv_sem=recv_sems.at[3 * t + j], device_id=(px, py, c), device_id_type=MESH))
    return out


def scatter_start(name, parts, dep):
    lands = [lax.empty((3,) + a.shape[1:], a.dtype) for a in parts]
    return _split_start(name, parts, lands, dep, _scatter_copies)


def scatter_wait(name, started, after):
    return _split_wait(name, started, after, _scatter_copies)


def sum_parts(name, part, land, me):
    shape = land.shape[1:]
    c = shape[-1]
    r = land.size // (3 * c)
    tm = _tile(r, (256, 128, 64, 32, 16, 8))

    def body(me_ref, p_ref, l_ref, o_ref):
        o_ref[...] = p_ref[...].astype(f32) + l_ref[0].astype(f32) + l_ref[1].astype(f32) + l_ref[2].astype(f32)

    grid_spec = pltpu.PrefetchScalarGridSpec(
        num_scalar_prefetch=1, grid=(r // tm,),
        in_specs=[pl.BlockSpec((None, tm, c), lambda i, me_ref: (me_ref[0], i, 0)),
                  pl.BlockSpec((3, tm, c), lambda i, me_ref: (0, i, 0))],
        out_specs=pl.BlockSpec((tm, c), lambda i, me_ref: (i, 0)))
    out = pl.pallas_call(body, name=name, grid_spec=grid_spec, out_shape=_S((r, c), f32), compiler_params=_cp())(
        me.reshape(1).astype(jnp.int32), part.reshape(NCHIP, r, c), land.reshape(3, r, c))
    return out.reshape(shape)


def _swap_copies(src_refs, land_refs, send_sems, recv_sems):
    x, y, c, _ = _place()
    return [pltpu.make_async_remote_copy(src_ref=s, dst_ref=l, send_sem=send_sems.at[3 * t], recv_sem=recv_sems.at[3 * t],
                                         device_id=(x, y, 1 - c), device_id_type=MESH)
            for t, (s, l) in enumerate(zip(src_refs, land_refs, strict=True))]


def swap_start(name, tensors, dep):
    return _split_start(name, tensors, [lax.empty(a.shape, a.dtype) for a in tensors], dep, _swap_copies)


def swap_wait(name, started, after):
    return _split_wait(name, started, after, _swap_copies)


def allreduce_small(v):
    r, n = v.shape

    def body(x_ref, out_ref, gat, send_sems, recv_sems, local_sem):
        x, y, c, chips = _place()
        me, sibling = (x, y, c), (x, y, 1 - c)

        def rows(px, py, pc):
            return gat.at[pl.ds((4 * px + 2 * py + pc) * r, r), :]

        def copy(k, block, to, src=None):
            return pltpu.make_async_remote_copy(src_ref=rows(*block) if src is None else src, dst_ref=rows(*block),
                                                send_sem=send_sems.at[k], recv_sem=recv_sems.at[k], device_id=to,
                                                device_id_type=MESH)

        mine = pltpu.make_async_copy(x_ref, rows(*me), local_sem)
        mine.start()
        first = [copy(0, me, sibling, src=x_ref)] + [copy(1 + j, me, (*chip, c), src=x_ref) for j, chip in enumerate(chips)]
        for cp in first:
            cp.start()
        passed = [copy(4 + j, (*chip, c), sibling) for j, chip in enumerate(chips)]
        for j, chip in enumerate(chips):
            copy(1 + j, (*chip, c), me).wait_recv()
            passed[j].start()
        copy(0, sibling, me).wait_recv()
        for j, chip in enumerate(chips):
            copy(4 + j, (*chip, 1 - c), me).wait_recv()
        for cp in first + passed:
            cp.wait_send()
        mine.wait()
        acc = gat[0:r, :]
        for d in range(1, 8):
            acc = acc + gat[d * r:(d + 1) * r, :]
        out_ref[...] = acc

    vm = pl.BlockSpec(memory_space=pltpu.VMEM)
    return pl.pallas_call(
        body, name="allreduce_small", in_specs=[vm], out_specs=vm, out_shape=_S((r, n), v.dtype),
        scratch_shapes=[pltpu.VMEM((8 * r, n), v.dtype), pltpu.SemaphoreType.DMA((7,)), pltpu.SemaphoreType.DMA((7,)),
                        pltpu.SemaphoreType.DMA],
        compiler_params=_cp())(v)


SMALL_ROW = 1024


def kernel(x, mem, ln_mix, ln_xattn, ln_mem, ln_ffn, final_norm, m_in_w, m_conv_w, m_conv_b, m_dt_bias, m_a_log, m_d, m_norm_w, m_out_w, h_in_w, h_lower_bounds, h_norm_w, h_out_w, g_in_w, g_conv_w, g_a_log, g_dt_bias, g_norm_w, g_out_w, xa_q, xa_kv, xa_o, f_up, f_conv_w, f_conv_b, f_down, loss_target, m_ln_mix, m_ln_xattn, m_ln_mem, m_ln_ffn, m_final_norm, m_m_in_w, m_m_conv_w, m_m_conv_b, m_m_dt_bias, m_m_a_log, m_m_d, m_m_norm_w, m_m_out_w, m_h_in_w, m_h_lower_bounds, m_h_norm_w, m_h_out_w, m_g_in_w, m_g_conv_w, m_g_a_log, m_g_dt_bias, m_g_norm_w, m_g_out_w, m_xa_q, m_xa_kv, m_xa_o, m_f_up, m_f_conv_w, m_f_conv_b, m_f_down, v_ln_mix, v_ln_xattn, v_ln_mem, v_ln_ffn, v_final_norm, v_m_in_w, v_m_conv_w, v_m_conv_b, v_m_dt_bias, v_m_a_log, v_m_d, v_m_norm_w, v_m_out_w, v_h_in_w, v_h_lower_bounds, v_h_norm_w, v_h_out_w, v_g_in_w, v_g_conv_w, v_g_a_log, v_g_dt_bias, v_g_norm_w, v_g_out_w, v_xa_q, v_xa_kv, v_xa_o, v_f_up, v_f_conv_w, v_f_conv_b, v_f_down):
    local = dict(zip(WEIGHTS, (ln_mix, ln_xattn, ln_mem, ln_ffn, final_norm, m_in_w, m_conv_w, m_conv_b, m_dt_bias, m_a_log, m_d, m_norm_w, m_out_w, h_in_w, h_lower_bounds, h_norm_w, h_out_w, g_in_w, g_conv_w, g_a_log, g_dt_bias, g_norm_w, g_out_w, xa_q, xa_kv, xa_o, f_up, f_conv_w, f_conv_b, f_down), strict=True))
    mom_m = dict(zip(WEIGHTS, (m_ln_mix, m_ln_xattn, m_ln_mem, m_ln_ffn, m_final_norm, m_m_in_w, m_m_conv_w, m_m_conv_b, m_m_dt_bias, m_m_a_log, m_m_d, m_m_norm_w, m_m_out_w, m_h_in_w, m_h_lower_bounds, m_h_norm_w, m_h_out_w, m_g_in_w, m_g_conv_w, m_g_a_log, m_g_dt_bias, m_g_norm_w, m_g_out_w, m_xa_q, m_xa_kv, m_xa_o, m_f_up, m_f_conv_w, m_f_conv_b, m_f_down), strict=True))
    mom_v = dict(zip(WEIGHTS, (v_ln_mix, v_ln_xattn, v_ln_mem, v_ln_ffn, v_final_norm, v_m_in_w, v_m_conv_w, v_m_conv_b, v_m_dt_bias, v_m_a_log, v_m_d, v_m_norm_w, v_m_out_w, v_h_in_w, v_h_lower_bounds, v_h_norm_w, v_h_out_w, v_g_in_w, v_g_conv_w, v_g_a_log, v_g_dt_bias, v_g_norm_w, v_g_out_w, v_xa_q, v_xa_kv, v_xa_o, v_f_up, v_f_conv_w, v_f_conv_b, v_f_down), strict=True))
    nb, seq, _ = x.shape
    me = 2 * lax.axis_index("x") + lax.axis_index("y")

    repl = {n: local[n] for n in REPLICATED}
    lb = lower_bounds_fwd(repl["h_lower_bounds"])
    nstage = 2 * DEPTH
    names = [layer_weight_names(s // 2, s % 2) for s in range(nstage)]
    shards = [[local[n][k].astype(bf16) if n in MATRICES else local[n][k] for n, k in names[s]] for s in range(nstage)]
    flying = {}

    def weights_of(i, part, x_in):
        s = 2 * i + part
        gathered = gather_shards("gather_s0", shards[0]) if s == 0 else gather_wait(f"gather_wait_s{s}", flying.pop(s), x_in)[1]
        w = prep_layer(i, part, {n: g for (n, _), g in zip(names[s], gathered, strict=True)}, repl, lb)
        if s + 1 < nstage:
            flying[s + 1] = gather_start(f"gather_start_s{s + 1}", shards[s + 1], me, gathered[0])
            norm = "ln_mix" if part == 0 else "ln_xattn"
            w[norm] = w[norm] + flying[s + 1][4][0, 0]
        return w

    scattering, swapping = {}, []

    def landed(s, after):
        part_names, started = scattering.pop(s)
        sent, got = scatter_wait(f"scatter_wait_s{s}", started, after)
        sums = [sum_parts(f"sum_s{s}_{n}", p, l, me) for n, p, l in zip(part_names, sent, got, strict=True)]
        swapping.append((s, part_names, swap_start(f"swap_start_s{s}", sums, sums[0])))

    def grads_done(i, part, g, dx_i):
        s = 2 * i + part
        parts = matrix_grad_parts(i, part, g)
        scattering[s] = (list(parts), scatter_start(f"scatter_start_s{s}", list(parts.values()), dx_i))
        token = scattering[s][1][4]
        if s + 1 in scattering:
            landed(s + 1, dx_i)
        return token

    loss, dx, lgrads, dfinal = local_step(x.reshape(nb * seq, D), mem.reshape(nb * N_MEM, D), loss_target.reshape(nb * seq, D),
                                          weights_of, repl["final_norm"].reshape(1, D), nb, grads_done)
    grads = small_grads(lgrads, dfinal, repl["h_lower_bounds"])

    small_names = REPLICATED + SMALL_SHARDED
    flat = jnp.concatenate([grads[n].astype(f32).reshape(-1) for n in small_names] + [loss[0, 0:1]])
    rows = -(-flat.shape[0] // (8 * SMALL_ROW)) * 8
    flat = jnp.pad(flat, (0, rows * SMALL_ROW - flat.shape[0])).reshape(rows, SMALL_ROW)
    red = allreduce_small(flat).reshape(-1)
    gsum, off = {}, 0
    for n in small_names:
        size = grads[n].size
        g = red[off:off + size].reshape(grads[n].shape)
        off += size
        if n in SHARD_AXIS:
            ax = SHARD_AXIS[n]
            w = g.shape[ax] // NCHIP
            g = lax.dynamic_slice_in_dim(g, me * w, w, axis=ax)
        gsum[n] = g
    loss_out = red[off]

    landed(0, dx)
    mine, theirs = {n: {} for n in MATRICES}, {n: {} for n in MATRICES}
    for s, part_names, started in swapping:
        sent, got = swap_wait(f"swap_wait_s{s}", started, dx)
        for n, a, b in zip(part_names, sent, got, strict=True):
            mine[n][s // 2], theirs[n][s // 2] = a, b

    outs = {}
    for n in MATRICES:
        g_mine, g_theirs = (jnp.stack([d[n][i] for i in sorted(d[n])]) for d in (mine, theirs))
        outs[n] = adamw(local[n], g_mine, mom_m[n], mom_v[n], f"adamw_{n}", g2=g_theirs)
    for n in small_names:
        outs[n] = adamw(local[n], gsum[n].reshape(local[n].shape), mom_m[n], mom_v[n], f"adamw_{n}")
    res = [loss_out, dx.reshape(nb, seq, D)]
    for k in range(4):
        res += [outs[n][k] for n in WEIGHTS]
    return tuple(res)
```

```python
import functools

import jax
import jax.numpy as jnp
from jax import lax
from jax.experimental import pallas as pl
from jax.experimental.pallas import tpu as pltpu

f32 = jnp.float32
bf16 = jnp.bfloat16
HIGHEST = lax.Precision.HIGHEST
MESH = pl.DeviceIdType.MESH

D = 1024
DEPTH = 4
EPS = 1e-6
N_MEM = 256
M_INNER, M_P, M_H, M_G, M_N, M_Q = 2048, 64, 32, 8, 128, 64
M_CONV = M_INNER + 2 * M_G * M_N
M_MAIN = M_INNER + M_CONV
M_IN = M_MAIN + M_H
H_H, H_K, H_Q = 8, 128, 32
G_HV, G_HK, G_K, G_Q = 16, 8, 128, 64
G_CONV, G_VAL = 4096, 2048
G_MAIN = G_CONV + G_VAL
G_IN = G_MAIN + 2 * G_HV
X_H, X_D = 4, 256
D_FF = 2816
ADAM_LR, ADAM_B1, ADAM_B2, ADAM_EPS, ADAM_WD, ADAM_STEP = 0.001, 0.9, 0.999, 1e-08, 0.01, 10
VMEM_LIMIT = 56 * 1024 * 1024
NCHIP = 4


def _cp(**kw):
    return pltpu.CompilerParams(vmem_limit_bytes=VMEM_LIMIT, **kw)


def _S(shape, dtype):
    return jax.ShapeDtypeStruct(tuple(shape), dtype)


def _dg(a, b, ca, cb, prec=None):
    return lax.dot_general(a, b, (((ca,), (cb,)), ((), ())), precision=prec, preferred_element_type=f32)


def _hdot(a, b, ca=1, cb=0, prec=lax.Precision.HIGH):
    return _dg(a.astype(f32), b.astype(f32), ca, cb, prec)


def _bdot_raw(a, b, ca, cb):
    return _dg(a.astype(bf16), b.astype(bf16), ca, cb)


@functools.partial(jax.custom_vjp, nondiff_argnums=(2, 3))
def _bdot(a, b, ca, cb):
    return _bdot_raw(a, b, ca, cb)


def _bdot_fwd(a, b, ca, cb):
    return _bdot_raw(a, b, ca, cb), (a, b)


def _bdot_bwd(ca, cb, res, g):
    a, b = res
    if ca == 1:
        da = _bdot_raw(g, b, 1, 1 if cb == 0 else 0)
    else:
        da = _bdot_raw(b, g, 1 if cb == 0 else 0, 1)
    if cb == 0:
        db = _bdot_raw(a, g, 0 if ca == 1 else 1, 0)
    else:
        db = _bdot_raw(g, a, 0, 0 if ca == 1 else 1)
    return da.astype(a.dtype), db.astype(b.dtype)


_bdot.defvjp(_bdot_fwd, _bdot_bwd)


def _shift_down_raw(x, k):
    r = lax.broadcasted_iota(jnp.int32, x.shape, 0)
    return jnp.where(r >= k, pltpu.roll(x, k, 0), 0.0)


def _shift_up_raw(x, k):
    n = x.shape[0]
    r = lax.broadcasted_iota(jnp.int32, x.shape, 0)
    return jnp.where(r < n - k, pltpu.roll(x, n - k, 0), 0.0)


@functools.partial(jax.custom_vjp, nondiff_argnums=(1,))
def _shift_down(x, k):
    return _shift_down_raw(x, k)


_shift_down.defvjp(lambda x, k: (_shift_down_raw(x, k), None), lambda k, _, g: (_shift_up_raw(g, k),))


def _rms(x, w):
    return x * lax.rsqrt(jnp.mean(x * x, axis=-1, keepdims=True) + EPS) * w


def _silu(x):
    return x * jax.nn.sigmoid(x)


def _masks(q):
    r = lax.broadcasted_iota(jnp.int32, (q, q), 0)
    c = lax.broadcasted_iota(jnp.int32, (q, q), 1)
    return r >= c, r > c


def _colvec(row):
    return jnp.transpose(jnp.broadcast_to(row, (8, row.shape[1])))[:, 0:1]


def _tile(n, cands):
    for c in cands:
        if n % c == 0:
            return c
    return n


def mm(a, b, *, ta=False, tb=False, bsel=None, out_stack=None, out_slots=None, into=None, res=None, out_dtype=f32, name):
    m, k = (a.shape[1], a.shape[0]) if ta else a.shape
    ca, cb = (0 if ta else 1), (1 if tb else 0)
    tm = _tile(m, (512, 256, 128))
    if bsel is not None:
        s0, cnt = bsel
        ns = b.shape[2]
        if tb:
            n, tn, tk = b.shape[1], b.shape[1], ns
            b_spec = pl.BlockSpec((None, tn, ns), lambda i, j, kk: (s0 + kk, j, 0))
        else:
            n, tn, tk = cnt * ns, ns, k
            b_spec = pl.BlockSpec((None, tk, ns), lambda i, j, kk: (s0 + j, kk, 0))
    else:
        n = b.shape[0] if tb else b.shape[1]
        tn = out_stack if out_stack else (n if n <= 2816 else _tile(n, (2048, 1024, 512, 256, 128)))
        tk = k if (k <= 4096 and not ta) else _tile(k, (1024, 512, 256, 128))
        b_spec = pl.BlockSpec((tn, tk), lambda i, j, kk: (j, kk)) if tb else pl.BlockSpec((tk, tn), lambda i, j, kk: (kk, j))
    nk = k // tk
    if out_stack:
        total, first = out_slots if out_slots else (n // tn, 0)
        out_spec = pl.BlockSpec((None, tm, tn), lambda i, j, kk: (first + j, i, 0))
        out_shape = _S((total, m, tn), out_dtype)
    else:
        out_spec = pl.BlockSpec((tm, tn), lambda i, j, kk: (i, j))
        out_shape = _S((m, n), out_dtype)

    def body(*refs):
        a_ref, b_ref = refs[:2]
        r_ref = refs[2] if res is not None else None
        o_ref, acc = refs[-2:]
        kk = pl.program_id(2)

        @pl.when(kk == 0)
        def _():
            acc[...] = jnp.zeros_like(acc)

        acc[...] += _bdot_raw(a_ref[...], b_ref[...], ca, cb)

        @pl.when(kk == nk - 1)
        def _():
            v = acc[...]
            if r_ref is not None:
                v = v + r_ref[...]
            o_ref[...] = v.astype(o_ref.dtype)

    a_spec = pl.BlockSpec((tk, tm), lambda i, j, kk: (kk, i)) if ta else pl.BlockSpec((tm, tk), lambda i, j, kk: (i, kk))
    in_specs = [a_spec, b_spec]
    args = [a, b]
    if res is not None:
        in_specs.append(pl.BlockSpec((tm, tn), lambda i, j, kk: (i, j)))
        args.append(res)
    aliases = {}
    if into is not None:
        aliases = {len(args): 0}
        in_specs.append(pl.BlockSpec(memory_space=pl.ANY))
        args.append(into)
    return pl.pallas_call(
        body, name=name, grid=(m // tm, n // tn, nk), in_specs=in_specs, out_specs=out_spec, out_shape=out_shape,
        scratch_shapes=[pltpu.VMEM((tm, tn), f32)], input_output_aliases=aliases, compiler_params=_cp())(*args)


def rows_call(name, fn, rows, pars, row_out, acc_out=(), tm=512):
    t = rows[0].shape[0]
    tm = min(tm, t)
    assert t % tm == 0, (name, t, tm)
    nr, npar, nro = len(rows), len(pars), len(row_out)

    def body(*refs):
        rv = [r[...] for r in refs[:nr]]
        pv = [r[...] for r in refs[nr:nr + npar]]
        ro_refs = refs[nr + npar:nr + npar + nro]
        ao_refs = refs[nr + npar + nro:]
        ro, ao = fn(*rv, *pv)
        for r, v in zip(ro_refs, ro, strict=True):
            r[...] = v.astype(r.dtype)
        if ao_refs:
            @pl.when(pl.program_id(0) == 0)
            def _():
                for r in ao_refs:
                    r[...] = jnp.zeros_like(r)
            for r, v in zip(ao_refs, ao, strict=True):
                r[...] += v.astype(r.dtype)

    in_specs = [pl.BlockSpec((tm, r.shape[1]), lambda i: (i, 0)) for r in rows]
    in_specs += [pl.BlockSpec(p.shape, lambda i: (0, 0)) for p in pars]
    out_specs = [pl.BlockSpec((tm, c), lambda i: (i, 0)) for c, _ in row_out]
    out_specs += [pl.BlockSpec(s, lambda i: (0, 0)) for s, _ in acc_out]
    out_shape = [_S((t, c), dt) for c, dt in row_out] + [_S(s, dt) for s, dt in acc_out]
    return pl.pallas_call(body, name=name, grid=(t // tm,), in_specs=in_specs, out_specs=out_specs,
                          out_shape=out_shape, compiler_params=_cp())(*rows, *pars)


def rms_fwd(x, w, name):
    return rows_call(name, lambda xv, wv: ((_rms(xv, wv),), ()), [x], [w], [(x.shape[1], bf16)])[0]


def rms_bwd(x, w, dy, dres, name):
    def fn(*a):
        if dres is None:
            xv, dyv, wv = a
        else:
            xv, dyv, drv, wv = a
        _, vjp = jax.vjp(_rms, xv, wv)
        dx, dw = vjp(dyv.astype(f32))
        if dres is not None:
            dx = dx + drv
        return (dx,), (dw,)
    rows = [x, dy] + ([] if dres is None else [dres])
    return rows_call(name, fn, rows, [w], [(x.shape[1], f32)], [(w.shape, f32)])


def cols_call(name, fn, seqs, pars, outs, *, nb, ct, ncol, dseed=None):
    ns, npar = len(seqs), len(pars)
    seq_len = seqs[0].shape[0] // nb
    nd = 0 if dseed is None else len(dseed)

    def body(*refs):
        sv = [r[...] for r in refs[:ns]]
        pv = [r[...] for r in refs[ns:ns + npar]]
        if dseed is None:
            o_refs = refs[ns + npar:]
            for r, v in zip(o_refs, fn(*[v.astype(f32) for v in sv], *pv), strict=True):
                r[...] = v.astype(r.dtype)
            return
        dv = [r[...].astype(f32) for r in refs[ns + npar:ns + npar + nd]]
        ds_refs = refs[ns + npar + nd:ns + npar + nd + ns]
        dp_refs = refs[ns + npar + nd + ns:]
        _, vjp = jax.vjp(fn, *[v.astype(f32) for v in sv], *pv)
        g = vjp(tuple(dv))
        for r, v in zip(ds_refs, g[:ns], strict=True):
            r[...] = v.astype(r.dtype)

        @pl.when(pl.program_id(1) == 0)
        def _():
            for r in dp_refs:
                r[...] = jnp.zeros_like(r)
        for r, v in zip(dp_refs, g[ns:], strict=True):
            r[...] += v

    full = pl.BlockSpec((seq_len, ct), lambda j, b: (b, j))
    in_specs = [full for _ in seqs]
    in_specs += [pl.BlockSpec((p.shape[0], ct), lambda j, b: (0, j)) for p in pars]
    args = list(seqs) + list(pars)
    if dseed is None:
        out_specs = [full for _ in outs]
        out_shape = [_S((nb * seq_len, ncol * ct), dt) for dt in outs]
    else:
        in_specs += [full for _ in dseed]
        args += list(dseed)
        out_specs = [full for _ in seqs] + [pl.BlockSpec((p.shape[0], ct), lambda j, b: (0, j)) for p in pars]
        out_shape = [_S((nb * seq_len, ncol * ct), bf16) for _ in seqs] + [_S(p.shape, f32) for p in pars]
    return pl.pallas_call(body, name=name, grid=(ncol, nb), in_specs=in_specs, out_specs=out_specs,
                          out_shape=out_shape, compiler_params=_cp())(*args)


def _conv4_silu(x, w, b):
    y = x * w[3:4] + _shift_down(x, 1) * w[2:3] + _shift_down(x, 2) * w[1:2] + _shift_down(x, 3) * w[0:1] + b
    return (_silu(y),)


def _conv4_silu_nobias(x, w):
    y = x * w[3:4] + _shift_down(x, 1) * w[2:3] + _shift_down(x, 2) * w[1:2] + _shift_down(x, 3) * w[0:1]
    return (_silu(y),)


def _ffn_act(gate, up, w, b):
    y = gate * w[2:3] + _shift_down(gate, 1) * w[1:2] + _shift_down(gate, 2) * w[0:1] + b
    return (_silu(y) * up,)


def scan_call(name, chunk_fn, seqs, pars, consts, outs, *, nb, nh, q, state_shape, states=None, dseed=None):
    t = seqs[0][0].shape[0]
    nc = t // (nb * q)
    ns, npar, ncon, no = len(seqs), len(pars), len(consts), len(outs)
    s0, s1 = state_shape
    bwd = dseed is not None

    def cidx(c):
        return (nc - 1 - c) if bwd else c

    def rowblk(b, c):
        return b * nc + cidx(c)

    def seq_spec(w, colfn):
        return pl.BlockSpec((q, w), lambda b, c, h: (rowblk(b, c), colfn(h)))

    def par_spec(shape, idxfn):
        return pl.BlockSpec(shape, lambda b, c, h: idxfn(h))

    st_spec = pl.BlockSpec((s0, s1), lambda b, c, h: ((rowblk(b, c)) * nh + h, 0))
    in_specs = [seq_spec(w, cf) for _, w, cf, _ in seqs]
    in_specs += [par_spec(s, f) for _, s, f in pars] + [par_spec(s, f) for _, s, f in consts]
    args = [a for a, _, _, _ in seqs] + [a for a, _, _ in pars] + [a for a, _, _ in consts]

    if not bwd:
        def body(*refs):
            sv = [r[...] for r in refs[:ns]]
            pv = [r[...] for r in refs[ns:ns + npar]]
            cv = [r[...] for r in refs[ns + npar:ns + npar + ncon]]
            o_refs = refs[ns + npar + ncon:ns + npar + ncon + no]
            save_ref = refs[ns + npar + ncon + no]
            st = refs[-1]
            c, h = pl.program_id(1), pl.program_id(2)

            @pl.when(c == 0)
            def _():
                st[h] = jnp.zeros((s0, s1), f32)
            s_in = st[h]
            save_ref[...] = s_in
            o, s_out = chunk_fn(*sv, *pv, s_in, *cv)
            st[h] = s_out
            for r, v in zip(o_refs, o, strict=True):
                r[...] = v.astype(r.dtype)

        out_specs = [seq_spec(w, cf) for _, w, cf, _ in outs] + [st_spec]
        out_shape = [_S((t, cc), dt) for cc, _, _, dt in outs] + [_S((nb * nc * nh * s0, s1), f32)]
        return pl.pallas_call(body, name=name, grid=(nb, nc, nh), in_specs=in_specs, out_specs=out_specs,
                              out_shape=out_shape, scratch_shapes=[pltpu.VMEM((nh, s0, s1), f32)],
                              compiler_params=_cp())(*args)

    def body(*refs):
        i = 0
        sv = [r[...] for r in refs[i:i + ns]]; i += ns
        pv = [r[...] for r in refs[i:i + npar]]; i += npar
        cv = [r[...] for r in refs[i:i + ncon]]; i += ncon
        dv = [r[...].astype(f32) for r in refs[i:i + no]]; i += no
        s_in = refs[i][...]; i += 1
        ds_refs = refs[i:i + ns]; i += ns
        dp_refs = refs[i:i + npar]; i += npar
        dst = refs[-1]
        b, c, h = pl.program_id(0), pl.program_id(1), pl.program_id(2)

        @pl.when(c == 0)
        def _():
            dst[h] = jnp.zeros((s0, s1), f32)

        @pl.when((b == 0) & (c == 0) & (h == 0))
        def _():
            for r in dp_refs:
                r[...] = jnp.zeros_like(r)

        fn = lambda *a: chunk_fn(*a, *cv)
        _, vjp = jax.vjp(fn, *[v.astype(f32) for v in sv], *pv, s_in)
        g = vjp((tuple(dv), dst[h]))
        dst[h] = g[ns + npar]
        for (_, _, _, rep), r, v in zip(seqs, ds_refs, g[:ns], strict=True):
            if rep == 1:
                r[...] = v.astype(r.dtype)
            else:
                @pl.when(h % rep == 0)
                def _(r=r, v=v):
                    r[...] = v.astype(r.dtype)

                @pl.when(h % rep != 0)
                def _(r=r, v=v):
                    r[...] += v.astype(r.dtype)
        for r, v in zip(dp_refs, g[ns:ns + npar], strict=True):
            r[h] += v

    in_specs += [seq_spec(w, cf) for _, w, cf, _ in outs] + [st_spec]
    args += list(dseed) + [states]
    out_specs = [seq_spec(w, cf) for _, w, cf, _ in seqs]
    out_specs += [pl.BlockSpec((nh,) + tuple(s), lambda b, c, h: (0, 0, 0)) for _, s, _ in pars]
    out_shape = [_S(a.shape, bf16 if rep == 1 else f32) for a, _, _, rep in seqs] + [_S((nh,) + tuple(s), f32) for _, s, _ in pars]
    return pl.pallas_call(body, name=name, grid=(nb, nc, nh), in_specs=in_specs, out_specs=out_specs,
                          out_shape=out_shape, scratch_shapes=[pltpu.VMEM((nh, s0, s1), f32)],
                          compiler_params=_cp())(*args)


def _ssd_group(xs, bm, cm, z, dtr, dtb, alog, dsk, nw, st, e):
    q = xs.shape[0]
    heads = range(M_H)
    sl = [slice(i * M_P, (i + 1) * M_P) for i in heads]
    gsl = [slice(g * M_N, (g + 1) * M_N) for g in range(M_G)]
    incl, _ = _masks(q)
    dt = jax.nn.softplus(dtr + dtb[0:1])
    dte = _hdot(dt, e, prec=HIGHEST)
    ae = _hdot(-jnp.exp(alog), e, prec=HIGHEST)[0:1]
    de = _hdot(dsk, e, prec=HIGHEST)[0:1]
    xc = xs * dte
    acum = _hdot(incl.astype(f32), dte * ae, prec=HIGHEST)
    last = acum[q - 1:q]
    eac, eend, elast = jnp.exp(acum), jnp.exp(last - acum), jnp.exp(last)
    xe = xc * eend
    bms, cms = [bm[:, s] for s in gsl], [cm[:, s] for s in gsl]
    cb = [_bdot(cms[g], bms[g], 1, 1) for g in range(M_G)]
    decs = []
    for i in heads:
        a_i = acum[:, sl[i]]
        diff = jnp.where(incl, a_i[:, 0:1] - jnp.transpose(a_i)[0:1, :], 0.0)
        decs.append(jnp.where(incl, jnp.exp(diff), 0.0))
    sts = [st[sl[i], :] for i in heads]
    yd = [_bdot(cb[i // 4] * decs[i], xc[:, sl[i]], 1, 0) for i in heads]
    yo = [_bdot(cms[i // 4], sts[i], 1, 1) for i in heads]
    ds = [_bdot(xe[:, sl[i]], bms[i // 4], 0, 0) for i in heads]
    new = [sts[i] * elast[:, i * M_P:i * M_P + 1] + ds[i] for i in heads]
    y = jnp.concatenate(yd, axis=1) + jnp.concatenate(yo, axis=1) * eac + de * xs
    y = y * _silu(z)
    yn = [_rms(y[:, g * 256:(g + 1) * 256], nw[:, g * 256:(g + 1) * 256]) for g in range(M_G)]
    return (jnp.concatenate(yn, axis=1),), jnp.concatenate(new, axis=0)


def _gla_group(qr, fr, ir, gr, lb, nw, st):
    q, hp = qr.shape[0], GLA_HP
    heads = range(hp)
    sl = [slice(i * H_K, (i + 1) * H_K) for i in heads]
    incl, _ = _masks(q)
    fg = lb + (1.0 - lb) * jax.nn.sigmoid(fr)
    qq = _silu(qr) * (H_K ** -0.5)
    k = 1.0 - fg
    gc = _hdot(incl.astype(f32), jnp.log(fg))
    gl = gc[q - 1:q]
    qd, ki, ke = qq * jnp.exp(gc), k * jnp.exp(-gc), k * jnp.exp(gl - gc)
    egl = jnp.exp(gl)
    sts = [st[sl[i], :] for i in heads]
    att = [jnp.where(incl, _bdot(qd[:, sl[i]], ki[:, sl[i]], 1, 1), 0.0) for i in heads]
    o1 = [_bdot(att[i], ir[:, sl[i]], 1, 0) for i in heads]
    o2 = [_bdot(qd[:, sl[i]], sts[i], 1, 0) for i in heads]
    kv = [_bdot(ke[:, sl[i]], ir[:, sl[i]], 0, 0) for i in heads]
    new = [sts[i] * _colvec(egl[:, sl[i]]) + kv[i] for i in heads]
    on = [_rms(o1[i] + o2[i], nw) * _silu(gr[:, sl[i]]) for i in heads]
    return (jnp.concatenate(on, axis=1),), jnp.concatenate(new, axis=0)


def _tri_inv_many(ms):
    n = ms[0].shape[0]
    r = lax.broadcasted_iota(jnp.int32, (n, n), 0)
    c = lax.broadcasted_iota(jnp.int32, (n, n), 1)
    eye = (r == c).astype(f32)
    ts = [eye - m for m in ms]
    ps = list(ms)
    for _ in range(max(1, (n - 1).bit_length() - 1)):
        ps = [_hdot(p, p) for p in ps]
        ts = [t + _hdot(t, p) for t, p in zip(ts, ps)]
    return ts


def _gdn_group(qr, kr, v, z, ba, alog, dtb, nw, st):
    q, hp = qr.shape[0], G_HV
    heads = range(hp)
    sl = [slice(i * G_K, (i + 1) * G_K) for i in heads]
    incl, strict = _masks(q)
    beta_all = jax.nn.sigmoid(ba)
    gc_all = _hdot(incl.astype(f32), -jnp.exp(alog[0:1]) * jax.nn.softplus(ba + dtb[0:1]))
    gc_t = jnp.transpose(gc_all)
    gl_all = gc_all[q - 1:q]
    egc_all, eend_all, egl_all = jnp.exp(gc_all), jnp.exp(gl_all - gc_all), jnp.exp(gl_all)
    lane = lambda a, i: a[:, G_HV + i:G_HV + i + 1]
    beta = [beta_all[:, i:i + 1] for i in heads]
    egc = [lane(egc_all, i) for i in heads]
    qn, kn = [], []
    for j in range(hp // 2):
        qj, kj = qr[:, sl[j]], kr[:, sl[j]]
        qn.append(qj * lax.rsqrt(jnp.sum(qj * qj, axis=-1, keepdims=True) + EPS) * (G_K ** -0.5))
        kn.append(kj * lax.rsqrt(jnp.sum(kj * kj, axis=-1, keepdims=True) + EPS))
    qk = [_bdot(qn[j], kn[j], 1, 1) for j in range(hp // 2)]
    decs = []
    for i in heads:
        diff = jnp.where(incl, lane(gc_all, i) - gc_t[G_HV + i:G_HV + i + 1, :], 0.0)
        decs.append(jnp.where(incl, jnp.exp(diff), 0.0))
    kbs = [kn[i // 2] * beta[i] for i in heads]
    kk = [_bdot(kbs[i], kn[i // 2], 1, 1) for i in heads]
    tinv = _tri_inv_many([jnp.where(strict, kk[i] * decs[i], 0.0) for i in heads])
    uw = [_hdot(tinv[i], jnp.concatenate([v[:, sl[i]] * beta[i], kbs[i] * egc[i]], axis=1)) for i in heads]
    sts = [st[sl[i], :] for i in heads]
    ws = [_bdot(jnp.concatenate([uw[i][:, G_K:], qn[i // 2] * egc[i]], axis=0), sts[i], 1, 0) for i in heads]
    v_new = [uw[i][:, :G_K] - ws[i][:q] for i in heads]
    o = [ws[i][q:] + _bdot(qk[i // 2] * decs[i], v_new[i], 1, 0) for i in heads]
    new = [sts[i] * lane(egl_all, i) + _bdot(kn[i // 2] * lane(eend_all, i), v_new[i], 0, 0) for i in heads]
    on = [_rms(o[i], nw) * _silu(z[:, sl[i]]) for i in heads]
    return (jnp.concatenate(on, axis=1),), jnp.concatenate(new, axis=0)


def _xattn_fn(q, k, v):
    s = _bdot(q, k, 1, 1) * (X_D ** -0.5)
    return _bdot(jax.nn.softmax(s, axis=-1), v, 1, 0)


def xattn_fwd(q, k, v, nb, name, tl=512):
    t = q.shape[0]
    tl = min(tl, t // nb)
    nl = t // nb // tl

    def body(q_ref, k_ref, v_ref, o_ref):
        o_ref[...] = _xattn_fn(q_ref[...], k_ref[...], v_ref[...]).astype(o_ref.dtype)

    qs = pl.BlockSpec((tl, X_D), lambda b, i, h: (b * nl + i, h))
    ks = pl.BlockSpec((N_MEM, X_D), lambda b, i, h: (b, h))
    return pl.pallas_call(body, name=name, grid=(nb, nl, X_H), in_specs=[qs, ks, ks], out_specs=qs,
                          out_shape=_S(q.shape, bf16), compiler_params=_cp())(q, k, v)


def xattn_bwd(q, k, v, do, nb, name, tl=512):
    t = q.shape[0]
    tl = min(tl, t // nb)
    nl = t // nb // tl

    def body(q_ref, k_ref, v_ref, do_ref, dq_ref, dk_ref, dv_ref):
        _, vjp = jax.vjp(_xattn_fn, q_ref[...].astype(f32), k_ref[...].astype(f32), v_ref[...].astype(f32))
        dq, dk, dv = vjp(do_ref[...].astype(f32))
        dq_ref[...] = dq.astype(dq_ref.dtype)

        @pl.when(pl.program_id(2) == 0)
        def _():
            dk_ref[...] = jnp.zeros_like(dk_ref)
            dv_ref[...] = jnp.zeros_like(dv_ref)
        dk_ref[...] += dk
        dv_ref[...] += dv

    qs = pl.BlockSpec((tl, X_D), lambda b, h, i: (b * nl + i, h))
    ks = pl.BlockSpec((N_MEM, X_D), lambda b, h, i: (b, h))
    return pl.pallas_call(body, name=name, grid=(nb, X_H, nl), in_specs=[qs, ks, ks, qs], out_specs=[qs, ks, ks],
                          out_shape=[_S(q.shape, bf16), _S(k.shape, f32), _S(v.shape, f32)],
                          compiler_params=_cp())(q, k, v, do)


def _lower_bounds(hlb):
    sm = jax.nn.softmax(hlb, axis=0)
    rows, run = [], None
    for r in range(hlb.shape[0]):
        run = sm[r:r + 1] if run is None else run + sm[r:r + 1]
        rows.append(run - sm[0:1])
    return jnp.concatenate(rows, axis=0)


def lower_bounds_fwd(hlb):
    return rows_call("lb_fwd", lambda v: ((_lower_bounds(v),), ()), [hlb], [], [(hlb.shape[1], f32)], tm=hlb.shape[0])[0]


def lower_bounds_bwd(hlb, dlb):
    def fn(v, d):
        _, vjp = jax.vjp(_lower_bounds, v)
        return (vjp(d)[0],), ()
    return rows_call("lb_bwd", fn, [hlb, dlb], [], [(hlb.shape[1], f32)], tm=hlb.shape[0])[0]


def loss_head(x, target, w):
    def fn(xv, tv, wv):
        def loss(xx, ww):
            err = _rms(xx, ww) - tv
            return 0.5 * jnp.sum(jnp.mean(err * err, axis=-1))
        val, (dx, dw) = jax.value_and_grad(loss, argnums=(0, 1))(xv, wv)
        return (dx,), (jnp.broadcast_to(val, (1, 128)), dw)
    dx, loss, dw = rows_call("loss_head", fn, [x, target], [w], [(x.shape[1], f32)], [((1, 128), f32), (w.shape, f32)])
    return dx, loss, dw


def _adamw_fn(w, g, m, v):
    m2 = ADAM_B1 * m + (1.0 - ADAM_B1) * g
    v2 = ADAM_B2 * v + (1.0 - ADAM_B2) * (g * g)
    m_hat = m2 / (1.0 - ADAM_B1 ** ADAM_STEP)
    v_hat = v2 / (1.0 - ADAM_B2 ** ADAM_STEP)
    delta = -ADAM_LR * (m_hat / (jnp.sqrt(v_hat) + ADAM_EPS) + ADAM_WD * w)
    return delta, m2, v2


def adamw(w, g, m, v, name, g2=None):
    shape = w.shape
    c = shape[-1]
    r = w.size // c
    to2 = lambda a: a.reshape(r, c)
    tm = r if r * c * 4 <= (1 << 20) else _tile(r, (256, 128, 64, 32, 16, 8))

    def fn(*a):
        if g2 is None:
            wv, gv, mv, vv = a
        else:
            wv, gv, g2v, mv, vv = a
            gv = gv + g2v
        return (gv,) + _adamw_fn(wv, gv, mv, vv), ()
    rows = [to2(w), to2(g)] + ([] if g2 is None else [to2(g2)]) + [to2(m), to2(v)]
    outs = rows_call(name, fn, rows, [], [(c, f32)] * 4, tm=tm)
    return tuple(o.reshape(shape) for o in outs)


def _pad_row(v, lane0=0):
    return jnp.pad(v.astype(f32).reshape(1, -1), ((0, 7), (lane0, 128 - lane0 - v.shape[0])))


def _pad_cols(w, n=128):
    return jnp.pad(w, ((0, 0), (0, n - w.shape[1])))


_COL = lambda h: h
_C00 = lambda h: (0, 0)
_CONV_CT = 256


def _conv(name, x, w, b, nb, dseed=None):
    fn = _conv4_silu if b is not None else _conv4_silu_nobias
    pars = [w] + ([] if b is None else [b])
    return cols_call(name, fn, [x], pars, [f32], nb=nb, ct=_CONV_CT, ncol=x.shape[1] // _CONV_CT,
                     dseed=None if dseed is None else [dseed])


GLA_HP = 8


def _ssd_scan(name, xs, bm, cm, z, dtr, p, nb, states=None, dseed=None):
    seqs = [(xs, M_INNER, _COL, 1), (bm, M_G * M_N, _COL, 1), (cm, M_G * M_N, _COL, 1), (z, M_INNER, _COL, 1), (dtr, 128, _COL, 1)]
    pars = [(p["dtb"], (8, 128), _C00), (p["alog"], (8, 128), _C00), (p["dsk"], (8, 128), _C00), (p["nw"], (1, M_INNER), _C00)]
    r = jnp.arange(128)[:, None]
    c = jnp.arange(M_INNER)[None, :]
    consts = [((r == c // M_P).astype(f32), (128, M_INNER), _C00)]
    outs = [(M_INNER, M_INNER, _COL, bf16)]
    return scan_call(name, _ssd_group, seqs, pars, consts, outs, nb=nb, nh=1, q=M_Q, state_shape=(M_H * M_P, M_N),
                     states=states, dseed=dseed)


def _gla_scan(name, qr, fr, ir, gr, p, nb, states=None, dseed=None):
    hp, ng = GLA_HP, H_H // GLA_HP
    seqs = [(a, 128 * hp, _COL, 1) for a in (qr, fr, ir, gr)]
    pars = [(p["lb"], (1, 128 * hp), lambda h: (0, h)), (p["nw"], (1, 128), _C00)]
    outs = [(D, 128 * hp, _COL, bf16)]
    return scan_call(name, _gla_group, seqs, pars, [], outs, nb=nb, nh=ng, q=H_Q, state_shape=(hp * H_K, H_K),
                     states=states, dseed=dseed)


def _gdn_scan(name, qc, kc, vc, z, ba, p, nb, states=None, dseed=None):
    seqs = [(qc, D, _COL, 1), (kc, D, _COL, 1), (vc, G_VAL, _COL, 1), (z, G_VAL, _COL, 1), (ba, 128, _COL, 1)]
    pars = [(p["alog"], (8, 128), _C00), (p["dtb"], (8, 128), _C00), (p["nw"], (1, 128), _C00)]
    outs = [(G_VAL, G_VAL, _COL, bf16)]
    return scan_call(name, _gdn_group, seqs, pars, [], outs, nb=nb, nh=1, q=G_Q, state_shape=(G_HV * G_K, G_K),
                     states=states, dseed=dseed)


def _w(wt):
    return wt if isinstance(wt, tuple) else (wt, None)


def _proj(a, wt, name, res=None, out_dtype=f32):
    arr, bsel = _w(wt)
    return mm(a, arr, bsel=bsel, res=res, out_dtype=out_dtype, name=name)


def _proj_bwd(tag, hn, pieces):
    dhn, dws, bufs = None, [], {}
    for i, (d, wt) in enumerate(pieces):
        arr, bsel = _w(wt)
        if bsel is None:
            dws.append(mm(hn, d, ta=True, out_dtype=bf16, name=f"{tag}_dw{i}"))
        else:
            bufs[id(arr)] = mm(hn, d, ta=True, out_stack=arr.shape[2], out_slots=(arr.shape[0], bsel[0]),
                               into=bufs.get(id(arr)), out_dtype=bf16, name=f"{tag}_dw{i}")
            dws.append(None)
        dhn = mm(d, arr, tb=True, bsel=bsel, res=dhn, name=f"{tag}_dh{i}")
    dws = [dw if dw is not None else bufs[id(_w(wt)[0])] for dw, (_, wt) in zip(dws, pieces, strict=True)]
    return dhn, dws


def ssd_mixer_fwd(tag, hn, w, nb):
    z, xr, br, cr, dtr = (_proj(hn, w[k], f"{tag}_in_{k}") for k in ("wz", "wx", "wb", "wc", "wdt"))
    xs = _conv(f"{tag}_convx", xr, w["cwx"], w["cbx"], nb)[0]
    bm = _conv(f"{tag}_convb", br, w["cwb"], w["cbb"], nb)[0]
    cm = _conv(f"{tag}_convc", cr, w["cwc"], w["cbc"], nb)[0]
    yn, states = _ssd_scan(f"{tag}_scan", xs, bm, cm, z, dtr, w, nb)
    return yn, (hn, z, xr, br, cr, dtr, xs, bm, cm, yn, states)


def ssd_mixer_bwd(tag, saved, dout, w, nb):
    hn, z, xr, br, cr, dtr, xs, bm, cm, yn, states = saved
    g = {"wout": mm(yn, dout, ta=True, out_dtype=bf16, name=f"{tag}_dwout")}
    dyn = mm(dout, w["wout"], tb=True, out_dtype=bf16, name=f"{tag}_dyn")
    dxs, dbm, dcm, dz, ddtr, ddtb, dalog, ddsk, dnw = _ssd_scan(f"{tag}_scanb", xs, bm, cm, z, dtr, w, nb, states, [dyn])
    dxr, g["cwx"], g["cbx"] = _conv(f"{tag}_convxb", xr, w["cwx"], w["cbx"], nb, dxs)
    dbr, g["cwb"], g["cbb"] = _conv(f"{tag}_convbb", br, w["cwb"], w["cbb"], nb, dbm)
    dcr, g["cwc"], g["cbc"] = _conv(f"{tag}_convcb", cr, w["cwc"], w["cbc"], nb, dcm)
    dhn, (g["wz"], g["wx"], g["wb"], g["wc"], g["wdt"]) = _proj_bwd(
        tag, hn, [(dz, w["wz"]), (dxr, w["wx"]), (dbr, w["wb"]), (dcr, w["wc"]), (ddtr, w["wdt"])])
    g["dtb"], g["alog"], g["dsk"] = (jnp.sum(a, axis=0)[0, :M_H] for a in (ddtb, dalog, ddsk))
    g["nw"] = dnw.reshape(M_INNER)
    return dhn, g


def gla_mixer_fwd(tag, hn, w, nb):
    qr, fr, ir, gr = (_proj(hn, w[k], f"{tag}_in_{k}") for k in ("wq", "wf", "wi", "wg"))
    on, states = _gla_scan(f"{tag}_scan", qr, fr, ir, gr, w, nb)
    return on, (hn, qr, fr, ir, gr, on, states)


def gla_mixer_bwd(tag, saved, dout, w, nb):
    hn, qr, fr, ir, gr, on, states = saved
    g = {"wout": mm(on, dout, ta=True, out_dtype=bf16, name=f"{tag}_dwout")}
    don = mm(dout, w["wout"], tb=True, out_dtype=bf16, name=f"{tag}_don")
    dq, df, di, dg, dlb, dnw = _gla_scan(f"{tag}_scanb", qr, fr, ir, gr, w, nb, states, [don])
    dhn, (g["wq"], g["wf"], g["wi"], g["wg"]) = _proj_bwd(tag, hn, [(dq, w["wq"]), (df, w["wf"]), (di, w["wi"]), (dg, w["wg"])])
    g["lb"] = dlb.reshape(1, D)
    g["nw"] = jnp.sum(dnw, axis=0).reshape(H_K)
    return dhn, g


def gdn_mixer_fwd(tag, hn, w, nb):
    qr, kr, vr, z, ba = (_proj(hn, w[k], f"{tag}_in_{k}") for k in ("wq", "wk", "wv", "wz", "wba"))
    qc = _conv(f"{tag}_convq", qr, w["cwq"], None, nb)[0]
    kc = _conv(f"{tag}_convk", kr, w["cwk"], None, nb)[0]
    vc = _conv(f"{tag}_convv", vr, w["cwv"], None, nb)[0]
    on, states = _gdn_scan(f"{tag}_scan", qc, kc, vc, z, ba, w, nb)
    return on, (hn, qr, kr, vr, z, ba, qc, kc, vc, on, states)


def gdn_mixer_bwd(tag, saved, dout, w, nb):
    hn, qr, kr, vr, z, ba, qc, kc, vc, on, states = saved
    g = {"wout": mm(on, dout, ta=True, out_dtype=bf16, name=f"{tag}_dwout")}
    don = mm(dout, w["wout"], tb=True, out_dtype=bf16, name=f"{tag}_don")
    dqc, dkc, dvc, dz, dba, dalog, ddtb, dnw = _gdn_scan(f"{tag}_scanb", qc, kc, vc, z, ba, w, nb, states, [don])
    dqr, g["cwq"] = _conv(f"{tag}_convqb", qr, w["cwq"], None, nb, dqc)
    dkr, g["cwk"] = _conv(f"{tag}_convkb", kr, w["cwk"], None, nb, dkc)
    dvr, g["cwv"] = _conv(f"{tag}_convvb", vr, w["cwv"], None, nb, dvc)
    dhn, (g["wq"], g["wk"], g["wv"], g["wz"], g["wba"]) = _proj_bwd(
        tag, hn, [(dqr, w["wq"]), (dkr, w["wk"]), (dvr, w["wv"]), (dz, w["wz"]), (dba, w["wba"])])
    g["alog"], g["dtb"] = (jnp.sum(a, axis=0)[0, G_HV:2 * G_HV] for a in (dalog, ddtb))
    g["nw"] = jnp.sum(dnw, axis=0).reshape(G_K)
    return dhn, g


_MIXERS = {0: (ssd_mixer_fwd, ssd_mixer_bwd), 1: (gla_mixer_fwd, gla_mixer_bwd), 2: (gdn_mixer_fwd, gdn_mixer_bwd)}


def layer_fwd(i, x, mem, weights_of, nb):
    t = f"l{i}"
    wm = weights_of(i, 0, x)
    hn = rms_fwd(x, wm["ln_mix"], f"{t}_ln_mix")
    mix, s_mix = _MIXERS[i % 3][0](f"{t}_mix", hn, wm["mix"], nb)
    x1 = mm(mix, wm["mix"]["wout"], res=x, name=f"{t}_mix_out")
    w = weights_of(i, 1, x1)
    hx = rms_fwd(x1, w["ln_xattn"], f"{t}_ln_xattn")
    mn = rms_fwd(mem, w["ln_mem"], f"{t}_ln_mem")
    q = _proj(hx, w["xq"], f"{t}_xa_q", out_dtype=bf16)
    k = _proj(mn, w["xk"], f"{t}_xa_k", out_dtype=bf16)
    v = _proj(mn, w["xv"], f"{t}_xa_v", out_dtype=bf16)
    o = xattn_fwd(q, k, v, nb, f"{t}_xattn")
    x2 = mm(o, w["xo"], res=x1, name=f"{t}_xa_o")
    hf = rms_fwd(x2, w["ln_ffn"], f"{t}_ln_ffn")
    gate = _proj(hf, w["fg"], f"{t}_ffn_gate", out_dtype=bf16)
    up = _proj(hf, w["fu"], f"{t}_ffn_up", out_dtype=bf16)
    act = cols_call(f"{t}_ffn_act", _ffn_act, [gate, up], [w["fcw"], w["fcb"]], [bf16], nb=nb, ct=_CONV_CT,
                    ncol=D_FF // _CONV_CT)[0]
    x3 = mm(act, w["fd"], res=x2, name=f"{t}_ffn_down")
    return x3, (wm, w, x, s_mix, x1, hx, mn, q, k, v, o, x2, hf, gate, up, act)


def layer_bwd(i, saved, dx, mem, nb, token, grads_done):
    t = f"l{i}b"
    wm, w, x, s_mix, x1, hx, mn, q, k, v, o, x2, hf, gate, up, act = saved
    if token is not None:
        w = dict(w, fd=w["fd"] + token[0, 0].astype(w["fd"].dtype))
    g = {}
    g["fd"] = mm(act, dx, ta=True, out_dtype=bf16, name=f"{t}_dwd")
    dact = mm(dx, w["fd"], tb=True, out_dtype=bf16, name=f"{t}_dact")
    dgate, dup, g["fcw"], g["fcb"] = cols_call(f"{t}_ffn_act", _ffn_act, [gate, up], [w["fcw"], w["fcb"]], [bf16], nb=nb,
                                               ct=_CONV_CT, ncol=D_FF // _CONV_CT, dseed=[dact])
    dhf, (g["fg"], g["fu"]) = _proj_bwd(f"{t}_ffn", hf, [(dgate, w["fg"]), (dup, w["fu"])])
    dx, g["ln_ffn"] = rms_bwd(x2, w["ln_ffn"], dhf, dx, f"{t}_ln_ffn")
    g["xo"] = mm(o, dx, ta=True, out_dtype=bf16, name=f"{t}_dwo")
    do = mm(dx, w["xo"], tb=True, out_dtype=bf16, name=f"{t}_do")
    dq, dk, dv = xattn_bwd(q, k, v, do, nb, f"{t}_xattn")
    dhx, (g["xq"],) = _proj_bwd(f"{t}_xq", hx, [(dq, w["xq"])])
    dmn, (g["xk"], g["xv"]) = _proj_bwd(f"{t}_xkv", mn, [(dk, w["xk"]), (dv, w["xv"])])
    _, g["ln_mem"] = rms_bwd(mem, w["ln_mem"], dmn, None, f"{t}_ln_mem")
    dx, g["ln_xattn"] = rms_bwd(x1, w["ln_xattn"], dhx, dx, f"{t}_ln_xattn")
    token = grads_done(i, 1, g, dx) if grads_done else None
    mixw = wm["mix"] if token is None else dict(wm["mix"], wout=wm["mix"]["wout"] + token[0, 0].astype(wm["mix"]["wout"].dtype))
    dhn, g["mix"] = _MIXERS[i % 3][1](f"{t}_mix", s_mix, dx, mixw, nb)
    dx, g["ln_mix"] = rms_bwd(x, wm["ln_mix"], dhn, dx, f"{t}_ln_mix")
    token = grads_done(i, 0, g, dx) if grads_done else None
    return dx, g, token


def local_step(x, mem, target, weights_of, final_norm, nb, grads_done=None):
    saved = []
    for i in range(DEPTH):
        x, s = layer_fwd(i, x, mem, weights_of, nb)
        saved.append(s)
    dx, loss, dfinal = loss_head(x, target, final_norm)
    grads = [None] * DEPTH
    token = None
    for i in reversed(range(DEPTH)):
        dx, grads[i], token = layer_bwd(i, saved[i], dx, mem, nb, token, grads_done)
    return loss, dx, grads, dfinal


WEIGHTS = ["ln_mix", "ln_xattn", "ln_mem", "ln_ffn", "final_norm", "m_in_w", "m_conv_w", "m_conv_b", "m_dt_bias", "m_a_log",
           "m_d", "m_norm_w", "m_out_w", "h_in_w", "h_lower_bounds", "h_norm_w", "h_out_w", "g_in_w", "g_conv_w", "g_a_log",
           "g_dt_bias", "g_norm_w", "g_out_w", "xa_q", "xa_kv", "xa_o", "f_up", "f_conv_w", "f_conv_b", "f_down"]
SHARD_AXIS = {"m_in_w": 2, "m_conv_w": 2, "m_conv_b": 1, "m_norm_w": 1, "m_out_w": 1, "h_in_w": 2, "h_out_w": 1, "g_in_w": 2,
              "g_conv_w": 2, "g_out_w": 1, "xa_q": 1, "xa_kv": 2, "xa_o": 1, "f_up": 2, "f_conv_w": 2, "f_down": 1}
MATRICES = ["m_in_w", "m_out_w", "h_in_w", "h_out_w", "g_in_w", "g_out_w", "xa_q", "xa_kv", "xa_o", "f_up", "f_down"]
SMALL_SHARDED = [n for n in WEIGHTS if n in SHARD_AXIS and n not in MATRICES]
REPLICATED = [n for n in WEIGHTS if n not in SHARD_AXIS]
_MIXER_PREFIX = {0: "m", 1: "h", 2: "g"}


def layer_weight_names(i, part):
    if part == 0:
        p = _MIXER_PREFIX[i % 3]
        return [(n, i // 3) for n in WEIGHTS if n in SHARD_AXIS and n.startswith(p + "_")]
    return [(n, i) for n in ("xa_q", "xa_kv", "xa_o", "f_up", "f_conv_w", "f_down")]


def _cols(st, lo, hi):
    ns = st.shape[-1]
    parts = []
    for j in range(NCHIP):
        a, b = max(lo, j * ns), min(hi, (j + 1) * ns)
        if a < b:
            parts.append(st[j][..., a - j * ns:b - j * ns])
    return parts[0] if len(parts) == 1 else jnp.concatenate(parts, axis=-1)


def _col_shards(pieces, ns):
    full = jnp.concatenate(pieces, axis=-1)
    return [full[..., j * ns:(j + 1) * ns] for j in range(NCHIP)]


def _rows(st):
    return st.reshape(st.shape[0] * st.shape[1], st.shape[2])


def prep_layer(i, part, G, R, lb):
    row = lambda a: a.reshape(1, -1)
    p, k = _MIXER_PREFIX[i % 3], i // 3
    if part == 1:
        kv, fup = G["xa_kv"], G["f_up"]
        return dict(ln_xattn=R["ln_xattn"][i:i + 1], ln_mem=R["ln_mem"][i:i + 1], ln_ffn=R["ln_ffn"][i:i + 1],
                    xq=_rows(G["xa_q"]), xk=(kv, (0, 2)), xv=(kv, (2, 2)), xo=_rows(G["xa_o"]), fg=(fup, (0, 2)), fu=(fup, (2, 2)),
                    fcw=_cols(G["f_conv_w"], 0, D_FF), fcb=R["f_conv_b"][i:i + 1], fd=_rows(G["f_down"]))
    layer = dict(ln_mix=R["ln_mix"][i:i + 1])
    inw, wout = G[p + "_in_w"], _rows(G[p + "_out_w"])
    if p == "m":
        cw, cb = G["m_conv_w"], G["m_conv_b"]
        a, b, c = M_INNER, M_INNER + M_G * M_N, M_CONV
        layer["mix"] = dict(
            wz=_cols(inw, 0, M_INNER), wx=_cols(inw, M_INNER, M_INNER + a), wb=_cols(inw, M_INNER + a, M_INNER + b),
            wc=_cols(inw, M_INNER + b, M_MAIN), wdt=_pad_cols(_cols(inw, M_MAIN, M_IN)),
            cwx=_cols(cw, 0, a), cwb=_cols(cw, a, b), cwc=_cols(cw, b, c),
            cbx=row(_cols(cb, 0, a)), cbb=row(_cols(cb, a, b)), cbc=row(_cols(cb, b, c)),
            dtb=_pad_row(R["m_dt_bias"][k]), alog=_pad_row(R["m_a_log"][k]), dsk=_pad_row(R["m_d"][k]),
            nw=row(_cols(G["m_norm_w"], 0, M_INNER)), wout=wout)
    elif p == "h":
        layer["mix"] = dict(wq=(inw, (0, 1)), wf=(inw, (1, 1)), wi=(inw, (2, 1)), wg=(inw, (3, 1)),
                            lb=lb[i:i + 1], nw=row(R["h_norm_w"][k]), wout=wout)
    else:
        cw = G["g_conv_w"]
        layer["mix"] = dict(
            wq=_cols(inw, 0, D), wk=_cols(inw, D, 2 * D), wv=_cols(inw, 2 * D, G_CONV), wz=_cols(inw, G_CONV, G_MAIN),
            wba=_pad_cols(_cols(inw, G_MAIN, G_IN)), cwq=_cols(cw, 0, D), cwk=_cols(cw, D, 2 * D), cwv=_cols(cw, 2 * D, G_CONV),
            alog=_pad_row(R["g_a_log"][k], G_HV), dtb=_pad_row(R["g_dt_bias"][k], G_HV),
            nw=row(R["g_norm_w"][k]), wout=wout)
    return layer


def matrix_grad_parts(i, part, g):
    by_rows = lambda a: a.reshape(NCHIP, a.shape[0] // NCHIP, a.shape[1])
    if part == 1:
        return {"xa_q": by_rows(g["xq"]), "xa_kv": g["xk"], "xa_o": by_rows(g["xo"]), "f_up": g["fg"], "f_down": by_rows(g["fd"])}
    p = _MIXER_PREFIX[i % 3]
    m = g["mix"]
    out = {p + "_out_w": by_rows(m["wout"])}
    if p == "m":
        out["m_in_w"] = jnp.stack(_col_shards([m["wz"], m["wx"], m["wb"], m["wc"], m["wdt"]], M_IN // NCHIP))
    elif p == "h":
        out["h_in_w"] = m["wq"]
    else:
        out["g_in_w"] = jnp.stack(_col_shards([m["wq"], m["wk"], m["wv"], m["wz"], m["wba"]], G_IN // NCHIP))
    return out


def small_grads(grads, dfinal, hlb):
    cat = lambda xs: jnp.concatenate(xs, axis=1)
    out = {k: jnp.concatenate([g[k] for g in grads], axis=0) for k in ("ln_mix", "ln_xattn", "ln_mem", "ln_ffn")}
    out["final_norm"] = dfinal.reshape(D)
    out["f_conv_w"] = jnp.stack([g["fcw"] for g in grads])
    out["f_conv_b"] = jnp.concatenate([g["fcb"] for g in grads], axis=0)
    ms = [g["mix"] for i, g in enumerate(grads) if i % 3 == 0]
    out["m_conv_w"] = jnp.stack([cat([m["cwx"], m["cwb"], m["cwc"]]) for m in ms])
    out["m_conv_b"] = jnp.concatenate([cat([m["cbx"], m["cbb"], m["cbc"]]) for m in ms], axis=0)
    out["m_dt_bias"] = jnp.stack([m["dtb"] for m in ms])
    out["m_a_log"] = jnp.stack([m["alog"] for m in ms])
    out["m_d"] = jnp.stack([m["dsk"] for m in ms])
    out["m_norm_w"] = jnp.stack([m["nw"] for m in ms])
    hs = [(i, g["mix"]) for i, g in enumerate(grads) if i % 3 == 1]
    lb_rows = dict(hs)
    dlb = jnp.concatenate([lb_rows[i]["lb"] if i in lb_rows else jnp.zeros((1, D), f32) for i in range(DEPTH)], axis=0)
    out["h_lower_bounds"] = lower_bounds_bwd(hlb, dlb)
    out["h_norm_w"] = jnp.stack([m["nw"] for _, m in hs])
    gs = [g["mix"] for i, g in enumerate(grads) if i % 3 == 2]
    out["g_conv_w"] = jnp.stack([cat([m["cwq"], m["cwk"], m["cwv"]]) for m in gs])
    out["g_a_log"] = jnp.stack([m["alog"] for m in gs])
    out["g_dt_bias"] = jnp.stack([m["dtb"] for m in gs])
    out["g_norm_w"] = jnp.stack([m["nw"] for m in gs])
    return out


_HBM = pl.BlockSpec(memory_space=pltpu.HBM)


def _place():
    x, y, c = lax.axis_index("x"), lax.axis_index("y"), lax.axis_index("c")
    chips = [(1 - x, y), (x, 1 - y), (1 - x, 1 - y)]
    return x, y, c, chips


def gather_shards(name, tensors):
    n = len(tensors)

    def body(*refs):
        ins, outs = refs[:n], refs[n:2 * n]
        send_sems, recv_sems, loc_sems = refs[2 * n:]
        x, y, c, chips = _place()
        me = 2 * x + y
        local_copies, sends = [], []
        for t in range(n):
            loc = pltpu.make_async_copy(ins[t], outs[t].at[me], loc_sems.at[t])
            loc.start()
            local_copies.append(loc)
            for j, (px, py) in enumerate(chips):
                cp = pltpu.make_async_remote_copy(src_ref=ins[t], dst_ref=outs[t].at[me], send_sem=send_sems.at[3 * t + j],
                                                  recv_sem=recv_sems.at[3 * t + j], device_id=(px, py, c), device_id_type=MESH)
                cp.start()
                sends.append(cp)
        for t in range(n):
            for j, (px, py) in enumerate(chips):
                pltpu.make_async_remote_copy(src_ref=ins[t], dst_ref=outs[t].at[2 * px + py], send_sem=send_sems.at[3 * t + j],
                                             recv_sem=recv_sems.at[3 * t + j], device_id=(px, py, c),
                                             device_id_type=MESH).wait_recv()
        for cp in sends:
            cp.wait_send()
        for cp in local_copies:
            cp.wait()

    return pl.pallas_call(
        body, name=name, in_specs=[_HBM] * n, out_specs=[_HBM] * n,
        out_shape=[_S((NCHIP,) + a.shape, a.dtype) for a in tensors],
        scratch_shapes=[pltpu.SemaphoreType.DMA((3 * n,)), pltpu.SemaphoreType.DMA((3 * n,)), pltpu.SemaphoreType.DMA((n,))])(*tensors)


_SEM = pl.BlockSpec(memory_space=pltpu.SEMAPHORE)
_ANY = pl.BlockSpec(memory_space=pl.ANY)
_SPLIT = pltpu.CompilerParams(has_side_effects=pltpu.SideEffectType.DATAFLOW_SIDE_EFFECTING)


def _hbm(a):
    return pltpu.with_memory_space_constraint(a, pltpu.HBM)


def _split_start(name, srcs, lands, dep, copies):
    n = len(srcs)

    def body(*refs):
        src_refs, land_refs = refs[:n], refs[n:2 * n]
        send_sems, recv_sems = refs[2 * n + 1], refs[2 * n + 2]
        token = refs[-1]
        for cp in copies(src_refs, land_refs, send_sems, recv_sems):
            cp.start()
        token[...] = jnp.zeros_like(token)

    thru = [pltpu.HBM(a.shape, a.dtype) for a in list(srcs) + list(lands)]
    out = pl.pallas_call(
        body, name=name, in_specs=[_HBM] * (2 * n) + [_ANY],
        out_specs=[_SEM, _SEM] + [_HBM] * (2 * n) + [pl.BlockSpec(memory_space=pltpu.VMEM)],
        out_shape=[pltpu.SemaphoreType.DMA((3 * n,)), pltpu.SemaphoreType.DMA((3 * n,))] + thru + [_S((8, 128), f32)],
        input_output_aliases={t: 2 + t for t in range(2 * n)}, compiler_params=_SPLIT,
    )(*[_hbm(a) for a in srcs], *[_hbm(a) for a in lands], dep)
    return out[0], out[1], out[2:2 + n], out[2 + n:2 + 2 * n], out[-1]


def _split_wait(name, started, after, copies):
    send_sems, recv_sems, srcs, lands, _ = started
    n = len(srcs)

    def body(*refs):
        src_refs, land_refs = refs[:n], refs[n:2 * n]
        s_sems, r_sems = refs[2 * n], refs[2 * n + 1]
        for cp in copies(src_refs, land_refs, s_sems, r_sems):
            cp.wait_send()
            cp.wait_recv()

    out = pl.pallas_call(
        body, name=name, in_specs=[_HBM] * (2 * n) + [_SEM, _SEM, _ANY], out_specs=[_HBM] * (2 * n),
        out_shape=[pltpu.HBM(a.shape, a.dtype) for a in list(srcs) + list(lands)],
        input_output_aliases={t: t for t in range(2 * n)}, compiler_params=_SPLIT,
    )(*srcs, *lands, send_sems, recv_sems, after)
    return out[:n], out[n:]


def _gather_copies(arrive):
    def copies(src_refs, land_refs, send_sems, recv_sems):
        x, y, c, chips = _place()
        out = []
        for t, (s, l) in enumerate(zip(src_refs, land_refs, strict=True)):
            for j, (px, py) in enumerate(chips):
                slot = 2 * px + py if arrive else 2 * x + y
                out.append(pltpu.make_async_remote_copy(src_ref=s, dst_ref=l.at[slot], send_sem=send_sems.at[3 * t + j],
                                                        recv_sem=recv_sems.at[3 * t + j], device_id=(px, py, c), device_id_type=MESH))
        return out
    return copies


def gather_start(name, tensors, me, dep):
    lands = [lax.dynamic_update_index_in_dim(jnp.zeros((NCHIP,) + a.shape, a.dtype), a, me, 0) for a in tensors]
    return _split_start(name, tensors, lands, dep, _gather_copies(False))


def gather_wait(name, started, after):
    return _split_wait(name, started, after, _gather_copies(True))


def _scatter_copies(src_refs, land_refs, send_sems, recv_sems):
    x, y, c, chips = _place()
    out = []
    for t, (s, l) in enumerate(zip(src_refs, land_refs, strict=True)):
        for j, (px, py) in enumerate(chips):
            out.append(pltpu.make_async_remote_copy(src_ref=s.at[2 * px + py], dst_ref=l.at[j], send_sem=send_sems.at[3 * t + j],
                                                    recv_sem=recv_sems.at[3 * t + j], device_id=(px, py, c), device_id_type=MESH))
    return out


def scatter_start(name, parts, dep):
    lands = [lax.empty((3,) + a.shape[1:], a.dtype) for a in parts]
    return _split_start(name, parts, lands, dep, _scatter_copies)


def scatter_wait(name, started, after):
    return _split_wait(name, started, after, _scatter_copies)


def sum_parts(name, part, land, me):
    shape = land.shape[1:]
    c = shape[-1]
    r = land.size // (3 * c)
    tm = _tile(r, (256, 128, 64, 32, 16, 8))

    def body(me_ref, p_ref, l_ref, o_ref):
        o_ref[...] = p_ref[...].astype(f32) + l_ref[0].astype(f32) + l_ref[1].astype(f32) + l_ref[2].astype(f32)

    grid_spec = pltpu.PrefetchScalarGridSpec(
        num_scalar_prefetch=1, grid=(r // tm,),
        in_specs=[pl.BlockSpec((None, tm, c), lambda i, me_ref: (me_ref[0], i, 0)),
                  pl.BlockSpec((3, tm, c), lambda i, me_ref: (0, i, 0))],
        out_specs=pl.BlockSpec((tm, c), lambda i, me_ref: (i, 0)))
    out = pl.pallas_call(body, name=name, grid_spec=grid_spec, out_shape=_S((r, c), f32), compiler_params=_cp())(
        me.reshape(1).astype(jnp.int32), part.reshape(NCHIP, r, c), land.reshape(3, r, c))
    return out.reshape(shape)


def _swap_copies(src_refs, land_refs, send_sems, recv_sems):
    x, y, c, _ = _place()
    return [pltpu.make_async_remote_copy(src_ref=s, dst_ref=l, send_sem=send_sems.at[3 * t], recv_sem=recv_sems.at[3 * t],
                                         device_id=(x, y, 1 - c), device_id_type=MESH)
            for t, (s, l) in enumerate(zip(src_refs, land_refs, strict=True))]


def swap_start(name, tensors, dep):
    return _split_start(name, tensors, [lax.empty(a.shape, a.dtype) for a in tensors], dep, _swap_copies)


def swap_wait(name, started, after):
    return _split_wait(name, started, after, _swap_copies)


def allreduce_small(v):
    r, n = v.shape

    def body(x_ref, out_ref, gat, send_sems, recv_sems, local_sem):
        x, y, c, chips = _place()
        me, sibling = (x, y, c), (x, y, 1 - c)

        def rows(px, py, pc):
            return gat.at[pl.ds((4 * px + 2 * py + pc) * r, r), :]

        def copy(k, block, to, src=None):
            return pltpu.make_async_remote_copy(src_ref=rows(*block) if src is None else src, dst_ref=rows(*block),
                                                send_sem=send_sems.at[k], recv_sem=recv_sems.at[k], device_id=to,
                                                device_id_type=MESH)

        mine = pltpu.make_async_copy(x_ref, rows(*me), local_sem)
        mine.start()
        first = [copy(0, me, sibling, src=x_ref)] + [copy(1 + j, me, (*chip, c), src=x_ref) for j, chip in enumerate(chips)]
        for cp in first:
            cp.start()
        passed = [copy(4 + j, (*chip, c), sibling) for j, chip in enumerate(chips)]
        for j, chip in enumerate(chips):
            copy(1 + j, (*chip, c), me).wait_recv()
            passed[j].start()
        copy(0, sibling, me).wait_recv()
        for j, chip in enumerate(chips):
            copy(4 + j, (*chip, 1 - c), me).wait_recv()
        for cp in first + passed:
            cp.wait_send()
        mine.wait()
        acc = gat[0:r, :]
        for d in range(1, 8):
            acc = acc + gat[d * r:(d + 1) * r, :]
        out_ref[...] = acc

    vm = pl.BlockSpec(memory_space=pltpu.VMEM)
    return pl.pallas_call(
        body, name="allreduce_small", in_specs=[vm], out_specs=vm, out_shape=_S((r, n), v.dtype),
        scratch_shapes=[pltpu.VMEM((8 * r, n), v.dtype), pltpu.SemaphoreType.DMA((7,)), pltpu.SemaphoreType.DMA((7,)),
                        pltpu.SemaphoreType.DMA],
        compiler_params=_cp())(v)


SMALL_ROW = 1024


def kernel(x, mem, ln_mix, ln_xattn, ln_mem, ln_ffn, final_norm, m_in_w, m_conv_w, m_conv_b, m_dt_bias, m_a_log, m_d, m_norm_w, m_out_w, h_in_w, h_lower_bounds, h_norm_w, h_out_w, g_in_w, g_conv_w, g_a_log, g_dt_bias, g_norm_w, g_out_w, xa_q, xa_kv, xa_o, f_up, f_conv_w, f_conv_b, f_down, loss_target, m_ln_mix, m_ln_xattn, m_ln_mem, m_ln_ffn, m_final_norm, m_m_in_w, m_m_conv_w, m_m_conv_b, m_m_dt_bias, m_m_a_log, m_m_d, m_m_norm_w, m_m_out_w, m_h_in_w, m_h_lower_bounds, m_h_norm_w, m_h_out_w, m_g_in_w, m_g_conv_w, m_g_a_log, m_g_dt_bias, m_g_norm_w, m_g_out_w, m_xa_q, m_xa_kv, m_xa_o, m_f_up, m_f_conv_w, m_f_conv_b, m_f_down, v_ln_mix, v_ln_xattn, v_ln_mem, v_ln_ffn, v_final_norm, v_m_in_w, v_m_conv_w, v_m_conv_b, v_m_dt_bias, v_m_a_log, v_m_d, v_m_norm_w, v_m_out_w, v_h_in_w, v_h_lower_bounds, v_h_norm_w, v_h_out_w, v_g_in_w, v_g_conv_w, v_g_a_log, v_g_dt_bias, v_g_norm_w, v_g_out_w, v_xa_q, v_xa_kv, v_xa_o, v_f_up, v_f_conv_w, v_f_conv_b, v_f_down):
    local = dict(zip(WEIGHTS, (ln_mix, ln_xattn, ln_mem, ln_ffn, final_norm, m_in_w, m_conv_w, m_conv_b, m_dt_bias, m_a_log, m_d, m_norm_w, m_out_w, h_in_w, h_lower_bounds, h_norm_w, h_out_w, g_in_w, g_conv_w, g_a_log, g_dt_bias, g_norm_w, g_out_w, xa_q, xa_kv, xa_o, f_up, f_conv_w, f_conv_b, f_down), strict=True))
    mom_m = dict(zip(WEIGHTS, (m_ln_mix, m_ln_xattn, m_ln_mem, m_ln_ffn, m_final_norm, m_m_in_w, m_m_conv_w, m_m_conv_b, m_m_dt_bias, m_m_a_log, m_m_d, m_m_norm_w, m_m_out_w, m_h_in_w, m_h_lower_bounds, m_h_norm_w, m_h_out_w, m_g_in_w, m_g_conv_w, m_g_a_log, m_g_dt_bias, m_g_norm_w, m_g_out_w, m_xa_q, m_xa_kv, m_xa_o, m_f_up, m_f_conv_w, m_f_conv_b, m_f_down), strict=True))
    mom_v = dict(zip(WEIGHTS, (v_ln_mix, v_ln_xattn, v_ln_mem, v_ln_ffn, v_final_norm, v_m_in_w, v_m_conv_w, v_m_conv_b, v_m_dt_bias, v_m_a_log, v_m_d, v_m_norm_w, v_m_out_w, v_h_in_w, v_h_lower_bounds, v_h_norm_w, v_h_out_w, v_g_in_w, v_g_conv_w, v_g_a_log, v_g_dt_bias, v_g_norm_w, v_g_out_w, v_xa_q, v_xa_kv, v_xa_o, v_f_up, v_f_conv_w, v_f_conv_b, v_f_down), strict=True))
    nb, seq, _ = x.shape
    me = 2 * lax.axis_index("x") + lax.axis_index("y")

    repl = {n: local[n] for n in REPLICATED}
    lb = lower_bounds_fwd(repl["h_lower_bounds"])
    nstage = 2 * DEPTH
    names = [layer_weight_names(s // 2, s % 2) for s in range(nstage)]
    shards = [[local[n][k].astype(bf16) if n in MATRICES else local[n][k] for n, k in names[s]] for s in range(nstage)]
    flying = {}

    def weights_of(i, part, x_in):
        s = 2 * i + part
        gathered = gather_shards("gather_s0", shards[0]) if s == 0 else gather_wait(f"gather_wait_s{s}", flying.pop(s), x_in)[1]
        w = prep_layer(i, part, {n: g for (n, _), g in zip(names[s], gathered, strict=True)}, repl, lb)
        if s + 1 < nstage:
            flying[s + 1] = gather_start(f"gather_start_s{s + 1}", shards[s + 1], me, gathered[0])
            norm = "ln_mix" if part == 0 else "ln_xattn"
            w[norm] = w[norm] + flying[s + 1][4][0, 0]
        return w

    scattering, swapping = {}, []

    def landed(s, after):
        part_names, started = scattering.pop(s)
        sent, got = scatter_wait(f"scatter_wait_s{s}", started, after)
        sums = [sum_parts(f"sum_s{s}_{n}", p, l, me) for n, p, l in zip(part_names, sent, got, strict=True)]
        swapping.append((s, part_names, swap_start(f"swap_start_s{s}", sums, sums[0])))

    def grads_done(i, part, g, dx_i):
        s = 2 * i + part
        parts = matrix_grad_parts(i, part, g)
        scattering[s] = (list(parts), scatter_start(f"scatter_start_s{s}", list(parts.values()), dx_i))
        token = scattering[s][1][4]
        if s + 1 in scattering:
            landed(s + 1, dx_i)
        return token

    loss, dx, lgrads, dfinal = local_step(x.reshape(nb * seq, D), mem.reshape(nb * N_MEM, D), loss_target.reshape(nb * seq, D),
                                          weights_of, repl["final_norm"].reshape(1, D), nb, grads_done)
    grads = small_grads(lgrads, dfinal, repl["h_lower_bounds"])

    small_names = REPLICATED + SMALL_SHARDED
    flat = jnp.concatenate([grads[n].astype(f32).reshape(-1) for n in small_names] + [loss[0, 0:1]])
    rows = -(-flat.shape[0] // (8 * SMALL_ROW)) * 8
    flat = jnp.pad(flat, (0, rows * SMALL_ROW - flat.shape[0])).reshape(rows, SMALL_ROW)
    red = allreduce_small(flat).reshape(-1)
    gsum, off = {}, 0
    for n in small_names:
        size = grads[n].size
        g = red[off:off + size].reshape(grads[n].shape)
        off += size
        if n in SHARD_AXIS:
            ax = SHARD_AXIS[n]
            w = g.shape[ax] // NCHIP
            g = lax.dynamic_slice_in_dim(g, me * w, w, axis=ax)
        gsum[n] = g
    loss_out = red[off]

    landed(0, dx)
    mine, theirs = {n: {} for n in MATRICES}, {n: {} for n in MATRICES}
    for s, part_names, started in swapping:
        sent, got = swap_wait(f"swap_wait_s{s}", started, dx)
        for n, a, b in zip(part_names, sent, got, strict=True):
            mine[n][s // 2], theirs[n][s // 2] = a, b

    outs = {}
    for n in MATRICES:
        g_mine, g_theirs = (jnp.stack([d[n][i] for i in sorted(d[n])]) for d in (mine, theirs))
        outs[n] = adamw(local[n], g_mine, mom_m[n], mom_v[n], f"adamw_{n}", g2=g_theirs)
    for n in small_names:
        outs[n] = adamw(local[n], gsum[n].reshape(local[n].shape), mom_m[n], mom_v[n], f"adamw_{n}")
    res = [loss_out, dx.reshape(nb, seq, D)]
    for k in range(4):
        res += [outs[n][k] for n in WEIGHTS]
    return tuple(res)
```

```python
import functools

import jax
import jax.numpy as jnp
from jax import lax
from jax.experimental import pallas as pl
from jax.experimental.pallas import tpu as pltpu

f32 = jnp.float32
bf16 = jnp.bfloat16
HIGHEST = lax.Precision.HIGHEST
MESH = pl.DeviceIdType.MESH

D = 1024
DEPTH = 4
EPS = 1e-6
N_MEM = 256
M_INNER, M_P, M_H, M_G, M_N, M_Q = 2048, 64, 32, 8, 128, 64
M_CONV = M_INNER + 2 * M_G * M_N
M_MAIN = M_INNER + M_CONV
M_IN = M_MAIN + M_H
H_H, H_K, H_Q = 8, 128, 32
G_HV, G_HK, G_K, G_Q = 16, 8, 128, 64
G_CONV, G_VAL = 4096, 2048
G_MAIN = G_CONV + G_VAL
G_IN = G_MAIN + 2 * G_HV
X_H, X_D = 4, 256
D_FF = 2816
ADAM_LR, ADAM_B1, ADAM_B2, ADAM_EPS, ADAM_WD, ADAM_STEP = 0.001, 0.9, 0.999, 1e-08, 0.01, 10
VMEM_LIMIT = 56 * 1024 * 1024
NCHIP = 4


def _cp(**kw):
    return pltpu.CompilerParams(vmem_limit_bytes=VMEM_LIMIT, **kw)


def _S(shape, dtype):
    return jax.ShapeDtypeStruct(tuple(shape), dtype)


def _dg(a, b, ca, cb, prec=None):
    return lax.dot_general(a, b, (((ca,), (cb,)), ((), ())), precision=prec, preferred_element_type=f32)


def _hdot(a, b, ca=1, cb=0, prec=lax.Precision.HIGH):
    return _dg(a.astype(f32), b.astype(f32), ca, cb, prec)


def _bdot_raw(a, b, ca, cb):
    return _dg(a.astype(bf16), b.astype(bf16), ca, cb)


@functools.partial(jax.custom_vjp, nondiff_argnums=(2, 3))
def _bdot(a, b, ca, cb):
    return _bdot_raw(a, b, ca, cb)


def _bdot_fwd(a, b, ca, cb):
    return _bdot_raw(a, b, ca, cb), (a, b)


def _bdot_bwd(ca, cb, res, g):
    a, b = res
    if ca == 1:
        da = _bdot_raw(g, b, 1, 1 if cb == 0 else 0)
    else:
        da = _bdot_raw(b, g, 1 if cb == 0 else 0, 1)
    if cb == 0:
        db = _bdot_raw(a, g, 0 if ca == 1 else 1, 0)
    else:
        db = _bdot_raw(g, a, 0, 0 if ca == 1 else 1)
    return da.astype(a.dtype), db.astype(b.dtype)


_bdot.defvjp(_bdot_fwd, _bdot_bwd)


def _shift_down_raw(x, k):
    r = lax.broadcasted_iota(jnp.int32, x.shape, 0)
    return jnp.where(r >= k, pltpu.roll(x, k, 0), 0.0)


def _shift_up_raw(x, k):
    n = x.shape[0]
    r = lax.broadcasted_iota(jnp.int32, x.shape, 0)
    return jnp.where(r < n - k, pltpu.roll(x, n - k, 0), 0.0)


@functools.partial(jax.custom_vjp, nondiff_argnums=(1,))
def _shift_down(x, k):
    return _shift_down_raw(x, k)


_shift_down.defvjp(lambda x, k: (_shift_down_raw(x, k), None), lambda k, _, g: (_shift_up_raw(g, k),))


def _rms(x, w):
    return x * lax.rsqrt(jnp.mean(x * x, axis=-1, keepdims=True) + EPS) * w


def _silu(x):
    return x * jax.nn.sigmoid(x)


def _masks(q):
    r = lax.broadcasted_iota(jnp.int32, (q, q), 0)
    c = lax.broadcasted_iota(jnp.int32, (q, q), 1)
    return r >= c, r > c


def _colvec(row):
    return jnp.transpose(jnp.broadcast_to(row, (8, row.shape[1])))[:, 0:1]


def _tile(n, cands):
    for c in cands:
        if n % c == 0:
            return c
    return n


def mm(a, b, *, ta=False, tb=False, bsel=None, out_stack=None, out_slots=None, into=None, res=None, out_dtype=f32, name):
    m, k = (a.shape[1], a.shape[0]) if ta else a.shape
    ca, cb = (0 if ta else 1), (1 if tb else 0)
    tm = _tile(m, (1408, 512, 256, 128) if ta else (512, 256, 128))
    if bsel is not None:
        s0, cnt = bsel
        ns = b.shape[2]
        if tb:
            n, tn, tk = b.shape[1], b.shape[1], ns
            b_spec = pl.BlockSpec((None, tn, ns), lambda i, j, kk: (s0 + kk, j, 0))
        else:
            n, tn, tk = cnt * ns, ns, k
            b_spec = pl.BlockSpec((None, tk, ns), lambda i, j, kk: (s0 + j, kk, 0))
    else:
        n = b.shape[0] if tb else b.shape[1]
        tn = out_stack if out_stack else (n if n <= 2816 else _tile(n, (2048, 1024, 512, 256, 128)))
        tk = k if (k <= 4096 and not ta) else _tile(k, (1024, 512, 256, 128))
        b_spec = pl.BlockSpec((tn, tk), lambda i, j, kk: (j, kk)) if tb else pl.BlockSpec((tk, tn), lambda i, j, kk: (kk, j))
    nk = k // tk
    if out_stack:
        total, first = out_slots if out_slots else (n // tn, 0)
        out_spec = pl.BlockSpec((None, tm, tn), lambda i, j, kk: (first + j, i, 0))
        out_shape = _S((total, m, tn), out_dtype)
    else:
        out_spec = pl.BlockSpec((tm, tn), lambda i, j, kk: (i, j))
        out_shape = _S((m, n), out_dtype)

    def body(*refs):
        a_ref, b_ref = refs[:2]
        r_ref = refs[2] if res is not None else None
        o_ref, acc = refs[-2:]
        kk = pl.program_id(2)

        @pl.when(kk == 0)
        def _():
            acc[...] = jnp.zeros_like(acc)

        acc[...] += _bdot_raw(a_ref[...], b_ref[...], ca, cb)

        @pl.when(kk == nk - 1)
        def _():
            v = acc[...]
            if r_ref is not None:
                v = v + r_ref[...]
            o_ref[...] = v.astype(o_ref.dtype)

    a_spec = pl.BlockSpec((tk, tm), lambda i, j, kk: (kk, i)) if ta else pl.BlockSpec((tm, tk), lambda i, j, kk: (i, kk))
    in_specs = [a_spec, b_spec]
    args = [a, b]
    if res is not None:
        in_specs.append(pl.BlockSpec((tm, tn), lambda i, j, kk: (i, j)))
        args.append(res)
    aliases = {}
    if into is not None:
        aliases = {len(args): 0}
        in_specs.append(pl.BlockSpec(memory_space=pl.ANY))
        args.append(into)
    return pl.pallas_call(
        body, name=name, grid=(m // tm, n // tn, nk), in_specs=in_specs, out_specs=out_spec, out_shape=out_shape,
        scratch_shapes=[pltpu.VMEM((tm, tn), f32)], input_output_aliases=aliases, compiler_params=_cp())(*args)


def rows_call(name, fn, rows, pars, row_out, acc_out=(), tm=512):
    t = rows[0].shape[0]
    tm = min(tm, t)
    assert t % tm == 0, (name, t, tm)
    nr, npar, nro = len(rows), len(pars), len(row_out)

    def body(*refs):
        rv = [r[...] for r in refs[:nr]]
        pv = [r[...] for r in refs[nr:nr + npar]]
        ro_refs = refs[nr + npar:nr + npar + nro]
        ao_refs = refs[nr + npar + nro:]
        ro, ao = fn(*rv, *pv)
        for r, v in zip(ro_refs, ro, strict=True):
            r[...] = v.astype(r.dtype)
        if ao_refs:
            @pl.when(pl.program_id(0) == 0)
            def _():
                for r in ao_refs:
                    r[...] = jnp.zeros_like(r)
            for r, v in zip(ao_refs, ao, strict=True):
                r[...] += v.astype(r.dtype)

    in_specs = [pl.BlockSpec((tm, r.shape[1]), lambda i: (i, 0)) for r in rows]
    in_specs += [pl.BlockSpec(p.shape, lambda i: (0, 0)) for p in pars]
    out_specs = [pl.BlockSpec((tm, c), lambda i: (i, 0)) for c, _ in row_out]
    out_specs += [pl.BlockSpec(s, lambda i: (0, 0)) for s, _ in acc_out]
    out_shape = [_S((t, c), dt) for c, dt in row_out] + [_S(s, dt) for s, dt in acc_out]
    return pl.pallas_call(body, name=name, grid=(t // tm,), in_specs=in_specs, out_specs=out_specs,
                          out_shape=out_shape, compiler_params=_cp())(*rows, *pars)


def rms_fwd(x, w, name):
    return rows_call(name, lambda xv, wv: ((_rms(xv, wv),), ()), [x], [w], [(x.shape[1], bf16)])[0]


def rms_bwd(x, w, dy, dres, name):
    def fn(*a):
        if dres is None:
            xv, dyv, wv = a
        else:
            xv, dyv, drv, wv = a
        _, vjp = jax.vjp(_rms, xv, wv)
        dx, dw = vjp(dyv.astype(f32))
        if dres is not None:
            dx = dx + drv
        return (dx,), (dw,)
    rows = [x, dy] + ([] if dres is None else [dres])
    return rows_call(name, fn, rows, [w], [(x.shape[1], f32)], [(w.shape, f32)])


def cols_call(name, fn, seqs, pars, outs, *, nb, ct, ncol, dseed=None):
    ns, npar = len(seqs), len(pars)
    seq_len = seqs[0].shape[0] // nb
    nd = 0 if dseed is None else len(dseed)

    def body(*refs):
        sv = [r[...] for r in refs[:ns]]
        pv = [r[...] for r in refs[ns:ns + npar]]
        if dseed is None:
            o_refs = refs[ns + npar:]
            for r, v in zip(o_refs, fn(*[v.astype(f32) for v in sv], *pv), strict=True):
                r[...] = v.astype(r.dtype)
            return
        dv = [r[...].astype(f32) for r in refs[ns + npar:ns + npar + nd]]
        ds_refs = refs[ns + npar + nd:ns + npar + nd + ns]
        dp_refs = refs[ns + npar + nd + ns:]
        _, vjp = jax.vjp(fn, *[v.astype(f32) for v in sv], *pv)
        g = vjp(tuple(dv))
        for r, v in zip(ds_refs, g[:ns], strict=True):
            r[...] = v.astype(r.dtype)

        @pl.when(pl.program_id(1) == 0)
        def _():
            for r in dp_refs:
                r[...] = jnp.zeros_like(r)
        for r, v in zip(dp_refs, g[ns:], strict=True):
            r[...] += v

    full = pl.BlockSpec((seq_len, ct), lambda j, b: (b, j))
    in_specs = [full for _ in seqs]
    in_specs += [pl.BlockSpec((p.shape[0], ct), lambda j, b: (0, j)) for p in pars]
    args = list(seqs) + list(pars)
    if dseed is None:
        out_specs = [full for _ in outs]
        out_shape = [_S((nb * seq_len, ncol * ct), dt) for dt in outs]
    else:
        in_specs += [full for _ in dseed]
        args += list(dseed)
        out_specs = [full for _ in seqs] + [pl.BlockSpec((p.shape[0], ct), lambda j, b: (0, j)) for p in pars]
        out_shape = [_S((nb * seq_len, ncol * ct), bf16) for _ in seqs] + [_S(p.shape, f32) for p in pars]
    return pl.pallas_call(body, name=name, grid=(ncol, nb), in_specs=in_specs, out_specs=out_specs,
                          out_shape=out_shape, compiler_params=_cp())(*args)


def _conv4_silu(x, w, b):
    y = x * w[3:4] + _shift_down(x, 1) * w[2:3] + _shift_down(x, 2) * w[1:2] + _shift_down(x, 3) * w[0:1] + b
    return (_silu(y),)


def _conv4_silu_nobias(x, w):
    y = x * w[3:4] + _shift_down(x, 1) * w[2:3] + _shift_down(x, 2) * w[1:2] + _shift_down(x, 3) * w[0:1]
    return (_silu(y),)


def _ffn_act(gate, up, w, b):
    y = gate * w[2:3] + _shift_down(gate, 1) * w[1:2] + _shift_down(gate, 2) * w[0:1] + b
    return (_silu(y) * up,)


def scan_call(name, chunk_fn, seqs, pars, consts, outs, *, nb, nh, q, state_shape, states=None, dseed=None):
    t = seqs[0][0].shape[0]
    nc = t // (nb * q)
    ns, npar, ncon, no = len(seqs), len(pars), len(consts), len(outs)
    s0, s1 = state_shape
    bwd = dseed is not None

    def cidx(c):
        return (nc - 1 - c) if bwd else c

    def rowblk(b, c):
        return b * nc + cidx(c)

    def seq_spec(w, colfn):
        return pl.BlockSpec((q, w), lambda b, c, h: (rowblk(b, c), colfn(h)))

    def par_spec(shape, idxfn):
        return pl.BlockSpec(shape, lambda b, c, h: idxfn(h))

    st_spec = pl.BlockSpec((s0, s1), lambda b, c, h: ((rowblk(b, c)) * nh + h, 0))
    in_specs = [seq_spec(w, cf) for _, w, cf, _ in seqs]
    in_specs += [par_spec(s, f) for _, s, f in pars] + [par_spec(s, f) for _, s, f in consts]
    args = [a for a, _, _, _ in seqs] + [a for a, _, _ in pars] + [a for a, _, _ in consts]

    if not bwd:
        def body(*refs):
            sv = [r[...] for r in refs[:ns]]
            pv = [r[...] for r in refs[ns:ns + npar]]
            cv = [r[...] for r in refs[ns + npar:ns + npar + ncon]]
            o_refs = refs[ns + npar + ncon:ns + npar + ncon + no]
            save_ref = refs[ns + npar + ncon + no]
            st = refs[-1]
            c, h = pl.program_id(1), pl.program_id(2)

            @pl.when(c == 0)
            def _():
                st[h] = jnp.zeros((s0, s1), f32)
            s_in = st[h]
            save_ref[...] = s_in
            o, s_out = chunk_fn(*sv, *pv, s_in, *cv)
            st[h] = s_out
            for r, v in zip(o_refs, o, strict=True):
                r[...] = v.astype(r.dtype)

        out_specs = [seq_spec(w, cf) for _, w, cf, _ in outs] + [st_spec]
        out_shape = [_S((t, cc), dt) for cc, _, _, dt in outs] + [_S((nb * nc * nh * s0, s1), f32)]
        return pl.pallas_call(body, name=name, grid=(nb, nc, nh), in_specs=in_specs, out_specs=out_specs,
                              out_shape=out_shape, scratch_shapes=[pltpu.VMEM((nh, s0, s1), f32)],
                              compiler_params=_cp())(*args)

    def body(*refs):
        i = 0
        sv = [r[...] for r in refs[i:i + ns]]; i += ns
        pv = [r[...] for r in refs[i:i + npar]]; i += npar
        cv = [r[...] for r in refs[i:i + ncon]]; i += ncon
        dv = [r[...].astype(f32) for r in refs[i:i + no]]; i += no
        s_in = refs[i][...]; i += 1
        ds_refs = refs[i:i + ns]; i += ns
        dp_refs = refs[i:i + npar]; i += npar
        dst = refs[-1]
        b, c, h = pl.program_id(0), pl.program_id(1), pl.program_id(2)

        @pl.when(c == 0)
        def _():
            dst[h] = jnp.zeros((s0, s1), f32)

        @pl.when((b == 0) & (c == 0) & (h == 0))
        def _():
            for r in dp_refs:
                r[...] = jnp.zeros_like(r)

        fn = lambda *a: chunk_fn(*a, *cv)
        _, vjp = jax.vjp(fn, *[v.astype(f32) for v in sv], *pv, s_in)
        g = vjp((tuple(dv), dst[h]))
        dst[h] = g[ns + npar]
        for (_, _, _, rep), r, v in zip(seqs, ds_refs, g[:ns], strict=True):
            if rep == 1:
                r[...] = v.astype(r.dtype)
            else:
                @pl.when(h % rep == 0)
                def _(r=r, v=v):
                    r[...] = v.astype(r.dtype)

                @pl.when(h % rep != 0)
                def _(r=r, v=v):
                    r[...] += v.astype(r.dtype)
        for r, v in zip(dp_refs, g[ns:ns + npar], strict=True):
            r[h] += v

    in_specs += [seq_spec(w, cf) for _, w, cf, _ in outs] + [st_spec]
    args += list(dseed) + [states]
    out_specs = [seq_spec(w, cf) for _, w, cf, _ in seqs]
    out_specs += [pl.BlockSpec((nh,) + tuple(s), lambda b, c, h: (0, 0, 0)) for _, s, _ in pars]
    out_shape = [_S(a.shape, bf16 if rep == 1 else f32) for a, _, _, rep in seqs] + [_S((nh,) + tuple(s), f32) for _, s, _ in pars]
    return pl.pallas_call(body, name=name, grid=(nb, nc, nh), in_specs=in_specs, out_specs=out_specs,
                          out_shape=out_shape, scratch_shapes=[pltpu.VMEM((nh, s0, s1), f32)],
                          compiler_params=_cp())(*args)


def _ssd_group(xs, bm, cm, z, dtr, dtb, alog, dsk, nw, st, e):
    q = xs.shape[0]
    heads = range(M_H)
    sl = [slice(i * M_P, (i + 1) * M_P) for i in heads]
    gsl = [slice(g * M_N, (g + 1) * M_N) for g in range(M_G)]
    incl, _ = _masks(q)
    dt = jax.nn.softplus(dtr + dtb[0:1])
    dte = _hdot(dt, e)
    de = _hdot(dsk, e, prec=HIGHEST)[0:1]
    xc = xs * dte
    acum = _hdot(_hdot(incl.astype(f32), dt * -jnp.exp(alog[0:1]), prec=HIGHEST), e)
    last = acum[q - 1:q]
    eac, eend, elast = jnp.exp(acum), jnp.exp(last - acum), jnp.exp(last)
    xe = xc * eend
    bms, cms = [bm[:, s] for s in gsl], [cm[:, s] for s in gsl]
    cb = [_bdot(cms[g], bms[g], 1, 1) for g in range(M_G)]
    decs = []
    for i in heads:
        a_i = acum[:, sl[i]]
        diff = jnp.where(incl, a_i[:, 0:1] - jnp.transpose(a_i)[0:1, :], 0.0)
        decs.append(jnp.where(incl, jnp.exp(diff), 0.0))
    sts = [st[sl[i], :] for i in heads]
    yd = [_bdot(cb[i // 4] * decs[i], xc[:, sl[i]], 1, 0) for i in heads]
    yo = [_bdot(cms[i // 4], sts[i], 1, 1) for i in heads]
    ds = [_bdot(xe[:, sl[i]], bms[i // 4], 0, 0) for i in heads]
    new = [sts[i] * elast[:, i * M_P:i * M_P + 1] + ds[i] for i in heads]
    y = jnp.concatenate(yd, axis=1) + jnp.concatenate(yo, axis=1) * eac + de * xs
    y = y * _silu(z)
    yn = [_rms(y[:, g * 256:(g + 1) * 256], nw[:, g * 256:(g + 1) * 256]) for g in range(M_G)]
    return (jnp.concatenate(yn, axis=1),), jnp.concatenate(new, axis=0)


def _gla_group(qr, fr, ir, gr, lb, nw, st):
    q, hp = qr.shape[0], GLA_HP
    heads = range(hp)
    sl = [slice(i * H_K, (i + 1) * H_K) for i in heads]
    incl, _ = _masks(q)
    fg = lb + (1.0 - lb) * jax.nn.sigmoid(fr)
    qq = _silu(qr) * (H_K ** -0.5)
    k = 1.0 - fg
    gc = _hdot(incl.astype(f32), jnp.log(fg))
    gl = gc[q - 1:q]
    qd, ki, ke = qq * jnp.exp(gc), k * jnp.exp(-gc), k * jnp.exp(gl - gc)
    egl = jnp.exp(gl)
    sts = [st[sl[i], :] for i in heads]
    att = [jnp.where(incl, _bdot(qd[:, sl[i]], ki[:, sl[i]], 1, 1), 0.0) for i in heads]
    o1 = [_bdot(att[i], ir[:, sl[i]], 1, 0) for i in heads]
    o2 = [_bdot(qd[:, sl[i]], sts[i], 1, 0) for i in heads]
    kv = [_bdot(ke[:, sl[i]], ir[:, sl[i]], 0, 0) for i in heads]
    new = [sts[i] * _colvec(egl[:, sl[i]]) + kv[i] for i in heads]
    on = [_rms(o1[i] + o2[i], nw) * _silu(gr[:, sl[i]]) for i in heads]
    return (jnp.concatenate(on, axis=1),), jnp.concatenate(new, axis=0)


def _tri_inv_many(ms):
    n = ms[0].shape[0]
    r = lax.broadcasted_iota(jnp.int32, (n, n), 0)
    c = lax.broadcasted_iota(jnp.int32, (n, n), 1)
    eye = (r == c).astype(f32)
    ts = [eye - m for m in ms]
    ps = list(ms)
    for _ in range(max(1, (n - 1).bit_length() - 1)):
        ps = [_hdot(p, p) for p in ps]
        ts = [t + _hdot(t, p) for t, p in zip(ts, ps)]
    return ts


def _gdn_group(qr, kr, v, z, ba, alog, dtb, nw, st):
    q, hp = qr.shape[0], G_HV
    heads = range(hp)
    sl = [slice(i * G_K, (i + 1) * G_K) for i in heads]
    incl, strict = _masks(q)
    beta_all = jax.nn.sigmoid(ba)
    gc_all = _hdot(incl.astype(f32), -jnp.exp(alog[0:1]) * jax.nn.softplus(ba + dtb[0:1]))
    gc_t = jnp.transpose(gc_all)
    gl_all = gc_all[q - 1:q]
    egc_all, eend_all, egl_all = jnp.exp(gc_all), jnp.exp(gl_all - gc_all), jnp.exp(gl_all)
    lane = lambda a, i: a[:, G_HV + i:G_HV + i + 1]
    beta = [beta_all[:, i:i + 1] for i in heads]
    egc = [lane(egc_all, i) for i in heads]
    qn, kn = [], []
    for j in range(hp // 2):
        qj, kj = qr[:, sl[j]], kr[:, sl[j]]
        qn.append(qj * lax.rsqrt(jnp.sum(qj * qj, axis=-1, keepdims=True) + EPS) * (G_K ** -0.5))
        kn.append(kj * lax.rsqrt(jnp.sum(kj * kj, axis=-1, keepdims=True) + EPS))
    qk = [_bdot(qn[j], kn[j], 1, 1) for j in range(hp // 2)]
    decs = []
    for i in heads:
        diff = jnp.where(incl, lane(gc_all, i) - gc_t[G_HV + i:G_HV + i + 1, :], 0.0)
        decs.append(jnp.where(incl, jnp.exp(diff), 0.0))
    kbs = [kn[i // 2] * beta[i] for i in heads]
    kk = [_bdot(kbs[i], kn[i // 2], 1, 1) for i in heads]
    tinv = _tri_inv_many([jnp.where(strict, kk[i] * decs[i], 0.0) for i in heads])
    uw = [_hdot(tinv[i], jnp.concatenate([v[:, sl[i]] * beta[i], kbs[i] * egc[i]], axis=1)) for i in heads]
    sts = [st[sl[i], :] for i in heads]
    ws = [_bdot(jnp.concatenate([uw[i][:, G_K:], qn[i // 2] * egc[i]], axis=0), sts[i], 1, 0) for i in heads]
    v_new = [uw[i][:, :G_K] - ws[i][:q] for i in heads]
    o = [ws[i][q:] + _bdot(qk[i // 2] * decs[i], v_new[i], 1, 0) for i in heads]
    new = [sts[i] * lane(egl_all, i) + _bdot(kn[i // 2] * lane(eend_all, i), v_new[i], 0, 0) for i in heads]
    on = [_rms(o[i], nw) * _silu(z[:, sl[i]]) for i in heads]
    return (jnp.concatenate(on, axis=1),), jnp.concatenate(new, axis=0)


def _xattn_fn(q, k, v):
    s = _bdot(q, k, 1, 1) * (X_D ** -0.5)
    return _bdot(jax.nn.softmax(s, axis=-1), v, 1, 0)


def xattn_fwd(q, k, v, nb, name, tl=512):
    t = q.shape[0]
    tl = min(tl, t // nb)
    nl = t // nb // tl

    def body(q_ref, k_ref, v_ref, o_ref):
        o_ref[...] = _xattn_fn(q_ref[...], k_ref[...], v_ref[...]).astype(o_ref.dtype)

    qs = pl.BlockSpec((tl, X_D), lambda b, i, h: (b * nl + i, h))
    ks = pl.BlockSpec((N_MEM, X_D), lambda b, i, h: (b, h))
    return pl.pallas_call(body, name=name, grid=(nb, nl, X_H), in_specs=[qs, ks, ks], out_specs=qs,
                          out_shape=_S(q.shape, bf16), compiler_params=_cp())(q, k, v)


def xattn_bwd(q, k, v, do, nb, name, tl=512):
    t = q.shape[0]
    tl = min(tl, t // nb)
    nl = t // nb // tl

    def body(q_ref, k_ref, v_ref, do_ref, dq_ref, dk_ref, dv_ref):
        _, vjp = jax.vjp(_xattn_fn, q_ref[...].astype(f32), k_ref[...].astype(f32), v_ref[...].astype(f32))
        dq, dk, dv = vjp(do_ref[...].astype(f32))
        dq_ref[...] = dq.astype(dq_ref.dtype)

        @pl.when(pl.program_id(2) == 0)
        def _():
            dk_ref[...] = jnp.zeros_like(dk_ref)
            dv_ref[...] = jnp.zeros_like(dv_ref)
        dk_ref[...] += dk
        dv_ref[...] += dv

    qs = pl.BlockSpec((tl, X_D), lambda b, h, i: (b * nl + i, h))
    ks = pl.BlockSpec((N_MEM, X_D), lambda b, h, i: (b, h))
    return pl.pallas_call(body, name=name, grid=(nb, X_H, nl), in_specs=[qs, ks, ks, qs], out_specs=[qs, ks, ks],
                          out_shape=[_S(q.shape, bf16), _S(k.shape, f32), _S(v.shape, f32)],
                          compiler_params=_cp())(q, k, v, do)


def _lower_bounds(hlb):
    sm = jax.nn.softmax(hlb, axis=0)
    rows, run = [], None
    for r in range(hlb.shape[0]):
        run = sm[r:r + 1] if run is None else run + sm[r:r + 1]
        rows.append(run - sm[0:1])
    return jnp.concatenate(rows, axis=0)


def lower_bounds_fwd(hlb):
    return rows_call("lb_fwd", lambda v: ((_lower_bounds(v),), ()), [hlb], [], [(hlb.shape[1], f32)], tm=hlb.shape[0])[0]


def lower_bounds_bwd(hlb, dlb):
    def fn(v, d):
        _, vjp = jax.vjp(_lower_bounds, v)
        return (vjp(d)[0],), ()
    return rows_call("lb_bwd", fn, [hlb, dlb], [], [(hlb.shape[1], f32)], tm=hlb.shape[0])[0]


def loss_head(x, target, w):
    def fn(xv, tv, wv):
        def loss(xx, ww):
            err = _rms(xx, ww) - tv
            return 0.5 * jnp.sum(jnp.mean(err * err, axis=-1))
        val, (dx, dw) = jax.value_and_grad(loss, argnums=(0, 1))(xv, wv)
        return (dx,), (jnp.broadcast_to(val, (1, 128)), dw)
    dx, loss, dw = rows_call("loss_head", fn, [x, target], [w], [(x.shape[1], f32)], [((1, 128), f32), (w.shape, f32)])
    return dx, loss, dw


def _adamw_fn(w, g, m, v):
    m2 = ADAM_B1 * m + (1.0 - ADAM_B1) * g
    v2 = ADAM_B2 * v + (1.0 - ADAM_B2) * (g * g)
    m_hat = m2 / (1.0 - ADAM_B1 ** ADAM_STEP)
    v_hat = v2 / (1.0 - ADAM_B2 ** ADAM_STEP)
    delta = -ADAM_LR * (m_hat / (jnp.sqrt(v_hat) + ADAM_EPS) + ADAM_WD * w)
    return delta, m2, v2


def adamw(w, g, m, v, name, g2=None):
    shape = w.shape
    c = shape[-1]
    r = w.size // c
    to2 = lambda a: a.reshape(r, c)
    tm = r if r * c * 4 <= (1 << 20) else _tile(r, (256, 128, 64, 32, 16, 8))

    def fn(*a):
        if g2 is None:
            wv, gv, mv, vv = a
        else:
            wv, gv, g2v, mv, vv = a
            gv = gv + g2v
        return (gv,) + _adamw_fn(wv, gv, mv, vv), ()
    rows = [to2(w), to2(g)] + ([] if g2 is None else [to2(g2)]) + [to2(m), to2(v)]
    outs = rows_call(name, fn, rows, [], [(c, f32)] * 4, tm=tm)
    return tuple(o.reshape(shape) for o in outs)


def _pad_row(v, lane0=0):
    return jnp.pad(v.astype(f32).reshape(1, -1), ((0, 7), (lane0, 128 - lane0 - v.shape[0])))


def _pad_cols(w, n=128):
    return jnp.pad(w, ((0, 0), (0, n - w.shape[1])))


_COL = lambda h: h
_C00 = lambda h: (0, 0)
_CONV_CT = 256


def _conv(name, x, w, b, nb, dseed=None):
    fn = _conv4_silu if b is not None else _conv4_silu_nobias
    pars = [w] + ([] if b is None else [b])
    return cols_call(name, fn, [x], pars, [f32], nb=nb, ct=_CONV_CT, ncol=x.shape[1] // _CONV_CT,
                     dseed=None if dseed is None else [dseed])


GLA_HP = 8


def _ssd_scan(name, xs, bm, cm, z, dtr, p, nb, states=None, dseed=None):
    seqs = [(xs, M_INNER, _COL, 1), (bm, M_G * M_N, _COL, 1), (cm, M_G * M_N, _COL, 1), (z, M_INNER, _COL, 1), (dtr, 128, _COL, 1)]
    pars = [(p["dtb"], (8, 128), _C00), (p["alog"], (8, 128), _C00), (p["dsk"], (8, 128), _C00), (p["nw"], (1, M_INNER), _C00)]
    r = jnp.arange(128)[:, None]
    c = jnp.arange(M_INNER)[None, :]
    consts = [((r == c // M_P).astype(f32), (128, M_INNER), _C00)]
    outs = [(M_INNER, M_INNER, _COL, bf16)]
    return scan_call(name, _ssd_group, seqs, pars, consts, outs, nb=nb, nh=1, q=M_Q, state_shape=(M_H * M_P, M_N),
                     states=states, dseed=dseed)


def _gla_scan(name, qr, fr, ir, gr, p, nb, states=None, dseed=None):
    hp, ng = GLA_HP, H_H // GLA_HP
    seqs = [(a, 128 * hp, _COL, 1) for a in (qr, fr, ir, gr)]
    pars = [(p["lb"], (1, 128 * hp), lambda h: (0, h)), (p["nw"], (1, 128), _C00)]
    outs = [(D, 128 * hp, _COL, bf16)]
    return scan_call(name, _gla_group, seqs, pars, [], outs, nb=nb, nh=ng, q=H_Q, state_shape=(hp * H_K, H_K),
                     states=states, dseed=dseed)


def _gdn_scan(name, qc, kc, vc, z, ba, p, nb, states=None, dseed=None):
    seqs = [(qc, D, _COL, 1), (kc, D, _COL, 1), (vc, G_VAL, _COL, 1), (z, G_VAL, _COL, 1), (ba, 128, _COL, 1)]
    pars = [(p["alog"], (8, 128), _C00), (p["dtb"], (8, 128), _C00), (p["nw"], (1, 128), _C00)]
    outs = [(G_VAL, G_VAL, _COL, bf16)]
    return scan_call(name, _gdn_group, seqs, pars, [], outs, nb=nb, nh=1, q=G_Q, state_shape=(G_HV * G_K, G_K),
                     states=states, dseed=dseed)


def _w(wt):
    return wt if isinstance(wt, tuple) else (wt, None)


def _proj(a, wt, name, res=None, out_dtype=f32):
    arr, bsel = _w(wt)
    return mm(a, arr, bsel=bsel, res=res, out_dtype=out_dtype, name=name)


def _proj_bwd(tag, hn, pieces):
    dhn, dws, bufs = None, [], {}
    for i, (d, wt) in enumerate(pieces):
        arr, bsel = _w(wt)
        if bsel is None:
            dws.append(mm(hn, d, ta=True, out_dtype=bf16, name=f"{tag}_dw{i}"))
        else:
            bufs[id(arr)] = mm(hn, d, ta=True, out_stack=arr.shape[2], out_slots=(arr.shape[0], bsel[0]),
                               into=bufs.get(id(arr)), out_dtype=bf16, name=f"{tag}_dw{i}")
            dws.append(None)
        dhn = mm(d, arr, tb=True, bsel=bsel, res=dhn, name=f"{tag}_dh{i}")
    dws = [dw if dw is not None else bufs[id(_w(wt)[0])] for dw, (_, wt) in zip(dws, pieces, strict=True)]
    return dhn, dws


def ssd_mixer_fwd(tag, hn, w, nb):
    z, xr, br, cr, dtr = (_proj(hn, w[k], f"{tag}_in_{k}") for k in ("wz", "wx", "wb", "wc", "wdt"))
    xs = _conv(f"{tag}_convx", xr, w["cwx"], w["cbx"], nb)[0]
    bm = _conv(f"{tag}_convb", br, w["cwb"], w["cbb"], nb)[0]
    cm = _conv(f"{tag}_convc", cr, w["cwc"], w["cbc"], nb)[0]
    yn, states = _ssd_scan(f"{tag}_scan", xs, bm, cm, z, dtr, w, nb)
    return yn, (hn, z, xr, br, cr, dtr, xs, bm, cm, yn, states)


def ssd_mixer_bwd(tag, saved, dout, w, nb):
    hn, z, xr, br, cr, dtr, xs, bm, cm, yn, states = saved
    g = {"wout": mm(yn, dout, ta=True, out_dtype=bf16, name=f"{tag}_dwout")}
    dyn = mm(dout, w["wout"], tb=True, out_dtype=bf16, name=f"{tag}_dyn")
    dxs, dbm, dcm, dz, ddtr, ddtb, dalog, ddsk, dnw = _ssd_scan(f"{tag}_scanb", xs, bm, cm, z, dtr, w, nb, states, [dyn])
    dxr, g["cwx"], g["cbx"] = _conv(f"{tag}_convxb", xr, w["cwx"], w["cbx"], nb, dxs)
    dbr, g["cwb"], g["cbb"] = _conv(f"{tag}_convbb", br, w["cwb"], w["cbb"], nb, dbm)
    dcr, g["cwc"], g["cbc"] = _conv(f"{tag}_convcb", cr, w["cwc"], w["cbc"], nb, dcm)
    dhn, (g["wz"], g["wx"], g["wb"], g["wc"], g["wdt"]) = _proj_bwd(
        tag, hn, [(dz, w["wz"]), (dxr, w["wx"]), (dbr, w["wb"]), (dcr, w["wc"]), (ddtr, w["wdt"])])
    g["dtb"], g["alog"], g["dsk"] = (jnp.sum(a, axis=0)[0, :M_H] for a in (ddtb, dalog, ddsk))
    g["nw"] = dnw.reshape(M_INNER)
    return dhn, g


def gla_mixer_fwd(tag, hn, w, nb):
    qr, fr, ir, gr = (_proj(hn, w[k], f"{tag}_in_{k}") for k in ("wq", "wf", "wi", "wg"))
    on, states = _gla_scan(f"{tag}_scan", qr, fr, ir, gr, w, nb)
    return on, (hn, qr, fr, ir, gr, on, states)


def gla_mixer_bwd(tag, saved, dout, w, nb):
    hn, qr, fr, ir, gr, on, states = saved
    g = {"wout": mm(on, dout, ta=True, out_dtype=bf16, name=f"{tag}_dwout")}
    don = mm(dout, w["wout"], tb=True, out_dtype=bf16, name=f"{tag}_don")
    dq, df, di, dg, dlb, dnw = _gla_scan(f"{tag}_scanb", qr, fr, ir, gr, w, nb, states, [don])
    dhn, (g["wq"], g["wf"], g["wi"], g["wg"]) = _proj_bwd(tag, hn, [(dq, w["wq"]), (df, w["wf"]), (di, w["wi"]), (dg, w["wg"])])
    g["lb"] = dlb.reshape(1, D)
    g["nw"] = jnp.sum(dnw, axis=0).reshape(H_K)
    return dhn, g


def gdn_mixer_fwd(tag, hn, w, nb):
    qr, kr, vr, z, ba = (_proj(hn, w[k], f"{tag}_in_{k}") for k in ("wq", "wk", "wv", "wz", "wba"))
    qc = _conv(f"{tag}_convq", qr, w["cwq"], None, nb)[0]
    kc = _conv(f"{tag}_convk", kr, w["cwk"], None, nb)[0]
    vc = _conv(f"{tag}_convv", vr, w["cwv"], None, nb)[0]
    on, states = _gdn_scan(f"{tag}_scan", qc, kc, vc, z, ba, w, nb)
    return on, (hn, qr, kr, vr, z, ba, qc, kc, vc, on, states)


def gdn_mixer_bwd(tag, saved, dout, w, nb):
    hn, qr, kr, vr, z, ba, qc, kc, vc, on, states = saved
    g = {"wout": mm(on, dout, ta=True, out_dtype=bf16, name=f"{tag}_dwout")}
    don = mm(dout, w["wout"], tb=True, out_dtype=bf16, name=f"{tag}_don")
    dqc, dkc, dvc, dz, dba, dalog, ddtb, dnw = _gdn_scan(f"{tag}_scanb", qc, kc, vc, z, ba, w, nb, states, [don])
    dqr, g["cwq"] = _conv(f"{tag}_convqb", qr, w["cwq"], None, nb, dqc)
    dkr, g["cwk"] = _conv(f"{tag}_convkb", kr, w["cwk"], None, nb, dkc)
    dvr, g["cwv"] = _conv(f"{tag}_convvb", vr, w["cwv"], None, nb, dvc)
    dhn, (g["wq"], g["wk"], g["wv"], g["wz"], g["wba"]) = _proj_bwd(
        tag, hn, [(dqr, w["wq"]), (dkr, w["wk"]), (dvr, w["wv"]), (dz, w["wz"]), (dba, w["wba"])])
    g["alog"], g["dtb"] = (jnp.sum(a, axis=0)[0, G_HV:2 * G_HV] for a in (dalog, ddtb))
    g["nw"] = jnp.sum(dnw, axis=0).reshape(G_K)
    return dhn, g


_MIXERS = {0: (ssd_mixer_fwd, ssd_mixer_bwd), 1: (gla_mixer_fwd, gla_mixer_bwd), 2: (gdn_mixer_fwd, gdn_mixer_bwd)}


def layer_fwd(i, x, mem, weights_of, nb):
    t = f"l{i}"
    wm = weights_of(i, 0, x)
    hn = rms_fwd(x, wm["ln_mix"], f"{t}_ln_mix")
    mix, s_mix = _MIXERS[i % 3][0](f"{t}_mix", hn, wm["mix"], nb)
    x1 = mm(mix, wm["mix"]["wout"], res=x, name=f"{t}_mix_out")
    w = weights_of(i, 1, x1)
    hx = rms_fwd(x1, w["ln_xattn"], f"{t}_ln_xattn")
    mn = rms_fwd(mem, w["ln_mem"], f"{t}_ln_mem")
    q = _proj(hx, w["xq"], f"{t}_xa_q", out_dtype=bf16)
    k = _proj(mn, w["xk"], f"{t}_xa_k", out_dtype=bf16)
    v = _proj(mn, w["xv"], f"{t}_xa_v", out_dtype=bf16)
    o = xattn_fwd(q, k, v, nb, f"{t}_xattn")
    x2 = mm(o, w["xo"], res=x1, name=f"{t}_xa_o")
    hf = rms_fwd(x2, w["ln_ffn"], f"{t}_ln_ffn")
    gate = _proj(hf, w["fg"], f"{t}_ffn_gate", out_dtype=bf16)
    up = _proj(hf, w["fu"], f"{t}_ffn_up", out_dtype=bf16)
    act = cols_call(f"{t}_ffn_act", _ffn_act, [gate, up], [w["fcw"], w["fcb"]], [bf16], nb=nb, ct=_CONV_CT,
                    ncol=D_FF // _CONV_CT)[0]
    x3 = mm(act, w["fd"], res=x2, name=f"{t}_ffn_down")
    return x3, (wm, w, x, s_mix, x1, hx, mn, q, k, v, o, x2, hf, gate, up, act)


def layer_bwd(i, saved, dx, mem, nb, token, grads_done):
    t = f"l{i}b"
    wm, w, x, s_mix, x1, hx, mn, q, k, v, o, x2, hf, gate, up, act = saved
    if token is not None:
        w = dict(w, fd=w["fd"] + token[0, 0].astype(w["fd"].dtype))
    g = {}
    g["fd"] = mm(act, dx, ta=True, out_dtype=bf16, name=f"{t}_dwd")
    dact = mm(dx, w["fd"], tb=True, out_dtype=bf16, name=f"{t}_dact")
    dgate, dup, g["fcw"], g["fcb"] = cols_call(f"{t}_ffn_act", _ffn_act, [gate, up], [w["fcw"], w["fcb"]], [bf16], nb=nb,
                                               ct=_CONV_CT, ncol=D_FF // _CONV_CT, dseed=[dact])
    dhf, (g["fg"], g["fu"]) = _proj_bwd(f"{t}_ffn", hf, [(dgate, w["fg"]), (dup, w["fu"])])
    dx, g["ln_ffn"] = rms_bwd(x2, w["ln_ffn"], dhf, dx, f"{t}_ln_ffn")
    g["xo"] = mm(o, dx, ta=True, out_dtype=bf16, name=f"{t}_dwo")
    do = mm(dx, w["xo"], tb=True, out_dtype=bf16, name=f"{t}_do")
    dq, dk, dv = xattn_bwd(q, k, v, do, nb, f"{t}_xattn")
    dhx, (g["xq"],) = _proj_bwd(f"{t}_xq", hx, [(dq, w["xq"])])
    dmn, (g["xk"], g["xv"]) = _proj_bwd(f"{t}_xkv", mn, [(dk, w["xk"]), (dv, w["xv"])])
    _, g["ln_mem"] = rms_bwd(mem, w["ln_mem"], dmn, None, f"{t}_ln_mem")
    dx, g["ln_xattn"] = rms_bwd(x1, w["ln_xattn"], dhx, dx, f"{t}_ln_xattn")
    token = grads_done(i, 1, g, dx) if grads_done else None
    mixw = wm["mix"] if token is None else dict(wm["mix"], wout=wm["mix"]["wout"] + token[0, 0].astype(wm["mix"]["wout"].dtype))
    dhn, g["mix"] = _MIXERS[i % 3][1](f"{t}_mix", s_mix, dx, mixw, nb)
    dx, g["ln_mix"] = rms_bwd(x, wm["ln_mix"], dhn, dx, f"{t}_ln_mix")
    token = grads_done(i, 0, g, dx) if grads_done else None
    return dx, g, token


def local_step(x, mem, target, weights_of, final_norm, nb, grads_done=None):
    saved = []
    for i in range(DEPTH):
        x, s = layer_fwd(i, x, mem, weights_of, nb)
        saved.append(s)
    dx, loss, dfinal = loss_head(x, target, final_norm)
    grads = [None] * DEPTH
    token = None
    for i in reversed(range(DEPTH)):
        dx, grads[i], token = layer_bwd(i, saved[i], dx, mem, nb, token, grads_done)
    return loss, dx, grads, dfinal


WEIGHTS = ["ln_mix", "ln_xattn", "ln_mem", "ln_ffn", "final_norm", "m_in_w", "m_conv_w", "m_conv_b", "m_dt_bias", "m_a_log",
           "m_d", "m_norm_w", "m_out_w", "h_in_w", "h_lower_bounds", "h_norm_w", "h_out_w", "g_in_w", "g_conv_w", "g_a_log",
           "g_dt_bias", "g_norm_w", "g_out_w", "xa_q", "xa_kv", "xa_o", "f_up", "f_conv_w", "f_conv_b", "f_down"]
SHARD_AXIS = {"m_in_w": 2, "m_conv_w": 2, "m_conv_b": 1, "m_norm_w": 1, "m_out_w": 1, "h_in_w": 2, "h_out_w": 1, "g_in_w": 2,
              "g_conv_w": 2, "g_out_w": 1, "xa_q": 1, "xa_kv": 2, "xa_o": 1, "f_up": 2, "f_conv_w": 2, "f_down": 1}
MATRICES = ["m_in_w", "m_out_w", "h_in_w", "h_out_w", "g_in_w", "g_out_w", "xa_q", "xa_kv", "xa_o", "f_up", "f_down"]
SMALL_SHARDED = [n for n in WEIGHTS if n in SHARD_AXIS and n not in MATRICES]
REPLICATED = [n for n in WEIGHTS if n not in SHARD_AXIS]
_MIXER_PREFIX = {0: "m", 1: "h", 2: "g"}


def layer_weight_names(i, part):
    if part == 0:
        p = _MIXER_PREFIX[i % 3]
        return [(n, i // 3) for n in WEIGHTS if n in SHARD_AXIS and n.startswith(p + "_")]
    return [(n, i) for n in ("xa_q", "xa_kv", "xa_o", "f_up", "f_conv_w", "f_down")]


def _cols(st, lo, hi):
    ns = st.shape[-1]
    parts = []
    for j in range(NCHIP):
        a, b = max(lo, j * ns), min(hi, (j + 1) * ns)
        if a < b:
            parts.append(st[j][..., a - j * ns:b - j * ns])
    return parts[0] if len(parts) == 1 else jnp.concatenate(parts, axis=-1)


def _col_shards(pieces, ns):
    full = jnp.concatenate(pieces, axis=-1)
    return [full[..., j * ns:(j + 1) * ns] for j in range(NCHIP)]


def _rows(st):
    return st.reshape(st.shape[0] * st.shape[1], st.shape[2])


def prep_layer(i, part, G, R, lb):
    row = lambda a: a.reshape(1, -1)
    p, k = _MIXER_PREFIX[i % 3], i // 3
    if part == 1:
        kv, fup = G["xa_kv"], G["f_up"]
        return dict(ln_xattn=R["ln_xattn"][i:i + 1], ln_mem=R["ln_mem"][i:i + 1], ln_ffn=R["ln_ffn"][i:i + 1],
                    xq=_rows(G["xa_q"]), xk=(kv, (0, 2)), xv=(kv, (2, 2)), xo=_rows(G["xa_o"]), fg=(fup, (0, 2)), fu=(fup, (2, 2)),
                    fcw=_cols(G["f_conv_w"], 0, D_FF), fcb=R["f_conv_b"][i:i + 1], fd=_rows(G["f_down"]))
    layer = dict(ln_mix=R["ln_mix"][i:i + 1])
    inw, wout = G[p + "_in_w"], _rows(G[p + "_out_w"])
    if p == "m":
        cw, cb = G["m_conv_w"], G["m_conv_b"]
        a, b, c = M_INNER, M_INNER + M_G * M_N, M_CONV
        layer["mix"] = dict(
            wz=_cols(inw, 0, M_INNER), wx=_cols(inw, M_INNER, M_INNER + a), wb=_cols(inw, M_INNER + a, M_INNER + b),
            wc=_cols(inw, M_INNER + b, M_MAIN), wdt=_pad_cols(_cols(inw, M_MAIN, M_IN)),
            cwx=_cols(cw, 0, a), cwb=_cols(cw, a, b), cwc=_cols(cw, b, c),
            cbx=row(_cols(cb, 0, a)), cbb=row(_cols(cb, a, b)), cbc=row(_cols(cb, b, c)),
            dtb=_pad_row(R["m_dt_bias"][k]), alog=_pad_row(R["m_a_log"][k]), dsk=_pad_row(R["m_d"][k]),
            nw=row(_cols(G["m_norm_w"], 0, M_INNER)), wout=wout)
    elif p == "h":
        layer["mix"] = dict(wq=(inw, (0, 1)), wf=(inw, (1, 1)), wi=(inw, (2, 1)), wg=(inw, (3, 1)),
                            lb=lb[i:i + 1], nw=row(R["h_norm_w"][k]), wout=wout)
    else:
        cw = G["g_conv_w"]
        layer["mix"] = dict(
            wq=_cols(inw, 0, D), wk=_cols(inw, D, 2 * D), wv=_cols(inw, 2 * D, G_CONV), wz=_cols(inw, G_CONV, G_MAIN),
            wba=_pad_cols(_cols(inw, G_MAIN, G_IN)), cwq=_cols(cw, 0, D), cwk=_cols(cw, D, 2 * D), cwv=_cols(cw, 2 * D, G_CONV),
            alog=_pad_row(R["g_a_log"][k], G_HV), dtb=_pad_row(R["g_dt_bias"][k], G_HV),
            nw=row(R["g_norm_w"][k]), wout=wout)
    return layer


def matrix_grad_parts(i, part, g):
    by_rows = lambda a: a.reshape(NCHIP, a.shape[0] // NCHIP, a.shape[1])
    if part == 1:
        return {"xa_q": by_rows(g["xq"]), "xa_kv": g["xk"], "xa_o": by_rows(g["xo"]), "f_up": g["fg"], "f_down": by_rows(g["fd"])}
    p = _MIXER_PREFIX[i % 3]
    m = g["mix"]
    out = {p + "_out_w": by_rows(m["wout"])}
    if p == "m":
        out["m_in_w"] = jnp.stack(_col_shards([m["wz"], m["wx"], m["wb"], m["wc"], m["wdt"]], M_IN // NCHIP))
    elif p == "h":
        out["h_in_w"] = m["wq"]
    else:
        out["g_in_w"] = jnp.stack(_col_shards([m["wq"], m["wk"], m["wv"], m["wz"], m["wba"]], G_IN // NCHIP))
    return out


def small_grads(grads, dfinal, hlb):
    cat = lambda xs: jnp.concatenate(xs, axis=1)
    out = {k: jnp.concatenate([g[k] for g in grads], axis=0) for k in ("ln_mix", "ln_xattn", "ln_mem", "ln_ffn")}
    out["final_norm"] = dfinal.reshape(D)
    out["f_conv_w"] = jnp.stack([g["fcw"] for g in grads])
    out["f_conv_b"] = jnp.concatenate([g["fcb"] for g in grads], axis=0)
    ms = [g["mix"] for i, g in enumerate(grads) if i % 3 == 0]
    out["m_conv_w"] = jnp.stack([cat([m["cwx"], m["cwb"], m["cwc"]]) for m in ms])
    out["m_conv_b"] = jnp.concatenate([cat([m["cbx"], m["cbb"], m["cbc"]]) for m in ms], axis=0)
    out["m_dt_bias"] = jnp.stack([m["dtb"] for m in ms])
    out["m_a_log"] = jnp.stack([m["alog"] for m in ms])
    out["m_d"] = jnp.stack([m["dsk"] for m in ms])
    out["m_norm_w"] = jnp.stack([m["nw"] for m in ms])
    hs = [(i, g["mix"]) for i, g in enumerate(grads) if i % 3 == 1]
    lb_rows = dict(hs)
    dlb = jnp.concatenate([lb_rows[i]["lb"] if i in lb_rows else jnp.zeros((1, D), f32) for i in range(DEPTH)], axis=0)
    out["h_lower_bounds"] = lower_bounds_bwd(hlb, dlb)
    out["h_norm_w"] = jnp.stack([m["nw"] for _, m in hs])
    gs = [g["mix"] for i, g in enumerate(grads) if i % 3 == 2]
    out["g_conv_w"] = jnp.stack([cat([m["cwq"], m["cwk"], m["cwv"]]) for m in gs])
    out["g_a_log"] = jnp.stack([m["alog"] for m in gs])
    out["g_dt_bias"] = jnp.stack([m["dtb"] for m in gs])
    out["g_norm_w"] = jnp.stack([m["nw"] for m in gs])
    return out


_HBM = pl.BlockSpec(memory_space=pltpu.HBM)


def _place():
    x, y, c = lax.axis_index("x"), lax.axis_index("y"), lax.axis_index("c")
    chips = [(1 - x, y), (x, 1 - y), (1 - x, 1 - y)]
    return x, y, c, chips


def place_own(name, tensors):
    n = len(tensors)

    def body(*refs):
        ins, outs = refs[:n], refs[n:2 * n]
        sems = refs[2 * n]
        me = 2 * lax.axis_index("x") + lax.axis_index("y")
        copies = [pltpu.make_async_copy(ins[t], outs[t].at[me], sems.at[t]) for t in range(n)]
        for cp in copies:
            cp.start()
        for cp in copies:
            cp.wait()

    return pl.pallas_call(
        body, name=name, in_specs=[_HBM] * n, out_specs=[_HBM] * n,
        out_shape=[_S((NCHIP,) + a.shape, a.dtype) for a in tensors],
        scratch_shapes=[pltpu.SemaphoreType.DMA((n,))])(*tensors)


_SEM = pl.BlockSpec(memory_space=pltpu.SEMAPHORE)
_ANY = pl.BlockSpec(memory_space=pl.ANY)
_SPLIT = pltpu.CompilerParams(has_side_effects=pltpu.SideEffectType.DATAFLOW_SIDE_EFFECTING)


def _hbm(a):
    return pltpu.with_memory_space_constraint(a, pltpu.HBM)


def _split_start(name, srcs, lands, dep, copies):
    n = len(srcs)

    def body(*refs):
        src_refs, land_refs = refs[:n], refs[n:2 * n]
        send_sems, recv_sems = refs[2 * n + 1], refs[2 * n + 2]
        token = refs[-1]
        for cp in copies(src_refs, land_refs, send_sems, recv_sems):
            cp.start()
        token[...] = jnp.zeros_like(token)

    thru = [pltpu.HBM(a.shape, a.dtype) for a in list(srcs) + list(lands)]
    out = pl.pallas_call(
        body, name=name, in_specs=[_HBM] * (2 * n) + [_ANY],
        out_specs=[_SEM, _SEM] + [_HBM] * (2 * n) + [pl.BlockSpec(memory_space=pltpu.VMEM)],
        out_shape=[pltpu.SemaphoreType.DMA((3 * n,)), pltpu.SemaphoreType.DMA((3 * n,))] + thru + [_S((8, 128), f32)],
        input_output_aliases={t: 2 + t for t in range(2 * n)}, compiler_params=_SPLIT,
    )(*[_hbm(a) for a in srcs], *[_hbm(a) for a in lands], dep)
    return out[0], out[1], out[2:2 + n], out[2 + n:2 + 2 * n], out[-1]


def _split_wait(name, started, after, copies):
    send_sems, recv_sems, srcs, lands, _ = started
    n = len(srcs)

    def body(*refs):
        src_refs, land_refs = refs[:n], refs[n:2 * n]
        s_sems, r_sems = refs[2 * n], refs[2 * n + 1]
        for cp in copies(src_refs, land_refs, s_sems, r_sems):
            cp.wait_send()
            cp.wait_recv()

    out = pl.pallas_call(
        body, name=name, in_specs=[_HBM] * (2 * n) + [_SEM, _SEM, _ANY], out_specs=[_HBM] * (2 * n),
        out_shape=[pltpu.HBM(a.shape, a.dtype) for a in list(srcs) + list(lands)],
        input_output_aliases={t: t for t in range(2 * n)}, compiler_params=_SPLIT,
    )(*srcs, *lands, send_sems, recv_sems, after)
    return out[:n], out[n:]


def _gather_copies(arrive):
    def copies(src_refs, land_refs, send_sems, recv_sems):
        x, y, c, chips = _place()
        out = []
        for t, (s, l) in enumerate(zip(src_refs, land_refs, strict=True)):
            for j, (px, py) in enumerate(chips):
                slot = 2 * px + py if arrive else 2 * x + y
                out.append(pltpu.make_async_remote_copy(src_ref=s, dst_ref=l.at[slot], send_sem=send_sems.at[3 * t + j],
                                                        recv_sem=recv_sems.at[3 * t + j], device_id=(px, py, c), device_id_type=MESH))
        return out
    return copies


def gather_start(name, tensors, dep):
    return _split_start(name, tensors, place_own(name + "_own", tensors), dep, _gather_copies(False))


def gather_wait(name, started, after):
    return _split_wait(name, started, after, _gather_copies(True))


def _scatter_copies(src_refs, land_refs, send_sems, recv_sems):
    x, y, c, chips = _place()
    out = []
    for t, (s, l) in enumerate(zip(src_refs, land_refs, strict=True)):
        for j, (px, py) in enumerate(chips):
            out.append(pltpu.make_async_remote_copy(src_ref=s.at[2 * px + py], dst_ref=l.at[j], send_sem=send_sems.at[3 * t + j],
                                                    recv_sem=recv_sems.at[3 * t + j], device_id=(px, py, c), device_id_type=MESH))
    return out


def scatter_start(name, parts, dep):
    lands = [lax.empty((3,) + a.shape[1:], a.dtype) for a in parts]
    return _split_start(name, parts, lands, dep, _scatter_copies)


def scatter_wait(name, started, after):
    return _split_wait(name, started, after, _scatter_copies)


def sum_parts(name, part, land, me):
    shape = land.shape[1:]
    c = shape[-1]
    r = land.size // (3 * c)
    tm = _tile(r, (256, 128, 64, 32, 16, 8))

    def body(me_ref, p_ref, l_ref, o_ref):
        o_ref[...] = p_ref[...].astype(f32) + l_ref[0].astype(f32) + l_ref[1].astype(f32) + l_ref[2].astype(f32)

    grid_spec = pltpu.PrefetchScalarGridSpec(
        num_scalar_prefetch=1, grid=(r // tm,),
        in_specs=[pl.BlockSpec((None, tm, c), lambda i, me_ref: (me_ref[0], i, 0)),
                  pl.BlockSpec((3, tm, c), lambda i, me_ref: (0, i, 0))],
        out_specs=pl.BlockSpec((tm, c), lambda i, me_ref: (i, 0)))
    out = pl.pallas_call(body, name=name, grid_spec=grid_spec, out_shape=_S((r, c), f32), compiler_params=_cp())(
        me.reshape(1).astype(jnp.int32), part.reshape(NCHIP, r, c), land.reshape(3, r, c))
    return out.reshape(shape)


def _swap_copies(src_refs, land_refs, send_sems, recv_sems):
    x, y, c, _ = _place()
    return [pltpu.make_async_remote_copy(src_ref=s, dst_ref=l, send_sem=send_sems.at[3 * t], recv_sem=recv_sems.at[3 * t],
                                         device_id=(x, y, 1 - c), device_id_type=MESH)
            for t, (s, l) in enumerate(zip(src_refs, land_refs, strict=True))]


def swap_start(name, tensors, dep):
    return _split_start(name, tensors, [lax.empty(a.shape, a.dtype) for a in tensors], dep, _swap_copies)


def swap_wait(name, started, after):
    return _split_wait(name, started, after, _swap_copies)


def allreduce_small(v):
    r, n = v.shape

    def body(x_ref, out_ref, gat, send_sems, recv_sems, local_sem):
        x, y, c, chips = _place()
        me, sibling = (x, y, c), (x, y, 1 - c)

        def rows(px, py, pc):
            return gat.at[pl.ds((4 * px + 2 * py + pc) * r, r), :]

        def copy(k, block, to, src=None):
            return pltpu.make_async_remote_copy(src_ref=rows(*block) if src is None else src, dst_ref=rows(*block),
                                                send_sem=send_sems.at[k], recv_sem=recv_sems.at[k], device_id=to,
                                                device_id_type=MESH)

        mine = pltpu.make_async_copy(x_ref, rows(*me), local_sem)
        mine.start()
        first = [copy(0, me, sibling, src=x_ref)] + [copy(1 + j, me, (*chip, c), src=x_ref) for j, chip in enumerate(chips)]
        for cp in first:
            cp.start()
        passed = [copy(4 + j, (*chip, c), sibling) for j, chip in enumerate(chips)]
        for j, chip in enumerate(chips):
            copy(1 + j, (*chip, c), me).wait_recv()
            passed[j].start()
        copy(0, sibling, me).wait_recv()
        for j, chip in enumerate(chips):
            copy(4 + j, (*chip, 1 - c), me).wait_recv()
        for cp in first + passed:
            cp.wait_send()
        mine.wait()
        acc = gat[0:r, :]
        for d in range(1, 8):
            acc = acc + gat[d * r:(d + 1) * r, :]
        out_ref[...] = acc

    vm = pl.BlockSpec(memory_space=pltpu.VMEM)
    return pl.pallas_call(
        body, name="allreduce_small", in_specs=[vm], out_specs=vm, out_shape=_S((r, n), v.dtype),
        scratch_shapes=[pltpu.VMEM((8 * r, n), v.dtype), pltpu.SemaphoreType.DMA((7,)), pltpu.SemaphoreType.DMA((7,)),
                        pltpu.SemaphoreType.DMA],
        compiler_params=_cp())(v)


SMALL_ROW = 1024


def kernel(x, mem, ln_mix, ln_xattn, ln_mem, ln_ffn, final_norm, m_in_w, m_conv_w, m_conv_b, m_dt_bias, m_a_log, m_d, m_norm_w, m_out_w, h_in_w, h_lower_bounds, h_norm_w, h_out_w, g_in_w, g_conv_w, g_a_log, g_dt_bias, g_norm_w, g_out_w, xa_q, xa_kv, xa_o, f_up, f_conv_w, f_conv_b, f_down, loss_target, m_ln_mix, m_ln_xattn, m_ln_mem, m_ln_ffn, m_final_norm, m_m_in_w, m_m_conv_w, m_m_conv_b, m_m_dt_bias, m_m_a_log, m_m_d, m_m_norm_w, m_m_out_w, m_h_in_w, m_h_lower_bounds, m_h_norm_w, m_h_out_w, m_g_in_w, m_g_conv_w, m_g_a_log, m_g_dt_bias, m_g_norm_w, m_g_out_w, m_xa_q, m_xa_kv, m_xa_o, m_f_up, m_f_conv_w, m_f_conv_b, m_f_down, v_ln_mix, v_ln_xattn, v_ln_mem, v_ln_ffn, v_final_norm, v_m_in_w, v_m_conv_w, v_m_conv_b, v_m_dt_bias, v_m_a_log, v_m_d, v_m_norm_w, v_m_out_w, v_h_in_w, v_h_lower_bounds, v_h_norm_w, v_h_out_w, v_g_in_w, v_g_conv_w, v_g_a_log, v_g_dt_bias, v_g_norm_w, v_g_out_w, v_xa_q, v_xa_kv, v_xa_o, v_f_up, v_f_conv_w, v_f_conv_b, v_f_down):
    local = dict(zip(WEIGHTS, (ln_mix, ln_xattn, ln_mem, ln_ffn, final_norm, m_in_w, m_conv_w, m_conv_b, m_dt_bias, m_a_log, m_d, m_norm_w, m_out_w, h_in_w, h_lower_bounds, h_norm_w, h_out_w, g_in_w, g_conv_w, g_a_log, g_dt_bias, g_norm_w, g_out_w, xa_q, xa_kv, xa_o, f_up, f_conv_w, f_conv_b, f_down), strict=True))
    mom_m = dict(zip(WEIGHTS, (m_ln_mix, m_ln_xattn, m_ln_mem, m_ln_ffn, m_final_norm, m_m_in_w, m_m_conv_w, m_m_conv_b, m_m_dt_bias, m_m_a_log, m_m_d, m_m_norm_w, m_m_out_w, m_h_in_w, m_h_lower_bounds, m_h_norm_w, m_h_out_w, m_g_in_w, m_g_conv_w, m_g_a_log, m_g_dt_bias, m_g_norm_w, m_g_out_w, m_xa_q, m_xa_kv, m_xa_o, m_f_up, m_f_conv_w, m_f_conv_b, m_f_down), strict=True))
    mom_v = dict(zip(WEIGHTS, (v_ln_mix, v_ln_xattn, v_ln_mem, v_ln_ffn, v_final_norm, v_m_in_w, v_m_conv_w, v_m_conv_b, v_m_dt_bias, v_m_a_log, v_m_d, v_m_norm_w, v_m_out_w, v_h_in_w, v_h_lower_bounds, v_h_norm_w, v_h_out_w, v_g_in_w, v_g_conv_w, v_g_a_log, v_g_dt_bias, v_g_norm_w, v_g_out_w, v_xa_q, v_xa_kv, v_xa_o, v_f_up, v_f_conv_w, v_f_conv_b, v_f_down), strict=True))
    nb, seq, _ = x.shape
    me = 2 * lax.axis_index("x") + lax.axis_index("y")

    repl = {n: local[n] for n in REPLICATED}
    lb = lower_bounds_fwd(repl["h_lower_bounds"])
    nstage = 2 * DEPTH
    names = [layer_weight_names(s // 2, s % 2) for s in range(nstage)]
    cast = lambda n, a: a.astype(bf16) if n in MATRICES else a
    flying = {0: gather_start("gather_start_s0", [cast(n, local[n][k]) for n, k in names[0]], x)}
    tok0 = flying[0][4][0, 0]
    shards = [None] + [[cast(n, local[n][k] + tok0) for n, k in names[s]] for s in range(1, nstage)]

    def weights_of(i, part, x_in):
        s = 2 * i + part
        gathered = gather_wait(f"gather_wait_s{s}", flying.pop(s), shards[-1][-1] if s == 0 else x_in)[1]
        w = prep_layer(i, part, {n: g for (n, _), g in zip(names[s], gathered, strict=True)}, repl, lb)
        if s + 1 < nstage:
            flying[s + 1] = gather_start(f"gather_start_s{s + 1}", shards[s + 1], gathered[0])
            norm = "ln_mix" if part == 0 else "ln_xattn"
            w[norm] = w[norm] + flying[s + 1][4][0, 0]
        return w

    scattering, swapping = {}, []

    def landed(s, after):
        part_names, started = scattering.pop(s)
        sent, got = scatter_wait(f"scatter_wait_s{s}", started, after)
        sums = [sum_parts(f"sum_s{s}_{n}", p, l, me) for n, p, l in zip(part_names, sent, got, strict=True)]
        swapping.append((s, part_names, swap_start(f"swap_start_s{s}", sums, sums[0])))

    def grads_done(i, part, g, dx_i):
        s = 2 * i + part
        parts = matrix_grad_parts(i, part, g)
        scattering[s] = (list(parts), scatter_start(f"scatter_start_s{s}", list(parts.values()), dx_i))
        token = scattering[s][1][4]
        if s + 1 in scattering:
            landed(s + 1, dx_i)
        return token

    loss, dx, lgrads, dfinal = local_step(x.reshape(nb * seq, D), mem.reshape(nb * N_MEM, D), loss_target.reshape(nb * seq, D),
                                          weights_of, repl["final_norm"].reshape(1, D), nb, grads_done)
    grads = small_grads(lgrads, dfinal, repl["h_lower_bounds"])

    small_names = REPLICATED + SMALL_SHARDED
    flat = jnp.concatenate([grads[n].astype(f32).reshape(-1) for n in small_names] + [loss[0, 0:1]])
    rows = -(-flat.shape[0] // (8 * SMALL_ROW)) * 8
    flat = jnp.pad(flat, (0, rows * SMALL_ROW - flat.shape[0])).reshape(rows, SMALL_ROW)
    red = allreduce_small(flat).reshape(-1)
    gsum, off = {}, 0
    for n in small_names:
        size = grads[n].size
        g = red[off:off + size].reshape(grads[n].shape)
        off += size
        if n in SHARD_AXIS:
            ax = SHARD_AXIS[n]
            w = g.shape[ax] // NCHIP
            g = lax.dynamic_slice_in_dim(g, me * w, w, axis=ax)
        gsum[n] = g
    loss_out = red[off]

    outs = {}
    for n in small_names:
        outs[n] = adamw(local[n], gsum[n].reshape(local[n].shape), mom_m[n], mom_v[n], f"adamw_{n}")
    mine, theirs = {n: {} for n in MATRICES}, {n: {} for n in MATRICES}

    def swapped(after):
        while swapping:
            s, part_names, started = swapping.pop()
            sent, got = swap_wait(f"swap_wait_s{s}", started, after)
            for n, a, b in zip(part_names, sent, got, strict=True):
                mine[n][s // 2], theirs[n][s // 2] = a, b

    def update(n):
        g_mine, g_theirs = (jnp.stack([d[n][i] for i in sorted(d[n])]) for d in (mine, theirs))
        outs[n] = adamw(local[n], g_mine, mom_m[n], mom_v[n], f"adamw_{n}", g2=g_theirs)

    last = [n for n, _ in names[0] if n in MATRICES]
    swapped(dx)
    for n in MATRICES:
        if n not in last:
            update(n)
    landed(0, outs["f_down"][1])
    swapped(outs["f_down"][1])
    for n in last:
        update(n)
    res = [loss_out, dx.reshape(nb, seq, D)]
    for k in range(4):
        res += [outs[n][k] for n in WEIGHTS]
    return tuple(res)
```

```python
import functools

import jax
import jax.numpy as jnp
from jax import lax
from jax.experimental import pallas as pl
from jax.experimental.pallas import tpu as pltpu

f32 = jnp.float32
bf16 = jnp.bfloat16
HIGHEST = lax.Precision.HIGHEST
MESH = pl.DeviceIdType.MESH

D = 1024
DEPTH = 4
EPS = 1e-6
N_MEM = 256
M_INNER, M_P, M_H, M_G, M_N, M_Q = 2048, 64, 32, 8, 128, 64
M_CONV = M_INNER + 2 * M_G * M_N
M_MAIN = M_INNER + M_CONV
M_IN = M_MAIN + M_H
H_H, H_K, H_Q = 8, 128, 32
G_HV, G_HK, G_K, G_Q = 16, 8, 128, 64
G_CONV, G_VAL = 4096, 2048
G_MAIN = G_CONV + G_VAL
G_IN = G_MAIN + 2 * G_HV
X_H, X_D = 4, 256
D_FF = 2816
ADAM_LR, ADAM_B1, ADAM_B2, ADAM_EPS, ADAM_WD, ADAM_STEP = 0.001, 0.9, 0.999, 1e-08, 0.01, 10
VMEM_LIMIT = 56 * 1024 * 1024
NCHIP = 4


def _cp(**kw):
    return pltpu.CompilerParams(vmem_limit_bytes=VMEM_LIMIT, **kw)


def _S(shape, dtype):
    return jax.ShapeDtypeStruct(tuple(shape), dtype)


def _dg(a, b, ca, cb, prec=None):
    return lax.dot_general(a, b, (((ca,), (cb,)), ((), ())), precision=prec, preferred_element_type=f32)


def _hdot(a, b, ca=1, cb=0, prec=lax.Precision.HIGH):
    return _dg(a.astype(f32), b.astype(f32), ca, cb, prec)


def _bdot_raw(a, b, ca, cb):
    return _dg(a.astype(bf16), b.astype(bf16), ca, cb)


@functools.partial(jax.custom_vjp, nondiff_argnums=(2, 3))
def _bdot(a, b, ca, cb):
    return _bdot_raw(a, b, ca, cb)


def _bdot_fwd(a, b, ca, cb):
    return _bdot_raw(a, b, ca, cb), (a, b)


def _bdot_bwd(ca, cb, res, g):
    a, b = res
    if ca == 1:
        da = _bdot_raw(g, b, 1, 1 if cb == 0 else 0)
    else:
        da = _bdot_raw(b, g, 1 if cb == 0 else 0, 1)
    if cb == 0:
        db = _bdot_raw(a, g, 0 if ca == 1 else 1, 0)
    else:
        db = _bdot_raw(g, a, 0, 0 if ca == 1 else 1)
    return da.astype(a.dtype), db.astype(b.dtype)


_bdot.defvjp(_bdot_fwd, _bdot_bwd)


def _shift_down_raw(x, k):
    r = lax.broadcasted_iota(jnp.int32, x.shape, 0)
    return jnp.where(r >= k, pltpu.roll(x, k, 0), 0.0)


def _shift_up_raw(x, k):
    n = x.shape[0]
    r = lax.broadcasted_iota(jnp.int32, x.shape, 0)
    return jnp.where(r < n - k, pltpu.roll(x, n - k, 0), 0.0)


@functools.partial(jax.custom_vjp, nondiff_argnums=(1,))
def _shift_down(x, k):
    return _shift_down_raw(x, k)


_shift_down.defvjp(lambda x, k: (_shift_down_raw(x, k), None), lambda k, _, g: (_shift_up_raw(g, k),))


def _rms(x, w):
    return x * lax.rsqrt(jnp.mean(x * x, axis=-1, keepdims=True) + EPS) * w


def _silu(x):
    return x * jax.nn.sigmoid(x)


def _masks(q):
    r = lax.broadcasted_iota(jnp.int32, (q, q), 0)
    c = lax.broadcasted_iota(jnp.int32, (q, q), 1)
    return r >= c, r > c


def _colvec(row):
    return jnp.transpose(jnp.broadcast_to(row, (8, row.shape[1])))[:, 0:1]


def _tile(n, cands):
    for c in cands:
        if n % c == 0:
            return c
    return n


def mm(a, b, *, ta=False, tb=False, bsel=None, out_stack=None, out_slots=None, into=None, res=None, out_dtype=f32, name):
    m, k = (a.shape[1], a.shape[0]) if ta else a.shape
    ca, cb = (0 if ta else 1), (1 if tb else 0)
    tm = _tile(m, (1408, 512, 256, 128) if ta else (512, 256, 128))
    if bsel is not None:
        s0, cnt = bsel
        ns = b.shape[2]
        if tb:
            n, tn, tk = b.shape[1], b.shape[1], ns
            b_spec = pl.BlockSpec((None, tn, ns), lambda i, j, kk: (s0 + kk, j, 0))
        else:
            n, tn, tk = cnt * ns, ns, k
            b_spec = pl.BlockSpec((None, tk, ns), lambda i, j, kk: (s0 + j, kk, 0))
    else:
        n = b.shape[0] if tb else b.shape[1]
        tn = out_stack if out_stack else (n if n <= 2816 else _tile(n, (2048, 1024, 512, 256, 128)))
        tk = k if (k <= 4096 and not ta) else _tile(k, (1024, 512, 256, 128))
        b_spec = pl.BlockSpec((tn, tk), lambda i, j, kk: (j, kk)) if tb else pl.BlockSpec((tk, tn), lambda i, j, kk: (kk, j))
    nk = k // tk
    if out_stack:
        total, first = out_slots if out_slots else (n // tn, 0)
        out_spec = pl.BlockSpec((None, tm, tn), lambda i, j, kk: (first + j, i, 0))
        out_shape = _S((total, m, tn), out_dtype)
    else:
        out_spec = pl.BlockSpec((tm, tn), lambda i, j, kk: (i, j))
        out_shape = _S((m, n), out_dtype)

    def body(*refs):
        a_ref, b_ref = refs[:2]
        r_ref = refs[2] if res is not None else None
        o_ref, acc = refs[-2:]
        kk = pl.program_id(2)

        @pl.when(kk == 0)
        def _():
            acc[...] = jnp.zeros_like(acc)

        acc[...] += _bdot_raw(a_ref[...], b_ref[...], ca, cb)

        @pl.when(kk == nk - 1)
        def _():
            v = acc[...]
            if r_ref is not None:
                v = v + r_ref[...]
            o_ref[...] = v.astype(o_ref.dtype)

    a_spec = pl.BlockSpec((tk, tm), lambda i, j, kk: (kk, i)) if ta else pl.BlockSpec((tm, tk), lambda i, j, kk: (i, kk))
    in_specs = [a_spec, b_spec]
    args = [a, b]
    if res is not None:
        in_specs.append(pl.BlockSpec((tm, tn), lambda i, j, kk: (i, j)))
        args.append(res)
    aliases = {}
    if into is not None:
        aliases = {len(args): 0}
        in_specs.append(pl.BlockSpec(memory_space=pl.ANY))
        args.append(into)
    return pl.pallas_call(
        body, name=name, grid=(m // tm, n // tn, nk), in_specs=in_specs, out_specs=out_spec, out_shape=out_shape,
        scratch_shapes=[pltpu.VMEM((tm, tn), f32)], input_output_aliases=aliases, compiler_params=_cp())(*args)


def rows_call(name, fn, rows, pars, row_out, acc_out=(), tm=512):
    t = rows[0].shape[0]
    tm = min(tm, t)
    assert t % tm == 0, (name, t, tm)
    nr, npar, nro = len(rows), len(pars), len(row_out)

    def body(*refs):
        rv = [r[...] for r in refs[:nr]]
        pv = [r[...] for r in refs[nr:nr + npar]]
        ro_refs = refs[nr + npar:nr + npar + nro]
        ao_refs = refs[nr + npar + nro:]
        ro, ao = fn(*rv, *pv)
        for r, v in zip(ro_refs, ro, strict=True):
            r[...] = v.astype(r.dtype)
        if ao_refs:
            @pl.when(pl.program_id(0) == 0)
            def _():
                for r in ao_refs:
                    r[...] = jnp.zeros_like(r)
            for r, v in zip(ao_refs, ao, strict=True):
                r[...] += v.astype(r.dtype)

    in_specs = [pl.BlockSpec((tm, r.shape[1]), lambda i: (i, 0)) for r in rows]
    in_specs += [pl.BlockSpec(p.shape, lambda i: (0, 0)) for p in pars]
    out_specs = [pl.BlockSpec((tm, c), lambda i: (i, 0)) for c, _ in row_out]
    out_specs += [pl.BlockSpec(s, lambda i: (0, 0)) for s, _ in acc_out]
    out_shape = [_S((t, c), dt) for c, dt in row_out] + [_S(s, dt) for s, dt in acc_out]
    return pl.pallas_call(body, name=name, grid=(t // tm,), in_specs=in_specs, out_specs=out_specs,
                          out_shape=out_shape, compiler_params=_cp())(*rows, *pars)


def rms_fwd(x, w, name):
    return rows_call(name, lambda xv, wv: ((_rms(xv, wv),), ()), [x], [w], [(x.shape[1], bf16)])[0]


def rms_bwd(x, w, dy, dres, name):
    def fn(*a):
        if dres is None:
            xv, dyv, wv = a
        else:
            xv, dyv, drv, wv = a
        _, vjp = jax.vjp(_rms, xv, wv)
        dx, dw = vjp(dyv.astype(f32))
        if dres is not None:
            dx = dx + drv
        return (dx,), (dw,)
    rows = [x, dy] + ([] if dres is None else [dres])
    return rows_call(name, fn, rows, [w], [(x.shape[1], f32)], [(w.shape, f32)])


def cols_call(name, fn, seqs, pars, outs, *, nb, ct, ncol, dseed=None):
    ns, npar = len(seqs), len(pars)
    seq_len = seqs[0].shape[0] // nb
    nd = 0 if dseed is None else len(dseed)

    def body(*refs):
        sv = [r[...] for r in refs[:ns]]
        pv = [r[...] for r in refs[ns:ns + npar]]
        if dseed is None:
            o_refs = refs[ns + npar:]
            for r, v in zip(o_refs, fn(*[v.astype(f32) for v in sv], *pv), strict=True):
                r[...] = v.astype(r.dtype)
            return
        dv = [r[...].astype(f32) for r in refs[ns + npar:ns + npar + nd]]
        ds_refs = refs[ns + npar + nd:ns + npar + nd + ns]
        dp_refs = refs[ns + npar + nd + ns:]
        _, vjp = jax.vjp(fn, *[v.astype(f32) for v in sv], *pv)
        g = vjp(tuple(dv))
        for r, v in zip(ds_refs, g[:ns], strict=True):
            r[...] = v.astype(r.dtype)

        @pl.when(pl.program_id(1) == 0)
        def _():
            for r in dp_refs:
                r[...] = jnp.zeros_like(r)
        for r, v in zip(dp_refs, g[ns:], strict=True):
            r[...] += v

    full = pl.BlockSpec((seq_len, ct), lambda j, b: (b, j))
    in_specs = [full for _ in seqs]
    in_specs += [pl.BlockSpec((p.shape[0], ct), lambda j, b: (0, j)) for p in pars]
    args = list(seqs) + list(pars)
    if dseed is None:
        out_specs = [full for _ in outs]
        out_shape = [_S((nb * seq_len, ncol * ct), dt) for dt in outs]
    else:
        in_specs += [full for _ in dseed]
        args += list(dseed)
        out_specs = [full for _ in seqs] + [pl.BlockSpec((p.shape[0], ct), lambda j, b: (0, j)) for p in pars]
        out_shape = [_S((nb * seq_len, ncol * ct), bf16) for _ in seqs] + [_S(p.shape, f32) for p in pars]
    return pl.pallas_call(body, name=name, grid=(ncol, nb), in_specs=in_specs, out_specs=out_specs,
                          out_shape=out_shape, compiler_params=_cp())(*args)


def _conv4_silu(x, w, b):
    y = x * w[3:4] + _shift_down(x, 1) * w[2:3] + _shift_down(x, 2) * w[1:2] + _shift_down(x, 3) * w[0:1] + b
    return (_silu(y),)


def _conv4_silu_nobias(x, w):
    y = x * w[3:4] + _shift_down(x, 1) * w[2:3] + _shift_down(x, 2) * w[1:2] + _shift_down(x, 3) * w[0:1]
    return (_silu(y),)


def _ffn_act(gate, up, w, b):
    y = gate * w[2:3] + _shift_down(gate, 1) * w[1:2] + _shift_down(gate, 2) * w[0:1] + b
    return (_silu(y) * up,)


def scan_call(name, chunk_fn, seqs, pars, consts, outs, *, nb, nh, q, state_shape, states=None, dseed=None):
    t = seqs[0][0].shape[0]
    nc = t // (nb * q)
    ns, npar, ncon, no = len(seqs), len(pars), len(consts), len(outs)
    s0, s1 = state_shape
    bwd = dseed is not None

    def cidx(c):
        return (nc - 1 - c) if bwd else c

    def rowblk(b, c):
        return b * nc + cidx(c)

    def seq_spec(w, colfn):
        return pl.BlockSpec((q, w), lambda b, c, h: (rowblk(b, c), colfn(h)))

    def par_spec(shape, idxfn):
        return pl.BlockSpec(shape, lambda b, c, h: idxfn(h))

    st_spec = pl.BlockSpec((s0, s1), lambda b, c, h: ((rowblk(b, c)) * nh + h, 0))
    in_specs = [seq_spec(w, cf) for _, w, cf, _ in seqs]
    in_specs += [par_spec(s, f) for _, s, f in pars] + [par_spec(s, f) for _, s, f in consts]
    args = [a for a, _, _, _ in seqs] + [a for a, _, _ in pars] + [a for a, _, _ in consts]

    if not bwd:
        def body(*refs):
            sv = [r[...] for r in refs[:ns]]
            pv = [r[...] for r in refs[ns:ns + npar]]
            cv = [r[...] for r in refs[ns + npar:ns + npar + ncon]]
            o_refs = refs[ns + npar + ncon:ns + npar + ncon + no]
            save_ref = refs[ns + npar + ncon + no]
            st = refs[-1]
            c, h = pl.program_id(1), pl.program_id(2)

            @pl.when(c == 0)
            def _():
                st[h] = jnp.zeros((s0, s1), f32)
            s_in = st[h]
            save_ref[...] = s_in
            o, s_out = chunk_fn(*sv, *pv, s_in, *cv)
            st[h] = s_out
            for r, v in zip(o_refs, o, strict=True):
                r[...] = v.astype(r.dtype)

        out_specs = [seq_spec(w, cf) for _, w, cf, _ in outs] + [st_spec]
        out_shape = [_S((t, cc), dt) for cc, _, _, dt in outs] + [_S((nb * nc * nh * s0, s1), f32)]
        return pl.pallas_call(body, name=name, grid=(nb, nc, nh), in_specs=in_specs, out_specs=out_specs,
                              out_shape=out_shape, scratch_shapes=[pltpu.VMEM((nh, s0, s1), f32)],
                              compiler_params=_cp())(*args)

    def body(*refs):
        i = 0
        sv = [r[...] for r in refs[i:i + ns]]; i += ns
        pv = [r[...] for r in refs[i:i + npar]]; i += npar
        cv = [r[...] for r in refs[i:i + ncon]]; i += ncon
        dv = [r[...].astype(f32) for r in refs[i:i + no]]; i += no
        s_in = refs[i][...]; i += 1
        ds_refs = refs[i:i + ns]; i += ns
        dp_refs = refs[i:i + npar]; i += npar
        dst = refs[-1]
        b, c, h = pl.program_id(0), pl.program_id(1), pl.program_id(2)

        @pl.when(c == 0)
        def _():
            dst[h] = jnp.zeros((s0, s1), f32)

        @pl.when((b == 0) & (c == 0) & (h == 0))
        def _():
            for r in dp_refs:
                r[...] = jnp.zeros_like(r)

        fn = lambda *a: chunk_fn(*a, *cv)
        _, vjp = jax.vjp(fn, *[v.astype(f32) for v in sv], *pv, s_in)
        g = vjp((tuple(dv), dst[h]))
        dst[h] = g[ns + npar]
        for (_, _, _, rep), r, v in zip(seqs, ds_refs, g[:ns], strict=True):
            if rep == 1:
                r[...] = v.astype(r.dtype)
            else:
                @pl.when(h % rep == 0)
                def _(r=r, v=v):
                    r[...] = v.astype(r.dtype)

                @pl.when(h % rep != 0)
                def _(r=r, v=v):
                    r[...] += v.astype(r.dtype)
        for r, v in zip(dp_refs, g[ns:ns + npar], strict=True):
            r[h] += v

    in_specs += [seq_spec(w, cf) for _, w, cf, _ in outs] + [st_spec]
    args += list(dseed) + [states]
    out_specs = [seq_spec(w, cf) for _, w, cf, _ in seqs]
    out_specs += [pl.BlockSpec((nh,) + tuple(s), lambda b, c, h: (0, 0, 0)) for _, s, _ in pars]
    out_shape = [_S(a.shape, bf16 if rep == 1 else f32) for a, _, _, rep in seqs] + [_S((nh,) + tuple(s), f32) for _, s, _ in pars]
    return pl.pallas_call(body, name=name, grid=(nb, nc, nh), in_specs=in_specs, out_specs=out_specs,
                          out_shape=out_shape, scratch_shapes=[pltpu.VMEM((nh, s0, s1), f32)],
                          compiler_params=_cp())(*args)


def _ssd_group(xs, bm, cm, z, dtr, dtb, alog, dsk, nw, st, e):
    q = xs.shape[0]
    heads = range(M_H)
    sl = [slice(i * M_P, (i + 1) * M_P) for i in heads]
    gsl = [slice(g * M_N, (g + 1) * M_N) for g in range(M_G)]
    incl, _ = _masks(q)
    dt = jax.nn.softplus(dtr + dtb[0:1])
    dte = _hdot(dt, e)
    de = _hdot(dsk, e, prec=HIGHEST)[0:1]
    xc = xs * dte
    acum = _hdot(_hdot(incl.astype(f32), dt * -jnp.exp(alog[0:1]), prec=HIGHEST), e)
    last = acum[q - 1:q]
    eac, eend, elast = jnp.exp(acum), jnp.exp(last - acum), jnp.exp(last)
    xe = xc * eend
    bms, cms = [bm[:, s] for s in gsl], [cm[:, s] for s in gsl]
    cb = [_bdot(cms[g], bms[g], 1, 1) for g in range(M_G)]
    decs = []
    for i in heads:
        a_i = acum[:, sl[i]]
        diff = jnp.where(incl, a_i[:, 0:1] - jnp.transpose(a_i)[0:1, :], 0.0)
        decs.append(jnp.where(incl, jnp.exp(diff), 0.0))
    sts = [st[sl[i], :] for i in heads]
    yd = [_bdot(cb[i // 4] * decs[i], xc[:, sl[i]], 1, 0) for i in heads]
    yo = [_bdot(cms[i // 4], sts[i], 1, 1) for i in heads]
    ds = [_bdot(xe[:, sl[i]], bms[i // 4], 0, 0) for i in heads]
    new = [sts[i] * elast[:, i * M_P:i * M_P + 1] + ds[i] for i in heads]
    y = jnp.concatenate(yd, axis=1) + jnp.concatenate(yo, axis=1) * eac + de * xs
    y = y * _silu(z)
    yn = [_rms(y[:, g * 256:(g + 1) * 256], nw[:, g * 256:(g + 1) * 256]) for g in range(M_G)]
    return (jnp.concatenate(yn, axis=1),), jnp.concatenate(new, axis=0)


def _gla_group(qr, fr, ir, gr, lb, nw, st):
    q, hp = qr.shape[0], GLA_HP
    heads = range(hp)
    sl = [slice(i * H_K, (i + 1) * H_K) for i in heads]
    incl, _ = _masks(q)
    fg = lb + (1.0 - lb) * jax.nn.sigmoid(fr)
    qq = _silu(qr) * (H_K ** -0.5)
    k = 1.0 - fg
    gc = _hdot(incl.astype(f32), jnp.log(fg))
    gl = gc[q - 1:q]
    qd, ki, ke = qq * jnp.exp(gc), k * jnp.exp(-gc), k * jnp.exp(gl - gc)
    egl = jnp.exp(gl)
    sts = [st[sl[i], :] for i in heads]
    att = [jnp.where(incl, _bdot(qd[:, sl[i]], ki[:, sl[i]], 1, 1), 0.0) for i in heads]
    o1 = [_bdot(att[i], ir[:, sl[i]], 1, 0) for i in heads]
    o2 = [_bdot(qd[:, sl[i]], sts[i], 1, 0) for i in heads]
    kv = [_bdot(ke[:, sl[i]], ir[:, sl[i]], 0, 0) for i in heads]
    new = [sts[i] * _colvec(egl[:, sl[i]]) + kv[i] for i in heads]
    on = [_rms(o1[i] + o2[i], nw) * _silu(gr[:, sl[i]]) for i in heads]
    return (jnp.concatenate(on, axis=1),), jnp.concatenate(new, axis=0)


def _tri_inv_many(ms):
    n = ms[0].shape[0]
    r = lax.broadcasted_iota(jnp.int32, (n, n), 0)
    c = lax.broadcasted_iota(jnp.int32, (n, n), 1)
    eye = (r == c).astype(f32)
    ts = [eye - m for m in ms]
    ps = list(ms)
    for _ in range(max(1, (n - 1).bit_length() - 1)):
        ps = [_hdot(p, p) for p in ps]
        ts = [t + _hdot(t, p) for t, p in zip(ts, ps)]
    return ts


def _gdn_group(qr, kr, v, z, ba, alog, dtb, nw, st):
    q, hp = qr.shape[0], G_HV
    heads = range(hp)
    sl = [slice(i * G_K, (i + 1) * G_K) for i in heads]
    incl, strict = _masks(q)
    beta_all = jax.nn.sigmoid(ba)
    gc_all = _hdot(incl.astype(f32), -jnp.exp(alog[0:1]) * jax.nn.softplus(ba + dtb[0:1]))
    gc_t = jnp.transpose(gc_all)
    gl_all = gc_all[q - 1:q]
    egc_all, eend_all, egl_all = jnp.exp(gc_all), jnp.exp(gl_all - gc_all), jnp.exp(gl_all)
    lane = lambda a, i: a[:, G_HV + i:G_HV + i + 1]
    beta = [beta_all[:, i:i + 1] for i in heads]
    egc = [lane(egc_all, i) for i in heads]
    qn, kn = [], []
    for j in range(hp // 2):
        qj, kj = qr[:, sl[j]], kr[:, sl[j]]
        qn.append(qj * lax.rsqrt(jnp.sum(qj * qj, axis=-1, keepdims=True) + EPS) * (G_K ** -0.5))
        kn.append(kj * lax.rsqrt(jnp.sum(kj * kj, axis=-1, keepdims=True) + EPS))
    qk = [_bdot(qn[j], kn[j], 1, 1) for j in range(hp // 2)]
    decs = []
    for i in heads:
        diff = jnp.where(incl, lane(gc_all, i) - gc_t[G_HV + i:G_HV + i + 1, :], 0.0)
        decs.append(jnp.where(incl, jnp.exp(diff), 0.0))
    kbs = [kn[i // 2] * beta[i] for i in heads]
    kk = [_bdot(kbs[i], kn[i // 2], 1, 1) for i in heads]
    tinv = _tri_inv_many([jnp.where(strict, kk[i] * decs[i], 0.0) for i in heads])
    uw = [_hdot(tinv[i], jnp.concatenate([v[:, sl[i]] * beta[i], kbs[i] * egc[i]], axis=1)) for i in heads]
    sts = [st[sl[i], :] for i in heads]
    ws = [_bdot(jnp.concatenate([uw[i][:, G_K:], qn[i // 2] * egc[i]], axis=0), sts[i], 1, 0) for i in heads]
    v_new = [uw[i][:, :G_K] - ws[i][:q] for i in heads]
    o = [ws[i][q:] + _bdot(qk[i // 2] * decs[i], v_new[i], 1, 0) for i in heads]
    new = [sts[i] * lane(egl_all, i) + _bdot(kn[i // 2] * lane(eend_all, i), v_new[i], 0, 0) for i in heads]
    on = [_rms(o[i], nw) * _silu(z[:, sl[i]]) for i in heads]
    return (jnp.concatenate(on, axis=1),), jnp.concatenate(new, axis=0)


def _xattn_fn(q, k, v):
    s = _bdot(q, k, 1, 1) * (X_D ** -0.5)
    return _bdot(jax.nn.softmax(s, axis=-1), v, 1, 0)


def xattn_fwd(q, k, v, nb, name, tl=512):
    t = q.shape[0]
    tl = min(tl, t // nb)
    nl = t // nb // tl

    def body(q_ref, k_ref, v_ref, o_ref):
        o_ref[...] = _xattn_fn(q_ref[...], k_ref[...], v_ref[...]).astype(o_ref.dtype)

    qs = pl.BlockSpec((tl, X_D), lambda b, i, h: (b * nl + i, h))
    ks = pl.BlockSpec((N_MEM, X_D), lambda b, i, h: (b, h))
    return pl.pallas_call(body, name=name, grid=(nb, nl, X_H), in_specs=[qs, ks, ks], out_specs=qs,
                          out_shape=_S(q.shape, bf16), compiler_params=_cp())(q, k, v)


def xattn_bwd(q, k, v, do, nb, name, tl=512):
    t = q.shape[0]
    tl = min(tl, t // nb)
    nl = t // nb // tl

    def body(q_ref, k_ref, v_ref, do_ref, dq_ref, dk_ref, dv_ref):
        _, vjp = jax.vjp(_xattn_fn, q_ref[...].astype(f32), k_ref[...].astype(f32), v_ref[...].astype(f32))
        dq, dk, dv = vjp(do_ref[...].astype(f32))
        dq_ref[...] = dq.astype(dq_ref.dtype)

        @pl.when(pl.program_id(2) == 0)
        def _():
            dk_ref[...] = jnp.zeros_like(dk_ref)
            dv_ref[...] = jnp.zeros_like(dv_ref)
        dk_ref[...] += dk
        dv_ref[...] += dv

    qs = pl.BlockSpec((tl, X_D), lambda b, h, i: (b * nl + i, h))
    ks = pl.BlockSpec((N_MEM, X_D), lambda b, h, i: (b, h))
    return pl.pallas_call(body, name=name, grid=(nb, X_H, nl), in_specs=[qs, ks, ks, qs], out_specs=[qs, ks, ks],
                          out_shape=[_S(q.shape, bf16), _S(k.shape, f32), _S(v.shape, f32)],
                          compiler_params=_cp())(q, k, v, do)


def _lower_bounds(hlb):
    sm = jax.nn.softmax(hlb, axis=0)
    rows, run = [], None
    for r in range(hlb.shape[0]):
        run = sm[r:r + 1] if run is None else run + sm[r:r + 1]
        rows.append(run - sm[0:1])
    return jnp.concatenate(rows, axis=0)


def lower_bounds_fwd(hlb):
    return rows_call("lb_fwd", lambda v: ((_lower_bounds(v),), ()), [hlb], [], [(hlb.shape[1], f32)], tm=hlb.shape[0])[0]


def lower_bounds_bwd(hlb, dlb):
    def fn(v, d):
        _, vjp = jax.vjp(_lower_bounds, v)
        return (vjp(d)[0],), ()
    return rows_call("lb_bwd", fn, [hlb, dlb], [], [(hlb.shape[1], f32)], tm=hlb.shape[0])[0]


def loss_head(x, target, w):
    def fn(xv, tv, wv):
        def loss(xx, ww):
            err = _rms(xx, ww) - tv
            return 0.5 * jnp.sum(jnp.mean(err * err, axis=-1))
        val, (dx, dw) = jax.value_and_grad(loss, argnums=(0, 1))(xv, wv)
        return (dx,), (jnp.broadcast_to(val, (1, 128)), dw)
    dx, loss, dw = rows_call("loss_head", fn, [x, target], [w], [(x.shape[1], f32)], [((1, 128), f32), (w.shape, f32)])
    return dx, loss, dw


def _adamw_fn(w, g, m, v):
    m2 = ADAM_B1 * m + (1.0 - ADAM_B1) * g
    v2 = ADAM_B2 * v + (1.0 - ADAM_B2) * (g * g)
    m_hat = m2 / (1.0 - ADAM_B1 ** ADAM_STEP)
    v_hat = v2 / (1.0 - ADAM_B2 ** ADAM_STEP)
    delta = -ADAM_LR * (m_hat / (jnp.sqrt(v_hat) + ADAM_EPS) + ADAM_WD * w)
    return delta, m2, v2


def adamw(w, g, m, v, name, g2=None):
    shape = w.shape
    c = shape[-1]
    r = w.size // c
    to2 = lambda a: a.reshape(r, c)
    tm = r if r * c * 4 <= (1 << 20) else _tile(r, (256, 128, 64, 32, 16, 8))

    def fn(*a):
        if g2 is None:
            wv, gv, mv, vv = a
        else:
            wv, gv, g2v, mv, vv = a
            gv = gv + g2v
        return (gv,) + _adamw_fn(wv, gv, mv, vv), ()
    rows = [to2(w), to2(g)] + ([] if g2 is None else [to2(g2)]) + [to2(m), to2(v)]
    outs = rows_call(name, fn, rows, [], [(c, f32)] * 4, tm=tm)
    return tuple(o.reshape(shape) for o in outs)


def _pad_row(v, lane0=0):
    return jnp.pad(v.astype(f32).reshape(1, -1), ((0, 7), (lane0, 128 - lane0 - v.shape[0])))


def _pad_cols(w, n=128):
    return jnp.pad(w, ((0, 0), (0, n - w.shape[1])))


_COL = lambda h: h
_C00 = lambda h: (0, 0)
_CONV_CT = 256


def _conv(name, x, w, b, nb, dseed=None):
    fn = _conv4_silu if b is not None else _conv4_silu_nobias
    pars = [w] + ([] if b is None else [b])
    return cols_call(name, fn, [x], pars, [f32], nb=nb, ct=_CONV_CT, ncol=x.shape[1] // _CONV_CT,
                     dseed=None if dseed is None else [dseed])


GLA_HP = 8


def _ssd_scan(name, xs, bm, cm, z, dtr, p, nb, states=None, dseed=None):
    seqs = [(xs, M_INNER, _COL, 1), (bm, M_G * M_N, _COL, 1), (cm, M_G * M_N, _COL, 1), (z, M_INNER, _COL, 1), (dtr, 128, _COL, 1)]
    pars = [(p["dtb"], (8, 128), _C00), (p["alog"], (8, 128), _C00), (p["dsk"], (8, 128), _C00), (p["nw"], (1, M_INNER), _C00)]
    r = jnp.arange(128)[:, None]
    c = jnp.arange(M_INNER)[None, :]
    consts = [((r == c // M_P).astype(f32), (128, M_INNER), _C00)]
    outs = [(M_INNER, M_INNER, _COL, bf16)]
    return scan_call(name, _ssd_group, seqs, pars, consts, outs, nb=nb, nh=1, q=M_Q, state_shape=(M_H * M_P, M_N),
                     states=states, dseed=dseed)


def _gla_scan(name, qr, fr, ir, gr, p, nb, states=None, dseed=None):
    hp, ng = GLA_HP, H_H // GLA_HP
    seqs = [(a, 128 * hp, _COL, 1) for a in (qr, fr, ir, gr)]
    pars = [(p["lb"], (1, 128 * hp), lambda h: (0, h)), (p["nw"], (1, 128), _C00)]
    outs = [(D, 128 * hp, _COL, bf16)]
    return scan_call(name, _gla_group, seqs, pars, [], outs, nb=nb, nh=ng, q=H_Q, state_shape=(hp * H_K, H_K),
                     states=states, dseed=dseed)


def _gdn_scan(name, qc, kc, vc, z, ba, p, nb, states=None, dseed=None):
    seqs = [(qc, D, _COL, 1), (kc, D, _COL, 1), (vc, G_VAL, _COL, 1), (z, G_VAL, _COL, 1), (ba, 128, _COL, 1)]
    pars = [(p["alog"], (8, 128), _C00), (p["dtb"], (8, 128), _C00), (p["nw"], (1, 128), _C00)]
    outs = [(G_VAL, G_VAL, _COL, bf16)]
    return scan_call(name, _gdn_group, seqs, pars, [], outs, nb=nb, nh=1, q=G_Q, state_shape=(G_HV * G_K, G_K),
                     states=states, dseed=dseed)


def _w(wt):
    return wt if isinstance(wt, tuple) else (wt, None)


def _proj(a, wt, name, res=None, out_dtype=f32):
    arr, bsel = _w(wt)
    return mm(a, arr, bsel=bsel, res=res, out_dtype=out_dtype, name=name)


def _proj_bwd(tag, hn, pieces):
    dhn, dws, bufs = None, [], {}
    for i, (d, wt) in enumerate(pieces):
        arr, bsel = _w(wt)
        if bsel is None:
            dws.append(mm(hn, d, ta=True, out_dtype=bf16, name=f"{tag}_dw{i}"))
        else:
            bufs[id(arr)] = mm(hn, d, ta=True, out_stack=arr.shape[2], out_slots=(arr.shape[0], bsel[0]),
                               into=bufs.get(id(arr)), out_dtype=bf16, name=f"{tag}_dw{i}")
            dws.append(None)
        dhn = mm(d, arr, tb=True, bsel=bsel, res=dhn, name=f"{tag}_dh{i}")
    dws = [dw if dw is not None else bufs[id(_w(wt)[0])] for dw, (_, wt) in zip(dws, pieces, strict=True)]
    return dhn, dws


def ssd_mixer_fwd(tag, hn, w, nb):
    z, xr, br, cr, dtr = (_proj(hn, w[k], f"{tag}_in_{k}") for k in ("wz", "wx", "wb", "wc", "wdt"))
    xs = _conv(f"{tag}_convx", xr, w["cwx"], w["cbx"], nb)[0]
    bm = _conv(f"{tag}_convb", br, w["cwb"], w["cbb"], nb)[0]
    cm = _conv(f"{tag}_convc", cr, w["cwc"], w["cbc"], nb)[0]
    yn, states = _ssd_scan(f"{tag}_scan", xs, bm, cm, z, dtr, w, nb)
    return yn, (hn, z, xr, br, cr, dtr, xs, bm, cm, yn, states)


def ssd_mixer_bwd(tag, saved, dout, w, nb):
    hn, z, xr, br, cr, dtr, xs, bm, cm, yn, states = saved
    g = {"wout": mm(yn, dout, ta=True, out_dtype=bf16, name=f"{tag}_dwout")}
    dyn = mm(dout, w["wout"], tb=True, out_dtype=bf16, name=f"{tag}_dyn")
    dxs, dbm, dcm, dz, ddtr, ddtb, dalog, ddsk, dnw = _ssd_scan(f"{tag}_scanb", xs, bm, cm, z, dtr, w, nb, states, [dyn])
    dxr, g["cwx"], g["cbx"] = _conv(f"{tag}_convxb", xr, w["cwx"], w["cbx"], nb, dxs)
    dbr, g["cwb"], g["cbb"] = _conv(f"{tag}_convbb", br, w["cwb"], w["cbb"], nb, dbm)
    dcr, g["cwc"], g["cbc"] = _conv(f"{tag}_convcb", cr, w["cwc"], w["cbc"], nb, dcm)
    dhn, (g["wz"], g["wx"], g["wb"], g["wc"], g["wdt"]) = _proj_bwd(
        tag, hn, [(dz, w["wz"]), (dxr, w["wx"]), (dbr, w["wb"]), (dcr, w["wc"]), (ddtr, w["wdt"])])
    g["dtb"], g["alog"], g["dsk"] = (jnp.sum(a, axis=0)[0, :M_H] for a in (ddtb, dalog, ddsk))
    g["nw"] = dnw.reshape(M_INNER)
    return dhn, g


def gla_mixer_fwd(tag, hn, w, nb):
    qr, fr, ir, gr = (_proj(hn, w[k], f"{tag}_in_{k}") for k in ("wq", "wf", "wi", "wg"))
    on, states = _gla_scan(f"{tag}_scan", qr, fr, ir, gr, w, nb)
    return on, (hn, qr, fr, ir, gr, on, states)


def gla_mixer_bwd(tag, saved, dout, w, nb):
    hn, qr, fr, ir, gr, on, states = saved
    g = {"wout": mm(on, dout, ta=True, out_dtype=bf16, name=f"{tag}_dwout")}
    don = mm(dout, w["wout"], tb=True, out_dtype=bf16, name=f"{tag}_don")
    dq, df, di, dg, dlb, dnw = _gla_scan(f"{tag}_scanb", qr, fr, ir, gr, w, nb, states, [don])
    dhn, (g["wq"], g["wf"], g["wi"], g["wg"]) = _proj_bwd(tag, hn, [(dq, w["wq"]), (df, w["wf"]), (di, w["wi"]), (dg, w["wg"])])
    g["lb"] = dlb.reshape(1, D)
    g["nw"] = jnp.sum(dnw, axis=0).reshape(H_K)
    return dhn, g


def gdn_mixer_fwd(tag, hn, w, nb):
    qr, kr, vr, z, ba = (_proj(hn, w[k], f"{tag}_in_{k}") for k in ("wq", "wk", "wv", "wz", "wba"))
    qc = _conv(f"{tag}_convq", qr, w["cwq"], None, nb)[0]
    kc = _conv(f"{tag}_convk", kr, w["cwk"], None, nb)[0]
    vc = _conv(f"{tag}_convv", vr, w["cwv"], None, nb)[0]
    on, states = _gdn_scan(f"{tag}_scan", qc, kc, vc, z, ba, w, nb)
    return on, (hn, qr, kr, vr, z, ba, qc, kc, vc, on, states)


def gdn_mixer_bwd(tag, saved, dout, w, nb):
    hn, qr, kr, vr, z, ba, qc, kc, vc, on, states = saved
    g = {"wout": mm(on, dout, ta=True, out_dtype=bf16, name=f"{tag}_dwout")}
    don = mm(dout, w["wout"], tb=True, out_dtype=bf16, name=f"{tag}_don")
    dqc, dkc, dvc, dz, dba, dalog, ddtb, dnw = _gdn_scan(f"{tag}_scanb", qc, kc, vc, z, ba, w, nb, states, [don])
    dqr, g["cwq"] = _conv(f"{tag}_convqb", qr, w["cwq"], None, nb, dqc)
    dkr, g["cwk"] = _conv(f"{tag}_convkb", kr, w["cwk"], None, nb, dkc)
    dvr, g["cwv"] = _conv(f"{tag}_convvb", vr, w["cwv"], None, nb, dvc)
    dhn, (g["wq"], g["wk"], g["wv"], g["wz"], g["wba"]) = _proj_bwd(
        tag, hn, [(dqr, w["wq"]), (dkr, w["wk"]), (dvr, w["wv"]), (dz, w["wz"]), (dba, w["wba"])])
    g["alog"], g["dtb"] = (jnp.sum(a, axis=0)[0, G_HV:2 * G_HV] for a in (dalog, ddtb))
    g["nw"] = jnp.sum(dnw, axis=0).reshape(G_K)
    return dhn, g


_MIXERS = {0: (ssd_mixer_fwd, ssd_mixer_bwd), 1: (gla_mixer_fwd, gla_mixer_bwd), 2: (gdn_mixer_fwd, gdn_mixer_bwd)}


def layer_fwd(i, x, mem, weights_of, nb):
    t = f"l{i}"
    wm = weights_of(i, 0, x)
    hn = rms_fwd(x, wm["ln_mix"], f"{t}_ln_mix")
    mix, s_mix = _MIXERS[i % 3][0](f"{t}_mix", hn, wm["mix"], nb)
    x1 = mm(mix, wm["mix"]["wout"], res=x, name=f"{t}_mix_out")
    w = weights_of(i, 1, x1)
    hx = rms_fwd(x1, w["ln_xattn"], f"{t}_ln_xattn")
    mn = rms_fwd(mem, w["ln_mem"], f"{t}_ln_mem")
    q = _proj(hx, w["xq"], f"{t}_xa_q", out_dtype=bf16)
    k = _proj(mn, w["xk"], f"{t}_xa_k", out_dtype=bf16)
    v = _proj(mn, w["xv"], f"{t}_xa_v", out_dtype=bf16)
    o = xattn_fwd(q, k, v, nb, f"{t}_xattn")
    x2 = mm(o, w["xo"], res=x1, name=f"{t}_xa_o")
    hf = rms_fwd(x2, w["ln_ffn"], f"{t}_ln_ffn")
    gate = _proj(hf, w["fg"], f"{t}_ffn_gate", out_dtype=bf16)
    up = _proj(hf, w["fu"], f"{t}_ffn_up", out_dtype=bf16)
    act = cols_call(f"{t}_ffn_act", _ffn_act, [gate, up], [w["fcw"], w["fcb"]], [bf16], nb=nb, ct=_CONV_CT,
                    ncol=D_FF // _CONV_CT)[0]
    x3 = mm(act, w["fd"], res=x2, name=f"{t}_ffn_down")
    return x3, (wm, w, x, s_mix, x1, hx, mn, q, k, v, o, x2, hf, gate, up, act)


def layer_bwd(i, saved, dx, mem, nb, token, grads_done):
    t = f"l{i}b"
    wm, w, x, s_mix, x1, hx, mn, q, k, v, o, x2, hf, gate, up, act = saved
    if token is not None:
        w = dict(w, fd=w["fd"] + token[0, 0].astype(w["fd"].dtype))
    g = {}
    g["fd"] = mm(act, dx, ta=True, out_dtype=bf16, name=f"{t}_dwd")
    dact = mm(dx, w["fd"], tb=True, out_dtype=bf16, name=f"{t}_dact")
    dgate, dup, g["fcw"], g["fcb"] = cols_call(f"{t}_ffn_act", _ffn_act, [gate, up], [w["fcw"], w["fcb"]], [bf16], nb=nb,
                                               ct=_CONV_CT, ncol=D_FF // _CONV_CT, dseed=[dact])
    dhf, (g["fg"], g["fu"]) = _proj_bwd(f"{t}_ffn", hf, [(dgate, w["fg"]), (dup, w["fu"])])
    dx, g["ln_ffn"] = rms_bwd(x2, w["ln_ffn"], dhf, dx, f"{t}_ln_ffn")
    g["xo"] = mm(o, dx, ta=True, out_dtype=bf16, name=f"{t}_dwo")
    do = mm(dx, w["xo"], tb=True, out_dtype=bf16, name=f"{t}_do")
    dq, dk, dv = xattn_bwd(q, k, v, do, nb, f"{t}_xattn")
    dhx, (g["xq"],) = _proj_bwd(f"{t}_xq", hx, [(dq, w["xq"])])
    dmn, (g["xk"], g["xv"]) = _proj_bwd(f"{t}_xkv", mn, [(dk, w["xk"]), (dv, w["xv"])])
    _, g["ln_mem"] = rms_bwd(mem, w["ln_mem"], dmn, None, f"{t}_ln_mem")
    dx, g["ln_xattn"] = rms_bwd(x1, w["ln_xattn"], dhx, dx, f"{t}_ln_xattn")
    token = grads_done(i, 1, g, dx) if grads_done else None
    mixw = wm["mix"] if token is None else dict(wm["mix"], wout=wm["mix"]["wout"] + token[0, 0].astype(wm["mix"]["wout"].dtype))
    dhn, g["mix"] = _MIXERS[i % 3][1](f"{t}_mix", s_mix, dx, mixw, nb)
    dx, g["ln_mix"] = rms_bwd(x, wm["ln_mix"], dhn, dx, f"{t}_ln_mix")
    token = grads_done(i, 0, g, dx) if grads_done else None
    return dx, g, token


def local_step(x, mem, target, weights_of, final_norm, nb, grads_done=None):
    saved = []
    for i in range(DEPTH):
        x, s = layer_fwd(i, x, mem, weights_of, nb)
        saved.append(s)
    dx, loss, dfinal = loss_head(x, target, final_norm)
    grads = [None] * DEPTH
    token = None
    for i in reversed(range(DEPTH)):
        dx, grads[i], token = layer_bwd(i, saved[i], dx, mem, nb, token, grads_done)
    return loss, dx, grads, dfinal


WEIGHTS = ["ln_mix", "ln_xattn", "ln_mem", "ln_ffn", "final_norm", "m_in_w", "m_conv_w", "m_conv_b", "m_dt_bias", "m_a_log",
           "m_d", "m_norm_w", "m_out_w", "h_in_w", "h_lower_bounds", "h_norm_w", "h_out_w", "g_in_w", "g_conv_w", "g_a_log",
           "g_dt_bias", "g_norm_w", "g_out_w", "xa_q", "xa_kv", "xa_o", "f_up", "f_conv_w", "f_conv_b", "f_down"]
SHARD_AXIS = {"m_in_w": 2, "m_conv_w": 2, "m_conv_b": 1, "m_norm_w": 1, "m_out_w": 1, "h_in_w": 2, "h_out_w": 1, "g_in_w": 2,
              "g_conv_w": 2, "g_out_w": 1, "xa_q": 1, "xa_kv": 2, "xa_o": 1, "f_up": 2, "f_conv_w": 2, "f_down": 1}
MATRICES = ["m_in_w", "m_out_w", "h_in_w", "h_out_w", "g_in_w", "g_out_w", "xa_q", "xa_kv", "xa_o", "f_up", "f_down"]
SMALL_SHARDED = [n for n in WEIGHTS if n in SHARD_AXIS and n not in MATRICES]
REPLICATED = [n for n in WEIGHTS if n not in SHARD_AXIS]
_MIXER_PREFIX = {0: "m", 1: "h", 2: "g"}


def layer_weight_names(i, part):
    if part == 0:
        p = _MIXER_PREFIX[i % 3]
        return [(n, i // 3) for n in WEIGHTS if n in SHARD_AXIS and n.startswith(p + "_")]
    return [(n, i) for n in ("xa_q", "xa_kv", "xa_o", "f_up", "f_conv_w", "f_down")]


def _cols(st, lo, hi):
    ns = st.shape[-1]
    parts = []
    for j in range(NCHIP):
        a, b = max(lo, j * ns), min(hi, (j + 1) * ns)
        if a < b:
            parts.append(st[j][..., a - j * ns:b - j * ns])
    return parts[0] if len(parts) == 1 else jnp.concatenate(parts, axis=-1)


def _col_shards(pieces, ns):
    full = jnp.concatenate(pieces, axis=-1)
    return [full[..., j * ns:(j + 1) * ns] for j in range(NCHIP)]


def _rows(st):
    return st.reshape(st.shape[0] * st.shape[1], st.shape[2])


def prep_layer(i, part, G, R, lb):
    row = lambda a: a.reshape(1, -1)
    p, k = _MIXER_PREFIX[i % 3], i // 3
    if part == 1:
        kv, fup = G["xa_kv"], G["f_up"]
        return dict(ln_xattn=R["ln_xattn"][i:i + 1], ln_mem=R["ln_mem"][i:i + 1], ln_ffn=R["ln_ffn"][i:i + 1],
                    xq=_rows(G["xa_q"]), xk=(kv, (0, 2)), xv=(kv, (2, 2)), xo=_rows(G["xa_o"]), fg=(fup, (0, 2)), fu=(fup, (2, 2)),
                    fcw=_cols(G["f_conv_w"], 0, D_FF), fcb=R["f_conv_b"][i:i + 1], fd=_rows(G["f_down"]))
    layer = dict(ln_mix=R["ln_mix"][i:i + 1])
    inw, wout = G[p + "_in_w"], _rows(G[p + "_out_w"])
    if p == "m":
        cw, cb = G["m_conv_w"], G["m_conv_b"]
        a, b, c = M_INNER, M_INNER + M_G * M_N, M_CONV
        layer["mix"] = dict(
            wz=_cols(inw, 0, M_INNER), wx=_cols(inw, M_INNER, M_INNER + a), wb=_cols(inw, M_INNER + a, M_INNER + b),
            wc=_cols(inw, M_INNER + b, M_MAIN), wdt=_pad_cols(_cols(inw, M_MAIN, M_IN)),
            cwx=_cols(cw, 0, a), cwb=_cols(cw, a, b), cwc=_cols(cw, b, c),
            cbx=row(_cols(cb, 0, a)), cbb=row(_cols(cb, a, b)), cbc=row(_cols(cb, b, c)),
            dtb=_pad_row(R["m_dt_bias"][k]), alog=_pad_row(R["m_a_log"][k]), dsk=_pad_row(R["m_d"][k]),
            nw=row(_cols(G["m_norm_w"], 0, M_INNER)), wout=wout)
    elif p == "h":
        layer["mix"] = dict(wq=(inw, (0, 1)), wf=(inw, (1, 1)), wi=(inw, (2, 1)), wg=(inw, (3, 1)),
                            lb=lb[i:i + 1], nw=row(R["h_norm_w"][k]), wout=wout)
    else:
        cw = G["g_conv_w"]
        layer["mix"] = dict(
            wq=_cols(inw, 0, D), wk=_cols(inw, D, 2 * D), wv=_cols(inw, 2 * D, G_CONV), wz=_cols(inw, G_CONV, G_MAIN),
            wba=_pad_cols(_cols(inw, G_MAIN, G_IN)), cwq=_cols(cw, 0, D), cwk=_cols(cw, D, 2 * D), cwv=_cols(cw, 2 * D, G_CONV),
            alog=_pad_row(R["g_a_log"][k], G_HV), dtb=_pad_row(R["g_dt_bias"][k], G_HV),
            nw=row(R["g_norm_w"][k]), wout=wout)
    return layer


def matrix_grad_parts(i, part, g):
    by_rows = lambda a: a.reshape(NCHIP, a.shape[0] // NCHIP, a.shape[1])
    if part == 1:
        return {"xa_q": by_rows(g["xq"]), "xa_kv": g["xk"], "xa_o": by_rows(g["xo"]), "f_up": g["fg"], "f_down": by_rows(g["fd"])}
    p = _MIXER_PREFIX[i % 3]
    m = g["mix"]
    out = {p + "_out_w": by_rows(m["wout"])}
    if p == "m":
        out["m_in_w"] = jnp.stack(_col_shards([m["wz"], m["wx"], m["wb"], m["wc"], m["wdt"]], M_IN // NCHIP))
    elif p == "h":
        out["h_in_w"] = m["wq"]
    else:
        out["g_in_w"] = jnp.stack(_col_shards([m["wq"], m["wk"], m["wv"], m["wz"], m["wba"]], G_IN // NCHIP))
    return out


def small_grads(grads, dfinal, hlb):
    cat = lambda xs: jnp.concatenate(xs, axis=1)
    out = {k: jnp.concatenate([g[k] for g in grads], axis=0) for k in ("ln_mix", "ln_xattn", "ln_mem", "ln_ffn")}
    out["final_norm"] = dfinal.reshape(D)
    out["f_conv_w"] = jnp.stack([g["fcw"] for g in grads])
    out["f_conv_b"] = jnp.concatenate([g["fcb"] for g in grads], axis=0)
    ms = [g["mix"] for i, g in enumerate(grads) if i % 3 == 0]
    out["m_conv_w"] = jnp.stack([cat([m["cwx"], m["cwb"], m["cwc"]]) for m in ms])
    out["m_conv_b"] = jnp.concatenate([cat([m["cbx"], m["cbb"], m["cbc"]]) for m in ms], axis=0)
    out["m_dt_bias"] = jnp.stack([m["dtb"] for m in ms])
    out["m_a_log"] = jnp.stack([m["alog"] for m in ms])
    out["m_d"] = jnp.stack([m["dsk"] for m in ms])
    out["m_norm_w"] = jnp.stack([m["nw"] for m in ms])
    hs = [(i, g["mix"]) for i, g in enumerate(grads) if i % 3 == 1]
    lb_rows = dict(hs)
    dlb = jnp.concatenate([lb_rows[i]["lb"] if i in lb_rows else jnp.zeros((1, D), f32) for i in range(DEPTH)], axis=0)
    out["h_lower_bounds"] = lower_bounds_bwd(hlb, dlb)
    out["h_norm_w"] = jnp.stack([m["nw"] for _, m in hs])
    gs = [g["mix"] for i, g in enumerate(grads) if i % 3 == 2]
    out["g_conv_w"] = jnp.stack([cat([m["cwq"], m["cwk"], m["cwv"]]) for m in gs])
    out["g_a_log"] = jnp.stack([m["alog"] for m in gs])
    out["g_dt_bias"] = jnp.stack([m["dtb"] for m in gs])
    out["g_norm_w"] = jnp.stack([m["nw"] for m in gs])
    return out


_HBM = pl.BlockSpec(memory_space=pltpu.HBM)


def _place():
    x, y, c = lax.axis_index("x"), lax.axis_index("y"), lax.axis_index("c")
    chips = [(1 - x, y), (x, 1 - y), (1 - x, 1 - y)]
    return x, y, c, chips


_SEM = pl.BlockSpec(memory_space=pltpu.SEMAPHORE)
_ANY = pl.BlockSpec(memory_space=pl.ANY)
_SPLIT = pltpu.CompilerParams(has_side_effects=pltpu.SideEffectType.DATAFLOW_SIDE_EFFECTING)


def _hbm(a):
    return pltpu.with_memory_space_constraint(a, pltpu.HBM)


def _split_start(name, srcs, lands, dep, copies):
    n = len(srcs)

    def body(*refs):
        src_refs, land_refs = refs[:n], refs[n:2 * n]
        send_sems, recv_sems = refs[2 * n + 1], refs[2 * n + 2]
        token = refs[-1]
        for cp in copies(src_refs, land_refs, send_sems, recv_sems):
            cp.start()
        token[...] = jnp.zeros_like(token)

    thru = [pltpu.HBM(a.shape, a.dtype) for a in list(srcs) + list(lands)]
    out = pl.pallas_call(
        body, name=name, in_specs=[_HBM] * (2 * n) + [_ANY],
        out_specs=[_SEM, _SEM] + [_HBM] * (2 * n) + [pl.BlockSpec(memory_space=pltpu.VMEM)],
        out_shape=[pltpu.SemaphoreType.DMA((3 * n,)), pltpu.SemaphoreType.DMA((3 * n,))] + thru + [_S((8, 128), f32)],
        input_output_aliases={t: 2 + t for t in range(2 * n)}, compiler_params=_SPLIT,
    )(*[_hbm(a) for a in srcs], *[_hbm(a) for a in lands], dep)
    return out[0], out[1], out[2:2 + n], out[2 + n:2 + 2 * n], out[-1]


def _split_wait(name, started, after, copies):
    send_sems, recv_sems, srcs, lands, _ = started
    n = len(srcs)

    def body(*refs):
        src_refs, land_refs = refs[:n], refs[n:2 * n]
        s_sems, r_sems = refs[2 * n], refs[2 * n + 1]
        for cp in copies(src_refs, land_refs, s_sems, r_sems):
            cp.wait_send()
            cp.wait_recv()

    out = pl.pallas_call(
        body, name=name, in_specs=[_HBM] * (2 * n) + [_SEM, _SEM, _ANY], out_specs=[_HBM] * (2 * n),
        out_shape=[pltpu.HBM(a.shape, a.dtype) for a in list(srcs) + list(lands)],
        input_output_aliases={t: t for t in range(2 * n)}, compiler_params=_SPLIT,
    )(*srcs, *lands, send_sems, recv_sems, after)
    return out[:n], out[n:]


def _gather_copies(arrive):
    def copies(src_refs, land_refs, send_sems, recv_sems):
        x, y, c, chips = _place()
        out = []
        for t, (s, l) in enumerate(zip(src_refs, land_refs, strict=True)):
            for j, (px, py) in enumerate(chips):
                slot = 2 * px + py if arrive else 2 * x + y
                out.append(pltpu.make_async_remote_copy(src_ref=s, dst_ref=l.at[slot], send_sem=send_sems.at[3 * t + j],
                                                        recv_sem=recv_sems.at[3 * t + j], device_id=(px, py, c), device_id_type=MESH))
        return out
    return copies


def gather_start(name, tensors, me, dep):
    lands = [lax.dynamic_update_index_in_dim(jnp.zeros((NCHIP,) + a.shape, a.dtype), a, me, 0) for a in tensors]
    return _split_start(name, tensors, lands, dep, _gather_copies(False))


def gather_wait(name, started, after):
    return _split_wait(name, started, after, _gather_copies(True))


def _scatter_copies(src_refs, land_refs, send_sems, recv_sems):
    x, y, c, chips = _place()
    out = []
    for t, (s, l) in enumerate(zip(src_refs, land_refs, strict=True)):
        for j, (px, py) in enumerate(chips):
            out.append(pltpu.make_async_remote_copy(src_ref=s.at[2 * px + py], dst_ref=l.at[j], send_sem=send_sems.at[3 * t + j],
                                                    recv_sem=recv_sems.at[3 * t + j], device_id=(px, py, c), device_id_type=MESH))
    return out


def scatter_start(name, parts, dep):
    lands = [lax.empty((3,) + a.shape[1:], a.dtype) for a in parts]
    return _split_start(name, parts, lands, dep, _scatter_copies)


def scatter_wait(name, started, after):
    return _split_wait(name, started, after, _scatter_copies)


def sum_parts(name, part, land, me):
    shape = land.shape[1:]
    c = shape[-1]
    r = land.size // (3 * c)
    tm = _tile(r, (256, 128, 64, 32, 16, 8))

    def body(me_ref, p_ref, l_ref, o_ref):
        o_ref[...] = p_ref[...].astype(f32) + l_ref[0].astype(f32) + l_ref[1].astype(f32) + l_ref[2].astype(f32)

    grid_spec = pltpu.PrefetchScalarGridSpec(
        num_scalar_prefetch=1, grid=(r // tm,),
        in_specs=[pl.BlockSpec((None, tm, c), lambda i, me_ref: (me_ref[0], i, 0)),
                  pl.BlockSpec((3, tm, c), lambda i, me_ref: (0, i, 0))],
        out_specs=pl.BlockSpec((tm, c), lambda i, me_ref: (i, 0)))
    out = pl.pallas_call(body, name=name, grid_spec=grid_spec, out_shape=_S((r, c), f32), compiler_params=_cp())(
        me.reshape(1).astype(jnp.int32), part.reshape(NCHIP, r, c), land.reshape(3, r, c))
    return out.reshape(shape)


def _swap_copies(src_refs, land_refs, send_sems, recv_sems):
    x, y, c, _ = _place()
    return [pltpu.make_async_remote_copy(src_ref=s, dst_ref=l, send_sem=send_sems.at[3 * t], recv_sem=recv_sems.at[3 * t],
                                         device_id=(x, y, 1 - c), device_id_type=MESH)
            for t, (s, l) in enumerate(zip(src_refs, land_refs, strict=True))]


def swap_start(name, tensors, dep):
    return _split_start(name, tensors, [lax.empty(a.shape, a.dtype) for a in tensors], dep, _swap_copies)


def swap_wait(name, started, after):
    return _split_wait(name, started, after, _swap_copies)


def allreduce_small(v):
    r, n = v.shape

    def body(x_ref, out_ref, gat, send_sems, recv_sems, local_sem):
        x, y, c, chips = _place()
        me, sibling = (x, y, c), (x, y, 1 - c)

        def rows(px, py, pc):
            return gat.at[pl.ds((4 * px + 2 * py + pc) * r, r), :]

        def copy(k, block, to, src=None):
            return pltpu.make_async_remote_copy(src_ref=rows(*block) if src is None else src, dst_ref=rows(*block),
                                                send_sem=send_sems.at[k], recv_sem=recv_sems.at[k], device_id=to,
                                                device_id_type=MESH)

        mine = pltpu.make_async_copy(x_ref, rows(*me), local_sem)
        mine.start()
        first = [copy(0, me, sibling, src=x_ref)] + [copy(1 + j, me, (*chip, c), src=x_ref) for j, chip in enumerate(chips)]
        for cp in first:
            cp.start()
        passed = [copy(4 + j, (*chip, c), sibling) for j, chip in enumerate(chips)]
        for j, chip in enumerate(chips):
            copy(1 + j, (*chip, c), me).wait_recv()
            passed[j].start()
        copy(0, sibling, me).wait_recv()
        for j, chip in enumerate(chips):
            copy(4 + j, (*chip, 1 - c), me).wait_recv()
        for cp in first + passed:
            cp.wait_send()
        mine.wait()
        acc = gat[0:r, :]
        for d in range(1, 8):
            acc = acc + gat[d * r:(d + 1) * r, :]
        out_ref[...] = acc

    vm = pl.BlockSpec(memory_space=pltpu.VMEM)
    return pl.pallas_call(
        body, name="allreduce_small", in_specs=[vm], out_specs=vm, out_shape=_S((r, n), v.dtype),
        scratch_shapes=[pltpu.VMEM((8 * r, n), v.dtype), pltpu.SemaphoreType.DMA((7,)), pltpu.SemaphoreType.DMA((7,)),
                        pltpu.SemaphoreType.DMA],
        compiler_params=_cp())(v)


SMALL_ROW = 1024


def kernel(x, mem, ln_mix, ln_xattn, ln_mem, ln_ffn, final_norm, m_in_w, m_conv_w, m_conv_b, m_dt_bias, m_a_log, m_d, m_norm_w, m_out_w, h_in_w, h_lower_bounds, h_norm_w, h_out_w, g_in_w, g_conv_w, g_a_log, g_dt_bias, g_norm_w, g_out_w, xa_q, xa_kv, xa_o, f_up, f_conv_w, f_conv_b, f_down, loss_target, m_ln_mix, m_ln_xattn, m_ln_mem, m_ln_ffn, m_final_norm, m_m_in_w, m_m_conv_w, m_m_conv_b, m_m_dt_bias, m_m_a_log, m_m_d, m_m_norm_w, m_m_out_w, m_h_in_w, m_h_lower_bounds, m_h_norm_w, m_h_out_w, m_g_in_w, m_g_conv_w, m_g_a_log, m_g_dt_bias, m_g_norm_w, m_g_out_w, m_xa_q, m_xa_kv, m_xa_o, m_f_up, m_f_conv_w, m_f_conv_b, m_f_down, v_ln_mix, v_ln_xattn, v_ln_mem, v_ln_ffn, v_final_norm, v_m_in_w, v_m_conv_w, v_m_conv_b, v_m_dt_bias, v_m_a_log, v_m_d, v_m_norm_w, v_m_out_w, v_h_in_w, v_h_lower_bounds, v_h_norm_w, v_h_out_w, v_g_in_w, v_g_conv_w, v_g_a_log, v_g_dt_bias, v_g_norm_w, v_g_out_w, v_xa_q, v_xa_kv, v_xa_o, v_f_up, v_f_conv_w, v_f_conv_b, v_f_down):
    local = dict(zip(WEIGHTS, (ln_mix, ln_xattn, ln_mem, ln_ffn, final_norm, m_in_w, m_conv_w, m_conv_b, m_dt_bias, m_a_log, m_d, m_norm_w, m_out_w, h_in_w, h_lower_bounds, h_norm_w, h_out_w, g_in_w, g_conv_w, g_a_log, g_dt_bias, g_norm_w, g_out_w, xa_q, xa_kv, xa_o, f_up, f_conv_w, f_conv_b, f_down), strict=True))
    mom_m = dict(zip(WEIGHTS, (m_ln_mix, m_ln_xattn, m_ln_mem, m_ln_ffn, m_final_norm, m_m_in_w, m_m_conv_w, m_m_conv_b, m_m_dt_bias, m_m_a_log, m_m_d, m_m_norm_w, m_m_out_w, m_h_in_w, m_h_lower_bounds, m_h_norm_w, m_h_out_w, m_g_in_w, m_g_conv_w, m_g_a_log, m_g_dt_bias, m_g_norm_w, m_g_out_w, m_xa_q, m_xa_kv, m_xa_o, m_f_up, m_f_conv_w, m_f_conv_b, m_f_down), strict=True))
    mom_v = dict(zip(WEIGHTS, (v_ln_mix, v_ln_xattn, v_ln_mem, v_ln_ffn, v_final_norm, v_m_in_w, v_m_conv_w, v_m_conv_b, v_m_dt_bias, v_m_a_log, v_m_d, v_m_norm_w, v_m_out_w, v_h_in_w, v_h_lower_bounds, v_h_norm_w, v_h_out_w, v_g_in_w, v_g_conv_w, v_g_a_log, v_g_dt_bias, v_g_norm_w, v_g_out_w, v_xa_q, v_xa_kv, v_xa_o, v_f_up, v_f_conv_w, v_f_conv_b, v_f_down), strict=True))
    nb, seq, _ = x.shape
    me = 2 * lax.axis_index("x") + lax.axis_index("y")

    repl = {n: local[n] for n in REPLICATED}
    lb = lower_bounds_fwd(repl["h_lower_bounds"])
    nstage = 2 * DEPTH
    names = [layer_weight_names(s // 2, s % 2) for s in range(nstage)]
    cast = lambda n, a: a.astype(bf16) if n in MATRICES else a
    flying = {0: gather_start("gather_start_s0", [cast(n, local[n][k]) for n, k in names[0]], me, x)}
    tok0 = flying[0][4][0, 0]
    shards = [None] + [[cast(n, local[n][k] + tok0) for n, k in names[s]] for s in range(1, nstage)]

    def weights_of(i, part, x_in):
        s = 2 * i + part
        gathered = gather_wait(f"gather_wait_s{s}", flying.pop(s), shards[-1][-1] if s == 0 else x_in)[1]
        w = prep_layer(i, part, {n: g for (n, _), g in zip(names[s], gathered, strict=True)}, repl, lb)
        if s + 1 < nstage:
            flying[s + 1] = gather_start(f"gather_start_s{s + 1}", shards[s + 1], me, gathered[0])
            norm = "ln_mix" if part == 0 else "ln_xattn"
            w[norm] = w[norm] + flying[s + 1][4][0, 0]
        return w

    scattering, swapping = {}, []

    def landed(s, after):
        part_names, started = scattering.pop(s)
        sent, got = scatter_wait(f"scatter_wait_s{s}", started, after)
        sums = [sum_parts(f"sum_s{s}_{n}", p, l, me) for n, p, l in zip(part_names, sent, got, strict=True)]
        swapping.append((s, part_names, swap_start(f"swap_start_s{s}", sums, sums[0])))

    def grads_done(i, part, g, dx_i):
        s = 2 * i + part
        parts = matrix_grad_parts(i, part, g)
        scattering[s] = (list(parts), scatter_start(f"scatter_start_s{s}", list(parts.values()), dx_i))
        token = scattering[s][1][4]
        if s + 1 in scattering:
            landed(s + 1, dx_i)
        return token

    loss, dx, lgrads, dfinal = local_step(x.reshape(nb * seq, D), mem.reshape(nb * N_MEM, D), loss_target.reshape(nb * seq, D),
                                          weights_of, repl["final_norm"].reshape(1, D), nb, grads_done)
    grads = small_grads(lgrads, dfinal, repl["h_lower_bounds"])

    small_names = REPLICATED + SMALL_SHARDED
    flat = jnp.concatenate([grads[n].astype(f32).reshape(-1) for n in small_names] + [loss[0, 0:1] + scattering[0][1][4][0, 0]])
    rows = -(-flat.shape[0] // (8 * SMALL_ROW)) * 8
    flat = jnp.pad(flat, (0, rows * SMALL_ROW - flat.shape[0])).reshape(rows, SMALL_ROW)
    red = allreduce_small(flat).reshape(-1)
    gsum, off = {}, 0
    for n in small_names:
        size = grads[n].size
        g = red[off:off + size].reshape(grads[n].shape)
        off += size
        if n in SHARD_AXIS:
            ax = SHARD_AXIS[n]
            w = g.shape[ax] // NCHIP
            g = lax.dynamic_slice_in_dim(g, me * w, w, axis=ax)
        gsum[n] = g
    loss_out = red[off]

    outs = {}
    for n in small_names:
        outs[n] = adamw(local[n], gsum[n].reshape(local[n].shape), mom_m[n], mom_v[n], f"adamw_{n}")
    mine, theirs = {n: {} for n in MATRICES}, {n: {} for n in MATRICES}

    def swapped(after):
        while swapping:
            s, part_names, started = swapping.pop()
            sent, got = swap_wait(f"swap_wait_s{s}", started, after)
            for n, a, b in zip(part_names, sent, got, strict=True):
                mine[n][s // 2], theirs[n][s // 2] = a, b

    def update(n):
        g_mine, g_theirs = (jnp.stack([d[n][i] for i in sorted(d[n])]) for d in (mine, theirs))
        outs[n] = adamw(local[n], g_mine, mom_m[n], mom_v[n], f"adamw_{n}", g2=g_theirs)

    last = [n for n, _ in names[0] if n in MATRICES]
    swapped(dx)
    for n in MATRICES:
        if n not in last:
            update(n)
    landed(0, outs["f_down"][1])
    swapped(outs["f_down"][1])
    for n in last:
        update(n)
    res = [loss_out, dx.reshape(nb, seq, D)]
    for k in range(4):
        res += [outs[n][k] for n in WEIGHTS]
    return tuple(res)
```

```python
import functools

import jax
import jax.numpy as jnp
from jax import lax
from jax.experimental import pallas as pl
from jax.experimental.pallas import tpu as pltpu

f32 = jnp.float32
bf16 = jnp.bfloat16
HIGHEST = lax.Precision.HIGHEST
MESH = pl.DeviceIdType.MESH

D = 1024
DEPTH = 4
EPS = 1e-6
N_MEM = 256
M_INNER, M_P, M_H, M_G, M_N, M_Q = 2048, 64, 32, 8, 128, 64
M_CONV = M_INNER + 2 * M_G * M_N
M_MAIN = M_INNER + M_CONV
M_IN = M_MAIN + M_H
H_H, H_K, H_Q = 8, 128, 32
G_HV, G_HK, G_K, G_Q = 16, 8, 128, 64
G_CONV, G_VAL = 4096, 2048
G_MAIN = G_CONV + G_VAL
G_IN = G_MAIN + 2 * G_HV
X_H, X_D = 4, 256
D_FF = 2816
ADAM_LR, ADAM_B1, ADAM_B2, ADAM_EPS, ADAM_WD, ADAM_STEP = 0.001, 0.9, 0.999, 1e-08, 0.01, 10
VMEM_LIMIT = 56 * 1024 * 1024
NCHIP = 4


def _cp(**kw):
    return pltpu.CompilerParams(vmem_limit_bytes=VMEM_LIMIT, **kw)


def _S(shape, dtype):
    return jax.ShapeDtypeStruct(tuple(shape), dtype)


def _dg(a, b, ca, cb, prec=None):
    return lax.dot_general(a, b, (((ca,), (cb,)), ((), ())), precision=prec, preferred_element_type=f32)


def _hdot(a, b, ca=1, cb=0, prec=lax.Precision.HIGH):
    return _dg(a.astype(f32), b.astype(f32), ca, cb, prec)


def _bdot_raw(a, b, ca, cb):
    return _dg(a.astype(bf16), b.astype(bf16), ca, cb)


@functools.partial(jax.custom_vjp, nondiff_argnums=(2, 3))
def _bdot(a, b, ca, cb):
    return _bdot_raw(a, b, ca, cb)


def _bdot_fwd(a, b, ca, cb):
    return _bdot_raw(a, b, ca, cb), (a, b)


def _bdot_bwd(ca, cb, res, g):
    a, b = res
    if ca == 1:
        da = _bdot_raw(g, b, 1, 1 if cb == 0 else 0)
    else:
        da = _bdot_raw(b, g, 1 if cb == 0 else 0, 1)
    if cb == 0:
        db = _bdot_raw(a, g, 0 if ca == 1 else 1, 0)
    else:
        db = _bdot_raw(g, a, 0, 0 if ca == 1 else 1)
    return da.astype(a.dtype), db.astype(b.dtype)


_bdot.defvjp(_bdot_fwd, _bdot_bwd)


def _shift_down_raw(x, k):
    r = lax.broadcasted_iota(jnp.int32, x.shape, 0)
    return jnp.where(r >= k, pltpu.roll(x, k, 0), 0.0)


def _shift_up_raw(x, k):
    n = x.shape[0]
    r = lax.broadcasted_iota(jnp.int32, x.shape, 0)
    return jnp.where(r < n - k, pltpu.roll(x, n - k, 0), 0.0)


@functools.partial(jax.custom_vjp, nondiff_argnums=(1,))
def _shift_down(x, k):
    return _shift_down_raw(x, k)


_shift_down.defvjp(lambda x, k: (_shift_down_raw(x, k), None), lambda k, _, g: (_shift_up_raw(g, k),))


def _rms(x, w):
    return x * lax.rsqrt(jnp.mean(x * x, axis=-1, keepdims=True) + EPS) * w


def _silu(x):
    return x * jax.nn.sigmoid(x)


def _masks(q):
    r = lax.broadcasted_iota(jnp.int32, (q, q), 0)
    c = lax.broadcasted_iota(jnp.int32, (q, q), 1)
    return r >= c, r > c


def _colvec(row):
    return jnp.transpose(jnp.broadcast_to(row, (8, row.shape[1])))[:, 0:1]


def _tile(n, cands):
    for c in cands:
        if n % c == 0:
            return c
    return n


def mm(a, b, *, ta=False, tb=False, bsel=None, out_stack=None, out_slots=None, into=None, res=None, norm_w=None,
       out_dtype=f32, name):
    m, k = (a.shape[1], a.shape[0]) if ta else a.shape
    ca, cb = (0 if ta else 1), (1 if tb else 0)
    tm = _tile(m, (1408, 512, 256, 128) if ta else (512, 256, 128))
    if bsel is not None:
        s0, cnt = bsel
        ns = b.shape[2]
        if tb:
            n, tn, tk = b.shape[1], b.shape[1], ns
            b_spec = pl.BlockSpec((None, tn, ns), lambda i, j, kk: (s0 + kk, j, 0))
        else:
            n, tn, tk = cnt * ns, ns, k
            b_spec = pl.BlockSpec((None, tk, ns), lambda i, j, kk: (s0 + j, kk, 0))
    else:
        n = b.shape[0] if tb else b.shape[1]
        tn = out_stack if out_stack else (n if n <= 2816 else _tile(n, (2048, 1024, 512, 256, 128)))
        tk = k if (k <= 4096 and not ta) else _tile(k, (1024, 512, 256, 128))
        b_spec = pl.BlockSpec((tn, tk), lambda i, j, kk: (j, kk)) if tb else pl.BlockSpec((tk, tn), lambda i, j, kk: (kk, j))
    nk = k // tk
    if out_stack:
        total, first = out_slots if out_slots else (n // tn, 0)
        out_spec = pl.BlockSpec((None, tm, tn), lambda i, j, kk: (first + j, i, 0))
        out_shape = _S((total, m, tn), out_dtype)
    else:
        out_spec = pl.BlockSpec((tm, tn), lambda i, j, kk: (i, j))
        out_shape = _S((m, n), out_dtype)

    if norm_w is not None:
        assert tn == n and not out_stack and into is None and res is not None
        out_spec, out_shape = [out_spec, out_spec], [out_shape, _S((m, n), bf16)]

    def body(*refs):
        a_ref, b_ref = refs[:2]
        r_ref = refs[2] if res is not None else None
        if norm_w is not None:
            nw_ref, o_ref, h_ref, acc = refs[-4:]
        else:
            o_ref, acc = refs[-2:]
        kk = pl.program_id(2)

        @pl.when(kk == 0)
        def _():
            acc[...] = jnp.zeros_like(acc)

        acc[...] += _bdot_raw(a_ref[...], b_ref[...], ca, cb)

        @pl.when(kk == nk - 1)
        def _():
            v = acc[...]
            if r_ref is not None:
                v = v + r_ref[...]
            o_ref[...] = v.astype(o_ref.dtype)
            if norm_w is not None:
                h_ref[...] = _rms(v, nw_ref[...]).astype(h_ref.dtype)

    a_spec = pl.BlockSpec((tk, tm), lambda i, j, kk: (kk, i)) if ta else pl.BlockSpec((tm, tk), lambda i, j, kk: (i, kk))
    in_specs = [a_spec, b_spec]
    args = [a, b]
    if res is not None:
        in_specs.append(pl.BlockSpec((tm, tn), lambda i, j, kk: (i, j)))
        args.append(res)
    aliases = {}
    if into is not None:
        aliases = {len(args): 0}
        in_specs.append(pl.BlockSpec(memory_space=pl.ANY))
        args.append(into)
    if norm_w is not None:
        in_specs.append(pl.BlockSpec(norm_w.shape, lambda i, j, kk: (0, 0)))
        args.append(norm_w)
    return pl.pallas_call(
        body, name=name, grid=(m // tm, n // tn, nk), in_specs=in_specs, out_specs=out_spec, out_shape=out_shape,
        scratch_shapes=[pltpu.VMEM((tm, tn), f32)], input_output_aliases=aliases, compiler_params=_cp())(*args)


def rows_call(name, fn, rows, pars, row_out, acc_out=(), tm=512):
    t = rows[0].shape[0]
    tm = min(tm, t)
    assert t % tm == 0, (name, t, tm)
    nr, npar, nro = len(rows), len(pars), len(row_out)

    def body(*refs):
        rv = [r[...] for r in refs[:nr]]
        pv = [r[...] for r in refs[nr:nr + npar]]
        ro_refs = refs[nr + npar:nr + npar + nro]
        ao_refs = refs[nr + npar + nro:]
        ro, ao = fn(*rv, *pv)
        for r, v in zip(ro_refs, ro, strict=True):
            r[...] = v.astype(r.dtype)
        if ao_refs:
            @pl.when(pl.program_id(0) == 0)
            def _():
                for r in ao_refs:
                    r[...] = jnp.zeros_like(r)
            for r, v in zip(ao_refs, ao, strict=True):
                r[...] += v.astype(r.dtype)

    in_specs = [pl.BlockSpec((tm, r.shape[1]), lambda i: (i, 0)) for r in rows]
    in_specs += [pl.BlockSpec(p.shape, lambda i: (0, 0)) for p in pars]
    out_specs = [pl.BlockSpec((tm, c), lambda i: (i, 0)) for c, _ in row_out]
    out_specs += [pl.BlockSpec(s, lambda i: (0, 0)) for s, _ in acc_out]
    out_shape = [_S((t, c), dt) for c, dt in row_out] + [_S(s, dt) for s, dt in acc_out]
    return pl.pallas_call(body, name=name, grid=(t // tm,), in_specs=in_specs, out_specs=out_specs,
                          out_shape=out_shape, compiler_params=_cp())(*rows, *pars)


def rms_fwd(x, w, name):
    return rows_call(name, lambda xv, wv: ((_rms(xv, wv),), ()), [x], [w], [(x.shape[1], bf16)])[0]


def rms_bwd(x, w, dy, dres, name):
    def fn(*a):
        if dres is None:
            xv, dyv, wv = a
        else:
            xv, dyv, drv, wv = a
        _, vjp = jax.vjp(_rms, xv, wv)
        dx, dw = vjp(dyv.astype(f32))
        if dres is not None:
            dx = dx + drv
        return (dx,), (dw,)
    rows = [x, dy] + ([] if dres is None else [dres])
    return rows_call(name, fn, rows, [w], [(x.shape[1], f32)], [(w.shape, f32)])


def cols_call(name, fn, seqs, pars, outs, *, nb, ct, ncol, dseed=None):
    ns, npar = len(seqs), len(pars)
    seq_len = seqs[0].shape[0] // nb
    nd = 0 if dseed is None else len(dseed)

    def body(*refs):
        sv = [r[...] for r in refs[:ns]]
        pv = [r[...] for r in refs[ns:ns + npar]]
        if dseed is None:
            o_refs = refs[ns + npar:]
            for r, v in zip(o_refs, fn(*[v.astype(f32) for v in sv], *pv), strict=True):
                r[...] = v.astype(r.dtype)
            return
        dv = [r[...].astype(f32) for r in refs[ns + npar:ns + npar + nd]]
        ds_refs = refs[ns + npar + nd:ns + npar + nd + ns]
        dp_refs = refs[ns + npar + nd + ns:]
        _, vjp = jax.vjp(fn, *[v.astype(f32) for v in sv], *pv)
        g = vjp(tuple(dv))
        for r, v in zip(ds_refs, g[:ns], strict=True):
            r[...] = v.astype(r.dtype)

        @pl.when(pl.program_id(1) == 0)
        def _():
            for r in dp_refs:
                r[...] = jnp.zeros_like(r)
        for r, v in zip(dp_refs, g[ns:], strict=True):
            r[...] += v

    full = pl.BlockSpec((seq_len, ct), lambda j, b: (b, j))
    in_specs = [full for _ in seqs]
    in_specs += [pl.BlockSpec((p.shape[0], ct), lambda j, b: (0, j)) for p in pars]
    args = list(seqs) + list(pars)
    if dseed is None:
        out_specs = [full for _ in outs]
        out_shape = [_S((nb * seq_len, ncol * ct), dt) for dt in outs]
    else:
        in_specs += [full for _ in dseed]
        args += list(dseed)
        out_specs = [full for _ in seqs] + [pl.BlockSpec((p.shape[0], ct), lambda j, b: (0, j)) for p in pars]
        out_shape = [_S((nb * seq_len, ncol * ct), bf16) for _ in seqs] + [_S(p.shape, f32) for p in pars]
    return pl.pallas_call(body, name=name, grid=(ncol, nb), in_specs=in_specs, out_specs=out_specs,
                          out_shape=out_shape, compiler_params=_cp())(*args)


def _conv4_silu(x, w, b):
    y = x * w[3:4] + _shift_down(x, 1) * w[2:3] + _shift_down(x, 2) * w[1:2] + _shift_down(x, 3) * w[0:1] + b
    return (_silu(y),)


def _conv4_silu_nobias(x, w):
    y = x * w[3:4] + _shift_down(x, 1) * w[2:3] + _shift_down(x, 2) * w[1:2] + _shift_down(x, 3) * w[0:1]
    return (_silu(y),)


def _ffn_act(gate, up, w, b):
    y = gate * w[2:3] + _shift_down(gate, 1) * w[1:2] + _shift_down(gate, 2) * w[0:1] + b
    return (_silu(y) * up,)


def scan_call(name, chunk_fn, seqs, pars, consts, outs, *, nb, nh, q, state_shape, states=None, dseed=None):
    t = seqs[0][0].shape[0]
    nc = t // (nb * q)
    ns, npar, ncon, no = len(seqs), len(pars), len(consts), len(outs)
    s0, s1 = state_shape
    bwd = dseed is not None

    def cidx(c):
        return (nc - 1 - c) if bwd else c

    def rowblk(b, c):
        return b * nc + cidx(c)

    def seq_spec(w, colfn):
        return pl.BlockSpec((q, w), lambda b, c, h: (rowblk(b, c), colfn(h)))

    def par_spec(shape, idxfn):
        return pl.BlockSpec(shape, lambda b, c, h: idxfn(h))

    st_spec = pl.BlockSpec((s0, s1), lambda b, c, h: ((rowblk(b, c)) * nh + h, 0))
    in_specs = [seq_spec(w, cf) for _, w, cf, _ in seqs]
    in_specs += [par_spec(s, f) for _, s, f in pars] + [par_spec(s, f) for _, s, f in consts]
    args = [a for a, _, _, _ in seqs] + [a for a, _, _ in pars] + [a for a, _, _ in consts]

    if not bwd:
        def body(*refs):
            sv = [r[...] for r in refs[:ns]]
            pv = [r[...] for r in refs[ns:ns + npar]]
            cv = [r[...] for r in refs[ns + npar:ns + npar + ncon]]
            o_refs = refs[ns + npar + ncon:ns + npar + ncon + no]
            save_ref = refs[ns + npar + ncon + no]
            st = refs[-1]
            c, h = pl.program_id(1), pl.program_id(2)

            @pl.when(c == 0)
            def _():
                st[h] = jnp.zeros((s0, s1), f32)
            s_in = st[h]
            save_ref[...] = s_in
            o, s_out = chunk_fn(*sv, *pv, s_in, *cv)
            st[h] = s_out
            for r, v in zip(o_refs, o, strict=True):
                r[...] = v.astype(r.dtype)

        out_specs = [seq_spec(w, cf) for _, w, cf, _ in outs] + [st_spec]
        out_shape = [_S((t, cc), dt) for cc, _, _, dt in outs] + [_S((nb * nc * nh * s0, s1), f32)]
        return pl.pallas_call(body, name=name, grid=(nb, nc, nh), in_specs=in_specs, out_specs=out_specs,
                              out_shape=out_shape, scratch_shapes=[pltpu.VMEM((nh, s0, s1), f32)],
                              compiler_params=_cp())(*args)

    def body(*refs):
        i = 0
        sv = [r[...] for r in refs[i:i + ns]]; i += ns
        pv = [r[...] for r in refs[i:i + npar]]; i += npar
        cv = [r[...] for r in refs[i:i + ncon]]; i += ncon
        dv = [r[...].astype(f32) for r in refs[i:i + no]]; i += no
        s_in = refs[i][...]; i += 1
        ds_refs = refs[i:i + ns]; i += ns
        dp_refs = refs[i:i + npar]; i += npar
        dst = refs[-1]
        b, c, h = pl.program_id(0), pl.program_id(1), pl.program_id(2)

        @pl.when(c == 0)
        def _():
            dst[h] = jnp.zeros((s0, s1), f32)

        @pl.when((b == 0) & (c == 0) & (h == 0))
        def _():
            for r in dp_refs:
                r[...] = jnp.zeros_like(r)

        fn = lambda *a: chunk_fn(*a, *cv)
        _, vjp = jax.vjp(fn, *[v.astype(f32) for v in sv], *pv, s_in)
        g = vjp((tuple(dv), dst[h]))
        dst[h] = g[ns + npar]
        for (_, _, _, rep), r, v in zip(seqs, ds_refs, g[:ns], strict=True):
            if rep == 1:
                r[...] = v.astype(r.dtype)
            else:
                @pl.when(h % rep == 0)
                def _(r=r, v=v):
                    r[...] = v.astype(r.dtype)

                @pl.when(h % rep != 0)
                def _(r=r, v=v):
                    r[...] += v.astype(r.dtype)
        for r, v in zip(dp_refs, g[ns:ns + npar], strict=True):
            r[h] += v

    in_specs += [seq_spec(w, cf) for _, w, cf, _ in outs] + [st_spec]
    args += list(dseed) + [states]
    out_specs = [seq_spec(w, cf) for _, w, cf, _ in seqs]
    out_specs += [pl.BlockSpec((nh,) + tuple(s), lambda b, c, h: (0, 0, 0)) for _, s, _ in pars]
    out_shape = [_S(a.shape, bf16 if rep == 1 else f32) for a, _, _, rep in seqs] + [_S((nh,) + tuple(s), f32) for _, s, _ in pars]
    return pl.pallas_call(body, name=name, grid=(nb, nc, nh), in_specs=in_specs, out_specs=out_specs,
                          out_shape=out_shape, scratch_shapes=[pltpu.VMEM((nh, s0, s1), f32)],
                          compiler_params=_cp())(*args)


def _ssd_group(xs, bm, cm, z, dtr, dtb, alog, dsk, nw, st, e):
    q = xs.shape[0]
    heads = range(M_H)
    sl = [slice(i * M_P, (i + 1) * M_P) for i in heads]
    gsl = [slice(g * M_N, (g + 1) * M_N) for g in range(M_G)]
    incl, _ = _masks(q)
    dt = jax.nn.softplus(dtr + dtb[0:1])
    dte = _hdot(dt, e)
    de = _hdot(dsk, e, prec=HIGHEST)[0:1]
    xc = xs * dte
    acum = _hdot(_hdot(incl.astype(f32), dt * -jnp.exp(alog[0:1]), prec=HIGHEST), e)
    last = acum[q - 1:q]
    eac, eend, elast = jnp.exp(acum), jnp.exp(last - acum), jnp.exp(last)
    xe = xc * eend
    bms, cms = [bm[:, s] for s in gsl], [cm[:, s] for s in gsl]
    cb = [_bdot(cms[g], bms[g], 1, 1) for g in range(M_G)]
    decs = []
    for i in heads:
        a_i = acum[:, sl[i]]
        diff = jnp.where(incl, a_i[:, 0:1] - jnp.transpose(a_i)[0:1, :], 0.0)
        decs.append(jnp.where(incl, jnp.exp(diff), 0.0))
    sts = [st[sl[i], :] for i in heads]
    yd = [_bdot(cb[i // 4] * decs[i], xc[:, sl[i]], 1, 0) for i in heads]
    yo = [_bdot(cms[i // 4], sts[i], 1, 1) for i in heads]
    ds = [_bdot(xe[:, sl[i]], bms[i // 4], 0, 0) for i in heads]
    new = [sts[i] * elast[:, i * M_P:i * M_P + 1] + ds[i] for i in heads]
    y = jnp.concatenate(yd, axis=1) + jnp.concatenate(yo, axis=1) * eac + de * xs
    y = y * _silu(z)
    yn = [_rms(y[:, g * 256:(g + 1) * 256], nw[:, g * 256:(g + 1) * 256]) for g in range(M_G)]
    return (jnp.concatenate(yn, axis=1),), jnp.concatenate(new, axis=0)


def _gla_group(qr, fr, ir, gr, lb, nw, st):
    q, hp = qr.shape[0], GLA_HP
    heads = range(hp)
    sl = [slice(i * H_K, (i + 1) * H_K) for i in heads]
    incl, _ = _masks(q)
    fg = lb + (1.0 - lb) * jax.nn.sigmoid(fr)
    qq = _silu(qr) * (H_K ** -0.5)
    k = 1.0 - fg
    gc = _hdot(incl.astype(f32), jnp.log(fg))
    gl = gc[q - 1:q]
    qd, ki, ke = qq * jnp.exp(gc), k * jnp.exp(-gc), k * jnp.exp(gl - gc)
    egl = jnp.exp(gl)
    sts = [st[sl[i], :] for i in heads]
    att = [jnp.where(incl, _bdot(qd[:, sl[i]], ki[:, sl[i]], 1, 1), 0.0) for i in heads]
    o1 = [_bdot(att[i], ir[:, sl[i]], 1, 0) for i in heads]
    o2 = [_bdot(qd[:, sl[i]], sts[i], 1, 0) for i in heads]
    kv = [_bdot(ke[:, sl[i]], ir[:, sl[i]], 0, 0) for i in heads]
    new = [sts[i] * _colvec(egl[:, sl[i]]) + kv[i] for i in heads]
    on = [_rms(o1[i] + o2[i], nw) * _silu(gr[:, sl[i]]) for i in heads]
    return (jnp.concatenate(on, axis=1),), jnp.concatenate(new, axis=0)


def _tri_inv_many(ms):
    n = ms[0].shape[0]
    r = lax.broadcasted_iota(jnp.int32, (n, n), 0)
    c = lax.broadcasted_iota(jnp.int32, (n, n), 1)
    eye = (r == c).astype(f32)
    ts = [eye - m for m in ms]
    ps = list(ms)
    for _ in range(max(1, (n - 1).bit_length() - 1)):
        ps = [_hdot(p, p) for p in ps]
        ts = [t + _hdot(t, p) for t, p in zip(ts, ps)]
    return ts


def _gdn_group(qr, kr, v, z, ba, alog, dtb, nw, st):
    q, hp = qr.shape[0], G_HV
    heads = range(hp)
    sl = [slice(i * G_K, (i + 1) * G_K) for i in heads]
    incl, strict = _masks(q)
    beta_all = jax.nn.sigmoid(ba)
    gc_all = _hdot(incl.astype(f32), -jnp.exp(alog[0:1]) * jax.nn.softplus(ba + dtb[0:1]))
    gc_t = jnp.transpose(gc_all)
    gl_all = gc_all[q - 1:q]
    egc_all, eend_all, egl_all = jnp.exp(gc_all), jnp.exp(gl_all - gc_all), jnp.exp(gl_all)
    lane = lambda a, i: a[:, G_HV + i:G_HV + i + 1]
    beta = [beta_all[:, i:i + 1] for i in heads]
    egc = [lane(egc_all, i) for i in heads]
    qn, kn = [], []
    for j in range(hp // 2):
        qj, kj = qr[:, sl[j]], kr[:, sl[j]]
        qn.append(qj * lax.rsqrt(jnp.sum(qj * qj, axis=-1, keepdims=True) + EPS) * (G_K ** -0.5))
        kn.append(kj * lax.rsqrt(jnp.sum(kj * kj, axis=-1, keepdims=True) + EPS))
    qk = [_bdot(qn[j], kn[j], 1, 1) for j in range(hp // 2)]
    decs = []
    for i in heads:
        diff = jnp.where(incl, lane(gc_all, i) - gc_t[G_HV + i:G_HV + i + 1, :], 0.0)
        decs.append(jnp.where(incl, jnp.exp(diff), 0.0))
    kbs = [kn[i // 2] * beta[i] for i in heads]
    kk = [_bdot(kbs[i], kn[i // 2], 1, 1) for i in heads]
    tinv = _tri_inv_many([jnp.where(strict, kk[i] * decs[i], 0.0) for i in heads])
    uw = [_hdot(tinv[i], jnp.concatenate([v[:, sl[i]] * beta[i], kbs[i] * egc[i]], axis=1)) for i in heads]
    sts = [st[sl[i], :] for i in heads]
    ws = [_bdot(jnp.concatenate([uw[i][:, G_K:], qn[i // 2] * egc[i]], axis=0), sts[i], 1, 0) for i in heads]
    v_new = [uw[i][:, :G_K] - ws[i][:q] for i in heads]
    o = [ws[i][q:] + _bdot(qk[i // 2] * decs[i], v_new[i], 1, 0) for i in heads]
    new = [sts[i] * lane(egl_all, i) + _bdot(kn[i // 2] * lane(eend_all, i), v_new[i], 0, 0) for i in heads]
    on = [_rms(o[i], nw) * _silu(z[:, sl[i]]) for i in heads]
    return (jnp.concatenate(on, axis=1),), jnp.concatenate(new, axis=0)


def _xattn_fn(q, k, v):
    s = _bdot(q, k, 1, 1) * (X_D ** -0.5)
    return _bdot(jax.nn.softmax(s, axis=-1), v, 1, 0)


def xattn_fwd(q, k, v, nb, name, tl=512):
    t = q.shape[0]
    tl = min(tl, t // nb)
    nl = t // nb // tl

    def body(q_ref, k_ref, v_ref, o_ref):
        o_ref[...] = _xattn_fn(q_ref[...], k_ref[...], v_ref[...]).astype(o_ref.dtype)

    qs = pl.BlockSpec((tl, X_D), lambda b, i, h: (b * nl + i, h))
    ks = pl.BlockSpec((N_MEM, X_D), lambda b, i, h: (b, h))
    return pl.pallas_call(body, name=name, grid=(nb, nl, X_H), in_specs=[qs, ks, ks], out_specs=qs,
                          out_shape=_S(q.shape, bf16), compiler_params=_cp())(q, k, v)


def xattn_bwd(q, k, v, do, nb, name, tl=512):
    t = q.shape[0]
    tl = min(tl, t // nb)
    nl = t // nb // tl

    def body(q_ref, k_ref, v_ref, do_ref, dq_ref, dk_ref, dv_ref):
        _, vjp = jax.vjp(_xattn_fn, q_ref[...].astype(f32), k_ref[...].astype(f32), v_ref[...].astype(f32))
        dq, dk, dv = vjp(do_ref[...].astype(f32))
        dq_ref[...] = dq.astype(dq_ref.dtype)

        @pl.when(pl.program_id(2) == 0)
        def _():
            dk_ref[...] = jnp.zeros_like(dk_ref)
            dv_ref[...] = jnp.zeros_like(dv_ref)
        dk_ref[...] += dk
        dv_ref[...] += dv

    qs = pl.BlockSpec((tl, X_D), lambda b, h, i: (b * nl + i, h))
    ks = pl.BlockSpec((N_MEM, X_D), lambda b, h, i: (b, h))
    return pl.pallas_call(body, name=name, grid=(nb, X_H, nl), in_specs=[qs, ks, ks, qs], out_specs=[qs, ks, ks],
                          out_shape=[_S(q.shape, bf16), _S(k.shape, f32), _S(v.shape, f32)],
                          compiler_params=_cp())(q, k, v, do)


def _lower_bounds(hlb):
    sm = jax.nn.softmax(hlb, axis=0)
    rows, run = [], None
    for r in range(hlb.shape[0]):
        run = sm[r:r + 1] if run is None else run + sm[r:r + 1]
        rows.append(run - sm[0:1])
    return jnp.concatenate(rows, axis=0)


def lower_bounds_fwd(hlb):
    return rows_call("lb_fwd", lambda v: ((_lower_bounds(v),), ()), [hlb], [], [(hlb.shape[1], f32)], tm=hlb.shape[0])[0]


def lower_bounds_bwd(hlb, dlb):
    def fn(v, d):
        _, vjp = jax.vjp(_lower_bounds, v)
        return (vjp(d)[0],), ()
    return rows_call("lb_bwd", fn, [hlb, dlb], [], [(hlb.shape[1], f32)], tm=hlb.shape[0])[0]


def loss_head(x, target, w):
    def fn(xv, tv, wv):
        def loss(xx, ww):
            err = _rms(xx, ww) - tv
            return 0.5 * jnp.sum(jnp.mean(err * err, axis=-1))
        val, (dx, dw) = jax.value_and_grad(loss, argnums=(0, 1))(xv, wv)
        return (dx,), (jnp.broadcast_to(val, (1, 128)), dw)
    dx, loss, dw = rows_call("loss_head", fn, [x, target], [w], [(x.shape[1], f32)], [((1, 128), f32), (w.shape, f32)])
    return dx, loss, dw


def _adamw_fn(w, g, m, v):
    m2 = ADAM_B1 * m + (1.0 - ADAM_B1) * g
    v2 = ADAM_B2 * v + (1.0 - ADAM_B2) * (g * g)
    m_hat = m2 / (1.0 - ADAM_B1 ** ADAM_STEP)
    v_hat = v2 / (1.0 - ADAM_B2 ** ADAM_STEP)
    delta = -ADAM_LR * (m_hat / (jnp.sqrt(v_hat) + ADAM_EPS) + ADAM_WD * w)
    return delta, m2, v2


def adamw(w, g, m, v, name, g2=None):
    shape = w.shape
    c = shape[-1]

    def fn(*a):
        if g2 is None:
            wv, gv, mv, vv = a
        else:
            wv, gv, g2v, mv, vv = a
            gv = gv + g2v
        return (gv,) + _adamw_fn(wv, gv, mv, vv), ()

    ins = [w, g] + ([] if g2 is None else [g2]) + [m, v]
    if w.ndim == 3 and shape[1] % 8 == 0:
        tm = next(t for t in (256, 128, 64, 32, 16, 8) if shape[1] % t == 0 and t * c * 4 <= (1 << 21))
        spec = pl.BlockSpec((None, tm, c), lambda l, i: (l, i, 0))

        def body(*refs):
            outs, _ = fn(*[r[...] for r in refs[:len(ins)]])
            for r, o in zip(refs[len(ins):], outs, strict=True):
                r[...] = o

        return tuple(pl.pallas_call(body, name=name, grid=(shape[0], shape[1] // tm), in_specs=[spec] * len(ins),
                                    out_specs=[spec] * 4, out_shape=[_S(shape, f32)] * 4, compiler_params=_cp())(*ins))
    r = w.size // c
    to2 = lambda a: a.reshape(r, c)
    tm = r if r * c * 4 <= (1 << 20) else _tile(r, (256, 128, 64, 32, 16, 8))
    rows = [to2(a) for a in ins]
    outs = rows_call(name, fn, rows, [], [(c, f32)] * 4, tm=tm)
    return tuple(o.reshape(shape) for o in outs)


def _pad_row(v, lane0=0):
    return jnp.pad(v.astype(f32).reshape(1, -1), ((0, 7), (lane0, 128 - lane0 - v.shape[0])))


def _pad_cols(w, n=128):
    return jnp.pad(w, ((0, 0), (0, n - w.shape[1])))


_COL = lambda h: h
_C00 = lambda h: (0, 0)
_CONV_CT = 256


def _conv(name, x, w, b, nb, dseed=None):
    fn = _conv4_silu if b is not None else _conv4_silu_nobias
    pars = [w] + ([] if b is None else [b])
    return cols_call(name, fn, [x], pars, [f32], nb=nb, ct=_CONV_CT, ncol=x.shape[1] // _CONV_CT,
                     dseed=None if dseed is None else [dseed])


GLA_HP = 8


def _ssd_scan(name, xs, bm, cm, z, dtr, p, nb, states=None, dseed=None):
    seqs = [(xs, M_INNER, _COL, 1), (bm, M_G * M_N, _COL, 1), (cm, M_G * M_N, _COL, 1), (z, M_INNER, _COL, 1), (dtr, 128, _COL, 1)]
    pars = [(p["dtb"], (8, 128), _C00), (p["alog"], (8, 128), _C00), (p["dsk"], (8, 128), _C00), (p["nw"], (1, M_INNER), _C00)]
    r = jnp.arange(128)[:, None]
    c = jnp.arange(M_INNER)[None, :]
    consts = [((r == c // M_P).astype(f32), (128, M_INNER), _C00)]
    outs = [(M_INNER, M_INNER, _COL, bf16)]
    return scan_call(name, _ssd_group, seqs, pars, consts, outs, nb=nb, nh=1, q=M_Q, state_shape=(M_H * M_P, M_N),
                     states=states, dseed=dseed)


def _gla_scan(name, qr, fr, ir, gr, p, nb, states=None, dseed=None):
    hp, ng = GLA_HP, H_H // GLA_HP
    seqs = [(a, 128 * hp, _COL, 1) for a in (qr, fr, ir, gr)]
    pars = [(p["lb"], (1, 128 * hp), lambda h: (0, h)), (p["nw"], (1, 128), _C00)]
    outs = [(D, 128 * hp, _COL, bf16)]
    return scan_call(name, _gla_group, seqs, pars, [], outs, nb=nb, nh=ng, q=H_Q, state_shape=(hp * H_K, H_K),
                     states=states, dseed=dseed)


def _gdn_scan(name, qc, kc, vc, z, ba, p, nb, states=None, dseed=None):
    seqs = [(qc, D, _COL, 1), (kc, D, _COL, 1), (vc, G_VAL, _COL, 1), (z, G_VAL, _COL, 1), (ba, 128, _COL, 1)]
    pars = [(p["alog"], (8, 128), _C00), (p["dtb"], (8, 128), _C00), (p["nw"], (1, 128), _C00)]
    outs = [(G_VAL, G_VAL, _COL, bf16)]
    return scan_call(name, _gdn_group, seqs, pars, [], outs, nb=nb, nh=1, q=G_Q, state_shape=(G_HV * G_K, G_K),
                     states=states, dseed=dseed)


def _w(wt):
    return wt if isinstance(wt, tuple) else (wt, None)


def _proj(a, wt, name, res=None, out_dtype=f32):
    arr, bsel = _w(wt)
    return mm(a, arr, bsel=bsel, res=res, out_dtype=out_dtype, name=name)


def _proj_bwd(tag, hn, pieces):
    dhn, dws, bufs = None, [], {}
    for i, (d, wt) in enumerate(pieces):
        arr, bsel = _w(wt)
        if bsel is None:
            dws.append(mm(hn, d, ta=True, out_dtype=bf16, name=f"{tag}_dw{i}"))
        else:
            bufs[id(arr)] = mm(hn, d, ta=True, out_stack=arr.shape[2], out_slots=(arr.shape[0], bsel[0]),
                               into=bufs.get(id(arr)), out_dtype=bf16, name=f"{tag}_dw{i}")
            dws.append(None)
        dhn = mm(d, arr, tb=True, bsel=bsel, res=dhn, name=f"{tag}_dh{i}")
    dws = [dw if dw is not None else bufs[id(_w(wt)[0])] for dw, (_, wt) in zip(dws, pieces, strict=True)]
    return dhn, dws


def ssd_mixer_fwd(tag, hn, w, nb):
    z, xr, br, cr, dtr = (_proj(hn, w[k], f"{tag}_in_{k}") for k in ("wz", "wx", "wb", "wc", "wdt"))
    xs = _conv(f"{tag}_convx", xr, w["cwx"], w["cbx"], nb)[0]
    bm = _conv(f"{tag}_convb", br, w["cwb"], w["cbb"], nb)[0]
    cm = _conv(f"{tag}_convc", cr, w["cwc"], w["cbc"], nb)[0]
    yn, states = _ssd_scan(f"{tag}_scan", xs, bm, cm, z, dtr, w, nb)
    return yn, (hn, z, xr, br, cr, dtr, xs, bm, cm, yn, states)


def ssd_mixer_bwd(tag, saved, dout, w, nb):
    hn, z, xr, br, cr, dtr, xs, bm, cm, yn, states = saved
    g = {"wout": mm(yn, dout, ta=True, out_dtype=bf16, name=f"{tag}_dwout")}
    dyn = mm(dout, w["wout"], tb=True, out_dtype=bf16, name=f"{tag}_dyn")
    dxs, dbm, dcm, dz, ddtr, ddtb, dalog, ddsk, dnw = _ssd_scan(f"{tag}_scanb", xs, bm, cm, z, dtr, w, nb, states, [dyn])
    dxr, g["cwx"], g["cbx"] = _conv(f"{tag}_convxb", xr, w["cwx"], w["cbx"], nb, dxs)
    dbr, g["cwb"], g["cbb"] = _conv(f"{tag}_convbb", br, w["cwb"], w["cbb"], nb, dbm)
    dcr, g["cwc"], g["cbc"] = _conv(f"{tag}_convcb", cr, w["cwc"], w["cbc"], nb, dcm)
    dhn, (g["wz"], g["wx"], g["wb"], g["wc"], g["wdt"]) = _proj_bwd(
        tag, hn, [(dz, w["wz"]), (dxr, w["wx"]), (dbr, w["wb"]), (dcr, w["wc"]), (ddtr, w["wdt"])])
    g["dtb"], g["alog"], g["dsk"] = (jnp.sum(a, axis=0)[0, :M_H] for a in (ddtb, dalog, ddsk))
    g["nw"] = dnw.reshape(M_INNER)
    return dhn, g


def gla_mixer_fwd(tag, hn, w, nb):
    qr, fr, ir, gr = (_proj(hn, w[k], f"{tag}_in_{k}") for k in ("wq", "wf", "wi", "wg"))
    on, states = _gla_scan(f"{tag}_scan", qr, fr, ir, gr, w, nb)
    return on, (hn, qr, fr, ir, gr, on, states)


def gla_mixer_bwd(tag, saved, dout, w, nb):
    hn, qr, fr, ir, gr, on, states = saved
    g = {"wout": mm(on, dout, ta=True, out_dtype=bf16, name=f"{tag}_dwout")}
    don = mm(dout, w["wout"], tb=True, out_dtype=bf16, name=f"{tag}_don")
    dq, df, di, dg, dlb, dnw = _gla_scan(f"{tag}_scanb", qr, fr, ir, gr, w, nb, states, [don])
    dhn, (g["wq"], g["wf"], g["wi"], g["wg"]) = _proj_bwd(tag, hn, [(dq, w["wq"]), (df, w["wf"]), (di, w["wi"]), (dg, w["wg"])])
    g["lb"] = dlb.reshape(1, D)
    g["nw"] = jnp.sum(dnw, axis=0).reshape(H_K)
    return dhn, g


def gdn_mixer_fwd(tag, hn, w, nb):
    qr, kr, vr, z, ba = (_proj(hn, w[k], f"{tag}_in_{k}") for k in ("wq", "wk", "wv", "wz", "wba"))
    qc = _conv(f"{tag}_convq", qr, w["cwq"], None, nb)[0]
    kc = _conv(f"{tag}_convk", kr, w["cwk"], None, nb)[0]
    vc = _conv(f"{tag}_convv", vr, w["cwv"], None, nb)[0]
    on, states = _gdn_scan(f"{tag}_scan", qc, kc, vc, z, ba, w, nb)
    return on, (hn, qr, kr, vr, z, ba, qc, kc, vc, on, states)


def gdn_mixer_bwd(tag, saved, dout, w, nb):
    hn, qr, kr, vr, z, ba, qc, kc, vc, on, states = saved
    g = {"wout": mm(on, dout, ta=True, out_dtype=bf16, name=f"{tag}_dwout")}
    don = mm(dout, w["wout"], tb=True, out_dtype=bf16, name=f"{tag}_don")
    dqc, dkc, dvc, dz, dba, dalog, ddtb, dnw = _gdn_scan(f"{tag}_scanb", qc, kc, vc, z, ba, w, nb, states, [don])
    dqr, g["cwq"] = _conv(f"{tag}_convqb", qr, w["cwq"], None, nb, dqc)
    dkr, g["cwk"] = _conv(f"{tag}_convkb", kr, w["cwk"], None, nb, dkc)
    dvr, g["cwv"] = _conv(f"{tag}_convvb", vr, w["cwv"], None, nb, dvc)
    dhn, (g["wq"], g["wk"], g["wv"], g["wz"], g["wba"]) = _proj_bwd(
        tag, hn, [(dqr, w["wq"]), (dkr, w["wk"]), (dvr, w["wv"]), (dz, w["wz"]), (dba, w["wba"])])
    g["alog"], g["dtb"] = (jnp.sum(a, axis=0)[0, G_HV:2 * G_HV] for a in (dalog, ddtb))
    g["nw"] = jnp.sum(dnw, axis=0).reshape(G_K)
    return dhn, g


_MIXERS = {0: (ssd_mixer_fwd, ssd_mixer_bwd), 1: (gla_mixer_fwd, gla_mixer_bwd), 2: (gdn_mixer_fwd, gdn_mixer_bwd)}


def layer_fwd(i, x, mem, weights_of, nb):
    t = f"l{i}"
    wm = weights_of(i, 0, x)
    hn = rms_fwd(x, wm["ln_mix"], f"{t}_ln_mix")
    mix, s_mix = _MIXERS[i % 3][0](f"{t}_mix", hn, wm["mix"], nb)
    x1, hx = mm(mix, wm["mix"]["wout"], res=x, norm_w=wm["ln_xattn"], name=f"{t}_mix_out")
    w = weights_of(i, 1, x1)
    mn = rms_fwd(mem, w["ln_mem"], f"{t}_ln_mem")
    q = _proj(hx, w["xq"], f"{t}_xa_q", out_dtype=bf16)
    k = _proj(mn, w["xk"], f"{t}_xa_k", out_dtype=bf16)
    v = _proj(mn, w["xv"], f"{t}_xa_v", out_dtype=bf16)
    o = xattn_fwd(q, k, v, nb, f"{t}_xattn")
    x2, hf = mm(o, w["xo"], res=x1, norm_w=w["ln_ffn"], name=f"{t}_xa_o")
    gate = _proj(hf, w["fg"], f"{t}_ffn_gate", out_dtype=bf16)
    up = _proj(hf, w["fu"], f"{t}_ffn_up", out_dtype=bf16)
    act = cols_call(f"{t}_ffn_act", _ffn_act, [gate, up], [w["fcw"], w["fcb"]], [bf16], nb=nb, ct=_CONV_CT,
                    ncol=D_FF // _CONV_CT)[0]
    x3 = mm(act, w["fd"], res=x2, name=f"{t}_ffn_down")
    return x3, (wm, w, x, s_mix, x1, hx, mn, q, k, v, o, x2, hf, gate, up, act)


def layer_bwd(i, saved, dx, mem, nb, token, grads_done):
    t = f"l{i}b"
    wm, w, x, s_mix, x1, hx, mn, q, k, v, o, x2, hf, gate, up, act = saved
    if token is not None:
        w = dict(w, fd=w["fd"] + token[0, 0].astype(w["fd"].dtype))
    g = {}
    g["fd"] = mm(act, dx, ta=True, out_dtype=bf16, name=f"{t}_dwd")
    dact = mm(dx, w["fd"], tb=True, out_dtype=bf16, name=f"{t}_dact")
    dgate, dup, g["fcw"], g["fcb"] = cols_call(f"{t}_ffn_act", _ffn_act, [gate, up], [w["fcw"], w["fcb"]], [bf16], nb=nb,
                                               ct=_CONV_CT, ncol=D_FF // _CONV_CT, dseed=[dact])
    dhf, (g["fg"], g["fu"]) = _proj_bwd(f"{t}_ffn", hf, [(dgate, w["fg"]), (dup, w["fu"])])
    dx, g["ln_ffn"] = rms_bwd(x2, w["ln_ffn"], dhf, dx, f"{t}_ln_ffn")
    g["xo"] = mm(o, dx, ta=True, out_dtype=bf16, name=f"{t}_dwo")
    do = mm(dx, w["xo"], tb=True, out_dtype=bf16, name=f"{t}_do")
    dq, dk, dv = xattn_bwd(q, k, v, do, nb, f"{t}_xattn")
    dhx, (g["xq"],) = _proj_bwd(f"{t}_xq", hx, [(dq, w["xq"])])
    dmn, (g["xk"], g["xv"]) = _proj_bwd(f"{t}_xkv", mn, [(dk, w["xk"]), (dv, w["xv"])])
    _, g["ln_mem"] = rms_bwd(mem, w["ln_mem"], dmn, None, f"{t}_ln_mem")
    dx, g["ln_xattn"] = rms_bwd(x1, w["ln_xattn"], dhx, dx, f"{t}_ln_xattn")
    token = grads_done(i, 1, g, dx) if grads_done else None
    mixw = wm["mix"] if token is None else dict(wm["mix"], wout=wm["mix"]["wout"] + token[0, 0].astype(wm["mix"]["wout"].dtype))
    dhn, g["mix"] = _MIXERS[i % 3][1](f"{t}_mix", s_mix, dx, mixw, nb)
    dx, g["ln_mix"] = rms_bwd(x, wm["ln_mix"], dhn, dx, f"{t}_ln_mix")
    token = grads_done(i, 0, g, dx) if grads_done else None
    return dx, g, token


def local_step(x, mem, target, weights_of, final_norm, nb, grads_done=None):
    saved = []
    for i in range(DEPTH):
        x, s = layer_fwd(i, x, mem, weights_of, nb)
        saved.append(s)
    dx, loss, dfinal = loss_head(x, target, final_norm)
    grads = [None] * DEPTH
    token = None
    for i in reversed(range(DEPTH)):
        dx, grads[i], token = layer_bwd(i, saved[i], dx, mem, nb, token, grads_done)
    return loss, dx, grads, dfinal


WEIGHTS = ["ln_mix", "ln_xattn", "ln_mem", "ln_ffn", "final_norm", "m_in_w", "m_conv_w", "m_conv_b", "m_dt_bias", "m_a_log",
           "m_d", "m_norm_w", "m_out_w", "h_in_w", "h_lower_bounds", "h_norm_w", "h_out_w", "g_in_w", "g_conv_w", "g_a_log",
           "g_dt_bias", "g_norm_w", "g_out_w", "xa_q", "xa_kv", "xa_o", "f_up", "f_conv_w", "f_conv_b", "f_down"]
SHARD_AXIS = {"m_in_w": 2, "m_conv_w": 2, "m_conv_b": 1, "m_norm_w": 1, "m_out_w": 1, "h_in_w": 2, "h_out_w": 1, "g_in_w": 2,
              "g_conv_w": 2, "g_out_w": 1, "xa_q": 1, "xa_kv": 2, "xa_o": 1, "f_up": 2, "f_conv_w": 2, "f_down": 1}
MATRICES = ["m_in_w", "m_out_w", "h_in_w", "h_out_w", "g_in_w", "g_out_w", "xa_q", "xa_kv", "xa_o", "f_up", "f_down"]
SMALL_SHARDED = [n for n in WEIGHTS if n in SHARD_AXIS and n not in MATRICES]
REPLICATED = [n for n in WEIGHTS if n not in SHARD_AXIS]
_MIXER_PREFIX = {0: "m", 1: "h", 2: "g"}


def layer_weight_names(i, part):
    if part == 0:
        p = _MIXER_PREFIX[i % 3]
        return [(n, i // 3) for n in WEIGHTS if n in SHARD_AXIS and n.startswith(p + "_")]
    return [(n, i) for n in ("xa_q", "xa_kv", "xa_o", "f_up", "f_conv_w", "f_down")]


def _cols(st, lo, hi):
    ns = st.shape[-1]
    parts = []
    for j in range(NCHIP):
        a, b = max(lo, j * ns), min(hi, (j + 1) * ns)
        if a < b:
            parts.append(st[j][..., a - j * ns:b - j * ns])
    return parts[0] if len(parts) == 1 else jnp.concatenate(parts, axis=-1)


def _col_shards(pieces, ns):
    full = jnp.concatenate(pieces, axis=-1)
    return [full[..., j * ns:(j + 1) * ns] for j in range(NCHIP)]


def _rows(st):
    return st.reshape(st.shape[0] * st.shape[1], st.shape[2])


def prep_layer(i, part, G, R, lb):
    row = lambda a: a.reshape(1, -1)
    p, k = _MIXER_PREFIX[i % 3], i // 3
    if part == 1:
        kv, fup = G["xa_kv"], G["f_up"]
        return dict(ln_xattn=R["ln_xattn"][i:i + 1], ln_mem=R["ln_mem"][i:i + 1], ln_ffn=R["ln_ffn"][i:i + 1],
                    xq=_rows(G["xa_q"]), xk=(kv, (0, 2)), xv=(kv, (2, 2)), xo=_rows(G["xa_o"]), fg=(fup, (0, 2)), fu=(fup, (2, 2)),
                    fcw=_cols(G["f_conv_w"], 0, D_FF), fcb=R["f_conv_b"][i:i + 1], fd=_rows(G["f_down"]))
    layer = dict(ln_mix=R["ln_mix"][i:i + 1], ln_xattn=R["ln_xattn"][i:i + 1])
    inw, wout = G[p + "_in_w"], _rows(G[p + "_out_w"])
    if p == "m":
        cw, cb = G["m_conv_w"], G["m_conv_b"]
        a, b, c = M_INNER, M_INNER + M_G * M_N, M_CONV
        layer["mix"] = dict(
            wz=_cols(inw, 0, M_INNER), wx=_cols(inw, M_INNER, M_INNER + a), wb=_cols(inw, M_INNER + a, M_INNER + b),
            wc=_cols(inw, M_INNER + b, M_MAIN), wdt=_pad_cols(_cols(inw, M_MAIN, M_IN)),
            cwx=_cols(cw, 0, a), cwb=_cols(cw, a, b), cwc=_cols(cw, b, c),
            cbx=row(_cols(cb, 0, a)), cbb=row(_cols(cb, a, b)), cbc=row(_cols(cb, b, c)),
            dtb=_pad_row(R["m_dt_bias"][k]), alog=_pad_row(R["m_a_log"][k]), dsk=_pad_row(R["m_d"][k]),
            nw=row(_cols(G["m_norm_w"], 0, M_INNER)), wout=wout)
    elif p == "h":
        layer["mix"] = dict(wq=(inw, (0, 1)), wf=(inw, (1, 1)), wi=(inw, (2, 1)), wg=(inw, (3, 1)),
                            lb=lb[i:i + 1], nw=row(R["h_norm_w"][k]), wout=wout)
    else:
        cw = G["g_conv_w"]
        layer["mix"] = dict(
            wq=_cols(inw, 0, D), wk=_cols(inw, D, 2 * D), wv=_cols(inw, 2 * D, G_CONV), wz=_cols(inw, G_CONV, G_MAIN),
            wba=_pad_cols(_cols(inw, G_MAIN, G_IN)), cwq=_cols(cw, 0, D), cwk=_cols(cw, D, 2 * D), cwv=_cols(cw, 2 * D, G_CONV),
            alog=_pad_row(R["g_a_log"][k], G_HV), dtb=_pad_row(R["g_dt_bias"][k], G_HV),
            nw=row(R["g_norm_w"][k]), wout=wout)
    return layer


def matrix_grad_parts(i, part, g):
    by_rows = lambda a: a.reshape(NCHIP, a.shape[0] // NCHIP, a.shape[1])
    if part == 1:
        return {"xa_q": by_rows(g["xq"]), "xa_kv": g["xk"], "xa_o": by_rows(g["xo"]), "f_up": g["fg"], "f_down": by_rows(g["fd"])}
    p = _MIXER_PREFIX[i % 3]
    m = g["mix"]
    out = {p + "_out_w": by_rows(m["wout"])}
    if p == "m":
        out["m_in_w"] = jnp.stack(_col_shards([m["wz"], m["wx"], m["wb"], m["wc"], m["wdt"]], M_IN // NCHIP))
    elif p == "h":
        out["h_in_w"] = m["wq"]
    else:
        out["g_in_w"] = jnp.stack(_col_shards([m["wq"], m["wk"], m["wv"], m["wz"], m["wba"]], G_IN // NCHIP))
    return out


def small_grads(grads, dfinal, hlb):
    cat = lambda xs: jnp.concatenate(xs, axis=1)
    out = {k: jnp.concatenate([g[k] for g in grads], axis=0) for k in ("ln_mix", "ln_xattn", "ln_mem", "ln_ffn")}
    out["final_norm"] = dfinal.reshape(D)
    out["f_conv_w"] = jnp.stack([g["fcw"] for g in grads])
    out["f_conv_b"] = jnp.concatenate([g["fcb"] for g in grads], axis=0)
    ms = [g["mix"] for i, g in enumerate(grads) if i % 3 == 0]
    out["m_conv_w"] = jnp.stack([cat([m["cwx"], m["cwb"], m["cwc"]]) for m in ms])
    out["m_conv_b"] = jnp.concatenate([cat([m["cbx"], m["cbb"], m["cbc"]]) for m in ms], axis=0)
    out["m_dt_bias"] = jnp.stack([m["dtb"] for m in ms])
    out["m_a_log"] = jnp.stack([m["alog"] for m in ms])
    out["m_d"] = jnp.stack([m["dsk"] for m in ms])
    out["m_norm_w"] = jnp.stack([m["nw"] for m in ms])
    hs = [(i, g["mix"]) for i, g in enumerate(grads) if i % 3 == 1]
    lb_rows = dict(hs)
    dlb = jnp.concatenate([lb_rows[i]["lb"] if i in lb_rows else jnp.zeros((1, D), f32) for i in range(DEPTH)], axis=0)
    out["h_lower_bounds"] = lower_bounds_bwd(hlb, dlb)
    out["h_norm_w"] = jnp.stack([m["nw"] for _, m in hs])
    gs = [g["mix"] for i, g in enumerate(grads) if i % 3 == 2]
    out["g_conv_w"] = jnp.stack([cat([m["cwq"], m["cwk"], m["cwv"]]) for m in gs])
    out["g_a_log"] = jnp.stack([m["alog"] for m in gs])
    out["g_dt_bias"] = jnp.stack([m["dtb"] for m in gs])
    out["g_norm_w"] = jnp.stack([m["nw"] for m in gs])
    return out


_HBM = pl.BlockSpec(memory_space=pltpu.HBM)


def _place():
    x, y, c = lax.axis_index("x"), lax.axis_index("y"), lax.axis_index("c")
    chips = [(1 - x, y), (x, 1 - y), (1 - x, 1 - y)]
    return x, y, c, chips


_SEM = pl.BlockSpec(memory_space=pltpu.SEMAPHORE)
_ANY = pl.BlockSpec(memory_space=pl.ANY)
_SPLIT = pltpu.CompilerParams(has_side_effects=pltpu.SideEffectType.DATAFLOW_SIDE_EFFECTING)


def _hbm(a):
    return pltpu.with_memory_space_constraint(a, pltpu.HBM)


def _split_start(name, srcs, lands, dep, copies):
    n = len(srcs)

    def body(*refs):
        src_refs, land_refs = refs[:n], refs[n:2 * n]
        send_sems, recv_sems = refs[2 * n + 1], refs[2 * n + 2]
        token = refs[-1]
        for cp in copies(src_refs, land_refs, send_sems, recv_sems):
            cp.start()
        token[...] = jnp.zeros_like(token)

    thru = [pltpu.HBM(a.shape, a.dtype) for a in list(srcs) + list(lands)]
    out = pl.pallas_call(
        body, name=name, in_specs=[_HBM] * (2 * n) + [_ANY],
        out_specs=[_SEM, _SEM] + [_HBM] * (2 * n) + [pl.BlockSpec(memory_space=pltpu.VMEM)],
        out_shape=[pltpu.SemaphoreType.DMA((3 * n,)), pltpu.SemaphoreType.DMA((3 * n,))] + thru + [_S((8, 128), f32)],
        input_output_aliases={t: 2 + t for t in range(2 * n)}, compiler_params=_SPLIT,
    )(*[_hbm(a) for a in srcs], *[_hbm(a) for a in lands], dep)
    return out[0], out[1], out[2:2 + n], out[2 + n:2 + 2 * n], out[-1]


def _split_wait(name, started, after, copies):
    send_sems, recv_sems, srcs, lands, _ = started
    n = len(srcs)

    def body(*refs):
        src_refs, land_refs = refs[:n], refs[n:2 * n]
        s_sems, r_sems = refs[2 * n], refs[2 * n + 1]
        for cp in copies(src_refs, land_refs, s_sems, r_sems):
            cp.wait_send()
            cp.wait_recv()

    out = pl.pallas_call(
        body, name=name, in_specs=[_HBM] * (2 * n) + [_SEM, _SEM, _ANY], out_specs=[_HBM] * (2 * n),
        out_shape=[pltpu.HBM(a.shape, a.dtype) for a in list(srcs) + list(lands)],
        input_output_aliases={t: t for t in range(2 * n)}, compiler_params=_SPLIT,
    )(*srcs, *lands, send_sems, recv_sems, after)
    return out[:n], out[n:]


def _gather_copies(arrive):
    def copies(src_refs, land_refs, send_sems, recv_sems):
        x, y, c, chips = _place()
        out = []
        for t, (s, l) in enumerate(zip(src_refs, land_refs, strict=True)):
            for j, (px, py) in enumerate(chips):
                slot = 2 * px + py if arrive else 2 * x + y
                out.append(pltpu.make_async_remote_copy(src_ref=s, dst_ref=l.at[slot], send_sem=send_sems.at[3 * t + j],
                                                        recv_sem=recv_sems.at[3 * t + j], device_id=(px, py, c), device_id_type=MESH))
        return out
    return copies


def gather_start(name, tensors, me, dep):
    lands = [lax.dynamic_update_index_in_dim(jnp.zeros((NCHIP,) + a.shape, a.dtype), a, me, 0) for a in tensors]
    return _split_start(name, tensors, lands, dep, _gather_copies(False))


def gather_wait(name, started, after):
    return _split_wait(name, started, after, _gather_copies(True))


def _scatter_copies(src_refs, land_refs, send_sems, recv_sems):
    x, y, c, chips = _place()
    out = []
    for t, (s, l) in enumerate(zip(src_refs, land_refs, strict=True)):
        for j, (px, py) in enumerate(chips):
            out.append(pltpu.make_async_remote_copy(src_ref=s.at[2 * px + py], dst_ref=l.at[j], send_sem=send_sems.at[3 * t + j],
                                                    recv_sem=recv_sems.at[3 * t + j], device_id=(px, py, c), device_id_type=MESH))
    return out


def scatter_start(name, parts, dep):
    lands = [lax.empty((3,) + a.shape[1:], a.dtype) for a in parts]
    return _split_start(name, parts, lands, dep, _scatter_copies)


def scatter_wait(name, started, after):
    return _split_wait(name, started, after, _scatter_copies)


def sum_parts(name, part, land, me):
    shape = land.shape[1:]
    c = shape[-1]
    r = land.size // (3 * c)
    tm = _tile(r, (256, 128, 64, 32, 16, 8))

    def body(me_ref, p_ref, l_ref, o_ref):
        o_ref[...] = p_ref[...].astype(f32) + l_ref[0].astype(f32) + l_ref[1].astype(f32) + l_ref[2].astype(f32)

    grid_spec = pltpu.PrefetchScalarGridSpec(
        num_scalar_prefetch=1, grid=(r // tm,),
        in_specs=[pl.BlockSpec((None, tm, c), lambda i, me_ref: (me_ref[0], i, 0)),
                  pl.BlockSpec((3, tm, c), lambda i, me_ref: (0, i, 0))],
        out_specs=pl.BlockSpec((tm, c), lambda i, me_ref: (i, 0)))
    out = pl.pallas_call(body, name=name, grid_spec=grid_spec, out_shape=_S((r, c), f32), compiler_params=_cp())(
        me.reshape(1).astype(jnp.int32), part.reshape(NCHIP, r, c), land.reshape(3, r, c))
    return out.reshape(shape)


def _swap_copies(src_refs, land_refs, send_sems, recv_sems):
    x, y, c, _ = _place()
    return [pltpu.make_async_remote_copy(src_ref=s, dst_ref=l, send_sem=send_sems.at[3 * t], recv_sem=recv_sems.at[3 * t],
                                         device_id=(x, y, 1 - c), device_id_type=MESH)
            for t, (s, l) in enumerate(zip(src_refs, land_refs, strict=True))]


def swap_start(name, tensors, dep):
    return _split_start(name, tensors, [lax.empty(a.shape, a.dtype) for a in tensors], dep, _swap_copies)


def swap_wait(name, started, after):
    return _split_wait(name, started, after, _swap_copies)


def allreduce_small(v):
    r, n = v.shape

    def body(x_ref, out_ref, gat, send_sems, recv_sems, local_sem):
        x, y, c, chips = _place()
        me, sibling = (x, y, c), (x, y, 1 - c)

        def rows(px, py, pc):
            return gat.at[pl.ds((4 * px + 2 * py + pc) * r, r), :]

        def copy(k, block, to, src=None):
            return pltpu.make_async_remote_copy(src_ref=rows(*block) if src is None else src, dst_ref=rows(*block),
                                                send_sem=send_sems.at[k], recv_sem=recv_sems.at[k], device_id=to,
                                                device_id_type=MESH)

        mine = pltpu.make_async_copy(x_ref, rows(*me), local_sem)
        mine.start()
        first = [copy(0, me, sibling, src=x_ref)] + [copy(1 + j, me, (*chip, c), src=x_ref) for j, chip in enumerate(chips)]
        for cp in first:
            cp.start()
        passed = [copy(4 + j, (*chip, c), sibling) for j, chip in enumerate(chips)]
        for j, chip in enumerate(chips):
            copy(1 + j, (*chip, c), me).wait_recv()
            passed[j].start()
        copy(0, sibling, me).wait_recv()
        for j, chip in enumerate(chips):
            copy(4 + j, (*chip, 1 - c), me).wait_recv()
        for cp in first + passed:
            cp.wait_send()
        mine.wait()
        acc = gat[0:r, :]
        for d in range(1, 8):
            acc = acc + gat[d * r:(d + 1) * r, :]
        out_ref[...] = acc

    vm = pl.BlockSpec(memory_space=pltpu.VMEM)
    return pl.pallas_call(
        body, name="allreduce_small", in_specs=[vm], out_specs=vm, out_shape=_S((r, n), v.dtype),
        scratch_shapes=[pltpu.VMEM((8 * r, n), v.dtype), pltpu.SemaphoreType.DMA((7,)), pltpu.SemaphoreType.DMA((7,)),
                        pltpu.SemaphoreType.DMA],
        compiler_params=_cp())(v)


SMALL_ROW = 1024


def kernel(x, mem, ln_mix, ln_xattn, ln_mem, ln_ffn, final_norm, m_in_w, m_conv_w, m_conv_b, m_dt_bias, m_a_log, m_d, m_norm_w, m_out_w, h_in_w, h_lower_bounds, h_norm_w, h_out_w, g_in_w, g_conv_w, g_a_log, g_dt_bias, g_norm_w, g_out_w, xa_q, xa_kv, xa_o, f_up, f_conv_w, f_conv_b, f_down, loss_target, m_ln_mix, m_ln_xattn, m_ln_mem, m_ln_ffn, m_final_norm, m_m_in_w, m_m_conv_w, m_m_conv_b, m_m_dt_bias, m_m_a_log, m_m_d, m_m_norm_w, m_m_out_w, m_h_in_w, m_h_lower_bounds, m_h_norm_w, m_h_out_w, m_g_in_w, m_g_conv_w, m_g_a_log, m_g_dt_bias, m_g_norm_w, m_g_out_w, m_xa_q, m_xa_kv, m_xa_o, m_f_up, m_f_conv_w, m_f_conv_b, m_f_down, v_ln_mix, v_ln_xattn, v_ln_mem, v_ln_ffn, v_final_norm, v_m_in_w, v_m_conv_w, v_m_conv_b, v_m_dt_bias, v_m_a_log, v_m_d, v_m_norm_w, v_m_out_w, v_h_in_w, v_h_lower_bounds, v_h_norm_w, v_h_out_w, v_g_in_w, v_g_conv_w, v_g_a_log, v_g_dt_bias, v_g_norm_w, v_g_out_w, v_xa_q, v_xa_kv, v_xa_o, v_f_up, v_f_conv_w, v_f_conv_b, v_f_down):
    local = dict(zip(WEIGHTS, (ln_mix, ln_xattn, ln_mem, ln_ffn, final_norm, m_in_w, m_conv_w, m_conv_b, m_dt_bias, m_a_log, m_d, m_norm_w, m_out_w, h_in_w, h_lower_bounds, h_norm_w, h_out_w, g_in_w, g_conv_w, g_a_log, g_dt_bias, g_norm_w, g_out_w, xa_q, xa_kv, xa_o, f_up, f_conv_w, f_conv_b, f_down), strict=True))
    mom_m = dict(zip(WEIGHTS, (m_ln_mix, m_ln_xattn, m_ln_mem, m_ln_ffn, m_final_norm, m_m_in_w, m_m_conv_w, m_m_conv_b, m_m_dt_bias, m_m_a_log, m_m_d, m_m_norm_w, m_m_out_w, m_h_in_w, m_h_lower_bounds, m_h_norm_w, m_h_out_w, m_g_in_w, m_g_conv_w, m_g_a_log, m_g_dt_bias, m_g_norm_w, m_g_out_w, m_xa_q, m_xa_kv, m_xa_o, m_f_up, m_f_conv_w, m_f_conv_b, m_f_down), strict=True))
    mom_v = dict(zip(WEIGHTS, (v_ln_mix, v_ln_xattn, v_ln_mem, v_ln_ffn, v_final_norm, v_m_in_w, v_m_conv_w, v_m_conv_b, v_m_dt_bias, v_m_a_log, v_m_d, v_m_norm_w, v_m_out_w, v_h_in_w, v_h_lower_bounds, v_h_norm_w, v_h_out_w, v_g_in_w, v_g_conv_w, v_g_a_log, v_g_dt_bias, v_g_norm_w, v_g_out_w, v_xa_q, v_xa_kv, v_xa_o, v_f_up, v_f_conv_w, v_f_conv_b, v_f_down), strict=True))
    nb, seq, _ = x.shape
    me = 2 * lax.axis_index("x") + lax.axis_index("y")

    repl = {n: local[n] for n in REPLICATED}
    lb = lower_bounds_fwd(repl["h_lower_bounds"])
    nstage = 2 * DEPTH
    names = [layer_weight_names(s // 2, s % 2) for s in range(nstage)]
    cast = lambda n, a: a.astype(bf16) if n in MATRICES else a
    flying = {0: gather_start("gather_start_s0", [cast(n, local[n][k]) for n, k in names[0]], me, x)}
    tok0 = flying[0][4][0, 0]
    shards = [None] + [[cast(n, local[n][k] + tok0) for n, k in names[s]] for s in range(1, nstage)]

    def weights_of(i, part, x_in):
        s = 2 * i + part
        gathered = gather_wait(f"gather_wait_s{s}", flying.pop(s), shards[-1][-1] if s == 0 else x_in)[1]
        w = prep_layer(i, part, {n: g for (n, _), g in zip(names[s], gathered, strict=True)}, repl, lb)
        if s + 1 < nstage:
            flying[s + 1] = gather_start(f"gather_start_s{s + 1}", shards[s + 1], me, gathered[0])
            norm = "ln_mix" if part == 0 else "ln_mem"
            w[norm] = w[norm] + flying[s + 1][4][0, 0]
        return w

    scattering, swapping = {}, []

    def landed(s, after):
        part_names, started = scattering.pop(s)
        sent, got = scatter_wait(f"scatter_wait_s{s}", started, after)
        sums = [sum_parts(f"sum_s{s}_{n}", p, l, me) for n, p, l in zip(part_names, sent, got, strict=True)]
        swapping.append((s, part_names, swap_start(f"swap_start_s{s}", sums, sums[0])))

    def grads_done(i, part, g, dx_i):
        s = 2 * i + part
        parts = matrix_grad_parts(i, part, g)
        scattering[s] = (list(parts), scatter_start(f"scatter_start_s{s}", list(parts.values()), dx_i))
        token = scattering[s][1][4]
        if s + 1 in scattering:
            landed(s + 1, dx_i)
        return token

    loss, dx, lgrads, dfinal = local_step(x.reshape(nb * seq, D), mem.reshape(nb * N_MEM, D), loss_target.reshape(nb * seq, D),
                                          weights_of, repl["final_norm"].reshape(1, D), nb, grads_done)
    grads = small_grads(lgrads, dfinal, repl["h_lower_bounds"])

    small_names = REPLICATED + SMALL_SHARDED
    flat = jnp.concatenate([grads[n].astype(f32).reshape(-1) for n in small_names] + [loss[0, 0:1] + scattering[0][1][4][0, 0]])
    rows = -(-flat.shape[0] // (8 * SMALL_ROW)) * 8
    flat = jnp.pad(flat, (0, rows * SMALL_ROW - flat.shape[0])).reshape(rows, SMALL_ROW)
    red = allreduce_small(flat).reshape(-1)
    gsum, off = {}, 0
    for n in small_names:
        size = grads[n].size
        g = red[off:off + size].reshape(grads[n].shape)
        off += size
        if n in SHARD_AXIS:
            ax = SHARD_AXIS[n]
            w = g.shape[ax] // NCHIP
            g = lax.dynamic_slice_in_dim(g, me * w, w, axis=ax)
        gsum[n] = g
    loss_out = red[off]

    outs = {}
    for n in small_names:
        outs[n] = adamw(local[n], gsum[n].reshape(local[n].shape), mom_m[n], mom_v[n], f"adamw_{n}")
    mine, theirs = {n: {} for n in MATRICES}, {n: {} for n in MATRICES}

    def swapped(after):
        while swapping:
            s, part_names, started = swapping.pop()
            sent, got = swap_wait(f"swap_wait_s{s}", started, after)
            for n, a, b in zip(part_names, sent, got, strict=True):
                mine[n][s // 2], theirs[n][s // 2] = a, b

    def update(n):
        g_mine, g_theirs = (jnp.stack([d[n][i] for i in sorted(d[n])]) for d in (mine, theirs))
        outs[n] = adamw(local[n], g_mine, mom_m[n], mom_v[n], f"adamw_{n}", g2=g_theirs)

    last = [n for n, _ in names[0] if n in MATRICES]
    swapped(dx)
    for n in MATRICES:
        if n not in last:
            update(n)
    landed(0, outs["f_down"][1])
    swapped(outs["f_down"][1])
    for n in last:
        update(n)
    res = [loss_out, dx.reshape(nb, seq, D)]
    for k in range(4):
        res += [outs[n][k] for n in WEIGHTS]
    return tuple(res)
```

```python
import functools

import jax
import jax.numpy as jnp
from jax import lax
from jax.experimental import pallas as pl
from jax.experimental.pallas import tpu as pltpu

f32 = jnp.float32
bf16 = jnp.bfloat16
HIGHEST = lax.Precision.HIGHEST
MESH = pl.DeviceIdType.MESH

D = 1024
DEPTH = 4
EPS = 1e-6
N_MEM = 256
M_INNER, M_P, M_H, M_G, M_N, M_Q = 2048, 64, 32, 8, 128, 64
M_CONV = M_INNER + 2 * M_G * M_N
M_MAIN = M_INNER + M_CONV
M_IN = M_MAIN + M_H
H_H, H_K, H_Q = 8, 128, 32
G_HV, G_HK, G_K, G_Q = 16, 8, 128, 64
G_CONV, G_VAL = 4096, 2048
G_MAIN = G_CONV + G_VAL
G_IN = G_MAIN + 2 * G_HV
X_H, X_D = 4, 256
D_FF = 2816
ADAM_LR, ADAM_B1, ADAM_B2, ADAM_EPS, ADAM_WD, ADAM_STEP = 0.001, 0.9, 0.999, 1e-08, 0.01, 10
VMEM_LIMIT = 56 * 1024 * 1024
NCHIP = 4


def _cp(**kw):
    return pltpu.CompilerParams(vmem_limit_bytes=VMEM_LIMIT, **kw)


def _S(shape, dtype):
    return jax.ShapeDtypeStruct(tuple(shape), dtype)


def _dg(a, b, ca, cb, prec=None):
    return lax.dot_general(a, b, (((ca,), (cb,)), ((), ())), precision=prec, preferred_element_type=f32)


def _hdot(a, b, ca=1, cb=0, prec=lax.Precision.HIGH):
    return _dg(a.astype(f32), b.astype(f32), ca, cb, prec)


def _bdot_raw(a, b, ca, cb):
    return _dg(a.astype(bf16), b.astype(bf16), ca, cb)


@functools.partial(jax.custom_vjp, nondiff_argnums=(2, 3))
def _bdot(a, b, ca, cb):
    return _bdot_raw(a, b, ca, cb)


def _bdot_fwd(a, b, ca, cb):
    return _bdot_raw(a, b, ca, cb), (a, b)


def _bdot_bwd(ca, cb, res, g):
    a, b = res
    if ca == 1:
        da = _bdot_raw(g, b, 1, 1 if cb == 0 else 0)
    else:
        da = _bdot_raw(b, g, 1 if cb == 0 else 0, 1)
    if cb == 0:
        db = _bdot_raw(a, g, 0 if ca == 1 else 1, 0)
    else:
        db = _bdot_raw(g, a, 0, 0 if ca == 1 else 1)
    return da.astype(a.dtype), db.astype(b.dtype)


_bdot.defvjp(_bdot_fwd, _bdot_bwd)


def _shift_down_raw(x, k):
    r = lax.broadcasted_iota(jnp.int32, x.shape, 0)
    return jnp.where(r >= k, pltpu.roll(x, k, 0), 0.0)


def _shift_up_raw(x, k):
    n = x.shape[0]
    r = lax.broadcasted_iota(jnp.int32, x.shape, 0)
    return jnp.where(r < n - k, pltpu.roll(x, n - k, 0), 0.0)


@functools.partial(jax.custom_vjp, nondiff_argnums=(1,))
def _shift_down(x, k):
    return _shift_down_raw(x, k)


_shift_down.defvjp(lambda x, k: (_shift_down_raw(x, k), None), lambda k, _, g: (_shift_up_raw(g, k),))


def _rms(x, w):
    return x * lax.rsqrt(jnp.mean(x * x, axis=-1, keepdims=True) + EPS) * w


def _silu(x):
    return x * jax.nn.sigmoid(x)


def _masks(q):
    r = lax.broadcasted_iota(jnp.int32, (q, q), 0)
    c = lax.broadcasted_iota(jnp.int32, (q, q), 1)
    return r >= c, r > c


def _colvec(row):
    return jnp.transpose(jnp.broadcast_to(row, (8, row.shape[1])))[:, 0:1]


def _tile(n, cands):
    for c in cands:
        if n % c == 0:
            return c
    return n


def mm(a, b, *, ta=False, tb=False, bsel=None, out_stack=None, out_slots=None, into=None, res=None, out_dtype=f32, name):
    m, k = (a.shape[1], a.shape[0]) if ta else a.shape
    ca, cb = (0 if ta else 1), (1 if tb else 0)
    tm = _tile(m, (1408, 512, 256, 128) if ta else (512, 256, 128))
    if bsel is not None:
        s0, cnt = bsel
        ns = b.shape[2]
        if tb:
            n, tn, tk = b.shape[1], b.shape[1], ns
            b_spec = pl.BlockSpec((None, tn, ns), lambda i, j, kk: (s0 + kk, j, 0))
        else:
            n, tn, tk = cnt * ns, ns, k
            b_spec = pl.BlockSpec((None, tk, ns), lambda i, j, kk: (s0 + j, kk, 0))
    else:
        n = b.shape[0] if tb else b.shape[1]
        tn = out_stack if out_stack else (n if n <= 2816 else _tile(n, (2048, 1024, 512, 256, 128)))
        tk = k if (k <= 4096 and not ta) else _tile(k, (1024, 512, 256, 128))
        b_spec = pl.BlockSpec((tn, tk), lambda i, j, kk: (j, kk)) if tb else pl.BlockSpec((tk, tn), lambda i, j, kk: (kk, j))
    nk = k // tk
    if out_stack:
        total, first = out_slots if out_slots else (n // tn, 0)
        out_spec = pl.BlockSpec((None, tm, tn), lambda i, j, kk: (first + j, i, 0))
        out_shape = _S((total, m, tn), out_dtype)
    else:
        out_spec = pl.BlockSpec((tm, tn), lambda i, j, kk: (i, j))
        out_shape = _S((m, n), out_dtype)

    def body(*refs):
        a_ref, b_ref = refs[:2]
        r_ref = refs[2] if res is not None else None
        o_ref, acc = refs[-2:]
        kk = pl.program_id(2)

        @pl.when(kk == 0)
        def _():
            acc[...] = jnp.zeros_like(acc)

        acc[...] += _bdot_raw(a_ref[...], b_ref[...], ca, cb)

        @pl.when(kk == nk - 1)
        def _():
            v = acc[...]
            if r_ref is not None:
                v = v + r_ref[...]
            o_ref[...] = v.astype(o_ref.dtype)

    a_spec = pl.BlockSpec((tk, tm), lambda i, j, kk: (kk, i)) if ta else pl.BlockSpec((tm, tk), lambda i, j, kk: (i, kk))
    in_specs = [a_spec, b_spec]
    args = [a, b]
    if res is not None:
        in_specs.append(pl.BlockSpec((tm, tn), lambda i, j, kk: (i, j)))
        args.append(res)
    aliases = {}
    if into is not None:
        aliases = {len(args): 0}
        in_specs.append(pl.BlockSpec(memory_space=pl.ANY))
        args.append(into)
    return pl.pallas_call(
        body, name=name, grid=(m // tm, n // tn, nk), in_specs=in_specs, out_specs=out_spec, out_shape=out_shape,
        scratch_shapes=[pltpu.VMEM((tm, tn), f32)], input_output_aliases=aliases, compiler_params=_cp())(*args)


def rows_call(name, fn, rows, pars, row_out, acc_out=(), tm=512):
    t = rows[0].shape[0]
    tm = min(tm, t)
    assert t % tm == 0, (name, t, tm)
    nr, npar, nro = len(rows), len(pars), len(row_out)

    def body(*refs):
        rv = [r[...] for r in refs[:nr]]
        pv = [r[...] for r in refs[nr:nr + npar]]
        ro_refs = refs[nr + npar:nr + npar + nro]
        ao_refs = refs[nr + npar + nro:]
        ro, ao = fn(*rv, *pv)
        for r, v in zip(ro_refs, ro, strict=True):
            r[...] = v.astype(r.dtype)
        if ao_refs:
            @pl.when(pl.program_id(0) == 0)
            def _():
                for r in ao_refs:
                    r[...] = jnp.zeros_like(r)
            for r, v in zip(ao_refs, ao, strict=True):
                r[...] += v.astype(r.dtype)

    in_specs = [pl.BlockSpec((tm, r.shape[1]), lambda i: (i, 0)) for r in rows]
    in_specs += [pl.BlockSpec(p.shape, lambda i: (0, 0)) for p in pars]
    out_specs = [pl.BlockSpec((tm, c), lambda i: (i, 0)) for c, _ in row_out]
    out_specs += [pl.BlockSpec(s, lambda i: (0, 0)) for s, _ in acc_out]
    out_shape = [_S((t, c), dt) for c, dt in row_out] + [_S(s, dt) for s, dt in acc_out]
    return pl.pallas_call(body, name=name, grid=(t // tm,), in_specs=in_specs, out_specs=out_specs,
                          out_shape=out_shape, compiler_params=_cp())(*rows, *pars)


def rms_fwd(x, w, name):
    return rows_call(name, lambda xv, wv: ((_rms(xv, wv),), ()), [x], [w], [(x.shape[1], bf16)])[0]


def rms_bwd(x, w, dy, dres, name):
    def fn(*a):
        if dres is None:
            xv, dyv, wv = a
        else:
            xv, dyv, drv, wv = a
        _, vjp = jax.vjp(_rms, xv, wv)
        dx, dw = vjp(dyv.astype(f32))
        if dres is not None:
            dx = dx + drv
        return (dx,), (dw,)
    rows = [x, dy] + ([] if dres is None else [dres])
    return rows_call(name, fn, rows, [w], [(x.shape[1], f32)], [(w.shape, f32)])


def cols_call(name, fn, seqs, pars, outs, *, nb, ct, ncol, dseed=None):
    ns, npar = len(seqs), len(pars)
    seq_len = seqs[0].shape[0] // nb
    nd = 0 if dseed is None else len(dseed)

    def body(*refs):
        sv = [r[...] for r in refs[:ns]]
        pv = [r[...] for r in refs[ns:ns + npar]]
        if dseed is None:
            o_refs = refs[ns + npar:]
            for r, v in zip(o_refs, fn(*[v.astype(f32) for v in sv], *pv), strict=True):
                r[...] = v.astype(r.dtype)
            return
        dv = [r[...].astype(f32) for r in refs[ns + npar:ns + npar + nd]]
        ds_refs = refs[ns + npar + nd:ns + npar + nd + ns]
        dp_refs = refs[ns + npar + nd + ns:]
        _, vjp = jax.vjp(fn, *[v.astype(f32) for v in sv], *pv)
        g = vjp(tuple(dv))
        for r, v in zip(ds_refs, g[:ns], strict=True):
            r[...] = v.astype(r.dtype)

        @pl.when(pl.program_id(1) == 0)
        def _():
            for r in dp_refs:
                r[...] = jnp.zeros_like(r)
        for r, v in zip(dp_refs, g[ns:], strict=True):
            r[...] += v

    full = pl.BlockSpec((seq_len, ct), lambda j, b: (b, j))
    in_specs = [full for _ in seqs]
    in_specs += [pl.BlockSpec((p.shape[0], ct), lambda j, b: (0, j)) for p in pars]
    args = list(seqs) + list(pars)
    if dseed is None:
        out_specs = [full for _ in outs]
        out_shape = [_S((nb * seq_len, ncol * ct), dt) for dt in outs]
    else:
        in_specs += [full for _ in dseed]
        args += list(dseed)
        out_specs = [full for _ in seqs] + [pl.BlockSpec((p.shape[0], ct), lambda j, b: (0, j)) for p in pars]
        out_shape = [_S((nb * seq_len, ncol * ct), bf16) for _ in seqs] + [_S(p.shape, f32) for p in pars]
    return pl.pallas_call(body, name=name, grid=(ncol, nb), in_specs=in_specs, out_specs=out_specs,
                          out_shape=out_shape, compiler_params=_cp())(*args)


def _conv4_silu(x, w, b):
    y = x * w[3:4] + _shift_down(x, 1) * w[2:3] + _shift_down(x, 2) * w[1:2] + _shift_down(x, 3) * w[0:1] + b
    return (_silu(y),)


def _conv4_silu_nobias(x, w):
    y = x * w[3:4] + _shift_down(x, 1) * w[2:3] + _shift_down(x, 2) * w[1:2] + _shift_down(x, 3) * w[0:1]
    return (_silu(y),)


def _ffn_act(gate, up, w, b):
    y = gate * w[2:3] + _shift_down(gate, 1) * w[1:2] + _shift_down(gate, 2) * w[0:1] + b
    return (_silu(y) * up,)


def scan_call(name, chunk_fn, seqs, pars, consts, outs, *, nb, nh, q, state_shape, states=None, dseed=None):
    t = seqs[0][0].shape[0]
    nc = t // (nb * q)
    ns, npar, ncon, no = len(seqs), len(pars), len(consts), len(outs)
    s0, s1 = state_shape
    bwd = dseed is not None

    def cidx(c):
        return (nc - 1 - c) if bwd else c

    def rowblk(b, c):
        return b * nc + cidx(c)

    def seq_spec(w, colfn):
        return pl.BlockSpec((q, w), lambda b, c, h: (rowblk(b, c), colfn(h)))

    def par_spec(shape, idxfn):
        return pl.BlockSpec(shape, lambda b, c, h: idxfn(h))

    st_spec = pl.BlockSpec((s0, s1), lambda b, c, h: ((rowblk(b, c)) * nh + h, 0))
    in_specs = [seq_spec(w, cf) for _, w, cf, _ in seqs]
    in_specs += [par_spec(s, f) for _, s, f in pars] + [par_spec(s, f) for _, s, f in consts]
    args = [a for a, _, _, _ in seqs] + [a for a, _, _ in pars] + [a for a, _, _ in consts]

    if not bwd:
        def body(*refs):
            sv = [r[...] for r in refs[:ns]]
            pv = [r[...] for r in refs[ns:ns + npar]]
            cv = [r[...] for r in refs[ns + npar:ns + npar + ncon]]
            o_refs = refs[ns + npar + ncon:ns + npar + ncon + no]
            save_ref = refs[ns + npar + ncon + no]
            st = refs[-1]
            c, h = pl.program_id(1), pl.program_id(2)

            @pl.when(c == 0)
            def _():
                st[h] = jnp.zeros((s0, s1), f32)
            s_in = st[h]
            save_ref[...] = s_in
            o, s_out = chunk_fn(*sv, *pv, s_in, *cv)
            st[h] = s_out
            for r, v in zip(o_refs, o, strict=True):
                r[...] = v.astype(r.dtype)

        out_specs = [seq_spec(w, cf) for _, w, cf, _ in outs] + [st_spec]
        out_shape = [_S((t, cc), dt) for cc, _, _, dt in outs] + [_S((nb * nc * nh * s0, s1), f32)]
        return pl.pallas_call(body, name=name, grid=(nb, nc, nh), in_specs=in_specs, out_specs=out_specs,
                              out_shape=out_shape, scratch_shapes=[pltpu.VMEM((nh, s0, s1), f32)],
                              compiler_params=_cp())(*args)

    def body(*refs):
        i = 0
        sv = [r[...] for r in refs[i:i + ns]]; i += ns
        pv = [r[...] for r in refs[i:i + npar]]; i += npar
        cv = [r[...] for r in refs[i:i + ncon]]; i += ncon
        dv = [r[...].astype(f32) for r in refs[i:i + no]]; i += no
        s_in = refs[i][...]; i += 1
        ds_refs = refs[i:i + ns]; i += ns
        dp_refs = refs[i:i + npar]; i += npar
        dst = refs[-1]
        b, c, h = pl.program_id(0), pl.program_id(1), pl.program_id(2)

        @pl.when(c == 0)
        def _():
            dst[h] = jnp.zeros((s0, s1), f32)

        @pl.when((b == 0) & (c == 0) & (h == 0))
        def _():
            for r in dp_refs:
                r[...] = jnp.zeros_like(r)

        fn = lambda *a: chunk_fn(*a, *cv)
        _, vjp = jax.vjp(fn, *[v.astype(f32) for v in sv], *pv, s_in)
        g = vjp((tuple(dv), dst[h]))
        dst[h] = g[ns + npar]
        for (_, _, _, rep), r, v in zip(seqs, ds_refs, g[:ns], strict=True):
            if rep == 1:
                r[...] = v.astype(r.dtype)
            else:
                @pl.when(h % rep == 0)
                def _(r=r, v=v):
                    r[...] = v.astype(r.dtype)

                @pl.when(h % rep != 0)
                def _(r=r, v=v):
                    r[...] += v.astype(r.dtype)
        for r, v in zip(dp_refs, g[ns:ns + npar], strict=True):
            r[h] += v

    in_specs += [seq_spec(w, cf) for _, w, cf, _ in outs] + [st_spec]
    args += list(dseed) + [states]
    out_specs = [seq_spec(w, cf) for _, w, cf, _ in seqs]
    out_specs += [pl.BlockSpec((nh,) + tuple(s), lambda b, c, h: (0, 0, 0)) for _, s, _ in pars]
    out_shape = [_S(a.shape, bf16 if rep == 1 else f32) for a, _, _, rep in seqs] + [_S((nh,) + tuple(s), f32) for _, s, _ in pars]
    return pl.pallas_call(body, name=name, grid=(nb, nc, nh), in_specs=in_specs, out_specs=out_specs,
                          out_shape=out_shape, scratch_shapes=[pltpu.VMEM((nh, s0, s1), f32)],
                          compiler_params=_cp())(*args)


def _ssd_group(xs, bm, cm, z, dtr, dtb, alog, dsk, nw, st, e):
    q = xs.shape[0]
    heads = range(M_H)
    sl = [slice(i * M_P, (i + 1) * M_P) for i in heads]
    gsl = [slice(g * M_N, (g + 1) * M_N) for g in range(M_G)]
    incl, _ = _masks(q)
    dt = jax.nn.softplus(dtr + dtb[0:1])
    dte = _hdot(dt, e)
    de = _hdot(dsk, e, prec=HIGHEST)[0:1]
    xc = xs * dte
    acum = _hdot(_hdot(incl.astype(f32), dt * -jnp.exp(alog[0:1]), prec=HIGHEST), e)
    last = acum[q - 1:q]
    eac, eend, elast = jnp.exp(acum), jnp.exp(last - acum), jnp.exp(last)
    xe = xc * eend
    bms, cms = [bm[:, s] for s in gsl], [cm[:, s] for s in gsl]
    cb = [_bdot(cms[g], bms[g], 1, 1) for g in range(M_G)]
    decs = []
    for i in heads:
        a_i = acum[:, sl[i]]
        diff = jnp.where(incl, a_i[:, 0:1] - jnp.transpose(a_i)[0:1, :], 0.0)
        decs.append(jnp.where(incl, jnp.exp(diff), 0.0))
    sts = [st[sl[i], :] for i in heads]
    yd = [_bdot(cb[i // 4] * decs[i], xc[:, sl[i]], 1, 0) for i in heads]
    yo = [_bdot(cms[i // 4], sts[i], 1, 1) for i in heads]
    ds = [_bdot(xe[:, sl[i]], bms[i // 4], 0, 0) for i in heads]
    new = [sts[i] * elast[:, i * M_P:i * M_P + 1] + ds[i] for i in heads]
    y = jnp.concatenate(yd, axis=1) + jnp.concatenate(yo, axis=1) * eac + de * xs
    y = y * _silu(z)
    yn = [_rms(y[:, g * 256:(g + 1) * 256], nw[:, g * 256:(g + 1) * 256]) for g in range(M_G)]
    return (jnp.concatenate(yn, axis=1),), jnp.concatenate(new, axis=0)


def _gla_group(qr, fr, ir, gr, lb, nw, st):
    q, hp = qr.shape[0], GLA_HP
    heads = range(hp)
    sl = [slice(i * H_K, (i + 1) * H_K) for i in heads]
    incl, _ = _masks(q)
    fg = lb + (1.0 - lb) * jax.nn.sigmoid(fr)
    qq = _silu(qr) * (H_K ** -0.5)
    k = 1.0 - fg
    gc = _hdot(incl.astype(f32), jnp.log(fg))
    gl = gc[q - 1:q]
    qd, ki, ke = qq * jnp.exp(gc), k * jnp.exp(-gc), k * jnp.exp(gl - gc)
    egl = jnp.exp(gl)
    sts = [st[sl[i], :] for i in heads]
    att = [jnp.where(incl, _bdot(qd[:, sl[i]], ki[:, sl[i]], 1, 1), 0.0) for i in heads]
    o1 = [_bdot(att[i], ir[:, sl[i]], 1, 0) for i in heads]
    o2 = [_bdot(qd[:, sl[i]], sts[i], 1, 0) for i in heads]
    kv = [_bdot(ke[:, sl[i]], ir[:, sl[i]], 0, 0) for i in heads]
    new = [sts[i] * _colvec(egl[:, sl[i]]) + kv[i] for i in heads]
    on = [_rms(o1[i] + o2[i], nw) * _silu(gr[:, sl[i]]) for i in heads]
    return (jnp.concatenate(on, axis=1),), jnp.concatenate(new, axis=0)


def _tri_inv_many(ms):
    n = ms[0].shape[0]
    r = lax.broadcasted_iota(jnp.int32, (n, n), 0)
    c = lax.broadcasted_iota(jnp.int32, (n, n), 1)
    eye = (r == c).astype(f32)
    ts = [eye - m for m in ms]
    ps = list(ms)
    for _ in range(max(1, (n - 1).bit_length() - 1)):
        ps = [_hdot(p, p) for p in ps]
        ts = [t + _hdot(t, p) for t, p in zip(ts, ps)]
    return ts


def _gdn_group(qr, kr, v, z, ba, alog, dtb, nw, st):
    q, hp = qr.shape[0], G_HV
    heads = range(hp)
    sl = [slice(i * G_K, (i + 1) * G_K) for i in heads]
    incl, strict = _masks(q)
    beta_all = jax.nn.sigmoid(ba)
    gc_all = _hdot(incl.astype(f32), -jnp.exp(alog[0:1]) * jax.nn.softplus(ba + dtb[0:1]))
    gc_t = jnp.transpose(gc_all)
    gl_all = gc_all[q - 1:q]
    egc_all, eend_all, egl_all = jnp.exp(gc_all), jnp.exp(gl_all - gc_all), jnp.exp(gl_all)
    lane = lambda a, i: a[:, G_HV + i:G_HV + i + 1]
    beta = [beta_all[:, i:i + 1] for i in heads]
    egc = [lane(egc_all, i) for i in heads]
    qn, kn = [], []
    for j in range(hp // 2):
        qj, kj = qr[:, sl[j]], kr[:, sl[j]]
        qn.append(qj * lax.rsqrt(jnp.sum(qj * qj, axis=-1, keepdims=True) + EPS) * (G_K ** -0.5))
        kn.append(kj * lax.rsqrt(jnp.sum(kj * kj, axis=-1, keepdims=True) + EPS))
    qk = [_bdot(qn[j], kn[j], 1, 1) for j in range(hp // 2)]
    decs = []
    for i in heads:
        diff = jnp.where(incl, lane(gc_all, i) - gc_t[G_HV + i:G_HV + i + 1, :], 0.0)
        decs.append(jnp.where(incl, jnp.exp(diff), 0.0))
    kbs = [kn[i // 2] * beta[i] for i in heads]
    kk = [_bdot(kbs[i], kn[i // 2], 1, 1) for i in heads]
    tinv = _tri_inv_many([jnp.where(strict, kk[i] * decs[i], 0.0) for i in heads])
    uw = [_hdot(tinv[i], jnp.concatenate([v[:, sl[i]] * beta[i], kbs[i] * egc[i]], axis=1)) for i in heads]
    sts = [st[sl[i], :] for i in heads]
    ws = [_bdot(jnp.concatenate([uw[i][:, G_K:], qn[i // 2] * egc[i]], axis=0), sts[i], 1, 0) for i in heads]
    v_new = [uw[i][:, :G_K] - ws[i][:q] for i in heads]
    o = [ws[i][q:] + _bdot(qk[i // 2] * decs[i], v_new[i], 1, 0) for i in heads]
    new = [sts[i] * lane(egl_all, i) + _bdot(kn[i // 2] * lane(eend_all, i), v_new[i], 0, 0) for i in heads]
    on = [_rms(o[i], nw) * _silu(z[:, sl[i]]) for i in heads]
    return (jnp.concatenate(on, axis=1),), jnp.concatenate(new, axis=0)


def _xattn_fn(q, k, v):
    s = _bdot(q, k, 1, 1) * (X_D ** -0.5)
    return _bdot(jax.nn.softmax(s, axis=-1), v, 1, 0)


def xattn_fwd(q, k, v, nb, name, tl=512):
    t = q.shape[0]
    tl = min(tl, t // nb)
    nl = t // nb // tl

    def body(q_ref, k_ref, v_ref, o_ref):
        o_ref[...] = _xattn_fn(q_ref[...], k_ref[...], v_ref[...]).astype(o_ref.dtype)

    qs = pl.BlockSpec((tl, X_D), lambda b, i, h: (b * nl + i, h))
    ks = pl.BlockSpec((N_MEM, X_D), lambda b, i, h: (b, h))
    return pl.pallas_call(body, name=name, grid=(nb, nl, X_H), in_specs=[qs, ks, ks], out_specs=qs,
                          out_shape=_S(q.shape, bf16), compiler_params=_cp())(q, k, v)


def xattn_bwd(q, k, v, do, nb, name, tl=512):
    t = q.shape[0]
    tl = min(tl, t // nb)
    nl = t // nb // tl

    def body(q_ref, k_ref, v_ref, do_ref, dq_ref, dk_ref, dv_ref):
        _, vjp = jax.vjp(_xattn_fn, q_ref[...].astype(f32), k_ref[...].astype(f32), v_ref[...].astype(f32))
        dq, dk, dv = vjp(do_ref[...].astype(f32))
        dq_ref[...] = dq.astype(dq_ref.dtype)

        @pl.when(pl.program_id(2) == 0)
        def _():
            dk_ref[...] = jnp.zeros_like(dk_ref)
            dv_ref[...] = jnp.zeros_like(dv_ref)
        dk_ref[...] += dk
        dv_ref[...] += dv

    qs = pl.BlockSpec((tl, X_D), lambda b, h, i: (b * nl + i, h))
    ks = pl.BlockSpec((N_MEM, X_D), lambda b, h, i: (b, h))
    return pl.pallas_call(body, name=name, grid=(nb, X_H, nl), in_specs=[qs, ks, ks, qs], out_specs=[qs, ks, ks],
                          out_shape=[_S(q.shape, bf16), _S(k.shape, f32), _S(v.shape, f32)],
                          compiler_params=_cp())(q, k, v, do)


def _lower_bounds(hlb):
    sm = jax.nn.softmax(hlb, axis=0)
    rows, run = [], None
    for r in range(hlb.shape[0]):
        run = sm[r:r + 1] if run is None else run + sm[r:r + 1]
        rows.append(run - sm[0:1])
    return jnp.concatenate(rows, axis=0)


def lower_bounds_fwd(hlb):
    return rows_call("lb_fwd", lambda v: ((_lower_bounds(v),), ()), [hlb], [], [(hlb.shape[1], f32)], tm=hlb.shape[0])[0]


def lower_bounds_bwd(hlb, dlb):
    def fn(v, d):
        _, vjp = jax.vjp(_lower_bounds, v)
        return (vjp(d)[0],), ()
    return rows_call("lb_bwd", fn, [hlb, dlb], [], [(hlb.shape[1], f32)], tm=hlb.shape[0])[0]


def loss_head(x, target, w):
    def fn(xv, tv, wv):
        def loss(xx, ww):
            err = _rms(xx, ww) - tv
            return 0.5 * jnp.sum(jnp.mean(err * err, axis=-1))
        val, (dx, dw) = jax.value_and_grad(loss, argnums=(0, 1))(xv, wv)
        return (dx,), (jnp.broadcast_to(val, (1, 128)), dw)
    dx, loss, dw = rows_call("loss_head", fn, [x, target], [w], [(x.shape[1], f32)], [((1, 128), f32), (w.shape, f32)])
    return dx, loss, dw


def _adamw_fn(w, g, m, v):
    m2 = ADAM_B1 * m + (1.0 - ADAM_B1) * g
    v2 = ADAM_B2 * v + (1.0 - ADAM_B2) * (g * g)
    m_hat = m2 / (1.0 - ADAM_B1 ** ADAM_STEP)
    v_hat = v2 / (1.0 - ADAM_B2 ** ADAM_STEP)
    delta = -ADAM_LR * (m_hat / (jnp.sqrt(v_hat) + ADAM_EPS) + ADAM_WD * w)
    return delta, m2, v2


def adamw(w, g, m, v, name, g2=None):
    shape = w.shape
    c = shape[-1]

    def fn(*a):
        if g2 is None:
            wv, gv, mv, vv = a
        else:
            wv, gv, g2v, mv, vv = a
            gv = gv + g2v
        return (gv,) + _adamw_fn(wv, gv, mv, vv), ()

    ins = [w, g] + ([] if g2 is None else [g2]) + [m, v]
    if w.ndim == 3 and shape[1] % 8 == 0:
        tm = next(t for t in (256, 128, 64, 32, 16, 8) if shape[1] % t == 0 and t * c * 4 <= (1 << 21))
        spec = pl.BlockSpec((None, tm, c), lambda l, i: (l, i, 0))

        def body(*refs):
            outs, _ = fn(*[r[...] for r in refs[:len(ins)]])
            for r, o in zip(refs[len(ins):], outs, strict=True):
                r[...] = o

        return tuple(pl.pallas_call(body, name=name, grid=(shape[0], shape[1] // tm), in_specs=[spec] * len(ins),
                                    out_specs=[spec] * 4, out_shape=[_S(shape, f32)] * 4, compiler_params=_cp())(*ins))
    r = w.size // c
    to2 = lambda a: a.reshape(r, c)
    tm = r if r * c * 4 <= (1 << 20) else _tile(r, (256, 128, 64, 32, 16, 8))
    rows = [to2(a) for a in ins]
    outs = rows_call(name, fn, rows, [], [(c, f32)] * 4, tm=tm)
    return tuple(o.reshape(shape) for o in outs)


def _pad_row(v, lane0=0):
    return jnp.pad(v.astype(f32).reshape(1, -1), ((0, 7), (lane0, 128 - lane0 - v.shape[0])))


def _pad_cols(w, n=128):
    return jnp.pad(w, ((0, 0), (0, n - w.shape[1])))


_COL = lambda h: h
_C00 = lambda h: (0, 0)
_CONV_CT = 256


def _conv(name, x, w, b, nb, dseed=None):
    fn = _conv4_silu if b is not None else _conv4_silu_nobias
    pars = [w] + ([] if b is None else [b])
    return cols_call(name, fn, [x], pars, [f32], nb=nb, ct=_CONV_CT, ncol=x.shape[1] // _CONV_CT,
                     dseed=None if dseed is None else [dseed])


GLA_HP = 8


def _ssd_scan(name, xs, bm, cm, z, dtr, p, nb, states=None, dseed=None):
    seqs = [(xs, M_INNER, _COL, 1), (bm, M_G * M_N, _COL, 1), (cm, M_G * M_N, _COL, 1), (z, M_INNER, _COL, 1), (dtr, 128, _COL, 1)]
    pars = [(p["dtb"], (8, 128), _C00), (p["alog"], (8, 128), _C00), (p["dsk"], (8, 128), _C00), (p["nw"], (1, M_INNER), _C00)]
    r = jnp.arange(128)[:, None]
    c = jnp.arange(M_INNER)[None, :]
    consts = [((r == c // M_P).astype(f32), (128, M_INNER), _C00)]
    outs = [(M_INNER, M_INNER, _COL, bf16)]
    return scan_call(name, _ssd_group, seqs, pars, consts, outs, nb=nb, nh=1, q=M_Q, state_shape=(M_H * M_P, M_N),
                     states=states, dseed=dseed)


def _gla_scan(name, qr, fr, ir, gr, p, nb, states=None, dseed=None):
    hp, ng = GLA_HP, H_H // GLA_HP
    seqs = [(a, 128 * hp, _COL, 1) for a in (qr, fr, ir, gr)]
    pars = [(p["lb"], (1, 128 * hp), lambda h: (0, h)), (p["nw"], (1, 128), _C00)]
    outs = [(D, 128 * hp, _COL, bf16)]
    return scan_call(name, _gla_group, seqs, pars, [], outs, nb=nb, nh=ng, q=H_Q, state_shape=(hp * H_K, H_K),
                     states=states, dseed=dseed)


def _gdn_scan(name, qc, kc, vc, z, ba, p, nb, states=None, dseed=None):
    seqs = [(qc, D, _COL, 1), (kc, D, _COL, 1), (vc, G_VAL, _COL, 1), (z, G_VAL, _COL, 1), (ba, 128, _COL, 1)]
    pars = [(p["alog"], (8, 128), _C00), (p["dtb"], (8, 128), _C00), (p["nw"], (1, 128), _C00)]
    outs = [(G_VAL, G_VAL, _COL, bf16)]
    return scan_call(name, _gdn_group, seqs, pars, [], outs, nb=nb, nh=1, q=G_Q, state_shape=(G_HV * G_K, G_K),
                     states=states, dseed=dseed)


def _w(wt):
    return wt if isinstance(wt, tuple) else (wt, None)


def _proj(a, wt, name, res=None, out_dtype=f32):
    arr, bsel = _w(wt)
    return mm(a, arr, bsel=bsel, res=res, out_dtype=out_dtype, name=name)


def _proj_bwd(tag, hn, pieces):
    dhn, dws, bufs = None, [], {}
    for i, (d, wt) in enumerate(pieces):
        arr, bsel = _w(wt)
        if bsel is None:
            dws.append(mm(hn, d, ta=True, out_dtype=bf16, name=f"{tag}_dw{i}"))
        else:
            bufs[id(arr)] = mm(hn, d, ta=True, out_stack=arr.shape[2], out_slots=(arr.shape[0], bsel[0]),
                               into=bufs.get(id(arr)), out_dtype=bf16, name=f"{tag}_dw{i}")
            dws.append(None)
        dhn = mm(d, arr, tb=True, bsel=bsel, res=dhn, name=f"{tag}_dh{i}")
    dws = [dw if dw is not None else bufs[id(_w(wt)[0])] for dw, (_, wt) in zip(dws, pieces, strict=True)]
    return dhn, dws


def ssd_mixer_fwd(tag, hn, w, nb):
    z, xr, br, cr, dtr = (_proj(hn, w[k], f"{tag}_in_{k}") for k in ("wz", "wx", "wb", "wc", "wdt"))
    xs = _conv(f"{tag}_convx", xr, w["cwx"], w["cbx"], nb)[0]
    bm = _conv(f"{tag}_convb", br, w["cwb"], w["cbb"], nb)[0]
    cm = _conv(f"{tag}_convc", cr, w["cwc"], w["cbc"], nb)[0]
    yn, states = _ssd_scan(f"{tag}_scan", xs, bm, cm, z, dtr, w, nb)
    return yn, (hn, z, xr, br, cr, dtr, xs, bm, cm, yn, states)


def ssd_mixer_bwd(tag, saved, dout, w, nb):
    hn, z, xr, br, cr, dtr, xs, bm, cm, yn, states = saved
    g = {"wout": mm(yn, dout, ta=True, out_dtype=bf16, name=f"{tag}_dwout")}
    dyn = mm(dout, w["wout"], tb=True, out_dtype=bf16, name=f"{tag}_dyn")
    dxs, dbm, dcm, dz, ddtr, ddtb, dalog, ddsk, dnw = _ssd_scan(f"{tag}_scanb", xs, bm, cm, z, dtr, w, nb, states, [dyn])
    dxr, g["cwx"], g["cbx"] = _conv(f"{tag}_convxb", xr, w["cwx"], w["cbx"], nb, dxs)
    dbr, g["cwb"], g["cbb"] = _conv(f"{tag}_convbb", br, w["cwb"], w["cbb"], nb, dbm)
    dcr, g["cwc"], g["cbc"] = _conv(f"{tag}_convcb", cr, w["cwc"], w["cbc"], nb, dcm)
    dhn, (g["wz"], g["wx"], g["wb"], g["wc"], g["wdt"]) = _proj_bwd(
        tag, hn, [(dz, w["wz"]), (dxr, w["wx"]), (dbr, w["wb"]), (dcr, w["wc"]), (ddtr, w["wdt"])])
    g["dtb"], g["alog"], g["dsk"] = (jnp.sum(a, axis=0)[0, :M_H] for a in (ddtb, dalog, ddsk))
    g["nw"] = dnw.reshape(M_INNER)
    return dhn, g


def gla_mixer_fwd(tag, hn, w, nb):
    qr, fr, ir, gr = (_proj(hn, w[k], f"{tag}_in_{k}") for k in ("wq", "wf", "wi", "wg"))
    on, states = _gla_scan(f"{tag}_scan", qr, fr, ir, gr, w, nb)
    return on, (hn, qr, fr, ir, gr, on, states)


def gla_mixer_bwd(tag, saved, dout, w, nb):
    hn, qr, fr, ir, gr, on, states = saved
    g = {"wout": mm(on, dout, ta=True, out_dtype=bf16, name=f"{tag}_dwout")}
    don = mm(dout, w["wout"], tb=True, out_dtype=bf16, name=f"{tag}_don")
    dq, df, di, dg, dlb, dnw = _gla_scan(f"{tag}_scanb", qr, fr, ir, gr, w, nb, states, [don])
    dhn, (g["wq"], g["wf"], g["wi"], g["wg"]) = _proj_bwd(tag, hn, [(dq, w["wq"]), (df, w["wf"]), (di, w["wi"]), (dg, w["wg"])])
    g["lb"] = dlb.reshape(1, D)
    g["nw"] = jnp.sum(dnw, axis=0).reshape(H_K)
    return dhn, g


def gdn_mixer_fwd(tag, hn, w, nb):
    qr, kr, vr, z, ba = (_proj(hn, w[k], f"{tag}_in_{k}") for k in ("wq", "wk", "wv", "wz", "wba"))
    qc = _conv(f"{tag}_convq", qr, w["cwq"], None, nb)[0]
    kc = _conv(f"{tag}_convk", kr, w["cwk"], None, nb)[0]
    vc = _conv(f"{tag}_convv", vr, w["cwv"], None, nb)[0]
    on, states = _gdn_scan(f"{tag}_scan", qc, kc, vc, z, ba, w, nb)
    return on, (hn, qr, kr, vr, z, ba, qc, kc, vc, on, states)


def gdn_mixer_bwd(tag, saved, dout, w, nb):
    hn, qr, kr, vr, z, ba, qc, kc, vc, on, states = saved
    g = {"wout": mm(on, dout, ta=True, out_dtype=bf16, name=f"{tag}_dwout")}
    don = mm(dout, w["wout"], tb=True, out_dtype=bf16, name=f"{tag}_don")
    dqc, dkc, dvc, dz, dba, dalog, ddtb, dnw = _gdn_scan(f"{tag}_scanb", qc, kc, vc, z, ba, w, nb, states, [don])
    dqr, g["cwq"] = _conv(f"{tag}_convqb", qr, w["cwq"], None, nb, dqc)
    dkr, g["cwk"] = _conv(f"{tag}_convkb", kr, w["cwk"], None, nb, dkc)
    dvr, g["cwv"] = _conv(f"{tag}_convvb", vr, w["cwv"], None, nb, dvc)
    dhn, (g["wq"], g["wk"], g["wv"], g["wz"], g["wba"]) = _proj_bwd(
        tag, hn, [(dqr, w["wq"]), (dkr, w["wk"]), (dvr, w["wv"]), (dz, w["wz"]), (dba, w["wba"])])
    g["alog"], g["dtb"] = (jnp.sum(a, axis=0)[0, G_HV:2 * G_HV] for a in (dalog, ddtb))
    g["nw"] = jnp.sum(dnw, axis=0).reshape(G_K)
    return dhn, g


_MIXERS = {0: (ssd_mixer_fwd, ssd_mixer_bwd), 1: (gla_mixer_fwd, gla_mixer_bwd), 2: (gdn_mixer_fwd, gdn_mixer_bwd)}


def layer_fwd(i, x, mem, weights_of, nb):
    t = f"l{i}"
    wm = weights_of(i, 0, x)
    hn = rms_fwd(x, wm["ln_mix"], f"{t}_ln_mix")
    mix, s_mix = _MIXERS[i % 3][0](f"{t}_mix", hn, wm["mix"], nb)
    x1 = mm(mix, wm["mix"]["wout"], res=x, name=f"{t}_mix_out")
    w = weights_of(i, 1, x1)
    hx = rms_fwd(x1, w["ln_xattn"], f"{t}_ln_xattn")
    mn = rms_fwd(mem, w["ln_mem"], f"{t}_ln_mem")
    q = _proj(hx, w["xq"], f"{t}_xa_q", out_dtype=bf16)
    k = _proj(mn, w["xk"], f"{t}_xa_k", out_dtype=bf16)
    v = _proj(mn, w["xv"], f"{t}_xa_v", out_dtype=bf16)
    o = xattn_fwd(q, k, v, nb, f"{t}_xattn")
    x2 = mm(o, w["xo"], res=x1, name=f"{t}_xa_o")
    hf = rms_fwd(x2, w["ln_ffn"], f"{t}_ln_ffn")
    gate = _proj(hf, w["fg"], f"{t}_ffn_gate", out_dtype=bf16)
    up = _proj(hf, w["fu"], f"{t}_ffn_up", out_dtype=bf16)
    act = cols_call(f"{t}_ffn_act", _ffn_act, [gate, up], [w["fcw"], w["fcb"]], [bf16], nb=nb, ct=_CONV_CT,
                    ncol=D_FF // _CONV_CT)[0]
    x3 = mm(act, w["fd"], res=x2, name=f"{t}_ffn_down")
    return x3, (wm, w, x, s_mix, x1, hx, mn, q, k, v, o, x2, hf, gate, up, act)


def layer_bwd(i, saved, dx, mem, nb, token, grads_done):
    t = f"l{i}b"
    wm, w, x, s_mix, x1, hx, mn, q, k, v, o, x2, hf, gate, up, act = saved
    if token is not None:
        w = dict(w, fd=w["fd"] + token[0, 0].astype(w["fd"].dtype))
    g = {}
    g["fd"] = mm(act, dx, ta=True, out_dtype=bf16, name=f"{t}_dwd")
    dact = mm(dx, w["fd"], tb=True, out_dtype=bf16, name=f"{t}_dact")
    dgate, dup, g["fcw"], g["fcb"] = cols_call(f"{t}_ffn_act", _ffn_act, [gate, up], [w["fcw"], w["fcb"]], [bf16], nb=nb,
                                               ct=_CONV_CT, ncol=D_FF // _CONV_CT, dseed=[dact])
    dhf, (g["fg"], g["fu"]) = _proj_bwd(f"{t}_ffn", hf, [(dgate, w["fg"]), (dup, w["fu"])])
    dx, g["ln_ffn"] = rms_bwd(x2, w["ln_ffn"], dhf, dx, f"{t}_ln_ffn")
    g["xo"] = mm(o, dx, ta=True, out_dtype=bf16, name=f"{t}_dwo")
    do = mm(dx, w["xo"], tb=True, out_dtype=bf16, name=f"{t}_do")
    dq, dk, dv = xattn_bwd(q, k, v, do, nb, f"{t}_xattn")
    dhx, (g["xq"],) = _proj_bwd(f"{t}_xq", hx, [(dq, w["xq"])])
    dmn, (g["xk"], g["xv"]) = _proj_bwd(f"{t}_xkv", mn, [(dk, w["xk"]), (dv, w["xv"])])
    _, g["ln_mem"] = rms_bwd(mem, w["ln_mem"], dmn, None, f"{t}_ln_mem")
    dx, g["ln_xattn"] = rms_bwd(x1, w["ln_xattn"], dhx, dx, f"{t}_ln_xattn")
    token = grads_done(i, 1, g, dx) if grads_done else None
    mixw = wm["mix"] if token is None else dict(wm["mix"], wout=wm["mix"]["wout"] + token[0, 0].astype(wm["mix"]["wout"].dtype))
    dhn, g["mix"] = _MIXERS[i % 3][1](f"{t}_mix", s_mix, dx, mixw, nb)
    dx, g["ln_mix"] = rms_bwd(x, wm["ln_mix"], dhn, dx, f"{t}_ln_mix")
    token = grads_done(i, 0, g, dx) if grads_done else None
    return dx, g, token


def local_step(x, mem, target, weights_of, final_norm, nb, grads_done=None):
    saved = []
    for i in range(DEPTH):
        x, s = layer_fwd(i, x, mem, weights_of, nb)
        saved.append(s)
    dx, loss, dfinal = loss_head(x, target, final_norm)
    grads = [None] * DEPTH
    token = None
    for i in reversed(range(DEPTH)):
        dx, grads[i], token = layer_bwd(i, saved[i], dx, mem, nb, token, grads_done)
    return loss, dx, grads, dfinal


WEIGHTS = ["ln_mix", "ln_xattn", "ln_mem", "ln_ffn", "final_norm", "m_in_w", "m_conv_w", "m_conv_b", "m_dt_bias", "m_a_log",
           "m_d", "m_norm_w", "m_out_w", "h_in_w", "h_lower_bounds", "h_norm_w", "h_out_w", "g_in_w", "g_conv_w", "g_a_log",
           "g_dt_bias", "g_norm_w", "g_out_w", "xa_q", "xa_kv", "xa_o", "f_up", "f_conv_w", "f_conv_b", "f_down"]
SHARD_AXIS = {"m_in_w": 2, "m_conv_w": 2, "m_conv_b": 1, "m_norm_w": 1, "m_out_w": 1, "h_in_w": 2, "h_out_w": 1, "g_in_w": 2,
              "g_conv_w": 2, "g_out_w": 1, "xa_q": 1, "xa_kv": 2, "xa_o": 1, "f_up": 2, "f_conv_w": 2, "f_down": 1}
MATRICES = ["m_in_w", "m_out_w", "h_in_w", "h_out_w", "g_in_w", "g_out_w", "xa_q", "xa_kv", "xa_o", "f_up", "f_down"]
SMALL_SHARDED = [n for n in WEIGHTS if n in SHARD_AXIS and n not in MATRICES]
REPLICATED = [n for n in WEIGHTS if n not in SHARD_AXIS]
_MIXER_PREFIX = {0: "m", 1: "h", 2: "g"}


def layer_weight_names(i, part):
    if part == 0:
        p = _MIXER_PREFIX[i % 3]
        return [(n, i // 3) for n in WEIGHTS if n in SHARD_AXIS and n.startswith(p + "_")]
    return [(n, i) for n in ("xa_q", "xa_kv", "xa_o", "f_up", "f_conv_w", "f_down")]


def _cols(st, lo, hi):
    ns = st.shape[-1]
    parts = []
    for j in range(NCHIP):
        a, b = max(lo, j * ns), min(hi, (j + 1) * ns)
        if a < b:
            parts.append(st[j][..., a - j * ns:b - j * ns])
    return parts[0] if len(parts) == 1 else jnp.concatenate(parts, axis=-1)


def _col_shards(pieces, ns):
    full = jnp.concatenate(pieces, axis=-1)
    return [full[..., j * ns:(j + 1) * ns] for j in range(NCHIP)]


def _rows(st):
    return st.reshape(st.shape[0] * st.shape[1], st.shape[2])


def prep_layer(i, part, G, R, lb):
    row = lambda a: a.reshape(1, -1)
    p, k = _MIXER_PREFIX[i % 3], i // 3
    if part == 1:
        kv, fup = G["xa_kv"], G["f_up"]
        return dict(ln_xattn=R["ln_xattn"][i:i + 1], ln_mem=R["ln_mem"][i:i + 1], ln_ffn=R["ln_ffn"][i:i + 1],
                    xq=_rows(G["xa_q"]), xk=(kv, (0, 2)), xv=(kv, (2, 2)), xo=_rows(G["xa_o"]), fg=(fup, (0, 2)), fu=(fup, (2, 2)),
                    fcw=_cols(G["f_conv_w"], 0, D_FF), fcb=R["f_conv_b"][i:i + 1], fd=_rows(G["f_down"]))
    layer = dict(ln_mix=R["ln_mix"][i:i + 1])
    inw, wout = G[p + "_in_w"], _rows(G[p + "_out_w"])
    if p == "m":
        cw, cb = G["m_conv_w"], G["m_conv_b"]
        a, b, c = M_INNER, M_INNER + M_G * M_N, M_CONV
        layer["mix"] = dict(
            wz=_cols(inw, 0, M_INNER), wx=_cols(inw, M_INNER, M_INNER + a), wb=_cols(inw, M_INNER + a, M_INNER + b),
            wc=_cols(inw, M_INNER + b, M_MAIN), wdt=_pad_cols(_cols(inw, M_MAIN, M_IN)),
            cwx=_cols(cw, 0, a), cwb=_cols(cw, a, b), cwc=_cols(cw, b, c),
            cbx=row(_cols(cb, 0, a)), cbb=row(_cols(cb, a, b)), cbc=row(_cols(cb, b, c)),
            dtb=_pad_row(R["m_dt_bias"][k]), alog=_pad_row(R["m_a_log"][k]), dsk=_pad_row(R["m_d"][k]),
            nw=row(_cols(G["m_norm_w"], 0, M_INNER)), wout=wout)
    elif p == "h":
        layer["mix"] = dict(wq=(inw, (0, 1)), wf=(inw, (1, 1)), wi=(inw, (2, 1)), wg=(inw, (3, 1)),
                            lb=lb[i:i + 1], nw=row(R["h_norm_w"][k]), wout=wout)
    else:
        cw = G["g_conv_w"]
        layer["mix"] = dict(
            wq=_cols(inw, 0, D), wk=_cols(inw, D, 2 * D), wv=_cols(inw, 2 * D, G_CONV), wz=_cols(inw, G_CONV, G_MAIN),
            wba=_pad_cols(_cols(inw, G_MAIN, G_IN)), cwq=_cols(cw, 0, D), cwk=_cols(cw, D, 2 * D), cwv=_cols(cw, 2 * D, G_CONV),
            alog=_pad_row(R["g_a_log"][k], G_HV), dtb=_pad_row(R["g_dt_bias"][k], G_HV),
            nw=row(R["g_norm_w"][k]), wout=wout)
    return layer


def matrix_grad_parts(i, part, g):
    by_rows = lambda a: a.reshape(NCHIP, a.shape[0] // NCHIP, a.shape[1])
    if part == 1:
        return {"xa_q": by_rows(g["xq"]), "xa_kv": g["xk"], "xa_o": by_rows(g["xo"]), "f_up": g["fg"], "f_down": by_rows(g["fd"])}
    p = _MIXER_PREFIX[i % 3]
    m = g["mix"]
    out = {p + "_out_w": by_rows(m["wout"])}
    if p == "m":
        out["m_in_w"] = jnp.stack(_col_shards([m["wz"], m["wx"], m["wb"], m["wc"], m["wdt"]], M_IN // NCHIP))
    elif p == "h":
        out["h_in_w"] = m["wq"]
    else:
        out["g_in_w"] = jnp.stack(_col_shards([m["wq"], m["wk"], m["wv"], m["wz"], m["wba"]], G_IN // NCHIP))
    return out


def small_grads(grads, dfinal, hlb):
    cat = lambda xs: jnp.concatenate(xs, axis=1)
    out = {k: jnp.concatenate([g[k] for g in grads], axis=0) for k in ("ln_mix", "ln_xattn", "ln_mem", "ln_ffn")}
    out["final_norm"] = dfinal.reshape(D)
    out["f_conv_w"] = jnp.stack([g["fcw"] for g in grads])
    out["f_conv_b"] = jnp.concatenate([g["fcb"] for g in grads], axis=0)
    ms = [g["mix"] for i, g in enumerate(grads) if i % 3 == 0]
    out["m_conv_w"] = jnp.stack([cat([m["cwx"], m["cwb"], m["cwc"]]) for m in ms])
    out["m_conv_b"] = jnp.concatenate([cat([m["cbx"], m["cbb"], m["cbc"]]) for m in ms], axis=0)
    out["m_dt_bias"] = jnp.stack([m["dtb"] for m in ms])
    out["m_a_log"] = jnp.stack([m["alog"] for m in ms])
    out["m_d"] = jnp.stack([m["dsk"] for m in ms])
    out["m_norm_w"] = jnp.stack([m["nw"] for m in ms])
    hs = [(i, g["mix"]) for i, g in enumerate(grads) if i % 3 == 1]
    lb_rows = dict(hs)
    dlb = jnp.concatenate([lb_rows[i]["lb"] if i in lb_rows else jnp.zeros((1, D), f32) for i in range(DEPTH)], axis=0)
    out["h_lower_bounds"] = lower_bounds_bwd(hlb, dlb)
    out["h_norm_w"] = jnp.stack([m["nw"] for _, m in hs])
    gs = [g["mix"] for i, g in enumerate(grads) if i % 3 == 2]
    out["g_conv_w"] = jnp.stack([cat([m["cwq"], m["cwk"], m["cwv"]]) for m in gs])
    out["g_a_log"] = jnp.stack([m["alog"] for m in gs])
    out["g_dt_bias"] = jnp.stack([m["dtb"] for m in gs])
    out["g_norm_w"] = jnp.stack([m["nw"] for m in gs])
    return out


_HBM = pl.BlockSpec(memory_space=pltpu.HBM)


def _place():
    x, y, c = lax.axis_index("x"), lax.axis_index("y"), lax.axis_index("c")
    chips = [(1 - x, y), (x, 1 - y), (1 - x, 1 - y)]
    return x, y, c, chips


_SEM = pl.BlockSpec(memory_space=pltpu.SEMAPHORE)
_ANY = pl.BlockSpec(memory_space=pl.ANY)
_SPLIT = pltpu.CompilerParams(has_side_effects=pltpu.SideEffectType.DATAFLOW_SIDE_EFFECTING)


def _hbm(a):
    return pltpu.with_memory_space_constraint(a, pltpu.HBM)


def _split_start(name, srcs, lands, dep, copies):
    n = len(srcs)

    def body(*refs):
        src_refs, land_refs = refs[:n], refs[n:2 * n]
        send_sems, recv_sems = refs[2 * n + 1], refs[2 * n + 2]
        token = refs[-1]
        for cp in copies(src_refs, land_refs, send_sems, recv_sems):
            cp.start()
        token[...] = jnp.zeros_like(token)

    thru = [pltpu.HBM(a.shape, a.dtype) for a in list(srcs) + list(lands)]
    out = pl.pallas_call(
        body, name=name, in_specs=[_HBM] * (2 * n) + [_ANY],
        out_specs=[_SEM, _SEM] + [_HBM] * (2 * n) + [pl.BlockSpec(memory_space=pltpu.VMEM)],
        out_shape=[pltpu.SemaphoreType.DMA((3 * n,)), pltpu.SemaphoreType.DMA((3 * n,))] + thru + [_S((8, 128), f32)],
        input_output_aliases={t: 2 + t for t in range(2 * n)}, compiler_params=_SPLIT,
    )(*[_hbm(a) for a in srcs], *[_hbm(a) for a in lands], dep)
    return out[0], out[1], out[2:2 + n], out[2 + n:2 + 2 * n], out[-1]


def _split_wait(name, started, after, copies):
    send_sems, recv_sems, srcs, lands, _ = started
    n = len(srcs)

    def body(*refs):
        src_refs, land_refs = refs[:n], refs[n:2 * n]
        s_sems, r_sems = refs[2 * n], refs[2 * n + 1]
        for cp in copies(src_refs, land_refs, s_sems, r_sems):
            cp.wait_send()
            cp.wait_recv()

    out = pl.pallas_call(
        body, name=name, in_specs=[_HBM] * (2 * n) + [_SEM, _SEM, _ANY], out_specs=[_HBM] * (2 * n),
        out_shape=[pltpu.HBM(a.shape, a.dtype) for a in list(srcs) + list(lands)],
        input_output_aliases={t: t for t in range(2 * n)}, compiler_params=_SPLIT,
    )(*srcs, *lands, send_sems, recv_sems, after)
    return out[:n], out[n:]


def _gather_copies(arrive):
    def copies(src_refs, land_refs, send_sems, recv_sems):
        x, y, c, chips = _place()
        out = []
        for t, (s, l) in enumerate(zip(src_refs, land_refs, strict=True)):
            for j, (px, py) in enumerate(chips):
                slot = 2 * px + py if arrive else 2 * x + y
                out.append(pltpu.make_async_remote_copy(src_ref=s, dst_ref=l.at[slot], send_sem=send_sems.at[3 * t + j],
                                                        recv_sem=recv_sems.at[3 * t + j], device_id=(px, py, c), device_id_type=MESH))
        return out
    return copies


def gather_start(name, tensors, me, dep):
    lands = [lax.dynamic_update_index_in_dim(jnp.zeros((NCHIP,) + a.shape, a.dtype), a, me, 0) for a in tensors]
    return _split_start(name, tensors, lands, dep, _gather_copies(False))


def gather_wait(name, started, after):
    return _split_wait(name, started, after, _gather_copies(True))


def _scatter_copies(src_refs, land_refs, send_sems, recv_sems):
    x, y, c, chips = _place()
    out = []
    for t, (s, l) in enumerate(zip(src_refs, land_refs, strict=True)):
        for j, (px, py) in enumerate(chips):
            out.append(pltpu.make_async_remote_copy(src_ref=s.at[2 * px + py], dst_ref=l.at[j], send_sem=send_sems.at[3 * t + j],
                                                    recv_sem=recv_sems.at[3 * t + j], device_id=(px, py, c), device_id_type=MESH))
    return out


def scatter_start(name, parts, dep):
    lands = [lax.empty((3,) + a.shape[1:], a.dtype) for a in parts]
    return _split_start(name, parts, lands, dep, _scatter_copies)


def scatter_wait(name, started, after):
    return _split_wait(name, started, after, _scatter_copies)


def sum_parts(name, part, land, me):
    shape = land.shape[1:]
    c = shape[-1]
    r = land.size // (3 * c)
    tm = _tile(r, (256, 128, 64, 32, 16, 8))

    def body(me_ref, p_ref, l_ref, o_ref):
        o_ref[...] = p_ref[...].astype(f32) + l_ref[0].astype(f32) + l_ref[1].astype(f32) + l_ref[2].astype(f32)

    grid_spec = pltpu.PrefetchScalarGridSpec(
        num_scalar_prefetch=1, grid=(r // tm,),
        in_specs=[pl.BlockSpec((None, tm, c), lambda i, me_ref: (me_ref[0], i, 0)),
                  pl.BlockSpec((3, tm, c), lambda i, me_ref: (0, i, 0))],
        out_specs=pl.BlockSpec((tm, c), lambda i, me_ref: (i, 0)))
    out = pl.pallas_call(body, name=name, grid_spec=grid_spec, out_shape=_S((r, c), f32), compiler_params=_cp())(
        me.reshape(1).astype(jnp.int32), part.reshape(NCHIP, r, c), land.reshape(3, r, c))
    return out.reshape(shape)


def _swap_copies(src_refs, land_refs, send_sems, recv_sems):
    x, y, c, _ = _place()
    return [pltpu.make_async_remote_copy(src_ref=s, dst_ref=l, send_sem=send_sems.at[3 * t], recv_sem=recv_sems.at[3 * t],
                                         device_id=(x, y, 1 - c), device_id_type=MESH)
            for t, (s, l) in enumerate(zip(src_refs, land_refs, strict=True))]


def swap_start(name, tensors, dep):
    return _split_start(name, tensors, [lax.empty(a.shape, a.dtype) for a in tensors], dep, _swap_copies)


def swap_wait(name, started, after):
    return _split_wait(name, started, after, _swap_copies)


def allreduce_small(v):
    r, n = v.shape

    def body(x_ref, out_ref, gat, send_sems, recv_sems, local_sem):
        x, y, c, chips = _place()
        me, sibling = (x, y, c), (x, y, 1 - c)

        def rows(px, py, pc):
            return gat.at[pl.ds((4 * px + 2 * py + pc) * r, r), :]

        def copy(k, block, to, src=None):
            return pltpu.make_async_remote_copy(src_ref=rows(*block) if src is None else src, dst_ref=rows(*block),
                                                send_sem=send_sems.at[k], recv_sem=recv_sems.at[k], device_id=to,
                                                device_id_type=MESH)

        mine = pltpu.make_async_copy(x_ref, rows(*me), local_sem)
        mine.start()
        first = [copy(0, me, sibling, src=x_ref)] + [copy(1 + j, me, (*chip, c), src=x_ref) for j, chip in enumerate(chips)]
        for cp in first:
            cp.start()
        passed = [copy(4 + j, (*chip, c), sibling) for j, chip in enumerate(chips)]
        for j, chip in enumerate(chips):
            copy(1 + j, (*chip, c), me).wait_recv()
            passed[j].start()
        copy(0, sibling, me).wait_recv()
        for j, chip in enumerate(chips):
            copy(4 + j, (*chip, 1 - c), me).wait_recv()
        for cp in first + passed:
            cp.wait_send()
        mine.wait()
        acc = gat[0:r, :]
        for d in range(1, 8):
            acc = acc + gat[d * r:(d + 1) * r, :]
        out_ref[...] = acc

    vm = pl.BlockSpec(memory_space=pltpu.VMEM)
    return pl.pallas_call(
        body, name="allreduce_small", in_specs=[vm], out_specs=vm, out_shape=_S((r, n), v.dtype),
        scratch_shapes=[pltpu.VMEM((8 * r, n), v.dtype), pltpu.SemaphoreType.DMA((7,)), pltpu.SemaphoreType.DMA((7,)),
                        pltpu.SemaphoreType.DMA],
        compiler_params=_cp())(v)


SMALL_ROW = 1024


def kernel(x, mem, ln_mix, ln_xattn, ln_mem, ln_ffn, final_norm, m_in_w, m_conv_w, m_conv_b, m_dt_bias, m_a_log, m_d, m_norm_w, m_out_w, h_in_w, h_lower_bounds, h_norm_w, h_out_w, g_in_w, g_conv_w, g_a_log, g_dt_bias, g_norm_w, g_out_w, xa_q, xa_kv, xa_o, f_up, f_conv_w, f_conv_b, f_down, loss_target, m_ln_mix, m_ln_xattn, m_ln_mem, m_ln_ffn, m_final_norm, m_m_in_w, m_m_conv_w, m_m_conv_b, m_m_dt_bias, m_m_a_log, m_m_d, m_m_norm_w, m_m_out_w, m_h_in_w, m_h_lower_bounds, m_h_norm_w, m_h_out_w, m_g_in_w, m_g_conv_w, m_g_a_log, m_g_dt_bias, m_g_norm_w, m_g_out_w, m_xa_q, m_xa_kv, m_xa_o, m_f_up, m_f_conv_w, m_f_conv_b, m_f_down, v_ln_mix, v_ln_xattn, v_ln_mem, v_ln_ffn, v_final_norm, v_m_in_w, v_m_conv_w, v_m_conv_b, v_m_dt_bias, v_m_a_log, v_m_d, v_m_norm_w, v_m_out_w, v_h_in_w, v_h_lower_bounds, v_h_norm_w, v_h_out_w, v_g_in_w, v_g_conv_w, v_g_a_log, v_g_dt_bias, v_g_norm_w, v_g_out_w, v_xa_q, v_xa_kv, v_xa_o, v_f_up, v_f_conv_w, v_f_conv_b, v_f_down):
    local = dict(zip(WEIGHTS, (ln_mix, ln_xattn, ln_mem, ln_ffn, final_norm, m_in_w, m_conv_w, m_conv_b, m_dt_bias, m_a_log, m_d, m_norm_w, m_out_w, h_in_w, h_lower_bounds, h_norm_w, h_out_w, g_in_w, g_conv_w, g_a_log, g_dt_bias, g_norm_w, g_out_w, xa_q, xa_kv, xa_o, f_up, f_conv_w, f_conv_b, f_down), strict=True))
    mom_m = dict(zip(WEIGHTS, (m_ln_mix, m_ln_xattn, m_ln_mem, m_ln_ffn, m_final_norm, m_m_in_w, m_m_conv_w, m_m_conv_b, m_m_dt_bias, m_m_a_log, m_m_d, m_m_norm_w, m_m_out_w, m_h_in_w, m_h_lower_bounds, m_h_norm_w, m_h_out_w, m_g_in_w, m_g_conv_w, m_g_a_log, m_g_dt_bias, m_g_norm_w, m_g_out_w, m_xa_q, m_xa_kv, m_xa_o, m_f_up, m_f_conv_w, m_f_conv_b, m_f_down), strict=True))
    mom_v = dict(zip(WEIGHTS, (v_ln_mix, v_ln_xattn, v_ln_mem, v_ln_ffn, v_final_norm, v_m_in_w, v_m_conv_w, v_m_conv_b, v_m_dt_bias, v_m_a_log, v_m_d, v_m_norm_w, v_m_out_w, v_h_in_w, v_h_lower_bounds, v_h_norm_w, v_h_out_w, v_g_in_w, v_g_conv_w, v_g_a_log, v_g_dt_bias, v_g_norm_w, v_g_out_w, v_xa_q, v_xa_kv, v_xa_o, v_f_up, v_f_conv_w, v_f_conv_b, v_f_down), strict=True))
    nb, seq, _ = x.shape
    me = 2 * lax.axis_index("x") + lax.axis_index("y")

    repl = {n: local[n] for n in REPLICATED}
    lb = lower_bounds_fwd(repl["h_lower_bounds"])
    nstage = 2 * DEPTH
    names = [layer_weight_names(s // 2, s % 2) for s in range(nstage)]
    cast = lambda n, a: a.astype(bf16) if n in MATRICES else a
    flying = {0: gather_start("gather_start_s0", [cast(n, local[n][k]) for n, k in names[0]], me, x)}
    tok0 = flying[0][4][0, 0]
    shards = [None] + [[cast(n, local[n][k] + tok0) for n, k in names[s]] for s in range(1, nstage)]

    def weights_of(i, part, x_in):
        s = 2 * i + part
        gathered = gather_wait(f"gather_wait_s{s}", flying.pop(s), shards[-1][-1] if s == 0 else x_in)[1]
        w = prep_layer(i, part, {n: g for (n, _), g in zip(names[s], gathered, strict=True)}, repl, lb)
        if s + 1 < nstage:
            flying[s + 1] = gather_start(f"gather_start_s{s + 1}", shards[s + 1], me, gathered[0])
            norm = "ln_mix" if part == 0 else "ln_xattn"
            w[norm] = w[norm] + flying[s + 1][4][0, 0]
        return w

    scattering, swapping = {}, []

    def landed(s, after):
        part_names, started = scattering.pop(s)
        sent, got = scatter_wait(f"scatter_wait_s{s}", started, after)
        sums = [sum_parts(f"sum_s{s}_{n}", p, l, me) for n, p, l in zip(part_names, sent, got, strict=True)]
        swapping.append((s, part_names, swap_start(f"swap_start_s{s}", sums, sums[0])))

    def grads_done(i, part, g, dx_i):
        s = 2 * i + part
        parts = matrix_grad_parts(i, part, g)
        scattering[s] = (list(parts), scatter_start(f"scatter_start_s{s}", list(parts.values()), dx_i))
        token = scattering[s][1][4]
        if s + 1 in scattering:
            landed(s + 1, dx_i)
        return token

    loss, dx, lgrads, dfinal = local_step(x.reshape(nb * seq, D), mem.reshape(nb * N_MEM, D), loss_target.reshape(nb * seq, D),
                                          weights_of, repl["final_norm"].reshape(1, D), nb, grads_done)
    grads = small_grads(lgrads, dfinal, repl["h_lower_bounds"])

    small_names = REPLICATED + SMALL_SHARDED
    flat = jnp.concatenate([grads[n].astype(f32).reshape(-1) for n in small_names] + [loss[0, 0:1] + scattering[0][1][4][0, 0]])
    rows = -(-flat.shape[0] // (8 * SMALL_ROW)) * 8
    flat = jnp.pad(flat, (0, rows * SMALL_ROW - flat.shape[0])).reshape(rows, SMALL_ROW)
    red = allreduce_small(flat).reshape(-1)
    gsum, off = {}, 0
    for n in small_names:
        size = grads[n].size
        g = red[off:off + size].reshape(grads[n].shape)
        off += size
        if n in SHARD_AXIS:
            ax = SHARD_AXIS[n]
            w = g.shape[ax] // NCHIP
            g = lax.dynamic_slice_in_dim(g, me * w, w, axis=ax)
        gsum[n] = g
    loss_out = red[off]

    outs = {}
    for n in small_names:
        outs[n] = adamw(local[n], gsum[n].reshape(local[n].shape), mom_m[n], mom_v[n], f"adamw_{n}")
    mine, theirs = {n: {} for n in MATRICES}, {n: {} for n in MATRICES}

    def swapped(after):
        while swapping:
            s, part_names, started = swapping.pop()
            sent, got = swap_wait(f"swap_wait_s{s}", started, after)
            for n, a, b in zip(part_names, sent, got, strict=True):
                mine[n][s // 2], theirs[n][s // 2] = a, b

    def update(n):
        g_mine, g_theirs = (jnp.stack([d[n][i] for i in sorted(d[n])]) for d in (mine, theirs))
        outs[n] = adamw(local[n], g_mine, mom_m[n], mom_v[n], f"adamw_{n}", g2=g_theirs)

    last = [n for n, _ in names[0] if n in MATRICES]
    swapped(dx)
    for n in MATRICES:
        if n not in last:
            update(n)
    landed(0, outs["f_down"][1])
    swapped(outs["f_down"][1])
    for n in last:
        update(n)
    res = [loss_out, dx.reshape(nb, seq, D)]
    for k in range(4):
        res += [outs[n][k] for n in WEIGHTS]
    return tuple(res)
```

```python
import functools

import jax
import jax.numpy as jnp
from jax import lax
from jax.experimental import pallas as pl
from jax.experimental.pallas import tpu as pltpu

f32 = jnp.float32
bf16 = jnp.bfloat16
HIGHEST = lax.Precision.HIGHEST
MESH = pl.DeviceIdType.MESH

D = 1024
DEPTH = 4
EPS = 1e-6
N_MEM = 256
M_INNER, M_P, M_H, M_G, M_N, M_Q = 2048, 64, 32, 8, 128, 64
M_CONV = M_INNER + 2 * M_G * M_N
M_MAIN = M_INNER + M_CONV
M_IN = M_MAIN + M_H
H_H, H_K, H_Q = 8, 128, 32
G_HV, G_HK, G_K, G_Q = 16, 8, 128, 64
G_CONV, G_VAL = 4096, 2048
G_MAIN = G_CONV + G_VAL
G_IN = G_MAIN + 2 * G_HV
X_H, X_D = 4, 256
D_FF = 2816
ADAM_LR, ADAM_B1, ADAM_B2, ADAM_EPS, ADAM_WD, ADAM_STEP = 0.001, 0.9, 0.999, 1e-08, 0.01, 10
VMEM_LIMIT = 56 * 1024 * 1024
NCHIP = 4


def _cp(**kw):
    return pltpu.CompilerParams(vmem_limit_bytes=VMEM_LIMIT, **kw)


def _S(shape, dtype):
    return jax.ShapeDtypeStruct(tuple(shape), dtype)


def _dg(a, b, ca, cb, prec=None):
    return lax.dot_general(a, b, (((ca,), (cb,)), ((), ())), precision=prec, preferred_element_type=f32)


def _hdot(a, b, ca=1, cb=0, prec=lax.Precision.HIGH):
    return _dg(a.astype(f32), b.astype(f32), ca, cb, prec)


def _bdot_raw(a, b, ca, cb):
    return _dg(a.astype(bf16), b.astype(bf16), ca, cb)


@functools.partial(jax.custom_vjp, nondiff_argnums=(2, 3))
def _bdot(a, b, ca, cb):
    return _bdot_raw(a, b, ca, cb)


def _bdot_fwd(a, b, ca, cb):
    return _bdot_raw(a, b, ca, cb), (a, b)


def _bdot_bwd(ca, cb, res, g):
    a, b = res
    if ca == 1:
        da = _bdot_raw(g, b, 1, 1 if cb == 0 else 0)
    else:
        da = _bdot_raw(b, g, 1 if cb == 0 else 0, 1)
    if cb == 0:
        db = _bdot_raw(a, g, 0 if ca == 1 else 1, 0)
    else:
        db = _bdot_raw(g, a, 0, 0 if ca == 1 else 1)
    return da.astype(a.dtype), db.astype(b.dtype)


_bdot.defvjp(_bdot_fwd, _bdot_bwd)


def _shift_down_raw(x, k):
    r = lax.broadcasted_iota(jnp.int32, x.shape, 0)
    return jnp.where(r >= k, pltpu.roll(x, k, 0), 0.0)


def _shift_up_raw(x, k):
    n = x.shape[0]
    r = lax.broadcasted_iota(jnp.int32, x.shape, 0)
    return jnp.where(r < n - k, pltpu.roll(x, n - k, 0), 0.0)


@functools.partial(jax.custom_vjp, nondiff_argnums=(1,))
def _shift_down(x, k):
    return _shift_down_raw(x, k)


_shift_down.defvjp(lambda x, k: (_shift_down_raw(x, k), None), lambda k, _, g: (_shift_up_raw(g, k),))


def _rms(x, w):
    return x * lax.rsqrt(jnp.mean(x * x, axis=-1, keepdims=True) + EPS) * w


def _silu(x):
    return x * jax.nn.sigmoid(x)


def _masks(q):
    r = lax.broadcasted_iota(jnp.int32, (q, q), 0)
    c = lax.broadcasted_iota(jnp.int32, (q, q), 1)
    return r >= c, r > c


def _colvec(row):
    return jnp.transpose(jnp.broadcast_to(row, (8, row.shape[1])))[:, 0:1]


def _tile(n, cands):
    for c in cands:
        if n % c == 0:
            return c
    return n


def mm(a, b, *, ta=False, tb=False, bsel=None, out_stack=None, out_slots=None, into=None, res=None, out_dtype=f32, name):
    m, k = (a.shape[1], a.shape[0]) if ta else a.shape
    ca, cb = (0 if ta else 1), (1 if tb else 0)
    tm = _tile(m, (1408, 512, 256, 128) if ta else (512, 256, 128))
    if bsel is not None:
        s0, cnt = bsel
        ns = b.shape[2]
        if tb:
            n, tn, tk = b.shape[1], b.shape[1], ns
            b_spec = pl.BlockSpec((None, tn, ns), lambda i, j, kk: (s0 + kk, j, 0))
        else:
            n, tn, tk = cnt * ns, ns, k
            b_spec = pl.BlockSpec((None, tk, ns), lambda i, j, kk: (s0 + j, kk, 0))
    else:
        n = b.shape[0] if tb else b.shape[1]
        tn = out_stack if out_stack else (n if n <= 2816 else _tile(n, (2048, 1024, 512, 256, 128)))
        tk = k if (k <= 4096 and not ta) else _tile(k, (1024, 512, 256, 128))
        b_spec = pl.BlockSpec((tn, tk), lambda i, j, kk: (j, kk)) if tb else pl.BlockSpec((tk, tn), lambda i, j, kk: (kk, j))
    nk = k // tk
    if out_stack:
        total, first = out_slots if out_slots else (n // tn, 0)
        out_spec = pl.BlockSpec((None, tm, tn), lambda i, j, kk: (first + j, i, 0))
        out_shape = _S((total, m, tn), out_dtype)
    else:
        out_spec = pl.BlockSpec((tm, tn), lambda i, j, kk: (i, j))
        out_shape = _S((m, n), out_dtype)

    def body(*refs):
        a_ref, b_ref = refs[:2]
        r_ref = refs[2] if res is not None else None
        o_ref, acc = refs[-2:]
        kk = pl.program_id(2)

        @pl.when(kk == 0)
        def _():
            acc[...] = jnp.zeros_like(acc)

        acc[...] += _bdot_raw(a_ref[...], b_ref[...], ca, cb)

        @pl.when(kk == nk - 1)
        def _():
            v = acc[...]
            if r_ref is not None:
                v = v + r_ref[...]
            o_ref[...] = v.astype(o_ref.dtype)

    a_spec = pl.BlockSpec((tk, tm), lambda i, j, kk: (kk, i)) if ta else pl.BlockSpec((tm, tk), lambda i, j, kk: (i, kk))
    in_specs = [a_spec, b_spec]
    args = [a, b]
    if res is not None:
        in_specs.append(pl.BlockSpec((tm, tn), lambda i, j, kk: (i, j)))
        args.append(res)
    aliases = {}
    if into is not None:
        aliases = {len(args): 0}
        in_specs.append(pl.BlockSpec(memory_space=pl.ANY))
        args.append(into)
    return pl.pallas_call(
        body, name=name, grid=(m // tm, n // tn, nk), in_specs=in_specs, out_specs=out_spec, out_shape=out_shape,
        scratch_shapes=[pltpu.VMEM((tm, tn), f32)], input_output_aliases=aliases, compiler_params=_cp())(*args)


def rows_call(name, fn, rows, pars, row_out, acc_out=(), tm=512):
    t = rows[0].shape[0]
    tm = min(tm, t)
    assert t % tm == 0, (name, t, tm)
    nr, npar, nro = len(rows), len(pars), len(row_out)

    def body(*refs):
        rv = [r[...] for r in refs[:nr]]
        pv = [r[...] for r in refs[nr:nr + npar]]
        ro_refs = refs[nr + npar:nr + npar + nro]
        ao_refs = refs[nr + npar + nro:]
        ro, ao = fn(*rv, *pv)
        for r, v in zip(ro_refs, ro, strict=True):
            r[...] = v.astype(r.dtype)
        if ao_refs:
            @pl.when(pl.program_id(0) == 0)
            def _():
                for r in ao_refs:
                    r[...] = jnp.zeros_like(r)
            for r, v in zip(ao_refs, ao, strict=True):
                r[...] += v.astype(r.dtype)

    in_specs = [pl.BlockSpec((tm, r.shape[1]), lambda i: (i, 0)) for r in rows]
    in_specs += [pl.BlockSpec(p.shape, lambda i: (0, 0)) for p in pars]
    out_specs = [pl.BlockSpec((tm, c), lambda i: (i, 0)) for c, _ in row_out]
    out_specs += [pl.BlockSpec(s, lambda i: (0, 0)) for s, _ in acc_out]
    out_shape = [_S((t, c), dt) for c, dt in row_out] + [_S(s, dt) for s, dt in acc_out]
    return pl.pallas_call(body, name=name, grid=(t // tm,), in_specs=in_specs, out_specs=out_specs,
                          out_shape=out_shape, compiler_params=_cp())(*rows, *pars)


def rms_fwd(x, w, name):
    return rows_call(name, lambda xv, wv: ((_rms(xv, wv),), ()), [x], [w], [(x.shape[1], bf16)])[0]


def rms_bwd(x, w, dy, dres, name):
    def fn(*a):
        if dres is None:
            xv, dyv, wv = a
        else:
            xv, dyv, drv, wv = a
        _, vjp = jax.vjp(_rms, xv, wv)
        dx, dw = vjp(dyv.astype(f32))
        if dres is not None:
            dx = dx + drv
        return (dx,), (dw,)
    rows = [x, dy] + ([] if dres is None else [dres])
    return rows_call(name, fn, rows, [w], [(x.shape[1], f32)], [(w.shape, f32)])


def cols_call(name, fn, seqs, pars, outs, *, nb, ct, ncol, dseed=None):
    ns, npar = len(seqs), len(pars)
    seq_len = seqs[0].shape[0] // nb
    nd = 0 if dseed is None else len(dseed)

    def body(*refs):
        sv = [r[...] for r in refs[:ns]]
        pv = [r[...] for r in refs[ns:ns + npar]]
        if dseed is None:
            o_refs = refs[ns + npar:]
            for r, v in zip(o_refs, fn(*[v.astype(f32) for v in sv], *pv), strict=True):
                r[...] = v.astype(r.dtype)
            return
        dv = [r[...].astype(f32) for r in refs[ns + npar:ns + npar + nd]]
        ds_refs = refs[ns + npar + nd:ns + npar + nd + ns]
        dp_refs = refs[ns + npar + nd + ns:]
        _, vjp = jax.vjp(fn, *[v.astype(f32) for v in sv], *pv)
        g = vjp(tuple(dv))
        for r, v in zip(ds_refs, g[:ns], strict=True):
            r[...] = v.astype(r.dtype)

        @pl.when(pl.program_id(1) == 0)
        def _():
            for r in dp_refs:
                r[...] = jnp.zeros_like(r)
        for r, v in zip(dp_refs, g[ns:], strict=True):
            r[...] += v

    full = pl.BlockSpec((seq_len, ct), lambda j, b: (b, j))
    in_specs = [full for _ in seqs]
    in_specs += [pl.BlockSpec((p.shape[0], ct), lambda j, b: (0, j)) for p in pars]
    args = list(seqs) + list(pars)
    if dseed is None:
        out_specs = [full for _ in outs]
        out_shape = [_S((nb * seq_len, ncol * ct), dt) for dt in outs]
    else:
        in_specs += [full for _ in dseed]
        args += list(dseed)
        out_specs = [full for _ in seqs] + [pl.BlockSpec((p.shape[0], ct), lambda j, b: (0, j)) for p in pars]
        out_shape = [_S((nb * seq_len, ncol * ct), bf16) for _ in seqs] + [_S(p.shape, f32) for p in pars]
    return pl.pallas_call(body, name=name, grid=(ncol, nb), in_specs=in_specs, out_specs=out_specs,
                          out_shape=out_shape, compiler_params=_cp())(*args)


def _conv4_silu(x, w, b):
    y = x * w[3:4] + _shift_down(x, 1) * w[2:3] + _shift_down(x, 2) * w[1:2] + _shift_down(x, 3) * w[0:1] + b
    return (_silu(y),)


def _conv4_silu_nobias(x, w):
    y = x * w[3:4] + _shift_down(x, 1) * w[2:3] + _shift_down(x, 2) * w[1:2] + _shift_down(x, 3) * w[0:1]
    return (_silu(y),)


def _ffn_act(gate, up, w, b):
    y = gate * w[2:3] + _shift_down(gate, 1) * w[1:2] + _shift_down(gate, 2) * w[0:1] + b
    return (_silu(y) * up,)


def scan_call(name, chunk_fn, seqs, pars, consts, outs, *, nb, nh, q, state_shape, states=None, dseed=None):
    t = seqs[0][0].shape[0]
    nc = t // (nb * q)
    ns, npar, ncon, no = len(seqs), len(pars), len(consts), len(outs)
    s0, s1 = state_shape
    bwd = dseed is not None

    def cidx(c):
        return (nc - 1 - c) if bwd else c

    def rowblk(b, c):
        return b * nc + cidx(c)

    def seq_spec(w, colfn):
        return pl.BlockSpec((q, w), lambda b, c, h: (rowblk(b, c), colfn(h)))

    def par_spec(shape, idxfn):
        return pl.BlockSpec(shape, lambda b, c, h: idxfn(h))

    st_spec = pl.BlockSpec((s0, s1), lambda b, c, h: ((rowblk(b, c)) * nh + h, 0))
    in_specs = [seq_spec(w, cf) for _, w, cf, _ in seqs]
    in_specs += [par_spec(s, f) for _, s, f in pars] + [par_spec(s, f) for _, s, f in consts]
    args = [a for a, _, _, _ in seqs] + [a for a, _, _ in pars] + [a for a, _, _ in consts]

    if not bwd:
        def body(*refs):
            sv = [r[...] for r in refs[:ns]]
            pv = [r[...] for r in refs[ns:ns + npar]]
            cv = [r[...] for r in refs[ns + npar:ns + npar + ncon]]
            o_refs = refs[ns + npar + ncon:ns + npar + ncon + no]
            save_ref = refs[ns + npar + ncon + no]
            st = refs[-1]
            c, h = pl.program_id(1), pl.program_id(2)

            @pl.when(c == 0)
            def _():
                st[h] = jnp.zeros((s0, s1), f32)
            s_in = st[h]
            save_ref[...] = s_in
            o, s_out = chunk_fn(*sv, *pv, s_in, *cv)
            st[h] = s_out
            for r, v in zip(o_refs, o, strict=True):
                r[...] = v.astype(r.dtype)

        out_specs = [seq_spec(w, cf) for _, w, cf, _ in outs] + [st_spec]
        out_shape = [_S((t, cc), dt) for cc, _, _, dt in outs] + [_S((nb * nc * nh * s0, s1), f32)]
        return pl.pallas_call(body, name=name, grid=(nb, nc, nh), in_specs=in_specs, out_specs=out_specs,
                              out_shape=out_shape, scratch_shapes=[pltpu.VMEM((nh, s0, s1), f32)],
                              compiler_params=_cp())(*args)

    def body(*refs):
        i = 0
        sv = [r[...] for r in refs[i:i + ns]]; i += ns
        pv = [r[...] for r in refs[i:i + npar]]; i += npar
        cv = [r[...] for r in refs[i:i + ncon]]; i += ncon
        dv = [r[...].astype(f32) for r in refs[i:i + no]]; i += no
        s_in = refs[i][...]; i += 1
        ds_refs = refs[i:i + ns]; i += ns
        dp_refs = refs[i:i + npar]; i += npar
        dst = refs[-1]
        b, c, h = pl.program_id(0), pl.program_id(1), pl.program_id(2)

        @pl.when(c == 0)
        def _():
            dst[h] = jnp.zeros((s0, s1), f32)

        @pl.when((b == 0) & (c == 0) & (h == 0))
        def _():
            for r in dp_refs:
                r[...] = jnp.zeros_like(r)

        fn = lambda *a: chunk_fn(*a, *cv)
        _, vjp = jax.vjp(fn, *[v.astype(f32) for v in sv], *pv, s_in)
        g = vjp((tuple(dv), dst[h]))
        dst[h] = g[ns + npar]
        for (_, _, _, rep), r, v in zip(seqs, ds_refs, g[:ns], strict=True):
            if rep == 1:
                r[...] = v.astype(r.dtype)
            else:
                @pl.when(h % rep == 0)
                def _(r=r, v=v):
                    r[...] = v.astype(r.dtype)

                @pl.when(h % rep != 0)
                def _(r=r, v=v):
                    r[...] += v.astype(r.dtype)
        for r, v in zip(dp_refs, g[ns:ns + npar], strict=True):
            r[h] += v

    in_specs += [seq_spec(w, cf) for _, w, cf, _ in outs] + [st_spec]
    args += list(dseed) + [states]
    out_specs = [seq_spec(w, cf) for _, w, cf, _ in seqs]
    out_specs += [pl.BlockSpec((nh,) + tuple(s), lambda b, c, h: (0, 0, 0)) for _, s, _ in pars]
    out_shape = [_S(a.shape, bf16 if rep == 1 else f32) for a, _, _, rep in seqs] + [_S((nh,) + tuple(s), f32) for _, s, _ in pars]
    return pl.pallas_call(body, name=name, grid=(nb, nc, nh), in_specs=in_specs, out_specs=out_specs,
                          out_shape=out_shape, scratch_shapes=[pltpu.VMEM((nh, s0, s1), f32)],
                          compiler_params=_cp())(*args)


def _ssd_group(xs, bm, cm, z, dtr, dtb, alog, dsk, nw, st, e):
    q = xs.shape[0]
    heads = range(M_H)
    sl = [slice(i * M_P, (i + 1) * M_P) for i in heads]
    gsl = [slice(g * M_N, (g + 1) * M_N) for g in range(M_G)]
    incl, _ = _masks(q)
    dt = jax.nn.softplus(dtr + dtb[0:1])
    dte = _hdot(dt, e)
    de = _hdot(dsk, e, prec=HIGHEST)[0:1]
    xc = xs * dte
    acum = _hdot(_hdot(incl.astype(f32), dt * -jnp.exp(alog[0:1]), prec=HIGHEST), e)
    last = acum[q - 1:q]
    eac, eend, elast = jnp.exp(acum), jnp.exp(last - acum), jnp.exp(last)
    xe = xc * eend
    bms, cms = [bm[:, s] for s in gsl], [cm[:, s] for s in gsl]
    cb = [_bdot(cms[g], bms[g], 1, 1) for g in range(M_G)]
    decs = []
    for i in heads:
        a_i = acum[:, sl[i]]
        diff = jnp.where(incl, a_i[:, 0:1] - jnp.transpose(a_i)[0:1, :], 0.0)
        decs.append(jnp.where(incl, jnp.exp(diff), 0.0))
    sts = [st[sl[i], :] for i in heads]
    yd = [_bdot(cb[i // 4] * decs[i], xc[:, sl[i]], 1, 0) for i in heads]
    yo = [_bdot(cms[i // 4], sts[i], 1, 1) for i in heads]
    ds = [_bdot(xe[:, sl[i]], bms[i // 4], 0, 0) for i in heads]
    new = [sts[i] * elast[:, i * M_P:i * M_P + 1] + ds[i] for i in heads]
    y = jnp.concatenate(yd, axis=1) + jnp.concatenate(yo, axis=1) * eac + de * xs
    y = y * _silu(z)
    yn = [_rms(y[:, g * 256:(g + 1) * 256], nw[:, g * 256:(g + 1) * 256]) for g in range(M_G)]
    return (jnp.concatenate(yn, axis=1),), jnp.concatenate(new, axis=0)


def _gla_group(qr, fr, ir, gr, lb, nw, st):
    q, hp = qr.shape[0], GLA_HP
    heads = range(hp)
    sl = [slice(i * H_K, (i + 1) * H_K) for i in heads]
    incl, _ = _masks(q)
    fg = lb + (1.0 - lb) * jax.nn.sigmoid(fr)
    qq = _silu(qr) * (H_K ** -0.5)
    k = 1.0 - fg
    gc = _hdot(incl.astype(f32), jnp.log(fg))
    gl = gc[q - 1:q]
    qd, ki, ke = qq * jnp.exp(gc), k * jnp.exp(-gc), k * jnp.exp(gl - gc)
    egl = jnp.exp(gl)
    sts = [st[sl[i], :] for i in heads]
    att = [jnp.where(incl, _bdot(qd[:, sl[i]], ki[:, sl[i]], 1, 1), 0.0) for i in heads]
    o1 = [_bdot(att[i], ir[:, sl[i]], 1, 0) for i in heads]
    o2 = [_bdot(qd[:, sl[i]], sts[i], 1, 0) for i in heads]
    kv = [_bdot(ke[:, sl[i]], ir[:, sl[i]], 0, 0) for i in heads]
    new = [sts[i] * _colvec(egl[:, sl[i]]) + kv[i] for i in heads]
    on = [_rms(o1[i] + o2[i], nw) * _silu(gr[:, sl[i]]) for i in heads]
    return (jnp.concatenate(on, axis=1),), jnp.concatenate(new, axis=0)


def _tri_inv_many(ms):
    n = ms[0].shape[0]
    r = lax.broadcasted_iota(jnp.int32, (n, n), 0)
    c = lax.broadcasted_iota(jnp.int32, (n, n), 1)
    eye = (r == c).astype(f32)
    ts = [eye - m for m in ms]
    ps = list(ms)
    for _ in range(max(1, (n - 1).bit_length() - 1)):
        ps = [_hdot(p, p) for p in ps]
        ts = [t + _hdot(t, p) for t, p in zip(ts, ps)]
    return ts


def _gdn_group(qr, kr, v, z, ba, alog, dtb, nw, st):
    q, hp = qr.shape[0], G_HV
    heads = range(hp)
    sl = [slice(i * G_K, (i + 1) * G_K) for i in heads]
    incl, strict = _masks(q)
    beta_all = jax.nn.sigmoid(ba)
    gc_all = _hdot(incl.astype(f32), -jnp.exp(alog[0:1]) * jax.nn.softplus(ba + dtb[0:1]))
    gc_t = jnp.transpose(gc_all)
    gl_all = gc_all[q - 1:q]
    egc_all, eend_all, egl_all = jnp.exp(gc_all), jnp.exp(gl_all - gc_all), jnp.exp(gl_all)
    lane = lambda a, i: a[:, G_HV + i:G_HV + i + 1]
    beta = [beta_all[:, i:i + 1] for i in heads]
    egc = [lane(egc_all, i) for i in heads]
    qn, kn = [], []
    for j in range(hp // 2):
        qj, kj = qr[:, sl[j]], kr[:, sl[j]]
        qn.append(qj * lax.rsqrt(jnp.sum(qj * qj, axis=-1, keepdims=True) + EPS) * (G_K ** -0.5))
        kn.append(kj * lax.rsqrt(jnp.sum(kj * kj, axis=-1, keepdims=True) + EPS))
    qk = [_bdot(qn[j], kn[j], 1, 1) for j in range(hp // 2)]
    decs = []
    for i in heads:
        diff = jnp.where(incl, lane(gc_all, i) - gc_t[G_HV + i:G_HV + i + 1, :], 0.0)
        decs.append(jnp.where(incl, jnp.exp(diff), 0.0))
    kbs = [kn[i // 2] * beta[i] for i in heads]
    kk = [_bdot(kbs[i], kn[i // 2], 1, 1) for i in heads]
    tinv = _tri_inv_many([jnp.where(strict, kk[i] * decs[i], 0.0) for i in heads])
    uw = [_hdot(tinv[i], jnp.concatenate([v[:, sl[i]] * beta[i], kbs[i] * egc[i]], axis=1)) for i in heads]
    sts = [st[sl[i], :] for i in heads]
    ws = [_bdot(jnp.concatenate([uw[i][:, G_K:], qn[i // 2] * egc[i]], axis=0), sts[i], 1, 0) for i in heads]
    v_new = [uw[i][:, :G_K] - ws[i][:q] for i in heads]
    o = [ws[i][q:] + _bdot(qk[i // 2] * decs[i], v_new[i], 1, 0) for i in heads]
    new = [sts[i] * lane(egl_all, i) + _bdot(kn[i // 2] * lane(eend_all, i), v_new[i], 0, 0) for i in heads]
    on = [_rms(o[i], nw) * _silu(z[:, sl[i]]) for i in heads]
    return (jnp.concatenate(on, axis=1),), jnp.concatenate(new, axis=0)


def _xattn_fn(q, k, v):
    s = _bdot(q, k, 1, 1) * (X_D ** -0.5)
    return _bdot(jax.nn.softmax(s, axis=-1), v, 1, 0)


def xattn_fwd(q, k, v, nb, name, tl=512):
    t = q.shape[0]
    tl = min(tl, t // nb)
    nl = t // nb // tl

    def body(q_ref, k_ref, v_ref, o_ref):
        for h in range(X_H):
            sl = slice(h * X_D, (h + 1) * X_D)
            o_ref[:, sl] = _xattn_fn(q_ref[:, sl], k_ref[:, sl], v_ref[:, sl]).astype(o_ref.dtype)

    qs = pl.BlockSpec((tl, X_H * X_D), lambda b, i: (b * nl + i, 0))
    ks = pl.BlockSpec((N_MEM, X_H * X_D), lambda b, i: (b, 0))
    return pl.pallas_call(body, name=name, grid=(nb, nl), in_specs=[qs, ks, ks], out_specs=qs,
                          out_shape=_S(q.shape, bf16), compiler_params=_cp())(q, k, v)


def xattn_bwd(q, k, v, do, nb, name, tl=512):
    t = q.shape[0]
    tl = min(tl, t // nb)
    nl = t // nb // tl

    def body(q_ref, k_ref, v_ref, do_ref, dq_ref, dk_ref, dv_ref):
        @pl.when(pl.program_id(1) == 0)
        def _():
            dk_ref[...] = jnp.zeros_like(dk_ref)
            dv_ref[...] = jnp.zeros_like(dv_ref)

        for h in range(X_H):
            sl = slice(h * X_D, (h + 1) * X_D)
            _, vjp = jax.vjp(_xattn_fn, q_ref[:, sl].astype(f32), k_ref[:, sl].astype(f32), v_ref[:, sl].astype(f32))
            dq, dk, dv = vjp(do_ref[:, sl].astype(f32))
            dq_ref[:, sl] = dq.astype(dq_ref.dtype)
            dk_ref[:, sl] += dk
            dv_ref[:, sl] += dv

    qs = pl.BlockSpec((tl, X_H * X_D), lambda b, i: (b * nl + i, 0))
    ks = pl.BlockSpec((N_MEM, X_H * X_D), lambda b, i: (b, 0))
    return pl.pallas_call(body, name=name, grid=(nb, nl), in_specs=[qs, ks, ks, qs], out_specs=[qs, ks, ks],
                          out_shape=[_S(q.shape, bf16), _S(k.shape, f32), _S(v.shape, f32)],
                          compiler_params=_cp())(q, k, v, do)


def _lower_bounds(hlb):
    sm = jax.nn.softmax(hlb, axis=0)
    rows, run = [], None
    for r in range(hlb.shape[0]):
        run = sm[r:r + 1] if run is None else run + sm[r:r + 1]
        rows.append(run - sm[0:1])
    return jnp.concatenate(rows, axis=0)


def lower_bounds_fwd(hlb):
    return rows_call("lb_fwd", lambda v: ((_lower_bounds(v),), ()), [hlb], [], [(hlb.shape[1], f32)], tm=hlb.shape[0])[0]


def lower_bounds_bwd(hlb, dlb):
    def fn(v, d):
        _, vjp = jax.vjp(_lower_bounds, v)
        return (vjp(d)[0],), ()
    return rows_call("lb_bwd", fn, [hlb, dlb], [], [(hlb.shape[1], f32)], tm=hlb.shape[0])[0]


def loss_head(x, target, w):
    def fn(xv, tv, wv):
        def loss(xx, ww):
            err = _rms(xx, ww) - tv
            return 0.5 * jnp.sum(jnp.mean(err * err, axis=-1))
        val, (dx, dw) = jax.value_and_grad(loss, argnums=(0, 1))(xv, wv)
        return (dx,), (jnp.broadcast_to(val, (1, 128)), dw)
    dx, loss, dw = rows_call("loss_head", fn, [x, target], [w], [(x.shape[1], f32)], [((1, 128), f32), (w.shape, f32)])
    return dx, loss, dw


def _adamw_fn(w, g, m, v):
    m2 = ADAM_B1 * m + (1.0 - ADAM_B1) * g
    v2 = ADAM_B2 * v + (1.0 - ADAM_B2) * (g * g)
    m_hat = m2 / (1.0 - ADAM_B1 ** ADAM_STEP)
    v_hat = v2 / (1.0 - ADAM_B2 ** ADAM_STEP)
    delta = -ADAM_LR * (m_hat / (jnp.sqrt(v_hat) + ADAM_EPS) + ADAM_WD * w)
    return delta, m2, v2


def adamw(w, g, m, v, name, g2=None):
    shape = w.shape
    c = shape[-1]
    r = w.size // c
    to2 = lambda a: a.reshape(r, c)
    tm = r if r * c * 4 <= (1 << 20) else _tile(r, (256, 128, 64, 32, 16, 8))

    def fn(*a):
        if g2 is None:
            wv, gv, mv, vv = a
        else:
            wv, gv, g2v, mv, vv = a
            gv = gv + g2v
        return (gv,) + _adamw_fn(wv, gv, mv, vv), ()
    rows = [to2(w), to2(g)] + ([] if g2 is None else [to2(g2)]) + [to2(m), to2(v)]
    outs = rows_call(name, fn, rows, [], [(c, f32)] * 4, tm=tm)
    return tuple(o.reshape(shape) for o in outs)


def _pad_row(v, lane0=0):
    return jnp.pad(v.astype(f32).reshape(1, -1), ((0, 7), (lane0, 128 - lane0 - v.shape[0])))


def _pad_cols(w, n=128):
    return jnp.pad(w, ((0, 0), (0, n - w.shape[1])))


_COL = lambda h: h
_C00 = lambda h: (0, 0)
_CONV_CT = 256


def _conv(name, x, w, b, nb, dseed=None):
    fn = _conv4_silu if b is not None else _conv4_silu_nobias
    pars = [w] + ([] if b is None else [b])
    return cols_call(name, fn, [x], pars, [f32], nb=nb, ct=_CONV_CT, ncol=x.shape[1] // _CONV_CT,
                     dseed=None if dseed is None else [dseed])


GLA_HP = 8


def _ssd_scan(name, xs, bm, cm, z, dtr, p, nb, states=None, dseed=None):
    seqs = [(xs, M_INNER, _COL, 1), (bm, M_G * M_N, _COL, 1), (cm, M_G * M_N, _COL, 1), (z, M_INNER, _COL, 1), (dtr, 128, _COL, 1)]
    pars = [(p["dtb"], (8, 128), _C00), (p["alog"], (8, 128), _C00), (p["dsk"], (8, 128), _C00), (p["nw"], (1, M_INNER), _C00)]
    r = jnp.arange(128)[:, None]
    c = jnp.arange(M_INNER)[None, :]
    consts = [((r == c // M_P).astype(f32), (128, M_INNER), _C00)]
    outs = [(M_INNER, M_INNER, _COL, bf16)]
    return scan_call(name, _ssd_group, seqs, pars, consts, outs, nb=nb, nh=1, q=M_Q, state_shape=(M_H * M_P, M_N),
                     states=states, dseed=dseed)


def _gla_scan(name, qr, fr, ir, gr, p, nb, states=None, dseed=None):
    hp, ng = GLA_HP, H_H // GLA_HP
    seqs = [(a, 128 * hp, _COL, 1) for a in (qr, fr, ir, gr)]
    pars = [(p["lb"], (1, 128 * hp), lambda h: (0, h)), (p["nw"], (1, 128), _C00)]
    outs = [(D, 128 * hp, _COL, bf16)]
    return scan_call(name, _gla_group, seqs, pars, [], outs, nb=nb, nh=ng, q=H_Q, state_shape=(hp * H_K, H_K),
                     states=states, dseed=dseed)


def _gdn_scan(name, qc, kc, vc, z, ba, p, nb, states=None, dseed=None):
    seqs = [(qc, D, _COL, 1), (kc, D, _COL, 1), (vc, G_VAL, _COL, 1), (z, G_VAL, _COL, 1), (ba, 128, _COL, 1)]
    pars = [(p["alog"], (8, 128), _C00), (p["dtb"], (8, 128), _C00), (p["nw"], (1, 128), _C00)]
    outs = [(G_VAL, G_VAL, _COL, bf16)]
    return scan_call(name, _gdn_group, seqs, pars, [], outs, nb=nb, nh=1, q=G_Q, state_shape=(G_HV * G_K, G_K),
                     states=states, dseed=dseed)


def _w(wt):
    return wt if isinstance(wt, tuple) else (wt, None)


def _proj(a, wt, name, res=None, out_dtype=f32):
    arr, bsel = _w(wt)
    return mm(a, arr, bsel=bsel, res=res, out_dtype=out_dtype, name=name)


def _proj_bwd(tag, hn, pieces):
    dhn, dws, bufs = None, [], {}
    for i, (d, wt) in enumerate(pieces):
        arr, bsel = _w(wt)
        if bsel is None:
            dws.append(mm(hn, d, ta=True, out_dtype=bf16, name=f"{tag}_dw{i}"))
        else:
            bufs[id(arr)] = mm(hn, d, ta=True, out_stack=arr.shape[2], out_slots=(arr.shape[0], bsel[0]),
                               into=bufs.get(id(arr)), out_dtype=bf16, name=f"{tag}_dw{i}")
            dws.append(None)
        dhn = mm(d, arr, tb=True, bsel=bsel, res=dhn, name=f"{tag}_dh{i}")
    dws = [dw if dw is not None else bufs[id(_w(wt)[0])] for dw, (_, wt) in zip(dws, pieces, strict=True)]
    return dhn, dws


def ssd_mixer_fwd(tag, hn, w, nb):
    z, xr, br, cr, dtr = (_proj(hn, w[k], f"{tag}_in_{k}") for k in ("wz", "wx", "wb", "wc", "wdt"))
    xs = _conv(f"{tag}_convx", xr, w["cwx"], w["cbx"], nb)[0]
    bm = _conv(f"{tag}_convb", br, w["cwb"], w["cbb"], nb)[0]
    cm = _conv(f"{tag}_convc", cr, w["cwc"], w["cbc"], nb)[0]
    yn, states = _ssd_scan(f"{tag}_scan", xs, bm, cm, z, dtr, w, nb)
    return yn, (hn, z, xr, br, cr, dtr, xs, bm, cm, yn, states)


def ssd_mixer_bwd(tag, saved, dout, w, nb):
    hn, z, xr, br, cr, dtr, xs, bm, cm, yn, states = saved
    g = {"wout": mm(yn, dout, ta=True, out_dtype=bf16, name=f"{tag}_dwout")}
    dyn = mm(dout, w["wout"], tb=True, out_dtype=bf16, name=f"{tag}_dyn")
    dxs, dbm, dcm, dz, ddtr, ddtb, dalog, ddsk, dnw = _ssd_scan(f"{tag}_scanb", xs, bm, cm, z, dtr, w, nb, states, [dyn])
    dxr, g["cwx"], g["cbx"] = _conv(f"{tag}_convxb", xr, w["cwx"], w["cbx"], nb, dxs)
    dbr, g["cwb"], g["cbb"] = _conv(f"{tag}_convbb", br, w["cwb"], w["cbb"], nb, dbm)
    dcr, g["cwc"], g["cbc"] = _conv(f"{tag}_convcb", cr, w["cwc"], w["cbc"], nb, dcm)
    dhn, (g["wz"], g["wx"], g["wb"], g["wc"], g["wdt"]) = _proj_bwd(
        tag, hn, [(dz, w["wz"]), (dxr, w["wx"]), (dbr, w["wb"]), (dcr, w["wc"]), (ddtr, w["wdt"])])
    g["dtb"], g["alog"], g["dsk"] = (jnp.sum(a, axis=0)[0, :M_H] for a in (ddtb, dalog, ddsk))
    g["nw"] = dnw.reshape(M_INNER)
    return dhn, g


def gla_mixer_fwd(tag, hn, w, nb):
    qr, fr, ir, gr = (_proj(hn, w[k], f"{tag}_in_{k}") for k in ("wq", "wf", "wi", "wg"))
    on, states = _gla_scan(f"{tag}_scan", qr, fr, ir, gr, w, nb)
    return on, (hn, qr, fr, ir, gr, on, states)


def gla_mixer_bwd(tag, saved, dout, w, nb):
    hn, qr, fr, ir, gr, on, states = saved
    g = {"wout": mm(on, dout, ta=True, out_dtype=bf16, name=f"{tag}_dwout")}
    don = mm(dout, w["wout"], tb=True, out_dtype=bf16, name=f"{tag}_don")
    dq, df, di, dg, dlb, dnw = _gla_scan(f"{tag}_scanb", qr, fr, ir, gr, w, nb, states, [don])
    dhn, (g["wq"], g["wf"], g["wi"], g["wg"]) = _proj_bwd(tag, hn, [(dq, w["wq"]), (df, w["wf"]), (di, w["wi"]), (dg, w["wg"])])
    g["lb"] = dlb.reshape(1, D)
    g["nw"] = jnp.sum(dnw, axis=0).reshape(H_K)
    return dhn, g


def gdn_mixer_fwd(tag, hn, w, nb):
    qr, kr, vr, z, ba = (_proj(hn, w[k], f"{tag}_in_{k}") for k in ("wq", "wk", "wv", "wz", "wba"))
    qc = _conv(f"{tag}_convq", qr, w["cwq"], None, nb)[0]
    kc = _conv(f"{tag}_convk", kr, w["cwk"], None, nb)[0]
    vc = _conv(f"{tag}_convv", vr, w["cwv"], None, nb)[0]
    on, states = _gdn_scan(f"{tag}_scan", qc, kc, vc, z, ba, w, nb)
    return on, (hn, qr, kr, vr, z, ba, qc, kc, vc, on, states)


def gdn_mixer_bwd(tag, saved, dout, w, nb):
    hn, qr, kr, vr, z, ba, qc, kc, vc, on, states = saved
    g = {"wout": mm(on, dout, ta=True, out_dtype=bf16, name=f"{tag}_dwout")}
    don = mm(dout, w["wout"], tb=True, out_dtype=bf16, name=f"{tag}_don")
    dqc, dkc, dvc, dz, dba, dalog, ddtb, dnw = _gdn_scan(f"{tag}_scanb", qc, kc, vc, z, ba, w, nb, states, [don])
    dqr, g["cwq"] = _conv(f"{tag}_convqb", qr, w["cwq"], None, nb, dqc)
    dkr, g["cwk"] = _conv(f"{tag}_convkb", kr, w["cwk"], None, nb, dkc)
    dvr, g["cwv"] = _conv(f"{tag}_convvb", vr, w["cwv"], None, nb, dvc)
    dhn, (g["wq"], g["wk"], g["wv"], g["wz"], g["wba"]) = _proj_bwd(
        tag, hn, [(dqr, w["wq"]), (dkr, w["wk"]), (dvr, w["wv"]), (dz, w["wz"]), (dba, w["wba"])])
    g["alog"], g["dtb"] = (jnp.sum(a, axis=0)[0, G_HV:2 * G_HV] for a in (dalog, ddtb))
    g["nw"] = jnp.sum(dnw, axis=0).reshape(G_K)
    return dhn, g


_MIXERS = {0: (ssd_mixer_fwd, ssd_mixer_bwd), 1: (gla_mixer_fwd, gla_mixer_bwd), 2: (gdn_mixer_fwd, gdn_mixer_bwd)}


def layer_fwd(i, x, mem, weights_of, nb):
    t = f"l{i}"
    wm = weights_of(i, 0, x)
    hn = rms_fwd(x, wm["ln_mix"], f"{t}_ln_mix")
    mix, s_mix = _MIXERS[i % 3][0](f"{t}_mix", hn, wm["mix"], nb)
    x1 = mm(mix, wm["mix"]["wout"], res=x, name=f"{t}_mix_out")
    w = weights_of(i, 1, x1)
    hx = rms_fwd(x1, w["ln_xattn"], f"{t}_ln_xattn")
    mn = rms_fwd(mem, w["ln_mem"], f"{t}_ln_mem")
    q = _proj(hx, w["xq"], f"{t}_xa_q", out_dtype=bf16)
    k = _proj(mn, w["xk"], f"{t}_xa_k", out_dtype=bf16)
    v = _proj(mn, w["xv"], f"{t}_xa_v", out_dtype=bf16)
    o = xattn_fwd(q, k, v, nb, f"{t}_xattn")
    x2 = mm(o, w["xo"], res=x1, name=f"{t}_xa_o")
    hf = rms_fwd(x2, w["ln_ffn"], f"{t}_ln_ffn")
    gate = _proj(hf, w["fg"], f"{t}_ffn_gate", out_dtype=bf16)
    up = _proj(hf, w["fu"], f"{t}_ffn_up", out_dtype=bf16)
    act = cols_call(f"{t}_ffn_act", _ffn_act, [gate, up], [w["fcw"], w["fcb"]], [bf16], nb=nb, ct=_CONV_CT,
                    ncol=D_FF // _CONV_CT)[0]
    x3 = mm(act, w["fd"], res=x2, name=f"{t}_ffn_down")
    return x3, (wm, w, x, s_mix, x1, hx, mn, q, k, v, o, x2, hf, gate, up, act)


def layer_bwd(i, saved, dx, mem, nb, token, grads_done):
    t = f"l{i}b"
    wm, w, x, s_mix, x1, hx, mn, q, k, v, o, x2, hf, gate, up, act = saved
    if token is not None:
        w = dict(w, fd=w["fd"] + token[0, 0].astype(w["fd"].dtype))
    g = {}
    g["fd"] = mm(act, dx, ta=True, out_dtype=bf16, name=f"{t}_dwd")
    dact = mm(dx, w["fd"], tb=True, out_dtype=bf16, name=f"{t}_dact")
    dgate, dup, g["fcw"], g["fcb"] = cols_call(f"{t}_ffn_act", _ffn_act, [gate, up], [w["fcw"], w["fcb"]], [bf16], nb=nb,
                                               ct=_CONV_CT, ncol=D_FF // _CONV_CT, dseed=[dact])
    dhf, (g["fg"], g["fu"]) = _proj_bwd(f"{t}_ffn", hf, [(dgate, w["fg"]), (dup, w["fu"])])
    dx, g["ln_ffn"] = rms_bwd(x2, w["ln_ffn"], dhf, dx, f"{t}_ln_ffn")
    g["xo"] = mm(o, dx, ta=True, out_dtype=bf16, name=f"{t}_dwo")
    do = mm(dx, w["xo"], tb=True, out_dtype=bf16, name=f"{t}_do")
    dq, dk, dv = xattn_bwd(q, k, v, do, nb, f"{t}_xattn")
    dhx, (g["xq"],) = _proj_bwd(f"{t}_xq", hx, [(dq, w["xq"])])
    dmn, (g["xk"], g["xv"]) = _proj_bwd(f"{t}_xkv", mn, [(dk, w["xk"]), (dv, w["xv"])])
    _, g["ln_mem"] = rms_bwd(mem, w["ln_mem"], dmn, None, f"{t}_ln_mem")
    dx, g["ln_xattn"] = rms_bwd(x1, w["ln_xattn"], dhx, dx, f"{t}_ln_xattn")
    token = grads_done(i, 1, g, dx) if grads_done else None
    mixw = wm["mix"] if token is None else dict(wm["mix"], wout=wm["mix"]["wout"] + token[0, 0].astype(wm["mix"]["wout"].dtype))
    dhn, g["mix"] = _MIXERS[i % 3][1](f"{t}_mix", s_mix, dx, mixw, nb)
    dx, g["ln_mix"] = rms_bwd(x, wm["ln_mix"], dhn, dx, f"{t}_ln_mix")
    token = grads_done(i, 0, g, dx) if grads_done else None
    return dx, g, token


def local_step(x, mem, target, weights_of, final_norm, nb, grads_done=None):
    saved = []
    for i in range(DEPTH):
        x, s = layer_fwd(i, x, mem, weights_of, nb)
        saved.append(s)
    dx, loss, dfinal = loss_head(x, target, final_norm)
    grads = [None] * DEPTH
    token = None
    for i in reversed(range(DEPTH)):
        dx, grads[i], token = layer_bwd(i, saved[i], dx, mem, nb, token, grads_done)
    return loss, dx, grads, dfinal


WEIGHTS = ["ln_mix", "ln_xattn", "ln_mem", "ln_ffn", "final_norm", "m_in_w", "m_conv_w", "m_conv_b", "m_dt_bias", "m_a_log",
           "m_d", "m_norm_w", "m_out_w", "h_in_w", "h_lower_bounds", "h_norm_w", "h_out_w", "g_in_w", "g_conv_w", "g_a_log",
           "g_dt_bias", "g_norm_w", "g_out_w", "xa_q", "xa_kv", "xa_o", "f_up", "f_conv_w", "f_conv_b", "f_down"]
SHARD_AXIS = {"m_in_w": 2, "m_conv_w": 2, "m_conv_b": 1, "m_norm_w": 1, "m_out_w": 1, "h_in_w": 2, "h_out_w": 1, "g_in_w": 2,
              "g_conv_w": 2, "g_out_w": 1, "xa_q": 1, "xa_kv": 2, "xa_o": 1, "f_up": 2, "f_conv_w": 2, "f_down": 1}
MATRICES = ["m_in_w", "m_out_w", "h_in_w", "h_out_w", "g_in_w", "g_out_w", "xa_q", "xa_kv", "xa_o", "f_up", "f_down"]
SMALL_SHARDED = [n for n in WEIGHTS if n in SHARD_AXIS and n not in MATRICES]
REPLICATED = [n for n in WEIGHTS if n not in SHARD_AXIS]
_MIXER_PREFIX = {0: "m", 1: "h", 2: "g"}


def layer_weight_names(i, part):
    if part == 0:
        p = _MIXER_PREFIX[i % 3]
        return [(n, i // 3) for n in WEIGHTS if n in SHARD_AXIS and n.startswith(p + "_")]
    return [(n, i) for n in ("xa_q", "xa_kv", "xa_o", "f_up", "f_conv_w", "f_down")]


def _cols(st, lo, hi):
    ns = st.shape[-1]
    parts = []
    for j in range(NCHIP):
        a, b = max(lo, j * ns), min(hi, (j + 1) * ns)
        if a < b:
            parts.append(st[j][..., a - j * ns:b - j * ns])
    return parts[0] if len(parts) == 1 else jnp.concatenate(parts, axis=-1)


def _col_shards(pieces, ns):
    full = jnp.concatenate(pieces, axis=-1)
    return [full[..., j * ns:(j + 1) * ns] for j in range(NCHIP)]


def _rows(st):
    return st.reshape(st.shape[0] * st.shape[1], st.shape[2])


def prep_layer(i, part, G, R, lb):
    row = lambda a: a.reshape(1, -1)
    p, k = _MIXER_PREFIX[i % 3], i // 3
    if part == 1:
        kv, fup = G["xa_kv"], G["f_up"]
        return dict(ln_xattn=R["ln_xattn"][i:i + 1], ln_mem=R["ln_mem"][i:i + 1], ln_ffn=R["ln_ffn"][i:i + 1],
                    xq=_rows(G["xa_q"]), xk=(kv, (0, 2)), xv=(kv, (2, 2)), xo=_rows(G["xa_o"]), fg=(fup, (0, 2)), fu=(fup, (2, 2)),
                    fcw=_cols(G["f_conv_w"], 0, D_FF), fcb=R["f_conv_b"][i:i + 1], fd=_rows(G["f_down"]))
    layer = dict(ln_mix=R["ln_mix"][i:i + 1])
    inw, wout = G[p + "_in_w"], _rows(G[p + "_out_w"])
    if p == "m":
        cw, cb = G["m_conv_w"], G["m_conv_b"]
        a, b, c = M_INNER, M_INNER + M_G * M_N, M_CONV
        layer["mix"] = dict(
            wz=_cols(inw, 0, M_INNER), wx=_cols(inw, M_INNER, M_INNER + a), wb=_cols(inw, M_INNER + a, M_INNER + b),
            wc=_cols(inw, M_INNER + b, M_MAIN), wdt=_pad_cols(_cols(inw, M_MAIN, M_IN)),
            cwx=_cols(cw, 0, a), cwb=_cols(cw, a, b), cwc=_cols(cw, b, c),
            cbx=row(_cols(cb, 0, a)), cbb=row(_cols(cb, a, b)), cbc=row(_cols(cb, b, c)),
            dtb=_pad_row(R["m_dt_bias"][k]), alog=_pad_row(R["m_a_log"][k]), dsk=_pad_row(R["m_d"][k]),
            nw=row(_cols(G["m_norm_w"], 0, M_INNER)), wout=wout)
    elif p == "h":
        layer["mix"] = dict(wq=(inw, (0, 1)), wf=(inw, (1, 1)), wi=(inw, (2, 1)), wg=(inw, (3, 1)),
                            lb=lb[i:i + 1], nw=row(R["h_norm_w"][k]), wout=wout)
    else:
        cw = G["g_conv_w"]
        layer["mix"] = dict(
            wq=_cols(inw, 0, D), wk=_cols(inw, D, 2 * D), wv=_cols(inw, 2 * D, G_CONV), wz=_cols(inw, G_CONV, G_MAIN),
            wba=_pad_cols(_cols(inw, G_MAIN, G_IN)), cwq=_cols(cw, 0, D), cwk=_cols(cw, D, 2 * D), cwv=_cols(cw, 2 * D, G_CONV),
            alog=_pad_row(R["g_a_log"][k], G_HV), dtb=_pad_row(R["g_dt_bias"][k], G_HV),
            nw=row(R["g_norm_w"][k]), wout=wout)
    return layer


def matrix_grad_parts(i, part, g):
    by_rows = lambda a: a.reshape(NCHIP, a.shape[0] // NCHIP, a.shape[1])
    if part == 1:
        return {"xa_q": by_rows(g["xq"]), "xa_kv": g["xk"], "xa_o": by_rows(g["xo"]), "f_up": g["fg"], "f_down": by_rows(g["fd"])}
    p = _MIXER_PREFIX[i % 3]
    m = g["mix"]
    out = {p + "_out_w": by_rows(m["wout"])}
    if p == "m":
        out["m_in_w"] = jnp.stack(_col_shards([m["wz"], m["wx"], m["wb"], m["wc"], m["wdt"]], M_IN // NCHIP))
    elif p == "h":
        out["h_in_w"] = m["wq"]
    else:
        out["g_in_w"] = jnp.stack(_col_shards([m["wq"], m["wk"], m["wv"], m["wz"], m["wba"]], G_IN // NCHIP))
    return out


def small_grads(grads, dfinal, hlb):
    cat = lambda xs: jnp.concatenate(xs, axis=1)
    out = {k: jnp.concatenate([g[k] for g in grads], axis=0) for k in ("ln_mix", "ln_xattn", "ln_mem", "ln_ffn")}
    out["final_norm"] = dfinal.reshape(D)
    out["f_conv_w"] = jnp.stack([g["fcw"] for g in grads])
    out["f_conv_b"] = jnp.concatenate([g["fcb"] for g in grads], axis=0)
    ms = [g["mix"] for i, g in enumerate(grads) if i % 3 == 0]
    out["m_conv_w"] = jnp.stack([cat([m["cwx"], m["cwb"], m["cwc"]]) for m in ms])
    out["m_conv_b"] = jnp.concatenate([cat([m["cbx"], m["cbb"], m["cbc"]]) for m in ms], axis=0)
    out["m_dt_bias"] = jnp.stack([m["dtb"] for m in ms])
    out["m_a_log"] = jnp.stack([m["alog"] for m in ms])
    out["m_d"] = jnp.stack([m["dsk"] for m in ms])
    out["m_norm_w"] = jnp.stack([m["nw"] for m in ms])
    hs = [(i, g["mix"]) for i, g in enumerate(grads) if i % 3 == 1]
    lb_rows = dict(hs)
    dlb = jnp.concatenate([lb_rows[i]["lb"] if i in lb_rows else jnp.zeros((1, D), f32) for i in range(DEPTH)], axis=0)
    out["h_lower_bounds"] = lower_bounds_bwd(hlb, dlb)
    out["h_norm_w"] = jnp.stack([m["nw"] for _, m in hs])
    gs = [g["mix"] for i, g in enumerate(grads) if i % 3 == 2]
    out["g_conv_w"] = jnp.stack([cat([m["cwq"], m["cwk"], m["cwv"]]) for m in gs])
    out["g_a_log"] = jnp.stack([m["alog"] for m in gs])
    out["g_dt_bias"] = jnp.stack([m["dtb"] for m in gs])
    out["g_norm_w"] = jnp.stack([m["nw"] for m in gs])
    return out


_HBM = pl.BlockSpec(memory_space=pltpu.HBM)


def _place():
    x, y, c = lax.axis_index("x"), lax.axis_index("y"), lax.axis_index("c")
    chips = [(1 - x, y), (x, 1 - y), (1 - x, 1 - y)]
    return x, y, c, chips


_SEM = pl.BlockSpec(memory_space=pltpu.SEMAPHORE)
_ANY = pl.BlockSpec(memory_space=pl.ANY)
_SPLIT = pltpu.CompilerParams(has_side_effects=pltpu.SideEffectType.DATAFLOW_SIDE_EFFECTING)


def _hbm(a):
    return pltpu.with_memory_space_constraint(a, pltpu.HBM)


def _split_start(name, srcs, lands, dep, copies):
    n = len(srcs)

    def body(*refs):
        src_refs, land_refs = refs[:n], refs[n:2 * n]
        send_sems, recv_sems = refs[2 * n + 1], refs[2 * n + 2]
        token = refs[-1]
        for cp in copies(src_refs, land_refs, send_sems, recv_sems):
            cp.start()
        token[...] = jnp.zeros_like(token)

    thru = [pltpu.HBM(a.shape, a.dtype) for a in list(srcs) + list(lands)]
    out = pl.pallas_call(
        body, name=name, in_specs=[_HBM] * (2 * n) + [_ANY],
        out_specs=[_SEM, _SEM] + [_HBM] * (2 * n) + [pl.BlockSpec(memory_space=pltpu.VMEM)],
        out_shape=[pltpu.SemaphoreType.DMA((3 * n,)), pltpu.SemaphoreType.DMA((3 * n,))] + thru + [_S((8, 128), f32)],
        input_output_aliases={t: 2 + t for t in range(2 * n)}, compiler_params=_SPLIT,
    )(*[_hbm(a) for a in srcs], *[_hbm(a) for a in lands], dep)
    return out[0], out[1], out[2:2 + n], out[2 + n:2 + 2 * n], out[-1]


def _split_wait(name, started, after, copies):
    send_sems, recv_sems, srcs, lands, _ = started
    n = len(srcs)

    def body(*refs):
        src_refs, land_refs = refs[:n], refs[n:2 * n]
        s_sems, r_sems = refs[2 * n], refs[2 * n + 1]
        for cp in copies(src_refs, land_refs, s_sems, r_sems):
            cp.wait_send()
            cp.wait_recv()

    out = pl.pallas_call(
        body, name=name, in_specs=[_HBM] * (2 * n) + [_SEM, _SEM, _ANY], out_specs=[_HBM] * (2 * n),
        out_shape=[pltpu.HBM(a.shape, a.dtype) for a in list(srcs) + list(lands)],
        input_output_aliases={t: t for t in range(2 * n)}, compiler_params=_SPLIT,
    )(*srcs, *lands, send_sems, recv_sems, after)
    return out[:n], out[n:]


def _gather_copies(arrive):
    def copies(src_refs, land_refs, send_sems, recv_sems):
        x, y, c, chips = _place()
        out = []
        for t, (s, l) in enumerate(zip(src_refs, land_refs, strict=True)):
            for j, (px, py) in enumerate(chips):
                slot = 2 * px + py if arrive else 2 * x + y
                out.append(pltpu.make_async_remote_copy(src_ref=s, dst_ref=l.at[slot], send_sem=send_sems.at[3 * t + j],
                                                        recv_sem=recv_sems.at[3 * t + j], device_id=(px, py, c), device_id_type=MESH))
        return out
    return copies


def gather_start(name, tensors, me, dep):
    lands = [lax.dynamic_update_index_in_dim(jnp.zeros((NCHIP,) + a.shape, a.dtype), a, me, 0) for a in tensors]
    return _split_start(name, tensors, lands, dep, _gather_copies(False))


def gather_wait(name, started, after):
    return _split_wait(name, started, after, _gather_copies(True))


def _scatter_copies(src_refs, land_refs, send_sems, recv_sems):
    x, y, c, chips = _place()
    out = []
    for t, (s, l) in enumerate(zip(src_refs, land_refs, strict=True)):
        for j, (px, py) in enumerate(chips):
            out.append(pltpu.make_async_remote_copy(src_ref=s.at[2 * px + py], dst_ref=l.at[j], send_sem=send_sems.at[3 * t + j],
                                                    recv_sem=recv_sems.at[3 * t + j], device_id=(px, py, c), device_id_type=MESH))
    return out


def scatter_start(name, parts, dep):
    lands = [lax.empty((3,) + a.shape[1:], a.dtype) for a in parts]
    return _split_start(name, parts, lands, dep, _scatter_copies)


def scatter_wait(name, started, after):
    return _split_wait(name, started, after, _scatter_copies)


def sum_parts(name, part, land, me):
    shape = land.shape[1:]
    c = shape[-1]
    r = land.size // (3 * c)
    tm = _tile(r, (256, 128, 64, 32, 16, 8))

    def body(me_ref, p_ref, l_ref, o_ref):
        o_ref[...] = p_ref[...].astype(f32) + l_ref[0].astype(f32) + l_ref[1].astype(f32) + l_ref[2].astype(f32)

    grid_spec = pltpu.PrefetchScalarGridSpec(
        num_scalar_prefetch=1, grid=(r // tm,),
        in_specs=[pl.BlockSpec((None, tm, c), lambda i, me_ref: (me_ref[0], i, 0)),
                  pl.BlockSpec((3, tm, c), lambda i, me_ref: (0, i, 0))],
        out_specs=pl.BlockSpec((tm, c), lambda i, me_ref: (i, 0)))
    out = pl.pallas_call(body, name=name, grid_spec=grid_spec, out_shape=_S((r, c), f32), compiler_params=_cp())(
        me.reshape(1).astype(jnp.int32), part.reshape(NCHIP, r, c), land.reshape(3, r, c))
    return out.reshape(shape)


def _swap_copies(src_refs, land_refs, send_sems, recv_sems):
    x, y, c, _ = _place()
    return [pltpu.make_async_remote_copy(src_ref=s, dst_ref=l, send_sem=send_sems.at[3 * t], recv_sem=recv_sems.at[3 * t],
                                         device_id=(x, y, 1 - c), device_id_type=MESH)
            for t, (s, l) in enumerate(zip(src_refs, land_refs, strict=True))]


def swap_start(name, tensors, dep):
    return _split_start(name, tensors, [lax.empty(a.shape, a.dtype) for a in tensors], dep, _swap_copies)


def swap_wait(name, started, after):
    return _split_wait(name, started, after, _swap_copies)


def allreduce_small(v):
    r, n = v.shape

    def body(x_ref, out_ref, gat, send_sems, recv_sems, local_sem):
        x, y, c, chips = _place()
        me, sibling = (x, y, c), (x, y, 1 - c)

        def rows(px, py, pc):
            return gat.at[pl.ds((4 * px + 2 * py + pc) * r, r), :]

        def copy(k, block, to, src=None):
            return pltpu.make_async_remote_copy(src_ref=rows(*block) if src is None else src, dst_ref=rows(*block),
                                                send_sem=send_sems.at[k], recv_sem=recv_sems.at[k], device_id=to,
                                                device_id_type=MESH)

        mine = pltpu.make_async_copy(x_ref, rows(*me), local_sem)
        mine.start()
        first = [copy(0, me, sibling, src=x_ref)] + [copy(1 + j, me, (*chip, c), src=x_ref) for j, chip in enumerate(chips)]
        for cp in first:
            cp.start()
        passed = [copy(4 + j, (*chip, c), sibling) for j, chip in enumerate(chips)]
        for j, chip in enumerate(chips):
            copy(1 + j, (*chip, c), me).wait_recv()
            passed[j].start()
        copy(0, sibling, me).wait_recv()
        for j, chip in enumerate(chips):
            copy(4 + j, (*chip, 1 - c), me).wait_recv()
        for cp in first + passed:
            cp.wait_send()
        mine.wait()
        acc = gat[0:r, :]
        for d in range(1, 8):
            acc = acc + gat[d * r:(d + 1) * r, :]
        out_ref[...] = acc

    vm = pl.BlockSpec(memory_space=pltpu.VMEM)
    return pl.pallas_call(
        body, name="allreduce_small", in_specs=[vm], out_specs=vm, out_shape=_S((r, n), v.dtype),
        scratch_shapes=[pltpu.VMEM((8 * r, n), v.dtype), pltpu.SemaphoreType.DMA((7,)), pltpu.SemaphoreType.DMA((7,)),
                        pltpu.SemaphoreType.DMA],
        compiler_params=_cp())(v)


SMALL_ROW = 1024


def kernel(x, mem, ln_mix, ln_xattn, ln_mem, ln_ffn, final_norm, m_in_w, m_conv_w, m_conv_b, m_dt_bias, m_a_log, m_d, m_norm_w, m_out_w, h_in_w, h_lower_bounds, h_norm_w, h_out_w, g_in_w, g_conv_w, g_a_log, g_dt_bias, g_norm_w, g_out_w, xa_q, xa_kv, xa_o, f_up, f_conv_w, f_conv_b, f_down, loss_target, m_ln_mix, m_ln_xattn, m_ln_mem, m_ln_ffn, m_final_norm, m_m_in_w, m_m_conv_w, m_m_conv_b, m_m_dt_bias, m_m_a_log, m_m_d, m_m_norm_w, m_m_out_w, m_h_in_w, m_h_lower_bounds, m_h_norm_w, m_h_out_w, m_g_in_w, m_g_conv_w, m_g_a_log, m_g_dt_bias, m_g_norm_w, m_g_out_w, m_xa_q, m_xa_kv, m_xa_o, m_f_up, m_f_conv_w, m_f_conv_b, m_f_down, v_ln_mix, v_ln_xattn, v_ln_mem, v_ln_ffn, v_final_norm, v_m_in_w, v_m_conv_w, v_m_conv_b, v_m_dt_bias, v_m_a_log, v_m_d, v_m_norm_w, v_m_out_w, v_h_in_w, v_h_lower_bounds, v_h_norm_w, v_h_out_w, v_g_in_w, v_g_conv_w, v_g_a_log, v_g_dt_bias, v_g_norm_w, v_g_out_w, v_xa_q, v_xa_kv, v_xa_o, v_f_up, v_f_conv_w, v_f_conv_b, v_f_down):
    local = dict(zip(WEIGHTS, (ln_mix, ln_xattn, ln_mem, ln_ffn, final_norm, m_in_w, m_conv_w, m_conv_b, m_dt_bias, m_a_log, m_d, m_norm_w, m_out_w, h_in_w, h_lower_bounds, h_norm_w, h_out_w, g_in_w, g_conv_w, g_a_log, g_dt_bias, g_norm_w, g_out_w, xa_q, xa_kv, xa_o, f_up, f_conv_w, f_conv_b, f_down), strict=True))
    mom_m = dict(zip(WEIGHTS, (m_ln_mix, m_ln_xattn, m_ln_mem, m_ln_ffn, m_final_norm, m_m_in_w, m_m_conv_w, m_m_conv_b, m_m_dt_bias, m_m_a_log, m_m_d, m_m_norm_w, m_m_out_w, m_h_in_w, m_h_lower_bounds, m_h_norm_w, m_h_out_w, m_g_in_w, m_g_conv_w, m_g_a_log, m_g_dt_bias, m_g_norm_w, m_g_out_w, m_xa_q, m_xa_kv, m_xa_o, m_f_up, m_f_conv_w, m_f_conv_b, m_f_down), strict=True))
    mom_v = dict(zip(WEIGHTS, (v_ln_mix, v_ln_xattn, v_ln_mem, v_ln_ffn, v_final_norm, v_m_in_w, v_m_conv_w, v_m_conv_b, v_m_dt_bias, v_m_a_log, v_m_d, v_m_norm_w, v_m_out_w, v_h_in_w, v_h_lower_bounds, v_h_norm_w, v_h_out_w, v_g_in_w, v_g_conv_w, v_g_a_log, v_g_dt_bias, v_g_norm_w, v_g_out_w, v_xa_q, v_xa_kv, v_xa_o, v_f_up, v_f_conv_w, v_f_conv_b, v_f_down), strict=True))
    nb, seq, _ = x.shape
    me = 2 * lax.axis_index("x") + lax.axis_index("y")

    repl = {n: local[n] for n in REPLICATED}
    lb = lower_bounds_fwd(repl["h_lower_bounds"])
    nstage = 2 * DEPTH
    names = [layer_weight_names(s // 2, s % 2) for s in range(nstage)]
    cast = lambda n, a: a.astype(bf16) if n in MATRICES else a
    flying = {0: gather_start("gather_start_s0", [cast(n, local[n][k]) for n, k in names[0]], me, x)}
    tok0 = flying[0][4][0, 0]
    shards = [None] + [[cast(n, local[n][k] + tok0) for n, k in names[s]] for s in range(1, nstage)]

    def weights_of(i, part, x_in):
        s = 2 * i + part
        gathered = gather_wait(f"gather_wait_s{s}", flying.pop(s), shards[-1][-1] if s == 0 else x_in)[1]
        w = prep_layer(i, part, {n: g for (n, _), g in zip(names[s], gathered, strict=True)}, repl, lb)
        if s + 1 < nstage:
            flying[s + 1] = gather_start(f"gather_start_s{s + 1}", shards[s + 1], me, gathered[0])
            norm = "ln_mix" if part == 0 else "ln_xattn"
            w[norm] = w[norm] + flying[s + 1][4][0, 0]
        return w

    scattering, swapping = {}, []

    def landed(s, after):
        part_names, started = scattering.pop(s)
        sent, got = scatter_wait(f"scatter_wait_s{s}", started, after)
        sums = [sum_parts(f"sum_s{s}_{n}", p, l, me) for n, p, l in zip(part_names, sent, got, strict=True)]
        swapping.append((s, part_names, swap_start(f"swap_start_s{s}", sums, sums[0])))

    def grads_done(i, part, g, dx_i):
        s = 2 * i + part
        parts = matrix_grad_parts(i, part, g)
        scattering[s] = (list(parts), scatter_start(f"scatter_start_s{s}", list(parts.values()), dx_i))
        token = scattering[s][1][4]
        if s + 1 in scattering:
            landed(s + 1, dx_i)
        return token

    loss, dx, lgrads, dfinal = local_step(x.reshape(nb * seq, D), mem.reshape(nb * N_MEM, D), loss_target.reshape(nb * seq, D),
                                          weights_of, repl["final_norm"].reshape(1, D), nb, grads_done)
    grads = small_grads(lgrads, dfinal, repl["h_lower_bounds"])

    small_names = REPLICATED + SMALL_SHARDED
    flat = jnp.concatenate([grads[n].astype(f32).reshape(-1) for n in small_names] + [loss[0, 0:1] + scattering[0][1][4][0, 0]])
    rows = -(-flat.shape[0] // (8 * SMALL_ROW)) * 8
    flat = jnp.pad(flat, (0, rows * SMALL_ROW - flat.shape[0])).reshape(rows, SMALL_ROW)
    red = allreduce_small(flat).reshape(-1)
    gsum, off = {}, 0
    for n in small_names:
        size = grads[n].size
        g = red[off:off + size].reshape(grads[n].shape)
        off += size
        if n in SHARD_AXIS:
            ax = SHARD_AXIS[n]
            w = g.shape[ax] // NCHIP
            g = lax.dynamic_slice_in_dim(g, me * w, w, axis=ax)
        gsum[n] = g
    loss_out = red[off]

    outs = {}
    for n in small_names:
        outs[n] = adamw(local[n], gsum[n].reshape(local[n].shape), mom_m[n], mom_v[n], f"adamw_{n}")
    mine, theirs = {n: {} for n in MATRICES}, {n: {} for n in MATRICES}

    def swapped(after):
        while swapping:
            s, part_names, started = swapping.pop()
            sent, got = swap_wait(f"swap_wait_s{s}", started, after)
            for n, a, b in zip(part_names, sent, got, strict=True):
                mine[n][s // 2], theirs[n][s // 2] = a, b

    def update(n):
        g_mine, g_theirs = (jnp.stack([d[n][i] for i in sorted(d[n])]) for d in (mine, theirs))
        outs[n] = adamw(local[n], g_mine, mom_m[n], mom_v[n], f"adamw_{n}", g2=g_theirs)

    last = [n for n, _ in names[0] if n in MATRICES]
    swapped(dx)
    for n in MATRICES:
        if n not in last:
            update(n)
    landed(0, outs["f_down"][1])
    swapped(outs["f_down"][1])
    for n in last:
        update(n)
    res = [loss_out, dx.reshape(nb, seq, D)]
    for k in range(4):
        res += [outs[n][k] for n in WEIGHTS]
    return tuple(res)
```
